```python
import math
import jax, jax.numpy as jnp
from jax import lax
import numpy as np

D_MODEL = 1024
BATCH = 16
SEQ = 256
DEPTH = 1
DEC_BATCH = 2
DEC_SEQ = 4096
PAST_LEN = 512

GRID_W = 64
POS_BASE = 10000.0
D_SSD = 1024
SSD_HEAD_DIM = 64
SSD_HEADS = D_SSD // SSD_HEAD_DIM
SSD_GROUPS = 2
SSD_HPG = SSD_HEADS // SSD_GROUPS
SSD_STATE = 128
SSD_CONV = 3
SSD_CHUNK = 128
SSD_CONV_DIM = D_SSD + 2 * SSD_GROUPS * SSD_STATE
D_SGU = 1024
SGU_GROUPS = 8
SGU_GROUP_DIM = D_SGU // SGU_GROUPS
SGU_CHUNK = 128
D_FF = 4 * D_MODEL
D_IN_PROJ = D_SSD + SSD_CONV_DIM + 2 * SSD_HEADS + 2 * D_SGU + 2 * D_MODEL
DN_ALPHA = (2.0 * DEPTH) ** 0.25
DN_BETA = (8.0 * DEPTH) ** -0.25
EPS = 1e-5

kernel_name = 'hybrid_ssd_sgu_diffusion_step'


def layer_norm(x, g, b):
    xf = x.astype(jnp.float32)
    mu = jnp.mean(xf, axis=-1, keepdims=True)
    var = jnp.mean(jnp.square(xf - mu), axis=-1, keepdims=True)
    return ((xf - mu) * lax.rsqrt(var + EPS)).astype(x.dtype) * g + b


def rms_norm(x, g):
    xf = x.astype(jnp.float32)
    ms = jnp.mean(jnp.square(xf), axis=-1, keepdims=True)
    return (xf * lax.rsqrt(ms + EPS)).astype(x.dtype) * g


def grid_pos_embed(n_tokens, dtype):
    rows = n_tokens // GRID_W
    quarter = D_MODEL // 4
    omega = 1.0 / (POS_BASE ** (jnp.arange(quarter, dtype=jnp.float32) / quarter))
    r = jnp.arange(rows, dtype=jnp.float32)[:, None] * omega
    col = jnp.arange(GRID_W, dtype=jnp.float32)[:, None] * omega
    emb_r = jnp.concatenate([jnp.sin(r), jnp.cos(r)], axis=-1)
    emb_c = jnp.concatenate([jnp.sin(col), jnp.cos(col)], axis=-1)
    emb = jnp.concatenate([
        jnp.broadcast_to(emb_r[:, None, :], (rows, GRID_W, D_MODEL // 2)),
        jnp.broadcast_to(emb_c[None, :, :], (rows, GRID_W, D_MODEL // 2))], axis=-1)
    return emb.reshape(rows * GRID_W, D_MODEL).astype(dtype)


def depthwise_conv(x, w, b):
    out = lax.conv_general_dilated(
        x, w[:, None, :], window_strides=(1,), padding='SAME',
        dimension_numbers=('NWC', 'WIO', 'NWC'), feature_group_count=x.shape[-1])
    return out + b


def ssd_chunked_scan(x, dt, a_neg, B, C, h0):
    out_dtype = x.dtype
    f32 = jnp.float32
    bsz, L = x.shape[0], x.shape[1]
    nc = L // SSD_CHUNK
    xs = x.astype(f32).reshape(bsz, nc, SSD_CHUNK, SSD_GROUPS, SSD_HPG, SSD_HEAD_DIM)
    dtc = dt.astype(f32).reshape(bsz, nc, SSD_CHUNK, SSD_GROUPS, SSD_HPG)
    a = dtc * a_neg.astype(f32).reshape(SSD_GROUPS, SSD_HPG)
    xdt = xs * dtc[..., None]
    Bc = B.astype(f32).reshape(bsz, nc, SSD_CHUNK, SSD_GROUPS, SSD_STATE)
    Cc = C.astype(f32).reshape(bsz, nc, SSD_CHUNK, SSD_GROUPS, SSD_STATE)
    acum = jnp.moveaxis(jnp.cumsum(a, axis=2), 2, -1)
    seg = acum[..., :, None] - acum[..., None, :]
    lower = jnp.tril(jnp.ones((SSD_CHUNK, SSD_CHUNK), dtype=bool))
    decay = jnp.exp(jnp.where(lower, seg, -jnp.inf))
    cb = jnp.einsum('bcign,bcjgn->bcgij', Cc, Bc)
    y_diag = jnp.einsum('bcgij,bcgrij,bcjgrp->bcigrp', cb, decay, xdt)
    a_last = acum[..., -1]
    decay_to_end = jnp.exp(a_last[..., None] - acum)
    states = jnp.einsum('bcjgn,bcgrj,bcjgrp->bcgrpn', Bc, decay_to_end, xdt)
    h_init = h0.astype(f32).reshape(bsz, SSD_GROUPS, SSD_HPG, SSD_HEAD_DIM, SSD_STATE)

    def step(h, inp):
        st, al = inp
        return jnp.exp(al)[..., None, None] * h + st, h

    h_final, h_starts = lax.scan(step, h_init,
                                 (jnp.moveaxis(states, 1, 0), jnp.moveaxis(a_last, 1, 0)))
    h_starts = jnp.moveaxis(h_starts, 0, 1)
    y_off = jnp.einsum('bcign,bcgri,bcgrpn->bcigrp', Cc, jnp.exp(acum), h_starts)
    y = (y_diag + y_off).reshape(bsz, L, SSD_HEADS, SSD_HEAD_DIM)
    return (y.astype(out_dtype),
            h_final.reshape(bsz, SSD_HEADS, SSD_HEAD_DIM, SSD_STATE).astype(out_dtype))


def ssd_bidirectional(z, xbc, dt_raw, h0_f, h0_b, conv_w, conv_b, dt_bias_f, dt_bias_b,
                      a_log_f, a_log_b, d_skip, norm_w):
    bsz, L, _ = z.shape
    gn = SSD_GROUPS * SSD_STATE
    xbc = jax.nn.silu(depthwise_conv(xbc, conv_w, conv_b))
    xh = xbc[..., :D_SSD].reshape(bsz, L, SSD_HEADS, SSD_HEAD_DIM)
    B = xbc[..., D_SSD:D_SSD + gn].reshape(bsz, L, SSD_GROUPS, SSD_STATE)
    C = xbc[..., D_SSD + gn:].reshape(bsz, L, SSD_GROUPS, SSD_STATE)
    dt_f = jax.nn.softplus(dt_raw[..., :SSD_HEADS] + dt_bias_f)
    dt_b = jax.nn.softplus(dt_raw[..., SSD_HEADS:] + dt_bias_b)
    y_f, h_f = ssd_chunked_scan(xh, dt_f, -jnp.exp(a_log_f), B, C, h0_f)
    flip = lambda t: jnp.flip(t, axis=1)
    y_b, h_b = ssd_chunked_scan(flip(xh), flip(dt_b), -jnp.exp(a_log_b), flip(B), flip(C), h0_b)
    y = y_f + flip(y_b) + d_skip[:, None] * xh
    y = y.reshape(bsz, L, D_SSD)
    y = rms_norm(y * jax.nn.silu(z), norm_w)
    return y, h_f, h_b


def chunk_sgu(uv, ln_g, ln_b, w_spatial, b_spatial):
    uv = jax.nn.gelu(uv, approximate=False)
    u, v = uv[..., :D_SGU], uv[..., D_SGU:]
    v = layer_norm(v, ln_g, ln_b)
    bsz, L, _ = v.shape
    nc = L // SGU_CHUNK
    vg = v.reshape(bsz, nc, SGU_CHUNK, SGU_GROUPS, SGU_GROUP_DIM)
    s = jnp.einsum('gij,bcjgd->bcigd', w_spatial, vg) + jnp.swapaxes(b_spatial, 0, 1)[None, None, :, :, None]
    return u * s.reshape(bsz, L, D_SGU)


def trunk_layer(x, mod, h0_f, h0_b, w_in, conv_w, conv_b, dt_bias_f, dt_bias_b, a_log_f, a_log_b,
                d_skip, ssd_norm_w, sgu_ln_g, sgu_ln_b, w_spatial, b_spatial, w_proj_a, w_proj_b,
                w_out, ln1_g, ln1_b, w_ff1, w_ff2, ln2_g, ln2_b):
    shift1, scale1, gate1, shift2, scale2, gate2 = jnp.split(mod, 6, axis=-1)
    h = x * (1.0 + scale1) + shift1
    proj = h @ w_in
    o1 = D_SSD
    o2 = o1 + SSD_CONV_DIM
    o3 = o2 + 2 * SSD_HEADS
    o4 = o3 + 2 * D_SGU
    z, xbc, dt_raw, uv, g_logits = (proj[..., :o1], proj[..., o1:o2], proj[..., o2:o3],
                                    proj[..., o3:o4], proj[..., o4:])
    y_a, h_f, h_b = ssd_bidirectional(z, xbc, dt_raw, h0_f, h0_b, conv_w, conv_b, dt_bias_f,
                                      dt_bias_b, a_log_f, a_log_b, d_skip, ssd_norm_w)
    y_b = chunk_sgu(uv, sgu_ln_g, sgu_ln_b, w_spatial, b_spatial)
    gates = jax.nn.sigmoid(g_logits)
    merged = gates[..., :D_MODEL] * (y_a @ w_proj_a) + gates[..., D_MODEL:] * (y_b @ w_proj_b)
    x = layer_norm(DN_ALPHA * x + gate1 * (merged @ w_out), ln1_g, ln1_b)
    h = x * (1.0 + scale2) + shift2
    f = jnp.square(jax.nn.relu(h @ w_ff1)) @ w_ff2
    x = layer_norm(DN_ALPHA * x + gate2 * f, ln2_g, ln2_b)
    return x, h_f, h_b


def setup_inputs(seed: int = 0) -> dict:
    key = jax.random.key(seed)
    ks = jax.random.split(key, 40)
    f32 = jnp.float32
    nrm = lambda k, shape, scale: jax.random.normal(k, shape, f32) * scale
    L = DEPTH
    dt0 = jnp.exp(jax.random.uniform(ks[10], (L, SSD_HEADS), f32, math.log(1e-3), math.log(1e-1)))
    dt1 = jnp.exp(jax.random.uniform(ks[11], (L, SSD_HEADS), f32, math.log(1e-3), math.log(1e-1)))
    inv_softplus = lambda d: d + jnp.log(-jnp.expm1(-d))
    return {
        'x_prompt': nrm(ks[0], (BATCH, SEQ, D_MODEL), 1.0),
        'x_sample': nrm(ks[1], (DEC_BATCH, DEC_SEQ, D_MODEL), 1.0),
        'state_ssd_fwd': nrm(ks[2], (DEC_BATCH, DEPTH, SSD_HEADS, SSD_HEAD_DIM, SSD_STATE), 0.5),
        'state_ssd_bwd': nrm(ks[3], (DEC_BATCH, DEPTH, SSD_HEADS, SSD_HEAD_DIM, SSD_STATE), 0.5),
        'c': nrm(ks[4], (DEC_BATCH, D_MODEL), 1.0),
        'c_ctx': nrm(ks[5], (D_MODEL,), 1.0),
        'w_ada': nrm(ks[6], (L, D_MODEL, 6 * D_MODEL), 0.5 * D_MODEL ** -0.5),
        'b_ada': nrm(ks[7], (L, 6 * D_MODEL), 0.1),
        'w_in': nrm(ks[8], (L, D_MODEL, D_IN_PROJ), D_MODEL ** -0.5),
        'conv_w': nrm(ks[9], (L, SSD_CONV, SSD_CONV_DIM), SSD_CONV ** -0.5),
        'conv_b': nrm(ks[12], (L, SSD_CONV_DIM), 0.02),
        'dt_bias_fwd': inv_softplus(dt0),
        'dt_bias_bwd': inv_softplus(dt1),
        'a_log_fwd': jnp.log(jax.random.uniform(ks[13], (L, SSD_HEADS), f32, 1.0, 16.0)),
        'a_log_bwd': jnp.log(jax.random.uniform(ks[14], (L, SSD_HEADS), f32, 1.0, 16.0)),
        'd_skip': 1.0 + nrm(ks[15], (L, SSD_HEADS), 0.1),
        'ssd_norm_w': 1.0 + nrm(ks[16], (L, D_SSD), 0.05),
        'sgu_ln_g': 1.0 + nrm(ks[17], (L, D_SGU), 0.05),
        'sgu_ln_b': nrm(ks[18], (L, D_SGU), 0.02),
        'w_spatial': nrm(ks[19], (L, SGU_GROUPS, SGU_CHUNK, SGU_CHUNK), SGU_CHUNK ** -0.5),
        'b_spatial': 1.0 + nrm(ks[20], (L, SGU_GROUPS, SGU_CHUNK), 0.05),
        'w_proj_a': nrm(ks[21], (L, D_SSD, D_MODEL), D_SSD ** -0.5),
        'w_proj_b': nrm(ks[22], (L, D_SGU, D_MODEL), D_SGU ** -0.5),
        'w_out': nrm(ks[23], (L, D_MODEL, D_MODEL), DN_BETA * D_MODEL ** -0.5),
        'ln1_g': 1.0 + nrm(ks[24], (L, D_MODEL), 0.05),
        'ln1_b': nrm(ks[25], (L, D_MODEL), 0.02),
        'w_ff1': nrm(ks[26], (L, D_MODEL, D_FF), D_MODEL ** -0.5),
        'w_ff2': nrm(ks[27], (L, D_FF, D_MODEL), DN_BETA * D_FF ** -0.5),
        'ln2_g': 1.0 + nrm(ks[28], (L, D_MODEL), 0.05),
        'ln2_b': nrm(ks[29], (L, D_MODEL), 0.02),
    }


def reference(x_prompt, x_sample, state_ssd_fwd, state_ssd_bwd, c, c_ctx, w_ada, b_ada, w_in,
              conv_w, conv_b, dt_bias_fwd, dt_bias_bwd, a_log_fwd, a_log_bwd, d_skip, ssd_norm_w,
              sgu_ln_g, sgu_ln_b, w_spatial, b_spatial, w_proj_a, w_proj_b, w_out, ln1_g, ln1_b,
              w_ff1, w_ff2, ln2_g, ln2_b):
    xp = x_prompt
    xs = x_sample + grid_pos_embed(x_sample.shape[1], x_sample.dtype)[None]
    silu_ctx = jax.nn.silu(c_ctx)
    silu_c = jax.nn.silu(c)
    new_f, new_b = [], []
    for l in range(DEPTH):
        p = (w_in[l], conv_w[l], conv_b[l], dt_bias_fwd[l], dt_bias_bwd[l], a_log_fwd[l],
             a_log_bwd[l], d_skip[l], ssd_norm_w[l], sgu_ln_g[l], sgu_ln_b[l], w_spatial[l],
             b_spatial[l], w_proj_a[l], w_proj_b[l], w_out[l], ln1_g[l], ln1_b[l], w_ff1[l],
             w_ff2[l], ln2_g[l], ln2_b[l])
        mod_ctx = (silu_ctx @ w_ada[l] + b_ada[l])[None, None, :]
        h0 = jnp.zeros((xp.shape[0], SSD_HEADS, SSD_HEAD_DIM, SSD_STATE), xp.dtype)
        xp, h_f, h_b = trunk_layer(xp, mod_ctx, h0, h0, *p)
        new_f.append(h_f)
        new_b.append(h_b)
        mod_lat = (silu_c @ w_ada[l] + b_ada[l])[:, None, :]
        xs, _, _ = trunk_layer(xs, mod_lat, state_ssd_fwd[:, l], state_ssd_bwd[:, l], *p)
    new_state_ssd_fwd = jnp.stack(new_f, axis=1)
    new_state_ssd_bwd = jnp.stack(new_b, axis=1)
    return (xp, xs, new_state_ssd_fwd, new_state_ssd_bwd)
```

```python
import functools
import math

import jax
import jax.numpy as jnp
from jax import lax
from jax.experimental import pallas as pl
from jax.experimental.pallas import tpu as pltpu

D_MODEL = 1024
GRID_W = 64
POS_BASE = 10000.0
D_SSD = 1024
SSD_HEAD_DIM = 64
SSD_HEADS = D_SSD // SSD_HEAD_DIM
SSD_GROUPS = 2
SSD_STATE = 128
SSD_CHUNK = 128
SSD_CONV_DIM = D_SSD + 2 * SSD_GROUPS * SSD_STATE
D_SGU = 1024
SGU_GROUPS = 8
SGU_GROUP_DIM = D_SGU // SGU_GROUPS
SGU_CHUNK = 128
D_FF = 4 * D_MODEL
DEPTH = 1
DN_ALPHA = (2.0 * DEPTH) ** 0.25
EPS = 1e-5

LANES = 128
SUBLANES = 8
V7X_VMEM_BYTES = 64 * 1024 * 1024

DT_PAD = LANES
HEAD_PAIRS = SSD_HEADS // 2
PAIRS_PER_GROUP = HEAD_PAIRS // SSD_GROUPS

BF16 = jnp.bfloat16
F32 = jnp.float32
HIGHEST = lax.Precision.HIGHEST


def _dot(a, b):
    return jnp.dot(a.astype(BF16), b.astype(BF16), preferred_element_type=F32)


def _dot_exact(a, b):
    return jnp.dot(a, b, precision=HIGHEST, preferred_element_type=F32)


def _layer_norm(x, g, b):
    mu = jnp.mean(x, axis=-1, keepdims=True)
    xc = x - mu
    var = jnp.mean(xc * xc, axis=-1, keepdims=True)
    return xc * lax.rsqrt(var + EPS) * g + b


def _softplus(x):
    return jnp.maximum(x, 0.0) + jnp.log1p(jnp.exp(-jnp.abs(x)))


def _const_spec(shape):
    zeros = (0,) * len(shape)
    return pl.BlockSpec(shape, lambda *_: zeros, pipeline_mode=pl.Buffered(1))


def _params(vmem_mb, n_axes):
    return pltpu.CompilerParams(
        dimension_semantics=("arbitrary",) * n_axes,
        vmem_limit_bytes=vmem_mb * 1024 * 1024)


def _ada_kernel(c_ref, w_ref, b_ref, o_ref):
    c = c_ref[...]
    o_ref[...] = _dot_exact(c * jax.nn.sigmoid(c), w_ref[...]) + b_ref[...]


def _ada(cond, w_ada, b_ada):
    n = w_ada.shape[1]
    tn = 1024
    return pl.pallas_call(
        _ada_kernel,
        grid=(n // tn,),
        in_specs=[pl.BlockSpec((SUBLANES, D_MODEL), lambda j: (0, 0)),
                  pl.BlockSpec((D_MODEL, tn), lambda j: (0, j)),
                  pl.BlockSpec((1, tn), lambda j: (0, j))],
        out_specs=pl.BlockSpec((SUBLANES, tn), lambda j: (0, j)),
        out_shape=jax.ShapeDtypeStruct((SUBLANES, n), F32),
        compiler_params=_params(32, 1),
        name="ada",
    )(cond, w_ada, b_ada.reshape(1, n))


IN_SEGS = (D_SSD, SSD_CONV_DIM, DT_PAD, 2 * D_SGU, 2 * D_MODEL)
IN_COLS = sum(IN_SEGS)
IN_NTILE = 512


def _inproj_kernel(*refs, has_pos):
    if has_pos:
        x_ref, pos_ref, mod_ref, w_ref = refs[:4]
        outs = refs[4:]
    else:
        x_ref, mod_ref, w_ref = refs[:3]
        outs = refs[3:]
    x = x_ref[0]
    if has_pos:
        x = x + pos_ref[...]
    shift1 = mod_ref[0, :, 0:D_MODEL]
    scale1 = mod_ref[0, :, D_MODEL:2 * D_MODEL]
    h = (x * (1.0 + scale1) + shift1).astype(BF16)
    off = 0
    for o_ref, width in zip(outs, IN_SEGS):
        for c0 in range(0, width, IN_NTILE):
            c1 = min(c0 + IN_NTILE, width)
            o_ref[0, :, c0:c1] = jnp.dot(h, w_ref[:, off + c0:off + c1],
                                         preferred_element_type=F32)
        off += width


def _inproj(x, pos, mods, w_in_r, mod_row0, tm):
    s_n, l_n, _ = x.shape
    has_pos = pos is not None
    tok = lambda s, j: (s, j, 0)
    in_specs = [pl.BlockSpec((1, tm, D_MODEL), tok)]
    args = [x]
    if has_pos:
        in_specs.append(pl.BlockSpec((tm, D_MODEL), lambda s, j: (j, 0)))
        args.append(pos)
    in_specs += [pl.BlockSpec((1, 1, 6 * D_MODEL), lambda s, j: (mod_row0 + s, 0, 0)),
                 _const_spec((D_MODEL, IN_COLS))]
    args += [mods, w_in_r]
    return pl.pallas_call(
        functools.partial(_inproj_kernel, has_pos=has_pos),
        grid=(s_n, l_n // tm),
        in_specs=in_specs,
        out_specs=[pl.BlockSpec((1, tm, w), tok) for w in IN_SEGS],
        out_shape=[jax.ShapeDtypeStruct((s_n, l_n, w), F32) for w in IN_SEGS],
        compiler_params=_params(56, 2),
        name="inproj",
    )(*args)


def _block_diag_pair(xp):
    lane = lax.broadcasted_iota(jnp.int32, xp.shape, 1)
    first = lane < SSD_HEAD_DIM
    return jnp.concatenate([jnp.where(first, xp, 0.0), jnp.where(first, 0.0, xp)], axis=0)


def _ssd_kernel(*refs, nc, has_h0):
    (xbc_ref, xprev_ref, xnext_ref, dtp_ref, z_ref) = refs[:5]
    k = 5
    if has_h0:
        h0f_ref, h0b_ref = refs[k:k + 2]
        k += 2
    (convw_ref, convb_ref, dtbias_ref, aneg_ref, dskip_ref, normw_ref) = refs[k:k + 6]
    k += 6
    ya_ref, hff_ref, hfb_ref = refs[k:k + 3]
    hf_s, hb_s, hbstore_s = refs[k + 3:]

    q = SSD_CHUNK
    sweep = pl.program_id(1)
    c = pl.program_id(2)
    cidx = jnp.where(sweep == 0, nc - 1 - c, c)

    xm = xbc_ref[0]
    prev_row = jnp.where(cidx > 0, xprev_ref[0, SUBLANES - 1:SUBLANES, :], 0.0)
    next_row = jnp.where(cidx < nc - 1, xnext_ref[0, 0:1, :], 0.0)
    row = lax.broadcasted_iota(jnp.int32, (q, 1), 0)
    x_prev = jnp.where(row == 0, prev_row, pltpu.roll(xm, 1, 0))
    x_next = jnp.where(row == q - 1, next_row, pltpu.roll(xm, q - 1, 0))
    xc = (convw_ref[0:1, :] * x_prev + convw_ref[1:2, :] * xm + convw_ref[2:3, :] * x_next
          + convb_ref[...])
    xa = xc * jax.nn.sigmoid(xc)
    xh = xa[:, :D_SSD]
    gn = SSD_GROUPS * SSD_STATE
    b_all = xa[:, D_SSD:D_SSD + gn]
    c_all = xa[:, D_SSD + gn:]

    dt_all = _softplus(dtp_ref[0] + dtbias_ref[...])
    a_all = dt_all * aneg_ref[...]
    dt_t = dt_all.T
    a_t = a_all.T
    ii = lax.broadcasted_iota(jnp.int32, (q, q), 0)
    jj = lax.broadcasted_iota(jnp.int32, (q, q), 1)
    lower = ii >= jj
    upper = ii <= jj
    lower_f = lower.astype(F32)
    upper_f = upper.astype(F32)
    acum = _dot_exact(lower_f, a_all)
    rcum = _dot_exact(upper_f, a_all)
    acum_t = _dot_exact(a_t, upper_f)
    rcum_t = _dot_exact(a_t, lower_f)

    li = lax.broadcasted_iota(jnp.int32, (LANES, D_SSD), 0)
    hi = lax.broadcasted_iota(jnp.int32, (LANES, D_SSD), 1) // SSD_HEAD_DIM

    def expand_row(row_vec, head0):
        sel = (li == hi + head0).astype(F32)
        return _dot_exact(jnp.broadcast_to(row_vec, (SUBLANES, LANES)), sel)[0:1, :]

    b_t = [b_all[:, g * SSD_STATE:(g + 1) * SSD_STATE].T for g in range(SSD_GROUPS)]

    def state_update(h_old, w_t, head0, decay_row):
        cols = []
        for pr in range(HEAD_PAIRS):
            g = pr // PAIRS_PER_GROUP
            h1 = head0 + 2 * pr
            lhs = jnp.concatenate([b_t[g] * w_t[h1:h1 + 1, :], b_t[g] * w_t[h1 + 1:h1 + 2, :]],
                                  axis=1)
            rhs = _block_diag_pair(xh[:, pr * LANES:(pr + 1) * LANES])
            cols.append(_dot(lhs, rhs))
        return decay_row * h_old + jnp.concatenate(cols, axis=1)

    @pl.when(sweep == 0)
    def _backward_states():
        @pl.when(c == 0)
        def _():
            if has_h0:
                hb_s[...] = h0b_ref[0].T
            else:
                hb_s[...] = jnp.zeros_like(hb_s)

        h_in = hb_s[...]
        hbstore_s[cidx] = h_in
        rtot_t = rcum_t[:, 0:1]
        w_t = dt_t * jnp.exp(rtot_t - rcum_t)
        decay_row = jnp.exp(expand_row(rcum[0:1, :], SSD_HEADS))
        h_new = state_update(h_in, w_t, SSD_HEADS, decay_row)
        hb_s[...] = h_new

        @pl.when(c == nc - 1)
        def _():
            hfb_ref[0] = h_new.T

    @pl.when(sweep == 1)
    def _forward_and_output():
        @pl.when(c == 0)
        def _():
            if has_h0:
                hf_s[...] = h0f_ref[0].T
            else:
                hf_s[...] = jnp.zeros_like(hf_s)

        hf_in = hf_s[...]
        hb_in = hbstore_s[cidx]
        neg_inf = jnp.float32(-jnp.inf)
        cb = []
        for g in range(SSD_GROUPS):
            c_g = c_all[:, g * SSD_STATE:(g + 1) * SSD_STATE].astype(BF16)
            b_g = b_all[:, g * SSD_STATE:(g + 1) * SSD_STATE].astype(BF16)
            cb.append(lax.dot_general(c_g, b_g, (((1,), (1,)), ((), ())),
                                      preferred_element_type=F32))
        y_cols = []
        for pr in range(HEAD_PAIRS):
            g = pr // PAIRS_PER_GROUP
            c_g = c_all[:, g * SSD_STATE:(g + 1) * SSD_STATE]
            m_parts, cf_parts, cbk_parts = [], [], []
            for h in (2 * pr, 2 * pr + 1):
                hb_lane = SSD_HEADS + h
                colf = acum[:, h:h + 1]
                colb = rcum[:, hb_lane:hb_lane + 1]
                seg_f = jnp.where(lower, colf - acum_t[h:h + 1, :], neg_inf)
                seg_b = jnp.where(upper, colb - rcum_t[hb_lane:hb_lane + 1, :], neg_inf)
                w = (jnp.exp(seg_f) * dt_t[h:h + 1, :]
                     + jnp.exp(seg_b) * dt_t[hb_lane:hb_lane + 1, :])
                m_parts.append(cb[g] * w)
                cf_parts.append(c_g * jnp.exp(colf))
                cbk_parts.append(c_g * jnp.exp(colb))
            sl = slice(pr * LANES, (pr + 1) * LANES)
            xp = xh[:, sl]
            lhs = jnp.concatenate(m_parts + cf_parts + cbk_parts, axis=1)
            rhs = jnp.concatenate([_block_diag_pair(xp), _block_diag_pair(hf_in[:, sl]),
                                   _block_diag_pair(hb_in[:, sl])], axis=0)
            y_cols.append(_dot(lhs, rhs) + dskip_ref[:, sl] * xp)
        y = jnp.concatenate(y_cols, axis=1)
        zz = z_ref[0]
        yg = y * (zz * jax.nn.sigmoid(zz))
        ms = jnp.mean(yg * yg, axis=-1, keepdims=True)
        ya_ref[0] = yg * lax.rsqrt(ms + EPS) * normw_ref[...]

        atot_t = acum_t[:, q - 1:q]
        w_t = dt_t * jnp.exp(atot_t - acum_t)
        decay_row = jnp.exp(expand_row(acum[q - 1:q, :], 0))
        h_new = state_update(hf_in, w_t, 0, decay_row)
        hf_s[...] = h_new

        @pl.when(c == nc - 1)
        def _():
            hff_ref[0] = h_new.T


def _ssd(xbc, dtp, z, h0f, h0b, convw, convb, dtbias, aneg, dskip, normw):
    s_n, l_n, _ = xbc.shape
    q = SSD_CHUNK
    nc = l_n // q
    has_h0 = h0f is not None
    hb_per_chunk = q // SUBLANES
    n_hb = l_n // SUBLANES

    def chunk_of(p, c):
        return jnp.where(p == 0, nc - 1 - c, c)

    def fwd_only(p, c):
        return jnp.where(p == 0, 0, c)

    in_specs = [
        pl.BlockSpec((1, q, SSD_CONV_DIM), lambda s, p, c: (s, chunk_of(p, c), 0)),
        pl.BlockSpec((1, SUBLANES, SSD_CONV_DIM),
                     lambda s, p, c: (s, jnp.maximum(chunk_of(p, c) * hb_per_chunk - 1, 0), 0)),
        pl.BlockSpec((1, SUBLANES, SSD_CONV_DIM),
                     lambda s, p, c: (s, jnp.minimum((chunk_of(p, c) + 1) * hb_per_chunk,
                                                     n_hb - 1), 0)),
        pl.BlockSpec((1, q, DT_PAD), lambda s, p, c: (s, chunk_of(p, c), 0)),
        pl.BlockSpec((1, q, D_SSD), lambda s, p, c: (s, fwd_only(p, c), 0)),
    ]
    args = [xbc, xbc, xbc, dtp, z]
    state_spec = pl.BlockSpec((1, D_SSD, SSD_STATE), lambda s, p, c: (s, 0, 0))
    if has_h0:
        in_specs += [state_spec, state_spec]
        args += [h0f, h0b]
    for a in (convw, convb, dtbias, aneg, dskip, normw):
        in_specs.append(_const_spec(a.shape))
        args.append(a)
    return pl.pallas_call(
        functools.partial(_ssd_kernel, nc=nc, has_h0=has_h0),
        grid=(s_n, 2, nc),
        in_specs=in_specs,
        out_specs=[pl.BlockSpec((1, q, D_SSD), lambda s, p, c: (s, fwd_only(p, c), 0)),
                   state_spec, state_spec],
        out_shape=[jax.ShapeDtypeStruct((s_n, l_n, D_SSD), F32),
                   jax.ShapeDtypeStruct((s_n, D_SSD, SSD_STATE), F32),
                   jax.ShapeDtypeStruct((s_n, D_SSD, SSD_STATE), F32)],
        scratch_shapes=[pltpu.VMEM((SSD_STATE, D_SSD), F32),
                        pltpu.VMEM((SSD_STATE, D_SSD), F32),
                        pltpu.VMEM((nc, SSD_STATE, D_SSD), F32)],
        compiler_params=_params(48, 3),
        name="ssd",
    )(*args)


def _mix_kernel(*refs, has_pos, n_chunks):
    if has_pos:
        x_ref, pos_ref = refs[:2]
        refs = refs[2:]
    else:
        x_ref = refs[0]
        refs = refs[1:]
    (uv_ref, g_ref, ya_ref, mod_ref, lng_ref, lnb_ref, wsp_ref, bsp_ref, wpa_ref, wpb_ref,
     wout_ref, ln1g_ref, ln1b_ref, o_ref) = refs
    x = x_ref[0]
    if has_pos:
        x = x + pos_ref[...]
    uv = uv_ref[0]
    uv = 0.5 * uv * (1.0 + lax.erf(uv * (1.0 / math.sqrt(2.0))))
    u = uv[:, :D_SGU]
    v = _layer_norm(uv[:, D_SGU:], lng_ref[...], lnb_ref[...])
    rows = []
    for ch in range(n_chunks):
        vc = v[ch * SGU_CHUNK:(ch + 1) * SGU_CHUNK, :]
        cols = []
        for g in range(SGU_GROUPS):
            sl = slice(g * SGU_GROUP_DIM, (g + 1) * SGU_GROUP_DIM)
            cols.append(_dot(wsp_ref[g], vc[:, sl]))
        rows.append(jnp.concatenate(cols, axis=1) + bsp_ref[...])
    s = jnp.concatenate(rows, axis=0) if n_chunks > 1 else rows[0]
    yb = u * s
    gl = g_ref[0]
    merged = (jax.nn.sigmoid(gl[:, :D_MODEL]) * _dot(ya_ref[0], wpa_ref[...])
              + jax.nn.sigmoid(gl[:, D_MODEL:]) * _dot(yb, wpb_ref[...]))
    gate1 = mod_ref[0, :, 2 * D_MODEL:3 * D_MODEL]
    o = _dot(merged, wout_ref[...])
    o_ref[0] = _layer_norm(DN_ALPHA * x + gate1 * o, ln1g_ref[...], ln1b_ref[...])


def _mix(x, pos, uv, g, ya, mods, mod_row0, tm, lng, lnb, wsp, bsp, wpa, wpb, wout, ln1g, ln1b):
    s_n, l_n, _ = x.shape
    has_pos = pos is not None
    tok = lambda s, j: (s, j, 0)
    in_specs = [pl.BlockSpec((1, tm, D_MODEL), tok)]
    args = [x]
    if has_pos:
        in_specs.append(pl.BlockSpec((tm, D_MODEL), lambda s, j: (j, 0)))
        args.append(pos)
    in_specs += [pl.BlockSpec((1, tm, 2 * D_SGU), tok),
                 pl.BlockSpec((1, tm, 2 * D_MODEL), tok),
                 pl.BlockSpec((1, tm, D_SSD), tok),
                 pl.BlockSpec((1, 1, 6 * D_MODEL), lambda s, j: (mod_row0 + s, 0, 0))]
    args += [uv, g, ya, mods]
    for a in (lng, lnb, wsp, bsp, wpa, wpb, wout, ln1g, ln1b):
        in_specs.append(_const_spec(a.shape))
        args.append(a)
    return pl.pallas_call(
        functools.partial(_mix_kernel, has_pos=has_pos, n_chunks=tm // SGU_CHUNK),
        grid=(s_n, l_n // tm),
        in_specs=in_specs,
        out_specs=pl.BlockSpec((1, tm, D_MODEL), tok),
        out_shape=jax.ShapeDtypeStruct((s_n, l_n, D_MODEL), F32),
        compiler_params=_params(48, 2),
        name="mix",
    )(*args)


FF_TILE = 1024


def _ffn_kernel(x_ref, mod_ref, w1_ref, w2_ref, g_ref, b_ref, o_ref):
    x = x_ref[0]
    shift2 = mod_ref[0, :, 3 * D_MODEL:4 * D_MODEL]
    scale2 = mod_ref[0, :, 4 * D_MODEL:5 * D_MODEL]
    gate2 = mod_ref[0, :, 5 * D_MODEL:6 * D_MODEL]
    h = (x * (1.0 + scale2) + shift2).astype(BF16)
    f = jnp.zeros_like(x)
    for k0 in range(0, D_FF, FF_TILE):
        a = jnp.dot(h, w1_ref[:, k0:k0 + FF_TILE], preferred_element_type=F32)
        a = jnp.maximum(a, 0.0)
        f = f + _dot(a * a, w2_ref[k0:k0 + FF_TILE, :])
    o_ref[0] = _layer_norm(DN_ALPHA * x + gate2 * f, g_ref[...], b_ref[...])


def _ffn(x, mods, mod_row0, tm, w1, w2, g, b):
    s_n, l_n, _ = x.shape
    tok = lambda s, j: (s, j, 0)
    return pl.pallas_call(
        _ffn_kernel,
        grid=(s_n, l_n // tm),
        in_specs=[pl.BlockSpec((1, tm, D_MODEL), tok),
                  pl.BlockSpec((1, 1, 6 * D_MODEL), lambda s, j: (mod_row0 + s, 0, 0)),
                  _const_spec(w1.shape), _const_spec(w2.shape),
                  _const_spec(g.shape), _const_spec(b.shape)],
        out_specs=pl.BlockSpec((1, tm, D_MODEL), tok),
        out_shape=jax.ShapeDtypeStruct((s_n, l_n, D_MODEL), F32),
        compiler_params=_params(48, 2),
        name="ffn",
    )(x, mods, w1, w2, g, b)


def _grid_pos_embed(n_tokens):
    rows = n_tokens // GRID_W
    quarter = D_MODEL // 4
    omega = 1.0 / (POS_BASE ** (jnp.arange(quarter, dtype=F32) / quarter))
    r = jnp.arange(rows, dtype=F32)[:, None] * omega
    col = jnp.arange(GRID_W, dtype=F32)[:, None] * omega
    emb_r = jnp.concatenate([jnp.sin(r), jnp.cos(r)], axis=-1)
    emb_c = jnp.concatenate([jnp.sin(col), jnp.cos(col)], axis=-1)
    emb = jnp.concatenate([
        jnp.broadcast_to(emb_r[:, None, :], (rows, GRID_W, D_MODEL // 2)),
        jnp.broadcast_to(emb_c[None, :, :], (rows, GRID_W, D_MODEL // 2))], axis=-1)
    return emb.reshape(rows * GRID_W, D_MODEL)


def _pad_lanes(v):
    return jnp.pad(v, (0, LANES - v.shape[0])).reshape(1, LANES)


def _stream(x, pos, mods, mod_row0, seq_len, h0f, h0b, prm, tm):
    s_n, l_n, _ = x.shape
    z, xbc, dtp, uv, g = _inproj(x, pos, mods, prm["w_in"], mod_row0, tm)
    n_seq = s_n * l_n // seq_len
    as_seq = lambda t: t.reshape(n_seq, seq_len, t.shape[-1])
    ya, hf, hb = _ssd(as_seq(xbc), as_seq(dtp), as_seq(z), h0f, h0b, prm["conv_w"],
                      prm["conv_b"], prm["dt_bias"], prm["a_neg"], prm["d_skip"], prm["norm_w"])
    x1 = _mix(x, pos, uv, g, ya.reshape(s_n, l_n, D_SSD), mods, mod_row0, tm,
              prm["sgu_ln_g"], prm["sgu_ln_b"], prm["w_spatial"], prm["b_spatial"],
              prm["w_proj_a"], prm["w_proj_b"], prm["w_out"], prm["ln1_g"], prm["ln1_b"])
    y = _ffn(x1, mods, mod_row0, tm, prm["w_ff1"], prm["w_ff2"], prm["ln2_g"], prm["ln2_b"])
    return y, hf, hb


def kernel(x_prompt, x_sample, state_ssd_fwd, state_ssd_bwd, c, c_ctx, w_ada, b_ada, w_in, conv_w, conv_b, dt_bias_fwd, dt_bias_bwd, a_log_fwd, a_log_bwd, d_skip, ssd_norm_w, sgu_ln_g, sgu_ln_b, w_spatial, b_spatial, w_proj_a, w_proj_b, w_out, ln1_g, ln1_b, w_ff1, w_ff2, ln2_g, ln2_b):
    batch, seq, _ = x_prompt.shape
    dec_batch, dec_seq, _ = x_sample.shape
    l = 0
    row = lambda v: v.reshape(1, -1)

    cond = jnp.zeros((SUBLANES, D_MODEL), F32).at[0].set(c_ctx).at[1:1 + dec_batch].set(c)
    mods = _ada(cond, w_ada[l], b_ada[l]).reshape(SUBLANES, 1, 6 * D_MODEL)

    o1 = D_SSD
    o2 = o1 + SSD_CONV_DIM
    o3 = o2 + 2 * SSD_HEADS
    w = w_in[l]
    w_in_r = jnp.concatenate(
        [w[:, :o2], jnp.pad(w[:, o2:o3], ((0, 0), (0, DT_PAD - 2 * SSD_HEADS))), w[:, o3:]],
        axis=1).astype(BF16)

    prm = dict(
        w_in=w_in_r,
        conv_w=conv_w[l], conv_b=row(conv_b[l]),
        dt_bias=_pad_lanes(jnp.concatenate([dt_bias_fwd[l], dt_bias_bwd[l]])),
        a_neg=_pad_lanes(-jnp.exp(jnp.concatenate([a_log_fwd[l], a_log_bwd[l]]))),
        d_skip=row(jnp.repeat(d_skip[l], SSD_HEAD_DIM)),
        norm_w=row(ssd_norm_w[l]),
        sgu_ln_g=row(sgu_ln_g[l]), sgu_ln_b=row(sgu_ln_b[l]),
        w_spatial=w_spatial[l].astype(BF16),
        b_spatial=jnp.repeat(b_spatial[l].T, SGU_GROUP_DIM, axis=1),
        w_proj_a=w_proj_a[l].astype(BF16), w_proj_b=w_proj_b[l].astype(BF16),
        w_out=w_out[l].astype(BF16),
        ln1_g=row(ln1_g[l]), ln1_b=row(ln1_b[l]),
        w_ff1=w_ff1[l].astype(BF16), w_ff2=w_ff2[l].astype(BF16),
        ln2_g=row(ln2_g[l]), ln2_b=row(ln2_b[l]),
    )

    tm = 256
    xp, hf, hb = _stream(x_prompt.reshape(1, batch * seq, D_MODEL), None, mods, 0, seq,
                         None, None, prm, tm)
    pos = _grid_pos_embed(dec_seq)
    h0f = state_ssd_fwd[:, l].reshape(dec_batch, D_SSD, SSD_STATE)
    h0b = state_ssd_bwd[:, l].reshape(dec_batch, D_SSD, SSD_STATE)
    xs, _, _ = _stream(x_sample, pos, mods, 1, dec_seq, h0f, h0b, prm, tm)

    state_shape = (batch, DEPTH, SSD_HEADS, SSD_HEAD_DIM, SSD_STATE)
    return (xp.reshape(batch, seq, D_MODEL), xs,
            hf.reshape(state_shape), hb.reshape(state_shape))
```

```python
import functools
import math

import jax
import jax.numpy as jnp
from jax import lax
from jax.experimental import pallas as pl
from jax.experimental.pallas import tpu as pltpu

D_MODEL = 1024
GRID_W = 64
POS_BASE = 10000.0
D_SSD = 1024
SSD_HEAD_DIM = 64
SSD_HEADS = D_SSD // SSD_HEAD_DIM
SSD_GROUPS = 2
SSD_STATE = 128
SSD_CHUNK = 128
SSD_CONV_DIM = D_SSD + 2 * SSD_GROUPS * SSD_STATE
D_SGU = 1024
SGU_GROUPS = 8
SGU_GROUP_DIM = D_SGU // SGU_GROUPS
SGU_CHUNK = 128
D_FF = 4 * D_MODEL
DEPTH = 1
DN_ALPHA = (2.0 * DEPTH) ** 0.25
EPS = 1e-5

LANES = 128
SUBLANES = 8
BF16_ROWS = 16

DT_GROUP = 2 * SSD_HEADS
DT_COPIES = 3
DT_PAD = LANES
HEAD_PAIRS = SSD_HEADS // 2
PAIRS_PER_GROUP = HEAD_PAIRS // SSD_GROUPS
GROUP_COLS = D_SSD // SSD_GROUPS

BF16 = jnp.bfloat16
F32 = jnp.float32
HIGHEST = lax.Precision.HIGHEST


def _dot(a, b):
    return jnp.dot(a.astype(BF16), b.astype(BF16), preferred_element_type=F32)


def _dot_exact(a, b):
    return jnp.dot(a, b, precision=HIGHEST, preferred_element_type=F32)


def _layer_norm(x, g, b):
    mu = jnp.mean(x, axis=-1, keepdims=True)
    xc = x - mu
    var = jnp.mean(xc * xc, axis=-1, keepdims=True)
    return xc * lax.rsqrt(var + EPS) * g + b


def _softplus(x):
    return jnp.maximum(x, 0.0) + jnp.log1p(jnp.exp(-jnp.abs(x)))


def _const_spec(shape):
    zeros = (0,) * len(shape)
    return pl.BlockSpec(shape, lambda *_: zeros, pipeline_mode=pl.Buffered(1))


def _params(vmem_mb, n_axes):
    return pltpu.CompilerParams(
        dimension_semantics=("arbitrary",) * n_axes,
        vmem_limit_bytes=vmem_mb * 1024 * 1024)


def _ada_kernel(c_ref, w_ref, b_ref, o_ref):
    c = c_ref[...]
    o_ref[...] = _dot_exact(c * jax.nn.sigmoid(c), w_ref[...]) + b_ref[...]


def _ada(cond, w_ada, b_ada):
    n = w_ada.shape[1]
    tn = 1024
    return pl.pallas_call(
        _ada_kernel,
        grid=(n // tn,),
        in_specs=[pl.BlockSpec((SUBLANES, D_MODEL), lambda j: (0, 0)),
                  pl.BlockSpec((D_MODEL, tn), lambda j: (0, j)),
                  pl.BlockSpec((1, tn), lambda j: (0, j))],
        out_specs=pl.BlockSpec((SUBLANES, tn), lambda j: (0, j)),
        out_shape=jax.ShapeDtypeStruct((SUBLANES, n), F32),
        compiler_params=_params(32, 1),
        name="ada",
    )(cond, w_ada, b_ada.reshape(1, n))


IN_SEGS = (D_SSD, SSD_CONV_DIM, DT_PAD, 2 * D_SGU, 2 * D_MODEL)
IN_COLS = sum(IN_SEGS)
IN_NTILE = 512
XBC_SEG = 1
HALO = BF16_ROWS


def _inproj_kernel(*refs, has_pos, tiles_per_seq):
    if has_pos:
        x_ref, xp_ref, xn_ref, pos_ref, pp_ref, pn_ref = refs[:6]
        refs = refs[6:]
    else:
        x_ref, xp_ref, xn_ref = refs[:3]
        pos_ref = pp_ref = pn_ref = None
        refs = refs[3:]
    mod_ref, w_ref, convw_ref, convb_ref = refs[:4]
    outs = refs[4:]
    shift1 = mod_ref[0, :, 0:D_MODEL]
    scale1 = mod_ref[0, :, D_MODEL:2 * D_MODEL]

    def modulate(x, p_ref):
        if p_ref is not None:
            x = x + p_ref[...]
        return (x * (1.0 + scale1) + shift1).astype(BF16)

    h = modulate(x_ref[0], pos_ref)
    tm = h.shape[0]
    n_ext = tm + 2 * HALO
    h_ext = jnp.concatenate([modulate(xp_ref[0], pp_ref), h, modulate(xn_ref[0], pn_ref)], axis=0)

    row = lax.broadcasted_iota(jnp.int32, (tm, 1), 0)
    zero_prev = row == 0
    zero_next = row == tm - 1
    if tiles_per_seq > 1:
        t = pl.program_id(1) % tiles_per_seq
        zero_prev = jnp.logical_and(zero_prev, t == 0)
        zero_next = jnp.logical_and(zero_next, t == tiles_per_seq - 1)

    off = 0
    for seg, (o_ref, width) in enumerate(zip(outs, IN_SEGS)):
        for c0 in range(0, width, IN_NTILE):
            c1 = min(c0 + IN_NTILE, width)
            w = w_ref[:, off + c0:off + c1]
            if seg == XBC_SEG:
                p = jnp.dot(h_ext, w, preferred_element_type=F32)
                cur = p[HALO:HALO + tm]
                prv = jnp.where(zero_prev, 0.0, pltpu.roll(p, 1, 0)[HALO:HALO + tm])
                nxt = jnp.where(zero_next, 0.0, pltpu.roll(p, n_ext - 1, 0)[HALO:HALO + tm])
                xc = (convw_ref[0:1, c0:c1] * prv + convw_ref[1:2, c0:c1] * cur
                      + convw_ref[2:3, c0:c1] * nxt + convb_ref[:, c0:c1])
                o_ref[0, :, c0:c1] = xc * jax.nn.sigmoid(xc)
            else:
                o_ref[0, :, c0:c1] = jnp.dot(h, w, preferred_element_type=F32)
        off += width


def _inproj(x, pos, mods, w_in_r, convw, convb, mod_row0, tm, seq_len):
    s_n, l_n, _ = x.shape
    has_pos = pos is not None
    hb = tm // HALO
    n_hb = l_n // HALO
    tok = lambda s, j: (s, j, 0)
    prev_blk = lambda j: jnp.maximum(j * hb - 1, 0)
    next_blk = lambda j: jnp.minimum((j + 1) * hb, n_hb - 1)
    in_specs = [pl.BlockSpec((1, tm, D_MODEL), tok),
                pl.BlockSpec((1, HALO, D_MODEL), lambda s, j: (s, prev_blk(j), 0)),
                pl.BlockSpec((1, HALO, D_MODEL), lambda s, j: (s, next_blk(j), 0))]
    args = [x, x, x]
    if has_pos:
        in_specs += [pl.BlockSpec((tm, D_MODEL), lambda s, j: (j, 0)),
                     pl.BlockSpec((HALO, D_MODEL), lambda s, j: (prev_blk(j), 0)),
                     pl.BlockSpec((HALO, D_MODEL), lambda s, j: (next_blk(j), 0))]
        args += [pos, pos, pos]
    in_specs += [pl.BlockSpec((1, 1, 6 * D_MODEL), lambda s, j: (mod_row0 + s, 0, 0)),
                 _const_spec((D_MODEL, IN_COLS)), _const_spec(convw.shape),
                 _const_spec(convb.shape)]
    args += [mods, w_in_r, convw, convb]
    return pl.pallas_call(
        functools.partial(_inproj_kernel, has_pos=has_pos, tiles_per_seq=seq_len // tm),
        grid=(s_n, l_n // tm),
        in_specs=in_specs,
        out_specs=[pl.BlockSpec((1, tm, w), tok) for w in IN_SEGS],
        out_shape=[jax.ShapeDtypeStruct((s_n, l_n, w), F32) for w in IN_SEGS],
        compiler_params=_params(56, 2),
        name="inproj",
    )(*args)


def _block_diag_pair(xp):
    lane = lax.broadcasted_iota(jnp.int32, xp.shape, 1)
    first = lane < SSD_HEAD_DIM
    return jnp.concatenate([jnp.where(first, xp, 0.0), jnp.where(first, 0.0, xp)], axis=0)


def _split_terms(v):
    lane = lax.broadcasted_iota(jnp.int32, v.shape, 1)
    hi = v.astype(BF16).astype(F32)
    r1 = v - hi
    mid = r1.astype(BF16).astype(F32)
    lo = r1 - mid
    return jnp.where(lane < DT_GROUP, hi, jnp.where(lane < 2 * DT_GROUP, mid, lo)).astype(BF16)


def _expand(v, e_ref):
    return jnp.dot(_split_terms(v), e_ref[...], preferred_element_type=F32)


def _ssd_kernel(*refs, nc, has_h0):
    x_ref, b_ref, c_ref, dtp_ref, z_ref = refs[:5]
    k = 5
    if has_h0:
        h0f_ref, h0b_ref = refs[k:k + 2]
        k += 2
    dtbias_ref, aneg_ref, dskip_ref, normw_ref, ef_ref, eb_ref = refs[k:k + 6]
    k += 6
    ya_ref, hff_ref, hfb_ref = refs[k:k + 3]
    hf_s, hb_s, hbstore_s = refs[k + 3:]

    q = SSD_CHUNK
    sweep = pl.program_id(1)
    c = pl.program_id(2)
    cidx = jnp.where(sweep == 0, nc - 1 - c, c)

    xh = x_ref[0]
    b_all = b_ref[0]
    b_t = [b_all[:, g * SSD_STATE:(g + 1) * SSD_STATE].T.astype(BF16)
           for g in range(SSD_GROUPS)]

    dt_all = _softplus(dtp_ref[0] + dtbias_ref[...])
    a_all = dt_all * aneg_ref[...]
    ii = lax.broadcasted_iota(jnp.int32, (q, q), 0)
    jj = lax.broadcasted_iota(jnp.int32, (q, q), 1)
    lower = ii >= jj
    acum = _dot_exact(lower.astype(F32), a_all)
    atot = acum[q - 1:q, :]
    rcum = atot - acum + a_all
    rtot = rcum[0:1, :]

    def carry_update(h_old, w_tok, tot_row, e_ref):
        wx = (_expand(w_tok, e_ref) * xh).astype(BF16)
        s = jnp.concatenate(
            [jnp.dot(b_t[g], wx[:, g * GROUP_COLS:(g + 1) * GROUP_COLS],
                     preferred_element_type=F32) for g in range(SSD_GROUPS)], axis=1)
        decay = jnp.exp(_expand(jnp.broadcast_to(tot_row, (BF16_ROWS, LANES)), e_ref))[0:1, :]
        return decay * h_old + s

    @pl.when(sweep == 0)
    def _backward_states():
        @pl.when(c == 0)
        def _():
            if has_h0:
                hb_s[...] = h0b_ref[0].T
            else:
                hb_s[...] = jnp.zeros_like(hb_s)

        h_in = hb_s[...]
        hbstore_s[cidx] = h_in
        h_new = carry_update(h_in, dt_all * jnp.exp(rtot - rcum), rtot, eb_ref)
        hb_s[...] = h_new

        @pl.when(c == nc - 1)
        def _():
            hfb_ref[0] = h_new.T

    @pl.when(sweep == 1)
    def _forward_and_output():
        @pl.when(c == 0)
        def _():
            if has_h0:
                hf_s[...] = h0f_ref[0].T
            else:
                hf_s[...] = jnp.zeros_like(hf_s)

        hf_in = hf_s[...]
        hb_in = hbstore_s[cidx]
        c_all = c_ref[0]
        dt_t = dt_all.T
        a_t = a_all.T
        acum_t = _dot_exact(a_t, (ii <= jj).astype(F32))
        rcum_t = acum_t[:, q - 1:q] - acum_t + a_t
        diag = ii == jj

        ef = jnp.exp(_expand(acum, ef_ref))
        eb = jnp.exp(_expand(rcum, eb_ref))
        y_off = []
        cb = []
        for g in range(SSD_GROUPS):
            gs = slice(g * GROUP_COLS, (g + 1) * GROUP_COLS)
            c_g = c_all[:, g * SSD_STATE:(g + 1) * SSD_STATE].astype(BF16)
            b_g = b_all[:, g * SSD_STATE:(g + 1) * SSD_STATE].astype(BF16)
            cb.append(lax.dot_general(c_g, b_g, (((1,), (1,)), ((), ())),
                                      preferred_element_type=F32))
            y_off.append(ef[:, gs] * jnp.dot(c_g, hf_in[:, gs].astype(BF16),
                                             preferred_element_type=F32)
                         + eb[:, gs] * jnp.dot(c_g, hb_in[:, gs].astype(BF16),
                                               preferred_element_type=F32))
        y_off = jnp.concatenate(y_off, axis=1)

        y_cols = []
        for pr in range(HEAD_PAIRS):
            g = pr // PAIRS_PER_GROUP
            m_parts = []
            for h in (2 * pr, 2 * pr + 1):
                hb_lane = SSD_HEADS + h
                colf = jnp.broadcast_to(acum[:, h:h + 1], (q, q))
                colb = jnp.broadcast_to(rcum[:, hb_lane:hb_lane + 1], (q, q))
                seg = jnp.where(lower, colf - acum_t[h:h + 1, :],
                                colb - rcum_t[hb_lane:hb_lane + 1, :])
                dt_b_row = dt_t[hb_lane:hb_lane + 1, :]
                w = (jnp.exp(seg) * jnp.where(lower, dt_t[h:h + 1, :], dt_b_row)
                     + jnp.where(diag, dt_b_row, 0.0))
                m_parts.append(cb[g] * w)
            sl = slice(pr * LANES, (pr + 1) * LANES)
            y_cols.append(_dot(jnp.concatenate(m_parts, axis=1), _block_diag_pair(xh[:, sl])))
        y = jnp.concatenate(y_cols, axis=1) + y_off + dskip_ref[...] * xh
        zz = z_ref[0]
        yg = y * (zz * jax.nn.sigmoid(zz))
        ms = jnp.mean(yg * yg, axis=-1, keepdims=True)
        ya_ref[0] = yg * lax.rsqrt(ms + EPS) * normw_ref[...]

        h_new = carry_update(hf_in, dt_all * jnp.exp(atot - acum), atot, ef_ref)
        hf_s[...] = h_new

        @pl.when(c == nc - 1)
        def _():
            hff_ref[0] = h_new.T


def _ssd(xa, dtp, z, h0f, h0b, dtbias, aneg, dskip, normw, e_f, e_b):
    s_n, l_n, _ = xa.shape
    q = SSD_CHUNK
    nc = l_n // q
    has_h0 = h0f is not None
    gn = SSD_GROUPS * SSD_STATE
    b_blk = D_SSD // gn

    def chunk_of(p, c):
        return jnp.where(p == 0, nc - 1 - c, c)

    def fwd_only(p, c):
        return jnp.where(p == 0, 0, c)

    in_specs = [
        pl.BlockSpec((1, q, D_SSD), lambda s, p, c: (s, chunk_of(p, c), 0)),
        pl.BlockSpec((1, q, gn), lambda s, p, c: (s, chunk_of(p, c), b_blk)),
        pl.BlockSpec((1, q, gn), lambda s, p, c: (s, fwd_only(p, c), b_blk + 1)),
        pl.BlockSpec((1, q, DT_PAD), lambda s, p, c: (s, chunk_of(p, c), 0)),
        pl.BlockSpec((1, q, D_SSD), lambda s, p, c: (s, fwd_only(p, c), 0)),
    ]
    args = [xa, xa, xa, dtp, z]
    state_spec = pl.BlockSpec((1, D_SSD, SSD_STATE), lambda s, p, c: (s, 0, 0))
    if has_h0:
        in_specs += [state_spec, state_spec]
        args += [h0f, h0b]
    for a in (dtbias, aneg, dskip, normw, e_f, e_b):
        in_specs.append(_const_spec(a.shape))
        args.append(a)
    return pl.pallas_call(
        functools.partial(_ssd_kernel, nc=nc, has_h0=has_h0),
        grid=(s_n, 2, nc),
        in_specs=in_specs,
        out_specs=[pl.BlockSpec((1, q, D_SSD), lambda s, p, c: (s, fwd_only(p, c), 0)),
                   state_spec, state_spec],
        out_shape=[jax.ShapeDtypeStruct((s_n, l_n, D_SSD), F32),
                   jax.ShapeDtypeStruct((s_n, D_SSD, SSD_STATE), F32),
                   jax.ShapeDtypeStruct((s_n, D_SSD, SSD_STATE), F32)],
        scratch_shapes=[pltpu.VMEM((SSD_STATE, D_SSD), F32),
                        pltpu.VMEM((SSD_STATE, D_SSD), F32),
                        pltpu.VMEM((nc, SSD_STATE, D_SSD), F32)],
        compiler_params=_params(48, 3),
        name="ssd",
    )(*args)


def _mix_kernel(*refs, has_pos, n_chunks):
    if has_pos:
        x_ref, pos_ref = refs[:2]
        refs = refs[2:]
    else:
        x_ref = refs[0]
        refs = refs[1:]
    (uv_ref, g_ref, ya_ref, mod_ref, lng_ref, lnb_ref, wsp_ref, bsp_ref, wpa_ref, wpb_ref,
     wout_ref, ln1g_ref, ln1b_ref, o_ref) = refs
    x = x_ref[0]
    if has_pos:
        x = x + pos_ref[...]
    uv = uv_ref[0]
    uv = 0.5 * uv * (1.0 + lax.erf(uv * (1.0 / math.sqrt(2.0))))
    u = uv[:, :D_SGU]
    v = _layer_norm(uv[:, D_SGU:], lng_ref[...], lnb_ref[...])
    rows = []
    for ch in range(n_chunks):
        vc = v[ch * SGU_CHUNK:(ch + 1) * SGU_CHUNK, :]
        cols = []
        for g in range(SGU_GROUPS):
            sl = slice(g * SGU_GROUP_DIM, (g + 1) * SGU_GROUP_DIM)
            cols.append(_dot(wsp_ref[g], vc[:, sl]))
        rows.append(jnp.concatenate(cols, axis=1) + bsp_ref[...])
    s = jnp.concatenate(rows, axis=0) if n_chunks > 1 else rows[0]
    yb = u * s
    gl = g_ref[0]
    merged = (jax.nn.sigmoid(gl[:, :D_MODEL]) * _dot(ya_ref[0], wpa_ref[...])
              + jax.nn.sigmoid(gl[:, D_MODEL:]) * _dot(yb, wpb_ref[...]))
    gate1 = mod_ref[0, :, 2 * D_MODEL:3 * D_MODEL]
    o = _dot(merged, wout_ref[...])
    o_ref[0] = _layer_norm(DN_ALPHA * x + gate1 * o, ln1g_ref[...], ln1b_ref[...])


def _mix(x, pos, uv, g, ya, mods, mod_row0, tm, lng, lnb, wsp, bsp, wpa, wpb, wout, ln1g, ln1b):
    s_n, l_n, _ = x.shape
    has_pos = pos is not None
    tok = lambda s, j: (s, j, 0)
    in_specs = [pl.BlockSpec((1, tm, D_MODEL), tok)]
    args = [x]
    if has_pos:
        in_specs.append(pl.BlockSpec((tm, D_MODEL), lambda s, j: (j, 0)))
        args.append(pos)
    in_specs += [pl.BlockSpec((1, tm, 2 * D_SGU), tok),
                 pl.BlockSpec((1, tm, 2 * D_MODEL), tok),
                 pl.BlockSpec((1, tm, D_SSD), tok),
                 pl.BlockSpec((1, 1, 6 * D_MODEL), lambda s, j: (mod_row0 + s, 0, 0))]
    args += [uv, g, ya, mods]
    for a in (lng, lnb, wsp, bsp, wpa, wpb, wout, ln1g, ln1b):
        in_specs.append(_const_spec(a.shape))
        args.append(a)
    return pl.pallas_call(
        functools.partial(_mix_kernel, has_pos=has_pos, n_chunks=tm // SGU_CHUNK),
        grid=(s_n, l_n // tm),
        in_specs=in_specs,
        out_specs=pl.BlockSpec((1, tm, D_MODEL), tok),
        out_shape=jax.ShapeDtypeStruct((s_n, l_n, D_MODEL), F32),
        compiler_params=_params(48, 2),
        name="mix",
    )(*args)


FF_TILE = 1024


def _ffn_kernel(x_ref, mod_ref, w1_ref, w2_ref, g_ref, b_ref, o_ref):
    x = x_ref[0]
    shift2 = mod_ref[0, :, 3 * D_MODEL:4 * D_MODEL]
    scale2 = mod_ref[0, :, 4 * D_MODEL:5 * D_MODEL]
    gate2 = mod_ref[0, :, 5 * D_MODEL:6 * D_MODEL]
    h = (x * (1.0 + scale2) + shift2).astype(BF16)
    f = jnp.zeros_like(x)
    for k0 in range(0, D_FF, FF_TILE):
        a = jnp.dot(h, w1_ref[:, k0:k0 + FF_TILE], preferred_element_type=F32)
        a = jnp.maximum(a, 0.0)
        f = f + _dot(a * a, w2_ref[k0:k0 + FF_TILE, :])
    o_ref[0] = _layer_norm(DN_ALPHA * x + gate2 * f, g_ref[...], b_ref[...])


def _ffn(x, mods, mod_row0, tm, w1, w2, g, b):
    s_n, l_n, _ = x.shape
    tok = lambda s, j: (s, j, 0)
    return pl.pallas_call(
        _ffn_kernel,
        grid=(s_n, l_n // tm),
        in_specs=[pl.BlockSpec((1, tm, D_MODEL), tok),
                  pl.BlockSpec((1, 1, 6 * D_MODEL), lambda s, j: (mod_row0 + s, 0, 0)),
                  _const_spec(w1.shape), _const_spec(w2.shape),
                  _const_spec(g.shape), _const_spec(b.shape)],
        out_specs=pl.BlockSpec((1, tm, D_MODEL), tok),
        out_shape=jax.ShapeDtypeStruct((s_n, l_n, D_MODEL), F32),
        compiler_params=_params(48, 2),
        name="ffn",
    )(x, mods, w1, w2, g, b)


def _grid_pos_embed(n_tokens):
    rows = n_tokens // GRID_W
    quarter = D_MODEL // 4
    omega = 1.0 / (POS_BASE ** (jnp.arange(quarter, dtype=F32) / quarter))
    r = jnp.arange(rows, dtype=F32)[:, None] * omega
    col = jnp.arange(GRID_W, dtype=F32)[:, None] * omega
    emb_r = jnp.concatenate([jnp.sin(r), jnp.cos(r)], axis=-1)
    emb_c = jnp.concatenate([jnp.sin(col), jnp.cos(col)], axis=-1)
    emb = jnp.concatenate([
        jnp.broadcast_to(emb_r[:, None, :], (rows, GRID_W, D_MODEL // 2)),
        jnp.broadcast_to(emb_c[None, :, :], (rows, GRID_W, D_MODEL // 2))], axis=-1)
    return emb.reshape(rows * GRID_W, D_MODEL)


def _dt_lanes(v, axis):
    pad = [(0, 0)] * v.ndim
    pad[axis] = (0, DT_PAD - DT_COPIES * DT_GROUP)
    return jnp.pad(jnp.concatenate([v] * DT_COPIES, axis=axis), pad)


def _head_expanders():
    lane = jnp.arange(LANES)[:, None]
    head = jnp.arange(D_SSD)[None, :] // SSD_HEAD_DIM
    valid = lane < DT_COPIES * DT_GROUP
    e_f = jnp.logical_and(valid, lane % DT_GROUP == head)
    e_b = jnp.logical_and(valid, lane % DT_GROUP == head + SSD_HEADS)
    return e_f.astype(BF16), e_b.astype(BF16)


def _stream(x, pos, mods, mod_row0, seq_len, h0f, h0b, prm, tm):
    s_n, l_n, _ = x.shape
    z, xa, dtp, uv, g = _inproj(x, pos, mods, prm["w_in"], prm["conv_w"], prm["conv_b"],
                                mod_row0, tm, seq_len)
    n_seq = s_n * l_n // seq_len
    as_seq = lambda t: t.reshape(n_seq, seq_len, t.shape[-1])
    ya, hf, hb = _ssd(as_seq(xa), as_seq(dtp), as_seq(z), h0f, h0b, prm["dt_bias"],
                      prm["a_neg"], prm["d_skip"], prm["norm_w"], prm["e_f"], prm["e_b"])
    x1 = _mix(x, pos, uv, g, ya.reshape(s_n, l_n, D_SSD), mods, mod_row0, tm,
              prm["sgu_ln_g"], prm["sgu_ln_b"], prm["w_spatial"], prm["b_spatial"],
              prm["w_proj_a"], prm["w_proj_b"], prm["w_out"], prm["ln1_g"], prm["ln1_b"])
    y = _ffn(x1, mods, mod_row0, tm, prm["w_ff1"], prm["w_ff2"], prm["ln2_g"], prm["ln2_b"])
    return y, hf, hb


def kernel(x_prompt, x_sample, state_ssd_fwd, state_ssd_bwd, c, c_ctx, w_ada, b_ada, w_in, conv_w, conv_b, dt_bias_fwd, dt_bias_bwd, a_log_fwd, a_log_bwd, d_skip, ssd_norm_w, sgu_ln_g, sgu_ln_b, w_spatial, b_spatial, w_proj_a, w_proj_b, w_out, ln1_g, ln1_b, w_ff1, w_ff2, ln2_g, ln2_b):
    batch, seq, _ = x_prompt.shape
    dec_batch, dec_seq, _ = x_sample.shape
    l = 0
    row = lambda v: v.reshape(1, -1)

    cond = jnp.zeros((SUBLANES, D_MODEL), F32).at[0].set(c_ctx).at[1:1 + dec_batch].set(c)
    mods = _ada(cond, w_ada[l], b_ada[l]).reshape(SUBLANES, 1, 6 * D_MODEL)

    o1 = D_SSD
    o2 = o1 + SSD_CONV_DIM
    o3 = o2 + DT_GROUP
    w = w_in[l]
    w_in_r = jnp.concatenate([w[:, :o2], _dt_lanes(w[:, o2:o3], 1), w[:, o3:]],
                             axis=1).astype(BF16)
    e_f, e_b = _head_expanders()

    prm = dict(
        w_in=w_in_r,
        conv_w=conv_w[l], conv_b=row(conv_b[l]),
        dt_bias=row(_dt_lanes(jnp.concatenate([dt_bias_fwd[l], dt_bias_bwd[l]]), 0)),
        a_neg=row(_dt_lanes(-jnp.exp(jnp.concatenate([a_log_fwd[l], a_log_bwd[l]])), 0)),
        d_skip=row(jnp.repeat(d_skip[l], SSD_HEAD_DIM)),
        norm_w=row(ssd_norm_w[l]),
        e_f=e_f, e_b=e_b,
        sgu_ln_g=row(sgu_ln_g[l]), sgu_ln_b=row(sgu_ln_b[l]),
        w_spatial=w_spatial[l].astype(BF16),
        b_spatial=jnp.repeat(b_spatial[l].T, SGU_GROUP_DIM, axis=1),
        w_proj_a=w_proj_a[l].astype(BF16), w_proj_b=w_proj_b[l].astype(BF16),
        w_out=w_out[l].astype(BF16),
        ln1_g=row(ln1_g[l]), ln1_b=row(ln1_b[l]),
        w_ff1=w_ff1[l].astype(BF16), w_ff2=w_ff2[l].astype(BF16),
        ln2_g=row(ln2_g[l]), ln2_b=row(ln2_b[l]),
    )

    tm = 256
    xp, hf, hb = _stream(x_prompt.reshape(1, batch * seq, D_MODEL), None, mods, 0, seq,
                         None, None, prm, tm)
    pos = _grid_pos_embed(dec_seq)
    h0f = state_ssd_fwd[:, l].reshape(dec_batch, D_SSD, SSD_STATE)
    h0b = state_ssd_bwd[:, l].reshape(dec_batch, D_SSD, SSD_STATE)
    xs, _, _ = _stream(x_sample, pos, mods, 1, dec_seq, h0f, h0b, prm, tm)

    state_shape = (batch, DEPTH, SSD_HEADS, SSD_HEAD_DIM, SSD_STATE)
    return (xp.reshape(batch, seq, D_MODEL), xs,
            hf.reshape(state_shape), hb.reshape(state_shape))
```

```python
import functools
import math

import jax
import jax.numpy as jnp
from jax import lax
from jax.experimental import pallas as pl
from jax.experimental.pallas import tpu as pltpu

D_MODEL = 1024
GRID_W = 64
POS_BASE = 10000.0
D_SSD = 1024
SSD_HEAD_DIM = 64
SSD_HEADS = D_SSD // SSD_HEAD_DIM
SSD_GROUPS = 2
SSD_STATE = 128
SSD_CHUNK = 128
SSD_CONV_DIM = D_SSD + 2 * SSD_GROUPS * SSD_STATE
D_SGU = 1024
SGU_GROUPS = 8
SGU_GROUP_DIM = D_SGU // SGU_GROUPS
SGU_CHUNK = 128
D_FF = 4 * D_MODEL
DEPTH = 1
DN_ALPHA = (2.0 * DEPTH) ** 0.25
EPS = 1e-5

LANES = 128
SUBLANES = 8
BF16_ROWS = 16

DT_GROUP = 2 * SSD_HEADS
DT_COPIES = 3
DT_PAD = LANES
HEAD_PAIRS = SSD_HEADS // 2
PAIRS_PER_GROUP = HEAD_PAIRS // SSD_GROUPS
GROUP_COLS = D_SSD // SSD_GROUPS

BF16 = jnp.bfloat16
F32 = jnp.float32
HIGHEST = lax.Precision.HIGHEST


def _dot(a, b):
    return jnp.dot(a.astype(BF16), b.astype(BF16), preferred_element_type=F32)


def _dot_exact(a, b):
    return jnp.dot(a, b, precision=HIGHEST, preferred_element_type=F32)


def _layer_norm(x, g, b):
    mu = jnp.mean(x, axis=-1, keepdims=True)
    xc = x - mu
    var = jnp.mean(xc * xc, axis=-1, keepdims=True)
    return xc * lax.rsqrt(var + EPS) * g + b


def _softplus(x):
    return jnp.maximum(x, 0.0) + jnp.log1p(jnp.exp(-jnp.abs(x)))


def _const_spec(shape):
    zeros = (0,) * len(shape)
    return pl.BlockSpec(shape, lambda *_: zeros, pipeline_mode=pl.Buffered(1))


def _params(vmem_mb, n_axes):
    return pltpu.CompilerParams(
        dimension_semantics=("arbitrary",) * n_axes,
        vmem_limit_bytes=vmem_mb * 1024 * 1024)


def _ada_kernel(c_ref, w_ref, b_ref, o_ref):
    c = c_ref[...]
    o_ref[...] = _dot_exact(c * jax.nn.sigmoid(c), w_ref[...]) + b_ref[...]


def _ada(cond, w_ada, b_ada):
    n = w_ada.shape[1]
    tn = 1024
    return pl.pallas_call(
        _ada_kernel,
        grid=(n // tn,),
        in_specs=[pl.BlockSpec((SUBLANES, D_MODEL), lambda j: (0, 0)),
                  pl.BlockSpec((D_MODEL, tn), lambda j: (0, j)),
                  pl.BlockSpec((1, tn), lambda j: (0, j))],
        out_specs=pl.BlockSpec((SUBLANES, tn), lambda j: (0, j)),
        out_shape=jax.ShapeDtypeStruct((SUBLANES, n), F32),
        compiler_params=_params(32, 1),
        name="ada",
    )(cond, w_ada, b_ada.reshape(1, n))


IN_SEGS = (D_SSD, SSD_CONV_DIM, DT_PAD)
IN_COLS = sum(IN_SEGS)
IN_NTILE = 512
XBC_SEG = 1
HALO = BF16_ROWS


def _pos_rows(embr_ref, embc_ref, grid_row0, n_rows):
    left = [jnp.broadcast_to(embr_ref[pl.ds(grid_row0 + i, 1), :], (GRID_W, D_MODEL // 2))
            for i in range(n_rows)]
    right = [embc_ref[...]] * n_rows
    return jnp.concatenate([jnp.concatenate(left, axis=0), jnp.concatenate(right, axis=0)],
                           axis=1)


def _pos_halo(embr_ref, embc_ref, grid_row, col0):
    left = jnp.broadcast_to(embr_ref[pl.ds(grid_row, 1), :], (HALO, D_MODEL // 2))
    return jnp.concatenate([left, embc_ref[col0:col0 + HALO, :]], axis=1)


def _inproj_kernel(*refs, has_pos, tiles_per_seq):
    x_ref, xp_ref, xn_ref = refs[:3]
    refs = refs[3:]
    if has_pos:
        embr_ref, embc_ref = refs[:2]
        refs = refs[2:]
    mod_ref, w_ref, convw_ref, convb_ref = refs[:4]
    outs = refs[4:]
    shift1 = mod_ref[0, :, 0:D_MODEL]
    scale1 = mod_ref[0, :, D_MODEL:2 * D_MODEL]
    tm = x_ref.shape[1]
    n_ext = tm + 2 * HALO

    def modulate(x, pos):
        if pos is not None:
            x = x + pos
        return (x * (1.0 + scale1) + shift1).astype(BF16)

    pos = pos_prev = pos_next = None
    if has_pos:
        rows_per_tile = tm // GRID_W
        n_grid_rows = embr_ref.shape[0]
        r0 = pl.program_id(1) * rows_per_tile
        pos = _pos_rows(embr_ref, embc_ref, r0, rows_per_tile)
        pos_prev = _pos_halo(embr_ref, embc_ref, jnp.maximum(r0 - 1, 0), GRID_W - HALO)
        pos_next = _pos_halo(embr_ref, embc_ref,
                             jnp.minimum(r0 + rows_per_tile, n_grid_rows - 1), 0)
    h = modulate(x_ref[0], pos)
    h_ext = jnp.concatenate([modulate(xp_ref[0], pos_prev), h, modulate(xn_ref[0], pos_next)],
                            axis=0)

    row = lax.broadcasted_iota(jnp.int32, (tm, 1), 0)
    zero_prev = row == 0
    zero_next = row == tm - 1
    if tiles_per_seq > 1:
        t = pl.program_id(1) % tiles_per_seq
        zero_prev = jnp.logical_and(zero_prev, t == 0)
        zero_next = jnp.logical_and(zero_next, t == tiles_per_seq - 1)

    off = 0
    for seg, (o_ref, width) in enumerate(zip(outs, IN_SEGS)):
        for c0 in range(0, width, IN_NTILE):
            c1 = min(c0 + IN_NTILE, width)
            w = w_ref[:, off + c0:off + c1]
            if seg == XBC_SEG:
                p = jnp.dot(h_ext, w, preferred_element_type=F32)
                cur = p[HALO:HALO + tm]
                prv = jnp.where(zero_prev, 0.0, pltpu.roll(p, 1, 0)[HALO:HALO + tm])
                nxt = jnp.where(zero_next, 0.0, pltpu.roll(p, n_ext - 1, 0)[HALO:HALO + tm])
                xc = (convw_ref[0:1, c0:c1] * prv + convw_ref[1:2, c0:c1] * cur
                      + convw_ref[2:3, c0:c1] * nxt + convb_ref[:, c0:c1])
                o_ref[0, :, c0:c1] = xc * jax.nn.sigmoid(xc)
            else:
                o_ref[0, :, c0:c1] = jnp.dot(h, w, preferred_element_type=F32)
        off += width


def _inproj(x, pos, mods, w_in_r, convw, convb, mod_row0, tm, seq_len):
    s_n, l_n, _ = x.shape
    has_pos = pos is not None
    hb = tm // HALO
    n_hb = l_n // HALO
    tok = lambda s, j: (s, j, 0)
    prev_blk = lambda j: jnp.maximum(j * hb - 1, 0)
    next_blk = lambda j: jnp.minimum((j + 1) * hb, n_hb - 1)
    in_specs = [pl.BlockSpec((1, tm, D_MODEL), tok),
                pl.BlockSpec((1, HALO, D_MODEL), lambda s, j: (s, prev_blk(j), 0)),
                pl.BlockSpec((1, HALO, D_MODEL), lambda s, j: (s, next_blk(j), 0))]
    args = [x, x, x]
    if has_pos:
        in_specs += [_const_spec(pos[0].shape), _const_spec(pos[1].shape)]
        args += list(pos)
    in_specs += [pl.BlockSpec((1, 1, 6 * D_MODEL), lambda s, j: (mod_row0 + s, 0, 0)),
                 _const_spec((D_MODEL, IN_COLS)), _const_spec(convw.shape),
                 _const_spec(convb.shape)]
    args += [mods, w_in_r, convw, convb]
    return pl.pallas_call(
        functools.partial(_inproj_kernel, has_pos=has_pos, tiles_per_seq=seq_len // tm),
        grid=(s_n, l_n // tm),
        in_specs=in_specs,
        out_specs=[pl.BlockSpec((1, tm, w), tok) for w in IN_SEGS],
        out_shape=[jax.ShapeDtypeStruct((s_n, l_n, w), F32) for w in IN_SEGS],
        compiler_params=_params(56, 2),
        name="inproj",
    )(*args)


def _block_diag_pair(xp):
    lane = lax.broadcasted_iota(jnp.int32, xp.shape, 1)
    first = lane < SSD_HEAD_DIM
    return jnp.concatenate([jnp.where(first, xp, 0.0), jnp.where(first, 0.0, xp)], axis=0)


def _split_terms(v):
    lane = lax.broadcasted_iota(jnp.int32, v.shape, 1)
    hi = v.astype(BF16).astype(F32)
    r1 = v - hi
    mid = r1.astype(BF16).astype(F32)
    lo = r1 - mid
    return jnp.where(lane < DT_GROUP, hi, jnp.where(lane < 2 * DT_GROUP, mid, lo)).astype(BF16)


def _expand(v, e_ref):
    return jnp.dot(_split_terms(v), e_ref[...], preferred_element_type=F32)


def _ssd_kernel(*refs, nc, has_h0):
    x_ref, b_ref, c_ref, dtp_ref, z_ref = refs[:5]
    k = 5
    if has_h0:
        h0f_ref, h0b_ref = refs[k:k + 2]
        k += 2
    dtbias_ref, aneg_ref, dskip_ref, normw_ref, ef_ref, eb_ref = refs[k:k + 6]
    k += 6
    ya_ref, hff_ref, hfb_ref = refs[k:k + 3]
    hf_s, hb_s, hbstore_s = refs[k + 3:]

    q = SSD_CHUNK
    sweep = pl.program_id(1)
    c = pl.program_id(2)
    cidx = jnp.where(sweep == 0, nc - 1 - c, c)

    xh = x_ref[0]
    b_all = b_ref[0]
    b_t = [b_all[:, g * SSD_STATE:(g + 1) * SSD_STATE].T.astype(BF16)
           for g in range(SSD_GROUPS)]

    dt_all = _softplus(dtp_ref[0] + dtbias_ref[...])
    a_all = dt_all * aneg_ref[...]
    ii = lax.broadcasted_iota(jnp.int32, (q, q), 0)
    jj = lax.broadcasted_iota(jnp.int32, (q, q), 1)
    lower = ii >= jj
    acum = _dot_exact(lower.astype(F32), a_all)
    atot = acum[q - 1:q, :]
    rcum = atot - acum + a_all
    rtot = rcum[0:1, :]

    def carry_update(h_old, w_tok, tot_row, e_ref):
        wx = (_expand(w_tok, e_ref) * xh).astype(BF16)
        s = jnp.concatenate(
            [jnp.dot(b_t[g], wx[:, g * GROUP_COLS:(g + 1) * GROUP_COLS],
                     preferred_element_type=F32) for g in range(SSD_GROUPS)], axis=1)
        decay = jnp.exp(_expand(jnp.broadcast_to(tot_row, (BF16_ROWS, LANES)), e_ref))[0:1, :]
        return decay * h_old + s

    @pl.when(sweep == 0)
    def _backward_states():
        @pl.when(c == 0)
        def _():
            if has_h0:
                hb_s[...] = h0b_ref[0].T
            else:
                hb_s[...] = jnp.zeros_like(hb_s)

        h_in = hb_s[...]
        hbstore_s[cidx] = h_in
        h_new = carry_update(h_in, dt_all * jnp.exp(rtot - rcum), rtot, eb_ref)
        hb_s[...] = h_new

        @pl.when(c == nc - 1)
        def _():
            hfb_ref[0] = h_new.T

    @pl.when(sweep == 1)
    def _forward_and_output():
        @pl.when(c == 0)
        def _():
            if has_h0:
                hf_s[...] = h0f_ref[0].T
            else:
                hf_s[...] = jnp.zeros_like(hf_s)

        hf_in = hf_s[...]
        hb_in = hbstore_s[cidx]
        c_all = c_ref[0]
        dt_t = dt_all.T
        a_t = a_all.T
        acum_t = _dot_exact(a_t, (ii <= jj).astype(F32))
        rcum_t = acum_t[:, q - 1:q] - acum_t + a_t
        diag = ii == jj

        ef = jnp.exp(_expand(acum, ef_ref))
        eb = jnp.exp(_expand(rcum, eb_ref))
        y_off = []
        cb = []
        for g in range(SSD_GROUPS):
            gs = slice(g * GROUP_COLS, (g + 1) * GROUP_COLS)
            c_g = c_all[:, g * SSD_STATE:(g + 1) * SSD_STATE].astype(BF16)
            b_g = b_all[:, g * SSD_STATE:(g + 1) * SSD_STATE].astype(BF16)
            cb.append(lax.dot_general(c_g, b_g, (((1,), (1,)), ((), ())),
                                      preferred_element_type=F32))
            y_off.append(ef[:, gs] * jnp.dot(c_g, hf_in[:, gs].astype(BF16),
                                             preferred_element_type=F32)
                         + eb[:, gs] * jnp.dot(c_g, hb_in[:, gs].astype(BF16),
                                               preferred_element_type=F32))
        y_off = jnp.concatenate(y_off, axis=1)

        y_cols = []
        for pr in range(HEAD_PAIRS):
            g = pr // PAIRS_PER_GROUP
            m_parts = []
            for h in (2 * pr, 2 * pr + 1):
                hb_lane = SSD_HEADS + h
                colf = jnp.broadcast_to(acum[:, h:h + 1], (q, q))
                colb = jnp.broadcast_to(rcum[:, hb_lane:hb_lane + 1], (q, q))
                seg = jnp.where(lower, colf - acum_t[h:h + 1, :],
                                colb - rcum_t[hb_lane:hb_lane + 1, :])
                dt_b_row = dt_t[hb_lane:hb_lane + 1, :]
                w = (jnp.exp(seg) * jnp.where(lower, dt_t[h:h + 1, :], dt_b_row)
                     + jnp.where(diag, dt_b_row, 0.0))
                m_parts.append(cb[g] * w)
            sl = slice(pr * LANES, (pr + 1) * LANES)
            y_cols.append(_dot(jnp.concatenate(m_parts, axis=1), _block_diag_pair(xh[:, sl])))
        y = jnp.concatenate(y_cols, axis=1) + y_off + dskip_ref[...] * xh
        zz = z_ref[0]
        yg = y * (zz * jax.nn.sigmoid(zz))
        ms = jnp.mean(yg * yg, axis=-1, keepdims=True)
        ya_ref[0] = yg * lax.rsqrt(ms + EPS) * normw_ref[...]

        h_new = carry_update(hf_in, dt_all * jnp.exp(atot - acum), atot, ef_ref)
        hf_s[...] = h_new

        @pl.when(c == nc - 1)
        def _():
            hff_ref[0] = h_new.T


def _ssd(xa, dtp, z, h0f, h0b, dtbias, aneg, dskip, normw, e_f, e_b):
    s_n, l_n, _ = xa.shape
    q = SSD_CHUNK
    nc = l_n // q
    has_h0 = h0f is not None
    gn = SSD_GROUPS * SSD_STATE
    b_blk = D_SSD // gn

    def chunk_of(p, c):
        return jnp.where(p == 0, nc - 1 - c, c)

    def fwd_only(p, c):
        return jnp.where(p == 0, 0, c)

    in_specs = [
        pl.BlockSpec((1, q, D_SSD), lambda s, p, c: (s, chunk_of(p, c), 0)),
        pl.BlockSpec((1, q, gn), lambda s, p, c: (s, chunk_of(p, c), b_blk)),
        pl.BlockSpec((1, q, gn), lambda s, p, c: (s, fwd_only(p, c), b_blk + 1)),
        pl.BlockSpec((1, q, DT_PAD), lambda s, p, c: (s, chunk_of(p, c), 0)),
        pl.BlockSpec((1, q, D_SSD), lambda s, p, c: (s, fwd_only(p, c), 0)),
    ]
    args = [xa, xa, xa, dtp, z]
    state_spec = pl.BlockSpec((1, D_SSD, SSD_STATE), lambda s, p, c: (s, 0, 0))
    if has_h0:
        in_specs += [state_spec, state_spec]
        args += [h0f, h0b]
    for a in (dtbias, aneg, dskip, normw, e_f, e_b):
        in_specs.append(_const_spec(a.shape))
        args.append(a)
    return pl.pallas_call(
        functools.partial(_ssd_kernel, nc=nc, has_h0=has_h0),
        grid=(s_n, 2, nc),
        in_specs=in_specs,
        out_specs=[pl.BlockSpec((1, q, D_SSD), lambda s, p, c: (s, fwd_only(p, c), 0)),
                   state_spec, state_spec],
        out_shape=[jax.ShapeDtypeStruct((s_n, l_n, D_SSD), F32),
                   jax.ShapeDtypeStruct((s_n, D_SSD, SSD_STATE), F32),
                   jax.ShapeDtypeStruct((s_n, D_SSD, SSD_STATE), F32)],
        scratch_shapes=[pltpu.VMEM((SSD_STATE, D_SSD), F32),
                        pltpu.VMEM((SSD_STATE, D_SSD), F32),
                        pltpu.VMEM((nc, SSD_STATE, D_SSD), F32)],
        compiler_params=_params(48, 3),
        name="ssd",
    )(*args)


def _gelu(x):
    return 0.5 * x * (1.0 + lax.erf(x * (1.0 / math.sqrt(2.0))))


def _mix_kernel(*refs, has_pos, n_chunks):
    x_ref = refs[0]
    refs = refs[1:]
    if has_pos:
        embr_ref, embc_ref = refs[:2]
        refs = refs[2:]
    (ya_ref, mod_ref, wuvg_ref, lng_ref, lnb_ref, wsp_ref, bsp_ref, wpa_ref, wpb_ref,
     wout_ref, ln1g_ref, ln1b_ref, o_ref) = refs
    x = x_ref[0]
    if has_pos:
        rows_per_tile = x.shape[0] // GRID_W
        x = x + _pos_rows(embr_ref, embc_ref, pl.program_id(1) * rows_per_tile, rows_per_tile)
    shift1 = mod_ref[0, :, 0:D_MODEL]
    scale1 = mod_ref[0, :, D_MODEL:2 * D_MODEL]
    h = (x * (1.0 + scale1) + shift1).astype(BF16)

    def in_proj(c0, width):
        return jnp.concatenate(
            [jnp.dot(h, wuvg_ref[:, c:c + IN_NTILE], preferred_element_type=F32)
             for c in range(c0, c0 + width, IN_NTILE)], axis=1)

    u = _gelu(in_proj(0, D_SGU))
    v = _layer_norm(_gelu(in_proj(D_SGU, D_SGU)), lng_ref[...], lnb_ref[...])
    rows = []
    for ch in range(n_chunks):
        vc = v[ch * SGU_CHUNK:(ch + 1) * SGU_CHUNK, :]
        cols = []
        for g in range(SGU_GROUPS):
            sl = slice(g * SGU_GROUP_DIM, (g + 1) * SGU_GROUP_DIM)
            cols.append(_dot(wsp_ref[g], vc[:, sl]))
        rows.append(jnp.concatenate(cols, axis=1) + bsp_ref[...])
    s = jnp.concatenate(rows, axis=0) if n_chunks > 1 else rows[0]
    yb = u * s
    merged = (jax.nn.sigmoid(in_proj(2 * D_SGU, D_MODEL)) * _dot(ya_ref[0], wpa_ref[...])
              + jax.nn.sigmoid(in_proj(2 * D_SGU + D_MODEL, D_MODEL)) * _dot(yb, wpb_ref[...]))
    gate1 = mod_ref[0, :, 2 * D_MODEL:3 * D_MODEL]
    o = _dot(merged, wout_ref[...])
    o_ref[0] = _layer_norm(DN_ALPHA * x + gate1 * o, ln1g_ref[...], ln1b_ref[...])


def _mix(x, pos, ya, mods, mod_row0, tm, wuvg, lng, lnb, wsp, bsp, wpa, wpb, wout, ln1g, ln1b):
    s_n, l_n, _ = x.shape
    has_pos = pos is not None
    tok = lambda s, j: (s, j, 0)
    in_specs = [pl.BlockSpec((1, tm, D_MODEL), tok)]
    args = [x]
    if has_pos:
        in_specs += [_const_spec(pos[0].shape), _const_spec(pos[1].shape)]
        args += list(pos)
    in_specs += [pl.BlockSpec((1, tm, D_SSD), tok),
                 pl.BlockSpec((1, 1, 6 * D_MODEL), lambda s, j: (mod_row0 + s, 0, 0))]
    args += [ya, mods]
    for a in (wuvg, lng, lnb, wsp, bsp, wpa, wpb, wout, ln1g, ln1b):
        in_specs.append(_const_spec(a.shape))
        args.append(a)
    return pl.pallas_call(
        functools.partial(_mix_kernel, has_pos=has_pos, n_chunks=tm // SGU_CHUNK),
        grid=(s_n, l_n // tm),
        in_specs=in_specs,
        out_specs=pl.BlockSpec((1, tm, D_MODEL), tok),
        out_shape=jax.ShapeDtypeStruct((s_n, l_n, D_MODEL), F32),
        compiler_params=_params(48, 2),
        name="mix",
    )(*args)


FF_TILE = 1024


def _ffn_kernel(x_ref, mod_ref, w1_ref, w2_ref, g_ref, b_ref, o_ref):
    x = x_ref[0]
    shift2 = mod_ref[0, :, 3 * D_MODEL:4 * D_MODEL]
    scale2 = mod_ref[0, :, 4 * D_MODEL:5 * D_MODEL]
    gate2 = mod_ref[0, :, 5 * D_MODEL:6 * D_MODEL]
    h = (x * (1.0 + scale2) + shift2).astype(BF16)
    f = jnp.zeros_like(x)
    for k0 in range(0, D_FF, FF_TILE):
        a = jnp.dot(h, w1_ref[:, k0:k0 + FF_TILE], preferred_element_type=F32)
        a = jnp.maximum(a, 0.0)
        f = f + _dot(a * a, w2_ref[k0:k0 + FF_TILE, :])
    o_ref[0] = _layer_norm(DN_ALPHA * x + gate2 * f, g_ref[...], b_ref[...])


def _ffn(x, mods, mod_row0, tm, w1, w2, g, b):
    s_n, l_n, _ = x.shape
    tok = lambda s, j: (s, j, 0)
    return pl.pallas_call(
        _ffn_kernel,
        grid=(s_n, l_n // tm),
        in_specs=[pl.BlockSpec((1, tm, D_MODEL), tok),
                  pl.BlockSpec((1, 1, 6 * D_MODEL), lambda s, j: (mod_row0 + s, 0, 0)),
                  _const_spec(w1.shape), _const_spec(w2.shape),
                  _const_spec(g.shape), _const_spec(b.shape)],
        out_specs=pl.BlockSpec((1, tm, D_MODEL), tok),
        out_shape=jax.ShapeDtypeStruct((s_n, l_n, D_MODEL), F32),
        compiler_params=_params(48, 2),
        name="ffn",
    )(x, mods, w1, w2, g, b)


def _grid_pos_tables(n_tokens):
    rows = n_tokens // GRID_W
    quarter = D_MODEL // 4
    omega = 1.0 / (POS_BASE ** (jnp.arange(quarter, dtype=F32) / quarter))
    r = jnp.arange(rows, dtype=F32)[:, None] * omega
    col = jnp.arange(GRID_W, dtype=F32)[:, None] * omega
    emb_r = jnp.concatenate([jnp.sin(r), jnp.cos(r)], axis=-1)
    emb_c = jnp.concatenate([jnp.sin(col), jnp.cos(col)], axis=-1)
    return emb_r, emb_c


def _dt_lanes(v, axis):
    pad = [(0, 0)] * v.ndim
    pad[axis] = (0, DT_PAD - DT_COPIES * DT_GROUP)
    return jnp.pad(jnp.concatenate([v] * DT_COPIES, axis=axis), pad)


def _head_expanders():
    lane = jnp.arange(LANES)[:, None]
    head = jnp.arange(D_SSD)[None, :] // SSD_HEAD_DIM
    valid = lane < DT_COPIES * DT_GROUP
    e_f = jnp.logical_and(valid, lane % DT_GROUP == head)
    e_b = jnp.logical_and(valid, lane % DT_GROUP == head + SSD_HEADS)
    return e_f.astype(BF16), e_b.astype(BF16)


def _stream(x, pos, mods, mod_row0, seq_len, h0f, h0b, prm, tm):
    s_n, l_n, _ = x.shape
    z, xa, dtp = _inproj(x, pos, mods, prm["w_in"], prm["conv_w"], prm["conv_b"],
                         mod_row0, tm, seq_len)
    n_seq = s_n * l_n // seq_len
    as_seq = lambda t: t.reshape(n_seq, seq_len, t.shape[-1])
    ya, hf, hb = _ssd(as_seq(xa), as_seq(dtp), as_seq(z), h0f, h0b, prm["dt_bias"],
                      prm["a_neg"], prm["d_skip"], prm["norm_w"], prm["e_f"], prm["e_b"])
    x1 = _mix(x, pos, ya.reshape(s_n, l_n, D_SSD), mods, mod_row0, tm, prm["w_uvg"],
              prm["sgu_ln_g"], prm["sgu_ln_b"], prm["w_spatial"], prm["b_spatial"],
              prm["w_proj_a"], prm["w_proj_b"], prm["w_out"], prm["ln1_g"], prm["ln1_b"])
    y = _ffn(x1, mods, mod_row0, tm, prm["w_ff1"], prm["w_ff2"], prm["ln2_g"], prm["ln2_b"])
    return y, hf, hb


def kernel(x_prompt, x_sample, state_ssd_fwd, state_ssd_bwd, c, c_ctx, w_ada, b_ada, w_in, conv_w, conv_b, dt_bias_fwd, dt_bias_bwd, a_log_fwd, a_log_bwd, d_skip, ssd_norm_w, sgu_ln_g, sgu_ln_b, w_spatial, b_spatial, w_proj_a, w_proj_b, w_out, ln1_g, ln1_b, w_ff1, w_ff2, ln2_g, ln2_b):
    batch, seq, _ = x_prompt.shape
    dec_batch, dec_seq, _ = x_sample.shape
    l = 0
    row = lambda v: v.reshape(1, -1)

    cond = jnp.zeros((SUBLANES, D_MODEL), F32).at[0].set(c_ctx).at[1:1 + dec_batch].set(c)
    mods = _ada(cond, w_ada[l], b_ada[l]).reshape(SUBLANES, 1, 6 * D_MODEL)

    o2 = D_SSD + SSD_CONV_DIM
    o3 = o2 + DT_GROUP
    w = w_in[l]
    w_in_r = jnp.concatenate([w[:, :o2], _dt_lanes(w[:, o2:o3], 1)], axis=1).astype(BF16)
    e_f, e_b = _head_expanders()

    prm = dict(
        w_in=w_in_r, w_uvg=w[:, o3:].astype(BF16),
        conv_w=conv_w[l], conv_b=row(conv_b[l]),
        dt_bias=row(_dt_lanes(jnp.concatenate([dt_bias_fwd[l], dt_bias_bwd[l]]), 0)),
        a_neg=row(_dt_lanes(-jnp.exp(jnp.concatenate([a_log_fwd[l], a_log_bwd[l]])), 0)),
        d_skip=row(jnp.repeat(d_skip[l], SSD_HEAD_DIM)),
        norm_w=row(ssd_norm_w[l]),
        e_f=e_f, e_b=e_b,
        sgu_ln_g=row(sgu_ln_g[l]), sgu_ln_b=row(sgu_ln_b[l]),
        w_spatial=w_spatial[l].astype(BF16),
        b_spatial=jnp.repeat(b_spatial[l].T, SGU_GROUP_DIM, axis=1),
        w_proj_a=w_proj_a[l].astype(BF16), w_proj_b=w_proj_b[l].astype(BF16),
        w_out=w_out[l].astype(BF16),
        ln1_g=row(ln1_g[l]), ln1_b=row(ln1_b[l]),
        w_ff1=w_ff1[l].astype(BF16), w_ff2=w_ff2[l].astype(BF16),
        ln2_g=row(ln2_g[l]), ln2_b=row(ln2_b[l]),
    )

    tm = 256
    xp, hf, hb = _stream(x_prompt.reshape(1, batch * seq, D_MODEL), None, mods, 0, seq,
                         None, None, prm, tm)
    pos = _grid_pos_tables(dec_seq)
    h0f = state_ssd_fwd[:, l].reshape(dec_batch, D_SSD, SSD_STATE)
    h0b = state_ssd_bwd[:, l].reshape(dec_batch, D_SSD, SSD_STATE)
    xs, _, _ = _stream(x_sample, pos, mods, 1, dec_seq, h0f, h0b, prm, tm)

    state_shape = (batch, DEPTH, SSD_HEADS, SSD_HEAD_DIM, SSD_STATE)
    return (xp.reshape(batch, seq, D_MODEL), xs,
            hf.reshape(state_shape), hb.reshape(state_shape))
```

```python
import functools
import math

import jax
import jax.numpy as jnp
from jax import lax
from jax.experimental import pallas as pl
from jax.experimental.pallas import tpu as pltpu

D_MODEL = 1024
GRID_W = 64
POS_BASE = 10000.0
D_SSD = 1024
SSD_HEAD_DIM = 64
SSD_HEADS = D_SSD // SSD_HEAD_DIM
SSD_GROUPS = 2
SSD_STATE = 128
SSD_CHUNK = 128
SSD_CONV_DIM = D_SSD + 2 * SSD_GROUPS * SSD_STATE
D_SGU = 1024
SGU_GROUPS = 8
SGU_GROUP_DIM = D_SGU // SGU_GROUPS
SGU_CHUNK = 128
D_FF = 4 * D_MODEL
DEPTH = 1
DN_ALPHA = (2.0 * DEPTH) ** 0.25
EPS = 1e-5

LANES = 128
SUBLANES = 8
BF16_ROWS = 16

DT_GROUP = 2 * SSD_HEADS
DT_COPIES = 3
DT_PAD = LANES
HEAD_PAIRS = SSD_HEADS // 2
PAIRS_PER_GROUP = HEAD_PAIRS // SSD_GROUPS
GROUP_COLS = D_SSD // SSD_GROUPS

BF16 = jnp.bfloat16
F32 = jnp.float32
HIGHEST = lax.Precision.HIGHEST


def _dot(a, b):
    return jnp.dot(a.astype(BF16), b.astype(BF16), preferred_element_type=F32)


def _dot_exact(a, b):
    return jnp.dot(a, b, precision=HIGHEST, preferred_element_type=F32)


def _layer_norm(x, g, b):
    mu = jnp.mean(x, axis=-1, keepdims=True)
    xc = x - mu
    var = jnp.mean(xc * xc, axis=-1, keepdims=True)
    return xc * lax.rsqrt(var + EPS) * g + b


def _softplus(x):
    return jnp.maximum(x, 0.0) + jnp.log1p(jnp.exp(-jnp.abs(x)))


def _const_spec(shape):
    zeros = (0,) * len(shape)
    return pl.BlockSpec(shape, lambda *_: zeros, pipeline_mode=pl.Buffered(1))


def _params(vmem_mb, n_axes):
    return pltpu.CompilerParams(
        dimension_semantics=("arbitrary",) * n_axes,
        vmem_limit_bytes=vmem_mb * 1024 * 1024)


def _ada_kernel(c_ref, w_ref, b_ref, o_ref):
    c = c_ref[...]
    o_ref[...] = _dot_exact(c * jax.nn.sigmoid(c), w_ref[...]) + b_ref[...]


def _ada(cond, w_ada, b_ada):
    n = w_ada.shape[1]
    tn = 1024
    return pl.pallas_call(
        _ada_kernel,
        grid=(n // tn,),
        in_specs=[pl.BlockSpec((SUBLANES, D_MODEL), lambda j: (0, 0)),
                  pl.BlockSpec((D_MODEL, tn), lambda j: (0, j)),
                  pl.BlockSpec((1, tn), lambda j: (0, j))],
        out_specs=pl.BlockSpec((SUBLANES, tn), lambda j: (0, j)),
        out_shape=jax.ShapeDtypeStruct((SUBLANES, n), F32),
        compiler_params=_params(32, 1),
        name="ada",
    )(cond, w_ada, b_ada.reshape(1, n))


IN_SEGS = (D_SSD, SSD_CONV_DIM, DT_PAD)
IN_COLS = sum(IN_SEGS)
IN_NTILE = 512
XBC_SEG = 1
HALO = BF16_ROWS


def _pos_rows(embr_ref, embc_ref, grid_row0, n_rows):
    left = [jnp.broadcast_to(embr_ref[pl.ds(grid_row0 + i, 1), :], (GRID_W, D_MODEL // 2))
            for i in range(n_rows)]
    right = [embc_ref[...]] * n_rows
    return jnp.concatenate([jnp.concatenate(left, axis=0), jnp.concatenate(right, axis=0)],
                           axis=1)


def _pos_halo(embr_ref, embc_ref, grid_row, col0):
    left = jnp.broadcast_to(embr_ref[pl.ds(grid_row, 1), :], (HALO, D_MODEL // 2))
    return jnp.concatenate([left, embc_ref[col0:col0 + HALO, :]], axis=1)


def _front_kernel(*refs, has_pos, has_h0, tiles_per_seq, n_tiles):
    x_ref, xp_ref, xn_ref = refs[:3]
    refs = refs[3:]
    if has_pos:
        embr_ref, embc_ref = refs[:2]
        refs = refs[2:]
    if has_h0:
        h0b_ref = refs[0]
        refs = refs[1:]
    mod_ref, w_ref, convw_ref, convb_ref, dtbias_ref, aneg_ref, eb_ref = refs[:7]
    z_ref, xa_ref, dt_ref, hbs_ref, hfb_ref, hb_s = refs[7:]
    outs = (z_ref, xa_ref, dt_ref)
    shift1 = mod_ref[0, :, 0:D_MODEL]
    scale1 = mod_ref[0, :, D_MODEL:2 * D_MODEL]
    tm = x_ref.shape[1]
    n_ext = tm + 2 * HALO
    tile = n_tiles - 1 - pl.program_id(1)
    t = tile % tiles_per_seq

    def modulate(x, pos):
        if pos is not None:
            x = x + pos
        return (x * (1.0 + scale1) + shift1).astype(BF16)

    pos = pos_prev = pos_next = None
    if has_pos:
        rows_per_tile = tm // GRID_W
        n_grid_rows = embr_ref.shape[0]
        r0 = tile * rows_per_tile
        pos = _pos_rows(embr_ref, embc_ref, r0, rows_per_tile)
        pos_prev = _pos_halo(embr_ref, embc_ref, jnp.maximum(r0 - 1, 0), GRID_W - HALO)
        pos_next = _pos_halo(embr_ref, embc_ref,
                             jnp.minimum(r0 + rows_per_tile, n_grid_rows - 1), 0)
    h = modulate(x_ref[0], pos)
    h_ext = jnp.concatenate([modulate(xp_ref[0], pos_prev), h, modulate(xn_ref[0], pos_next)],
                            axis=0)

    row = lax.broadcasted_iota(jnp.int32, (tm, 1), 0)
    zero_prev = row == 0
    zero_next = row == tm - 1
    if tiles_per_seq > 1:
        zero_prev = jnp.logical_and(zero_prev, t == 0)
        zero_next = jnp.logical_and(zero_next, t == tiles_per_seq - 1)

    off = 0
    for seg, (o_ref, width) in enumerate(zip(outs, IN_SEGS)):
        for c0 in range(0, width, IN_NTILE):
            c1 = min(c0 + IN_NTILE, width)
            w = w_ref[:, off + c0:off + c1]
            if seg == XBC_SEG:
                p = jnp.dot(h_ext, w, preferred_element_type=F32)
                cur = p[HALO:HALO + tm]
                prv = jnp.where(zero_prev, 0.0, pltpu.roll(p, 1, 0)[HALO:HALO + tm])
                nxt = jnp.where(zero_next, 0.0, pltpu.roll(p, n_ext - 1, 0)[HALO:HALO + tm])
                xc = (convw_ref[0:1, c0:c1] * prv + convw_ref[1:2, c0:c1] * cur
                      + convw_ref[2:3, c0:c1] * nxt + convb_ref[:, c0:c1])
                o_ref[0, :, c0:c1] = xc * jax.nn.sigmoid(xc)
            else:
                o_ref[0, :, c0:c1] = jnp.dot(h, w, preferred_element_type=F32)
        off += width

    def init_state():
        if has_h0:
            hb_s[...] = h0b_ref[0].T
        else:
            hb_s[...] = jnp.zeros_like(hb_s)

    if tiles_per_seq > 1:
        pl.when(t == tiles_per_seq - 1)(init_state)
    else:
        init_state()
    h_state = hb_s[...]
    q = SSD_CHUNK
    for ch in reversed(range(tm // q)):
        rows = slice(ch * q, (ch + 1) * q)
        hbs_ref[0, ch] = h_state
        dt_all, _, _, _, rcum, rtot = _chunk_decay(dt_ref[0, rows, :], dtbias_ref, aneg_ref)
        h_state = _carry_update(h_state, dt_all * jnp.exp(rtot - rcum), rtot, eb_ref,
                                xa_ref[0, rows, 0:D_SSD],
                                xa_ref[0, rows, D_SSD:D_SSD + SSD_GROUPS * SSD_STATE])
    hb_s[...] = h_state

    def final_state():
        hfb_ref[0] = h_state.T

    if tiles_per_seq > 1:
        pl.when(t == 0)(final_state)
    else:
        final_state()


def _front(x, pos, mods, h0b, w_in_r, convw, convb, dtbias, aneg, e_b, mod_row0, tm, seq_len):
    s_n, l_n, _ = x.shape
    has_pos = pos is not None
    has_h0 = h0b is not None
    hb = tm // HALO
    n_hb = l_n // HALO
    n_tiles = l_n // tm
    tiles_per_seq = seq_len // tm
    seqs_per_row = l_n // seq_len
    cpt = tm // SSD_CHUNK
    fwd = lambda j: n_tiles - 1 - j
    tok = lambda s, j: (s, fwd(j), 0)
    seq_of = lambda s, j: (s * seqs_per_row + fwd(j) // tiles_per_seq, 0, 0)
    in_specs = [pl.BlockSpec((1, tm, D_MODEL), tok),
                pl.BlockSpec((1, HALO, D_MODEL),
                             lambda s, j: (s, jnp.maximum(fwd(j) * hb - 1, 0), 0)),
                pl.BlockSpec((1, HALO, D_MODEL),
                             lambda s, j: (s, jnp.minimum((fwd(j) + 1) * hb, n_hb - 1), 0))]
    args = [x, x, x]
    if has_pos:
        in_specs += [_const_spec(pos[0].shape), _const_spec(pos[1].shape)]
        args += list(pos)
    state_spec = pl.BlockSpec((1, D_SSD, SSD_STATE), seq_of)
    if has_h0:
        in_specs.append(state_spec)
        args.append(h0b)
    in_specs.append(pl.BlockSpec((1, 1, 6 * D_MODEL), lambda s, j: (mod_row0 + s, 0, 0)))
    args.append(mods)
    for a in (w_in_r, convw, convb, dtbias, aneg, e_b):
        in_specs.append(_const_spec(a.shape))
        args.append(a)
    return pl.pallas_call(
        functools.partial(_front_kernel, has_pos=has_pos, has_h0=has_h0,
                          tiles_per_seq=tiles_per_seq, n_tiles=n_tiles),
        grid=(s_n, n_tiles),
        in_specs=in_specs,
        out_specs=[pl.BlockSpec((1, tm, w), tok) for w in IN_SEGS]
        + [pl.BlockSpec((1, cpt, SSD_STATE, D_SSD), lambda s, j: (s, fwd(j), 0, 0)),
           state_spec],
        out_shape=[jax.ShapeDtypeStruct((s_n, l_n, w), F32) for w in IN_SEGS]
        + [jax.ShapeDtypeStruct((s_n, l_n // SSD_CHUNK, SSD_STATE, D_SSD), F32),
           jax.ShapeDtypeStruct((s_n * seqs_per_row, D_SSD, SSD_STATE), F32)],
        scratch_shapes=[pltpu.VMEM((SSD_STATE, D_SSD), F32)],
        compiler_params=_params(48, 2),
        name="front",
    )(*args)


def _block_diag_pair(xp):
    lane = lax.broadcasted_iota(jnp.int32, xp.shape, 1)
    first = lane < SSD_HEAD_DIM
    return jnp.concatenate([jnp.where(first, xp, 0.0), jnp.where(first, 0.0, xp)], axis=0)


def _split_terms(v):
    lane = lax.broadcasted_iota(jnp.int32, v.shape, 1)
    hi = v.astype(BF16).astype(F32)
    r1 = v - hi
    mid = r1.astype(BF16).astype(F32)
    lo = r1 - mid
    return jnp.where(lane < DT_GROUP, hi, jnp.where(lane < 2 * DT_GROUP, mid, lo)).astype(BF16)


def _expand(v, e_ref):
    return jnp.dot(_split_terms(v), e_ref[...], preferred_element_type=F32)


def _chunk_decay(dtp, dtbias_ref, aneg_ref):
    q = dtp.shape[0]
    dt_all = _softplus(dtp + dtbias_ref[...])
    a_all = dt_all * aneg_ref[...]
    ii = lax.broadcasted_iota(jnp.int32, (q, q), 0)
    jj = lax.broadcasted_iota(jnp.int32, (q, q), 1)
    acum = _dot_exact((ii >= jj).astype(F32), a_all)
    atot = acum[q - 1:q, :]
    rcum = atot - acum + a_all
    return dt_all, a_all, acum, atot, rcum, rcum[0:1, :]


def _carry_update(h_old, w_tok, tot_row, e_ref, xh, b_all):
    wx = (_expand(w_tok, e_ref) * xh).astype(BF16)
    s = jnp.concatenate(
        [jnp.dot(b_all[:, g * SSD_STATE:(g + 1) * SSD_STATE].T.astype(BF16),
                 wx[:, g * GROUP_COLS:(g + 1) * GROUP_COLS], preferred_element_type=F32)
         for g in range(SSD_GROUPS)], axis=1)
    decay = jnp.exp(_expand(jnp.broadcast_to(tot_row, (BF16_ROWS, LANES)), e_ref))[0:1, :]
    return decay * h_old + s


def _ssd_kernel(*refs, nc, has_h0):
    x_ref, b_ref, c_ref, dtp_ref, z_ref, hbs_ref = refs[:6]
    k = 6
    if has_h0:
        h0f_ref = refs[k]
        k += 1
    dtbias_ref, aneg_ref, dskip_ref, normw_ref, ef_ref, eb_ref = refs[k:k + 6]
    k += 6
    ya_ref, hff_ref, hf_s = refs[k:k + 3]
    c = pl.program_id(1)

    @pl.when(c == 0)
    def _():
        if has_h0:
            hf_s[...] = h0f_ref[0].T
        else:
            hf_s[...] = jnp.zeros_like(hf_s)

    y, h_new = _ssd_chunk(x_ref[0], b_ref[0], c_ref[0], dtp_ref[0], hf_s[...], hbs_ref[0, 0],
                          dtbias_ref, aneg_ref, dskip_ref, ef_ref, eb_ref)
    hf_s[...] = h_new
    zz = z_ref[0]
    yg = y * (zz * jax.nn.sigmoid(zz))
    ms = jnp.mean(yg * yg, axis=-1, keepdims=True)
    ya_ref[0] = yg * lax.rsqrt(ms + EPS) * normw_ref[...]

    @pl.when(c == nc - 1)
    def _():
        hff_ref[0] = h_new.T


def _ssd_chunk(xh, b_all, c_all, dtp, hf_in, hb_in, dtbias_ref, aneg_ref, dskip_ref, ef_ref,
               eb_ref):
    q = SSD_CHUNK
    dt_all, a_all, acum, atot, rcum, _ = _chunk_decay(dtp, dtbias_ref, aneg_ref)
    ii = lax.broadcasted_iota(jnp.int32, (q, q), 0)
    jj = lax.broadcasted_iota(jnp.int32, (q, q), 1)
    lower = ii >= jj
    diag = ii == jj
    dt_t = dt_all.T
    a_t = a_all.T
    acum_t = _dot_exact(a_t, (ii <= jj).astype(F32))
    rcum_t = acum_t[:, q - 1:q] - acum_t + a_t

    ef = jnp.exp(_expand(acum, ef_ref))
    eb = jnp.exp(_expand(rcum, eb_ref))
    y_off = []
    cb = []
    for g in range(SSD_GROUPS):
        gs = slice(g * GROUP_COLS, (g + 1) * GROUP_COLS)
        c_g = c_all[:, g * SSD_STATE:(g + 1) * SSD_STATE].astype(BF16)
        b_g = b_all[:, g * SSD_STATE:(g + 1) * SSD_STATE].astype(BF16)
        cb.append(lax.dot_general(c_g, b_g, (((1,), (1,)), ((), ())),
                                  preferred_element_type=F32))
        y_off.append(ef[:, gs] * jnp.dot(c_g, hf_in[:, gs].astype(BF16),
                                         preferred_element_type=F32)
                     + eb[:, gs] * jnp.dot(c_g, hb_in[:, gs].astype(BF16),
                                           preferred_element_type=F32))
    y_off = jnp.concatenate(y_off, axis=1)

    y_cols = []
    for pr in range(HEAD_PAIRS):
        g = pr // PAIRS_PER_GROUP
        m_parts = []
        for h in (2 * pr, 2 * pr + 1):
            hb_lane = SSD_HEADS + h
            colf = jnp.broadcast_to(acum[:, h:h + 1], (q, q))
            colb = jnp.broadcast_to(rcum[:, hb_lane:hb_lane + 1], (q, q))
            seg = jnp.where(lower, colf - acum_t[h:h + 1, :],
                            colb - rcum_t[hb_lane:hb_lane + 1, :])
            dt_b_row = dt_t[hb_lane:hb_lane + 1, :]
            w = (jnp.exp(seg) * jnp.where(lower, dt_t[h:h + 1, :], dt_b_row)
                 + jnp.where(diag, dt_b_row, 0.0))
            m_parts.append(cb[g] * w)
        sl = slice(pr * LANES, (pr + 1) * LANES)
        y_cols.append(_dot(jnp.concatenate(m_parts, axis=1), _block_diag_pair(xh[:, sl])))
    y = jnp.concatenate(y_cols, axis=1) + y_off + dskip_ref[...] * xh
    h_new = _carry_update(hf_in, dt_all * jnp.exp(atot - acum), atot, ef_ref, xh, b_all)
    return y, h_new


def _ssd(xa, dtp, z, hbs, h0f, dtbias, aneg, dskip, normw, e_f, e_b):
    s_n, l_n, _ = xa.shape
    q = SSD_CHUNK
    nc = l_n // q
    has_h0 = h0f is not None
    gn = SSD_GROUPS * SSD_STATE
    b_blk = D_SSD // gn
    in_specs = [
        pl.BlockSpec((1, q, D_SSD), lambda s, c: (s, c, 0)),
        pl.BlockSpec((1, q, gn), lambda s, c: (s, c, b_blk)),
        pl.BlockSpec((1, q, gn), lambda s, c: (s, c, b_blk + 1)),
        pl.BlockSpec((1, q, DT_PAD), lambda s, c: (s, c, 0)),
        pl.BlockSpec((1, q, D_SSD), lambda s, c: (s, c, 0)),
        pl.BlockSpec((1, 1, SSD_STATE, D_SSD), lambda s, c: (s, c, 0, 0)),
    ]
    args = [xa, xa, xa, dtp, z, hbs]
    state_spec = pl.BlockSpec((1, D_SSD, SSD_STATE), lambda s, c: (s, 0, 0))
    if has_h0:
        in_specs.append(state_spec)
        args.append(h0f)
    for a in (dtbias, aneg, dskip, normw, e_f, e_b):
        in_specs.append(_const_spec(a.shape))
        args.append(a)
    return pl.pallas_call(
        functools.partial(_ssd_kernel, nc=nc, has_h0=has_h0),
        grid=(s_n, nc),
        in_specs=in_specs,
        out_specs=[pl.BlockSpec((1, q, D_SSD), lambda s, c: (s, c, 0)), state_spec],
        out_shape=[jax.ShapeDtypeStruct((s_n, l_n, D_SSD), F32),
                   jax.ShapeDtypeStruct((s_n, D_SSD, SSD_STATE), F32)],
        scratch_shapes=[pltpu.VMEM((SSD_STATE, D_SSD), F32)],
        compiler_params=_params(32, 2),
        name="ssd",
    )(*args)


def _gelu(x):
    return 0.5 * x * (1.0 + lax.erf(x * (1.0 / math.sqrt(2.0))))


def _mix_kernel(*refs, has_pos, n_chunks):
    x_ref = refs[0]
    refs = refs[1:]
    if has_pos:
        embr_ref, embc_ref = refs[:2]
        refs = refs[2:]
    (ya_ref, mod_ref, wuvg_ref, lng_ref, lnb_ref, wsp_ref, bsp_ref, wpa_ref, wpb_ref,
     wout_ref, ln1g_ref, ln1b_ref, o_ref) = refs
    x = x_ref[0]
    if has_pos:
        rows_per_tile = x.shape[0] // GRID_W
        x = x + _pos_rows(embr_ref, embc_ref, pl.program_id(1) * rows_per_tile, rows_per_tile)
    shift1 = mod_ref[0, :, 0:D_MODEL]
    scale1 = mod_ref[0, :, D_MODEL:2 * D_MODEL]
    h = (x * (1.0 + scale1) + shift1).astype(BF16)

    def in_proj(c0, width):
        return jnp.concatenate(
            [jnp.dot(h, wuvg_ref[:, c:c + IN_NTILE], preferred_element_type=F32)
             for c in range(c0, c0 + width, IN_NTILE)], axis=1)

    u = _gelu(in_proj(0, D_SGU))
    v = _layer_norm(_gelu(in_proj(D_SGU, D_SGU)), lng_ref[...], lnb_ref[...])
    rows = []
    for ch in range(n_chunks):
        vc = v[ch * SGU_CHUNK:(ch + 1) * SGU_CHUNK, :]
        cols = []
        for g in range(SGU_GROUPS):
            sl = slice(g * SGU_GROUP_DIM, (g + 1) * SGU_GROUP_DIM)
            cols.append(_dot(wsp_ref[g], vc[:, sl]))
        rows.append(jnp.concatenate(cols, axis=1) + bsp_ref[...])
    s = jnp.concatenate(rows, axis=0) if n_chunks > 1 else rows[0]
    yb = u * s
    merged = (jax.nn.sigmoid(in_proj(2 * D_SGU, D_MODEL)) * _dot(ya_ref[0], wpa_ref[...])
              + jax.nn.sigmoid(in_proj(2 * D_SGU + D_MODEL, D_MODEL)) * _dot(yb, wpb_ref[...]))
    gate1 = mod_ref[0, :, 2 * D_MODEL:3 * D_MODEL]
    o = _dot(merged, wout_ref[...])
    o_ref[0] = _layer_norm(DN_ALPHA * x + gate1 * o, ln1g_ref[...], ln1b_ref[...])


def _mix(x, pos, ya, mods, mod_row0, tm, wuvg, lng, lnb, wsp, bsp, wpa, wpb, wout, ln1g, ln1b):
    s_n, l_n, _ = x.shape
    has_pos = pos is not None
    tok = lambda s, j: (s, j, 0)
    in_specs = [pl.BlockSpec((1, tm, D_MODEL), tok)]
    args = [x]
    if has_pos:
        in_specs += [_const_spec(pos[0].shape), _const_spec(pos[1].shape)]
        args += list(pos)
    in_specs += [pl.BlockSpec((1, tm, D_SSD), tok),
                 pl.BlockSpec((1, 1, 6 * D_MODEL), lambda s, j: (mod_row0 + s, 0, 0))]
    args += [ya, mods]
    for a in (wuvg, lng, lnb, wsp, bsp, wpa, wpb, wout, ln1g, ln1b):
        in_specs.append(_const_spec(a.shape))
        args.append(a)
    return pl.pallas_call(
        functools.partial(_mix_kernel, has_pos=has_pos, n_chunks=tm // SGU_CHUNK),
        grid=(s_n, l_n // tm),
        in_specs=in_specs,
        out_specs=pl.BlockSpec((1, tm, D_MODEL), tok),
        out_shape=jax.ShapeDtypeStruct((s_n, l_n, D_MODEL), F32),
        compiler_params=_params(48, 2),
        name="mix",
    )(*args)


FF_TILE = 1024


def _ffn_kernel(x_ref, mod_ref, w1_ref, w2_ref, g_ref, b_ref, o_ref):
    x = x_ref[0]
    shift2 = mod_ref[0, :, 3 * D_MODEL:4 * D_MODEL]
    scale2 = mod_ref[0, :, 4 * D_MODEL:5 * D_MODEL]
    gate2 = mod_ref[0, :, 5 * D_MODEL:6 * D_MODEL]
    h = (x * (1.0 + scale2) + shift2).astype(BF16)
    f = jnp.zeros_like(x)
    for k0 in range(0, D_FF, FF_TILE):
        a = jnp.dot(h, w1_ref[:, k0:k0 + FF_TILE], preferred_element_type=F32)
        a = jnp.maximum(a, 0.0)
        f = f + _dot(a * a, w2_ref[k0:k0 + FF_TILE, :])
    o_ref[0] = _layer_norm(DN_ALPHA * x + gate2 * f, g_ref[...], b_ref[...])


def _ffn(x, mods, mod_row0, tm, w1, w2, g, b):
    s_n, l_n, _ = x.shape
    tok = lambda s, j: (s, j, 0)
    return pl.pallas_call(
        _ffn_kernel,
        grid=(s_n, l_n // tm),
        in_specs=[pl.BlockSpec((1, tm, D_MODEL), tok),
                  pl.BlockSpec((1, 1, 6 * D_MODEL), lambda s, j: (mod_row0 + s, 0, 0)),
                  _const_spec(w1.shape), _const_spec(w2.shape),
                  _const_spec(g.shape), _const_spec(b.shape)],
        out_specs=pl.BlockSpec((1, tm, D_MODEL), tok),
        out_shape=jax.ShapeDtypeStruct((s_n, l_n, D_MODEL), F32),
        compiler_params=_params(48, 2),
        name="ffn",
    )(x, mods, w1, w2, g, b)


def _grid_pos_tables(n_tokens):
    rows = n_tokens // GRID_W
    quarter = D_MODEL // 4
    omega = 1.0 / (POS_BASE ** (jnp.arange(quarter, dtype=F32) / quarter))
    r = jnp.arange(rows, dtype=F32)[:, None] * omega
    col = jnp.arange(GRID_W, dtype=F32)[:, None] * omega
    emb_r = jnp.concatenate([jnp.sin(r), jnp.cos(r)], axis=-1)
    emb_c = jnp.concatenate([jnp.sin(col), jnp.cos(col)], axis=-1)
    return emb_r, emb_c


def _dt_lanes(v, axis):
    pad = [(0, 0)] * v.ndim
    pad[axis] = (0, DT_PAD - DT_COPIES * DT_GROUP)
    return jnp.pad(jnp.concatenate([v] * DT_COPIES, axis=axis), pad)


def _head_expanders():
    lane = jnp.arange(LANES)[:, None]
    head = jnp.arange(D_SSD)[None, :] // SSD_HEAD_DIM
    valid = lane < DT_COPIES * DT_GROUP
    e_f = jnp.logical_and(valid, lane % DT_GROUP == head)
    e_b = jnp.logical_and(valid, lane % DT_GROUP == head + SSD_HEADS)
    return e_f.astype(BF16), e_b.astype(BF16)


def _stream(x, pos, mods, mod_row0, seq_len, h0f, h0b, prm, tm):
    s_n, l_n, _ = x.shape
    z, xa, dtp, hbs, hb = _front(x, pos, mods, h0b, prm["w_in"], prm["conv_w"], prm["conv_b"],
                                 prm["dt_bias"], prm["a_neg"], prm["e_b"], mod_row0, tm, seq_len)
    n_seq = s_n * l_n // seq_len
    as_seq = lambda t: t.reshape((n_seq, seq_len // (l_n // t.shape[1])) + t.shape[2:])
    ya, hf = _ssd(as_seq(xa), as_seq(dtp), as_seq(z), as_seq(hbs), h0f, prm["dt_bias"],
                  prm["a_neg"], prm["d_skip"], prm["norm_w"], prm["e_f"], prm["e_b"])
    x1 = _mix(x, pos, ya.reshape(s_n, l_n, D_SSD), mods, mod_row0, tm, prm["w_uvg"],
              prm["sgu_ln_g"], prm["sgu_ln_b"], prm["w_spatial"], prm["b_spatial"],
              prm["w_proj_a"], prm["w_proj_b"], prm["w_out"], prm["ln1_g"], prm["ln1_b"])
    y = _ffn(x1, mods, mod_row0, tm, prm["w_ff1"], prm["w_ff2"], prm["ln2_g"], prm["ln2_b"])
    return y, hf, hb


def kernel(x_prompt, x_sample, state_ssd_fwd, state_ssd_bwd, c, c_ctx, w_ada, b_ada, w_in, conv_w, conv_b, dt_bias_fwd, dt_bias_bwd, a_log_fwd, a_log_bwd, d_skip, ssd_norm_w, sgu_ln_g, sgu_ln_b, w_spatial, b_spatial, w_proj_a, w_proj_b, w_out, ln1_g, ln1_b, w_ff1, w_ff2, ln2_g, ln2_b):
    batch, seq, _ = x_prompt.shape
    dec_batch, dec_seq, _ = x_sample.shape
    l = 0
    row = lambda v: v.reshape(1, -1)

    cond = jnp.zeros((SUBLANES, D_MODEL), F32).at[0].set(c_ctx).at[1:1 + dec_batch].set(c)
    mods = _ada(cond, w_ada[l], b_ada[l]).reshape(SUBLANES, 1, 6 * D_MODEL)

    o2 = D_SSD + SSD_CONV_DIM
    o3 = o2 + DT_GROUP
    w = w_in[l]
    w_in_r = jnp.concatenate([w[:, :o2], _dt_lanes(w[:, o2:o3], 1)], axis=1).astype(BF16)
    e_f, e_b = _head_expanders()

    prm = dict(
        w_in=w_in_r, w_uvg=w[:, o3:].astype(BF16),
        conv_w=conv_w[l], conv_b=row(conv_b[l]),
        dt_bias=row(_dt_lanes(jnp.concatenate([dt_bias_fwd[l], dt_bias_bwd[l]]), 0)),
        a_neg=row(_dt_lanes(-jnp.exp(jnp.concatenate([a_log_fwd[l], a_log_bwd[l]])), 0)),
        d_skip=row(jnp.repeat(d_skip[l], SSD_HEAD_DIM)),
        norm_w=row(ssd_norm_w[l]),
        e_f=e_f, e_b=e_b,
        sgu_ln_g=row(sgu_ln_g[l]), sgu_ln_b=row(sgu_ln_b[l]),
        w_spatial=w_spatial[l].astype(BF16),
        b_spatial=jnp.repeat(b_spatial[l].T, SGU_GROUP_DIM, axis=1),
        w_proj_a=w_proj_a[l].astype(BF16), w_proj_b=w_proj_b[l].astype(BF16),
        w_out=w_out[l].astype(BF16),
        ln1_g=row(ln1_g[l]), ln1_b=row(ln1_b[l]),
        w_ff1=w_ff1[l].astype(BF16), w_ff2=w_ff2[l].astype(BF16),
        ln2_g=row(ln2_g[l]), ln2_b=row(ln2_b[l]),
    )

    tm = 256
    xp, hf, hb = _stream(x_prompt.reshape(1, batch * seq, D_MODEL), None, mods, 0, seq,
                         None, None, prm, tm)
    pos = _grid_pos_tables(dec_seq)
    h0f = state_ssd_fwd[:, l].reshape(dec_batch, D_SSD, SSD_STATE)
    h0b = state_ssd_bwd[:, l].reshape(dec_batch, D_SSD, SSD_STATE)
    xs, _, _ = _stream(x_sample, pos, mods, 1, dec_seq, h0f, h0b, prm, tm)

    state_shape = (batch, DEPTH, SSD_HEADS, SSD_HEAD_DIM, SSD_STATE)
    return (xp.reshape(batch, seq, D_MODEL), xs,
            hf.reshape(state_shape), hb.reshape(state_shape))
```

```python
import functools
import math

import jax
import jax.numpy as jnp
from jax import lax
from jax.experimental import pallas as pl
from jax.experimental.pallas import tpu as pltpu

D_MODEL = 1024
GRID_W = 64
POS_BASE = 10000.0
D_SSD = 1024
SSD_HEAD_DIM = 64
SSD_HEADS = D_SSD // SSD_HEAD_DIM
SSD_GROUPS = 2
SSD_STATE = 128
SSD_CHUNK = 128
SSD_CONV_DIM = D_SSD + 2 * SSD_GROUPS * SSD_STATE
D_SGU = 1024
SGU_GROUPS = 8
SGU_GROUP_DIM = D_SGU // SGU_GROUPS
SGU_CHUNK = 128
D_FF = 4 * D_MODEL
DEPTH = 1
DN_ALPHA = (2.0 * DEPTH) ** 0.25
EPS = 1e-5

LANES = 128
SUBLANES = 8
BF16_ROWS = 16

DT_GROUP = 2 * SSD_HEADS
DT_COPIES = 3
DT_PAD = LANES
HEAD_PAIRS = SSD_HEADS // 2
PAIRS_PER_GROUP = HEAD_PAIRS // SSD_GROUPS
GROUP_COLS = D_SSD // SSD_GROUPS
BC_COLS = SSD_GROUPS * SSD_STATE

BF16 = jnp.bfloat16
F32 = jnp.float32
HIGHEST = lax.Precision.HIGHEST


def _dot(a, b):
    return jnp.dot(a.astype(BF16), b.astype(BF16), preferred_element_type=F32)


def _dot_exact(a, b):
    return jnp.dot(a, b, precision=HIGHEST, preferred_element_type=F32)


def _layer_norm(x, g, b):
    mu = jnp.mean(x, axis=-1, keepdims=True)
    xc = x - mu
    var = jnp.mean(xc * xc, axis=-1, keepdims=True)
    return xc * lax.rsqrt(var + EPS) * g + b


def _softplus(x):
    return jnp.maximum(x, 0.0) + jnp.log1p(jnp.exp(-jnp.abs(x)))


def _gelu(x):
    return 0.5 * x * (1.0 + lax.erf(x * (1.0 / math.sqrt(2.0))))


def _const_spec(shape):
    zeros = (0,) * len(shape)
    return pl.BlockSpec(shape, lambda *_: zeros, pipeline_mode=pl.Buffered(1))


def _params(vmem_mb, n_axes):
    return pltpu.CompilerParams(
        dimension_semantics=("arbitrary",) * n_axes,
        vmem_limit_bytes=vmem_mb * 1024 * 1024)


def _ada_kernel(c_ref, w_ref, b_ref, o_ref):
    c = c_ref[...]
    o_ref[...] = _dot_exact(c * jax.nn.sigmoid(c), w_ref[...]) + b_ref[...]


def _ada(cond, w_ada, b_ada):
    n = w_ada.shape[1]
    tn = 1024
    return pl.pallas_call(
        _ada_kernel,
        grid=(n // tn,),
        in_specs=[pl.BlockSpec((SUBLANES, D_MODEL), lambda j: (0, 0)),
                  pl.BlockSpec((D_MODEL, tn), lambda j: (0, j)),
                  pl.BlockSpec((1, tn), lambda j: (0, j))],
        out_specs=pl.BlockSpec((SUBLANES, tn), lambda j: (0, j)),
        out_shape=jax.ShapeDtypeStruct((SUBLANES, n), F32),
        compiler_params=_params(32, 1),
        name="ada",
    )(cond, w_ada, b_ada.reshape(1, n))


IN_NTILE = 512
HALO = BF16_ROWS


def _pos_rows(embr_ref, embc_ref, grid_row0, n_rows):
    left = [jnp.broadcast_to(embr_ref[pl.ds(grid_row0 + i, 1), :], (GRID_W, D_MODEL // 2))
            for i in range(n_rows)]
    right = [embc_ref[...]] * n_rows
    return jnp.concatenate([jnp.concatenate(left, axis=0), jnp.concatenate(right, axis=0)],
                           axis=1)


def _pos_halo(embr_ref, embc_ref, grid_row, col0):
    left = jnp.broadcast_to(embr_ref[pl.ds(grid_row, 1), :], (HALO, D_MODEL // 2))
    return jnp.concatenate([left, embc_ref[col0:col0 + HALO, :]], axis=1)


def _block_diag_pair(xp):
    lane = lax.broadcasted_iota(jnp.int32, xp.shape, 1)
    first = lane < SSD_HEAD_DIM
    return jnp.concatenate([jnp.where(first, xp, 0.0), jnp.where(first, 0.0, xp)], axis=0)


def _split_terms(v):
    lane = lax.broadcasted_iota(jnp.int32, v.shape, 1)
    hi = v.astype(BF16).astype(F32)
    r1 = v - hi
    mid = r1.astype(BF16).astype(F32)
    lo = r1 - mid
    return jnp.where(lane < DT_GROUP, hi, jnp.where(lane < 2 * DT_GROUP, mid, lo)).astype(BF16)


def _expand(v, e_ref):
    return jnp.dot(_split_terms(v), e_ref[...], preferred_element_type=F32)


def _chunk_decay(dtp, dtbias_ref, aneg_ref):
    q = dtp.shape[0]
    dt_all = _softplus(dtp + dtbias_ref[...])
    a_all = dt_all * aneg_ref[...]
    ii = lax.broadcasted_iota(jnp.int32, (q, q), 0)
    jj = lax.broadcasted_iota(jnp.int32, (q, q), 1)
    acum = _dot_exact((ii >= jj).astype(F32), a_all)
    atot = acum[q - 1:q, :]
    rcum = atot - acum + a_all
    return dt_all, a_all, acum, atot, rcum, rcum[0:1, :]


def _carry_update(h_old, w_tok, tot_row, e_ref, xh, b_all):
    wx = (_expand(w_tok, e_ref) * xh).astype(BF16)
    s = jnp.concatenate(
        [jnp.dot(b_all[:, g * SSD_STATE:(g + 1) * SSD_STATE].T.astype(BF16),
                 wx[:, g * GROUP_COLS:(g + 1) * GROUP_COLS], preferred_element_type=F32)
         for g in range(SSD_GROUPS)], axis=1)
    decay = jnp.exp(_expand(jnp.broadcast_to(tot_row, (BF16_ROWS, LANES)), e_ref))[0:1, :]
    return decay * h_old + s


def _ssd_chunk(xh, b_all, c_all, dtp, hf_in, hb_in, dtbias_ref, aneg_ref, dskip_ref, ef_ref,
               eb_ref):
    q = SSD_CHUNK
    dt_all, a_all, acum, atot, rcum, _ = _chunk_decay(dtp, dtbias_ref, aneg_ref)
    ii = lax.broadcasted_iota(jnp.int32, (q, q), 0)
    jj = lax.broadcasted_iota(jnp.int32, (q, q), 1)
    lower = ii >= jj
    diag = ii == jj
    dt_t = dt_all.T
    a_t = a_all.T
    acum_t = _dot_exact(a_t, (ii <= jj).astype(F32))
    rcum_t = acum_t[:, q - 1:q] - acum_t + a_t

    ef = jnp.exp(_expand(acum, ef_ref))
    eb = jnp.exp(_expand(rcum, eb_ref))
    y_off = []
    cb = []
    for g in range(SSD_GROUPS):
        gs = slice(g * GROUP_COLS, (g + 1) * GROUP_COLS)
        c_g = c_all[:, g * SSD_STATE:(g + 1) * SSD_STATE].astype(BF16)
        b_g = b_all[:, g * SSD_STATE:(g + 1) * SSD_STATE].astype(BF16)
        cb.append(lax.dot_general(c_g, b_g, (((1,), (1,)), ((), ())),
                                  preferred_element_type=F32))
        y_off.append(ef[:, gs] * jnp.dot(c_g, hf_in[:, gs].astype(BF16),
                                         preferred_element_type=F32)
                     + eb[:, gs] * jnp.dot(c_g, hb_in[:, gs].astype(BF16),
                                           preferred_element_type=F32))
    y_off = jnp.concatenate(y_off, axis=1)

    y_cols = []
    for pr in range(HEAD_PAIRS):
        g = pr // PAIRS_PER_GROUP
        m_parts = []
        for h in (2 * pr, 2 * pr + 1):
            hb_lane = SSD_HEADS + h
            colf = jnp.broadcast_to(acum[:, h:h + 1], (q, q))
            colb = jnp.broadcast_to(rcum[:, hb_lane:hb_lane + 1], (q, q))
            seg = jnp.where(lower, colf - acum_t[h:h + 1, :],
                            colb - rcum_t[hb_lane:hb_lane + 1, :])
            dt_b_row = dt_t[hb_lane:hb_lane + 1, :]
            w = (jnp.exp(seg) * jnp.where(lower, dt_t[h:h + 1, :], dt_b_row)
                 + jnp.where(diag, dt_b_row, 0.0))
            m_parts.append(cb[g] * w)
        sl = slice(pr * LANES, (pr + 1) * LANES)
        y_cols.append(_dot(jnp.concatenate(m_parts, axis=1), _block_diag_pair(xh[:, sl])))
    y = jnp.concatenate(y_cols, axis=1) + y_off + dskip_ref[...] * xh
    h_new = _carry_update(hf_in, dt_all * jnp.exp(atot - acum), atot, ef_ref, xh, b_all)
    return y, h_new


def _tiling(x, tm, seq_len):
    s_n, l_n, _ = x.shape
    return s_n, l_n, l_n // tm, seq_len // tm, l_n // seq_len


FRONT_SEGS = (SSD_CONV_DIM, DT_PAD)


def _front_kernel(*refs, has_pos, has_h0, tiles_per_seq, n_tiles):
    x_ref, xp_ref, xn_ref = refs[:3]
    refs = refs[3:]
    if has_pos:
        embr_ref, embc_ref = refs[:2]
        refs = refs[2:]
    if has_h0:
        h0b_ref = refs[0]
        refs = refs[1:]
    mod_ref, w_ref, convw_ref, convb_ref, dtbias_ref, aneg_ref, eb_ref = refs[:7]
    xa_ref, dt_ref, hbs_ref, hfb_ref, hb_s = refs[7:]
    shift1 = mod_ref[0, :, 0:D_MODEL]
    scale1 = mod_ref[0, :, D_MODEL:2 * D_MODEL]
    tm = x_ref.shape[1]
    n_ext = tm + 2 * HALO
    tile = n_tiles - 1 - pl.program_id(1)
    t = tile % tiles_per_seq

    def modulate(x, pos):
        if pos is not None:
            x = x + pos
        return (x * (1.0 + scale1) + shift1).astype(BF16)

    pos = pos_prev = pos_next = None
    if has_pos:
        rows_per_tile = tm // GRID_W
        n_grid_rows = embr_ref.shape[0]
        r0 = tile * rows_per_tile
        pos = _pos_rows(embr_ref, embc_ref, r0, rows_per_tile)
        pos_prev = _pos_halo(embr_ref, embc_ref, jnp.maximum(r0 - 1, 0), GRID_W - HALO)
        pos_next = _pos_halo(embr_ref, embc_ref,
                             jnp.minimum(r0 + rows_per_tile, n_grid_rows - 1), 0)
    h = modulate(x_ref[0], pos)
    h_ext = jnp.concatenate([modulate(xp_ref[0], pos_prev), h, modulate(xn_ref[0], pos_next)],
                            axis=0)

    row = lax.broadcasted_iota(jnp.int32, (tm, 1), 0)
    zero_prev = row == 0
    zero_next = row == tm - 1
    if tiles_per_seq > 1:
        zero_prev = jnp.logical_and(zero_prev, t == 0)
        zero_next = jnp.logical_and(zero_next, t == tiles_per_seq - 1)

    for c0 in range(0, SSD_CONV_DIM, IN_NTILE):
        c1 = c0 + IN_NTILE
        p = jnp.dot(h_ext, w_ref[:, c0:c1], preferred_element_type=F32)
        cur = p[HALO:HALO + tm]
        prv = jnp.where(zero_prev, 0.0, pltpu.roll(p, 1, 0)[HALO:HALO + tm])
        nxt = jnp.where(zero_next, 0.0, pltpu.roll(p, n_ext - 1, 0)[HALO:HALO + tm])
        xc = (convw_ref[0:1, c0:c1] * prv + convw_ref[1:2, c0:c1] * cur
              + convw_ref[2:3, c0:c1] * nxt + convb_ref[:, c0:c1])
        xa_ref[0, :, c0:c1] = xc * jax.nn.sigmoid(xc)
    dt_ref[0] = jnp.dot(h, w_ref[:, SSD_CONV_DIM:], preferred_element_type=F32)

    def init_state():
        if has_h0:
            hb_s[...] = h0b_ref[0].T
        else:
            hb_s[...] = jnp.zeros_like(hb_s)

    if tiles_per_seq > 1:
        pl.when(t == tiles_per_seq - 1)(init_state)
    else:
        init_state()
    h_state = hb_s[...]
    q = SSD_CHUNK
    for ch in reversed(range(tm // q)):
        rows = slice(ch * q, (ch + 1) * q)
        hbs_ref[0, ch] = h_state
        dt_all, _, _, _, rcum, rtot = _chunk_decay(dt_ref[0, rows, :], dtbias_ref, aneg_ref)
        h_state = _carry_update(h_state, dt_all * jnp.exp(rtot - rcum), rtot, eb_ref,
                                xa_ref[0, rows, 0:D_SSD], xa_ref[0, rows, D_SSD:D_SSD + BC_COLS])
    hb_s[...] = h_state

    def final_state():
        hfb_ref[0] = h_state.T

    if tiles_per_seq > 1:
        pl.when(t == 0)(final_state)
    else:
        final_state()


def _front(x, pos, mods, h0b, w_front, convw, convb, dtbias, aneg, e_b, mod_row0, tm, seq_len):
    s_n, l_n, n_tiles, tiles_per_seq, seqs_per_row = _tiling(x, tm, seq_len)
    has_pos = pos is not None
    has_h0 = h0b is not None
    hb = tm // HALO
    n_hb = l_n // HALO
    cpt = tm // SSD_CHUNK
    fwd = lambda j: n_tiles - 1 - j
    tok = lambda s, j: (s, fwd(j), 0)
    seq_of = lambda s, j: (s * seqs_per_row + fwd(j) // tiles_per_seq, 0, 0)
    in_specs = [pl.BlockSpec((1, tm, D_MODEL), tok),
                pl.BlockSpec((1, HALO, D_MODEL),
                             lambda s, j: (s, jnp.maximum(fwd(j) * hb - 1, 0), 0)),
                pl.BlockSpec((1, HALO, D_MODEL),
                             lambda s, j: (s, jnp.minimum((fwd(j) + 1) * hb, n_hb - 1), 0))]
    args = [x, x, x]
    if has_pos:
        in_specs += [_const_spec(pos[0].shape), _const_spec(pos[1].shape)]
        args += list(pos)
    state_spec = pl.BlockSpec((1, D_SSD, SSD_STATE), seq_of)
    if has_h0:
        in_specs.append(state_spec)
        args.append(h0b)
    in_specs.append(pl.BlockSpec((1, 1, 6 * D_MODEL), lambda s, j: (mod_row0 + s, 0, 0)))
    args.append(mods)
    for a in (w_front, convw, convb, dtbias, aneg, e_b):
        in_specs.append(_const_spec(a.shape))
        args.append(a)
    return pl.pallas_call(
        functools.partial(_front_kernel, has_pos=has_pos, has_h0=has_h0,
                          tiles_per_seq=tiles_per_seq, n_tiles=n_tiles),
        grid=(s_n, n_tiles),
        in_specs=in_specs,
        out_specs=[pl.BlockSpec((1, tm, w), tok) for w in FRONT_SEGS]
        + [pl.BlockSpec((1, cpt, SSD_STATE, D_SSD), lambda s, j: (s, fwd(j), 0, 0)),
           state_spec],
        out_shape=[jax.ShapeDtypeStruct((s_n, l_n, w), F32) for w in FRONT_SEGS]
        + [jax.ShapeDtypeStruct((s_n, l_n // SSD_CHUNK, SSD_STATE, D_SSD), F32),
           jax.ShapeDtypeStruct((s_n * seqs_per_row, D_SSD, SSD_STATE), F32)],
        scratch_shapes=[pltpu.VMEM((SSD_STATE, D_SSD), F32)],
        compiler_params=_params(48, 2),
        name="front",
    )(*args)


MAIN_Z, MAIN_U, MAIN_V, MAIN_GA, MAIN_GB = (i * D_MODEL for i in range(5))
MAIN_COLS = 5 * D_MODEL


def _main_kernel(*refs, has_pos, has_h0, tiles_per_seq):
    x_ref, xa_ref, dtp_ref, hbs_ref = refs[:4]
    refs = refs[4:]
    if has_pos:
        embr_ref, embc_ref = refs[:2]
        refs = refs[2:]
    if has_h0:
        h0f_ref = refs[0]
        refs = refs[1:]
    (mod_ref, w_ref, dtbias_ref, aneg_ref, dskip_ref, normw_ref, ef_ref, eb_ref, lng_ref,
     lnb_ref, wsp_ref, bsp_ref, wpa_ref, wpb_ref, wout_ref, ln1g_ref, ln1b_ref) = refs[:17]
    o_ref, hff_ref, hf_s = refs[17:]
    tm = x_ref.shape[1]
    q = SSD_CHUNK
    tile = pl.program_id(1)
    t = tile % tiles_per_seq

    x = x_ref[0]
    if has_pos:
        rows_per_tile = tm // GRID_W
        x = x + _pos_rows(embr_ref, embc_ref, tile * rows_per_tile, rows_per_tile)
    shift1 = mod_ref[0, :, 0:D_MODEL]
    scale1 = mod_ref[0, :, D_MODEL:2 * D_MODEL]
    gate1 = mod_ref[0, :, 2 * D_MODEL:3 * D_MODEL]
    h = (x * (1.0 + scale1) + shift1).astype(BF16)

    def in_proj(c0):
        return jnp.concatenate(
            [jnp.dot(h, w_ref[:, c:c + IN_NTILE], preferred_element_type=F32)
             for c in range(c0, c0 + D_MODEL, IN_NTILE)], axis=1)

    def init_state():
        if has_h0:
            hf_s[...] = h0f_ref[0].T
        else:
            hf_s[...] = jnp.zeros_like(hf_s)

    if tiles_per_seq > 1:
        pl.when(t == 0)(init_state)
    else:
        init_state()
    h_state = hf_s[...]
    zz = in_proj(MAIN_Z)
    zz = zz * jax.nn.sigmoid(zz)
    ya = []
    for ch in range(tm // q):
        rows = slice(ch * q, (ch + 1) * q)
        y, h_state = _ssd_chunk(
            xa_ref[0, rows, 0:D_SSD], xa_ref[0, rows, D_SSD:D_SSD + BC_COLS],
            xa_ref[0, rows, D_SSD + BC_COLS:], dtp_ref[0, rows, :], h_state, hbs_ref[0, ch],
            dtbias_ref, aneg_ref, dskip_ref, ef_ref, eb_ref)
        yg = y * zz[rows]
        ms = jnp.mean(yg * yg, axis=-1, keepdims=True)
        ya.append(yg * lax.rsqrt(ms + EPS) * normw_ref[...])
    hf_s[...] = h_state

    def final_state():
        hff_ref[0] = h_state.T

    if tiles_per_seq > 1:
        pl.when(t == tiles_per_seq - 1)(final_state)
    else:
        final_state()
    ya = jnp.concatenate(ya, axis=0)

    u = _gelu(in_proj(MAIN_U))
    v = _layer_norm(_gelu(in_proj(MAIN_V)), lng_ref[...], lnb_ref[...])
    rows_out = []
    for ch in range(tm // SGU_CHUNK):
        vc = v[ch * SGU_CHUNK:(ch + 1) * SGU_CHUNK, :]
        cols = [_dot(wsp_ref[g], vc[:, g * SGU_GROUP_DIM:(g + 1) * SGU_GROUP_DIM])
                for g in range(SGU_GROUPS)]
        rows_out.append(jnp.concatenate(cols, axis=1) + bsp_ref[...])
    yb = u * jnp.concatenate(rows_out, axis=0)

    merged = (jax.nn.sigmoid(in_proj(MAIN_GA)) * _dot(ya, wpa_ref[...])
              + jax.nn.sigmoid(in_proj(MAIN_GB)) * _dot(yb, wpb_ref[...]))
    o = _dot(merged, wout_ref[...])
    o_ref[0] = _layer_norm(DN_ALPHA * x + gate1 * o, ln1g_ref[...], ln1b_ref[...])


def _main(x, pos, xa, dtp, hbs, h0f, mods, mod_row0, tm, seq_len, consts):
    s_n, l_n, n_tiles, tiles_per_seq, seqs_per_row = _tiling(x, tm, seq_len)
    has_pos = pos is not None
    has_h0 = h0f is not None
    cpt = tm // SSD_CHUNK
    tok = lambda s, j: (s, j, 0)
    in_specs = [pl.BlockSpec((1, tm, D_MODEL), tok),
                pl.BlockSpec((1, tm, SSD_CONV_DIM), tok),
                pl.BlockSpec((1, tm, DT_PAD), tok),
                pl.BlockSpec((1, cpt, SSD_STATE, D_SSD), lambda s, j: (s, j, 0, 0))]
    args = [x, xa, dtp, hbs]
    if has_pos:
        in_specs += [_const_spec(pos[0].shape), _const_spec(pos[1].shape)]
        args += list(pos)
    state_spec = pl.BlockSpec((1, D_SSD, SSD_STATE),
                              lambda s, j: (s * seqs_per_row + j // tiles_per_seq, 0, 0))
    if has_h0:
        in_specs.append(state_spec)
        args.append(h0f)
    in_specs.append(pl.BlockSpec((1, 1, 6 * D_MODEL), lambda s, j: (mod_row0 + s, 0, 0)))
    args.append(mods)
    for a in consts:
        in_specs.append(_const_spec(a.shape))
        args.append(a)
    return pl.pallas_call(
        functools.partial(_main_kernel, has_pos=has_pos, has_h0=has_h0,
                          tiles_per_seq=tiles_per_seq),
        grid=(s_n, n_tiles),
        in_specs=in_specs,
        out_specs=[pl.BlockSpec((1, tm, D_MODEL), tok), state_spec],
        out_shape=[jax.ShapeDtypeStruct((s_n, l_n, D_MODEL), F32),
                   jax.ShapeDtypeStruct((s_n * seqs_per_row, D_SSD, SSD_STATE), F32)],
        scratch_shapes=[pltpu.VMEM((SSD_STATE, D_SSD), F32)],
        compiler_params=_params(56, 2),
        name="main",
    )(*args)


FF_TILE = 1024


def _ffn_kernel(x_ref, mod_ref, w1_ref, w2_ref, g_ref, b_ref, o_ref):
    x = x_ref[0]
    shift2 = mod_ref[0, :, 3 * D_MODEL:4 * D_MODEL]
    scale2 = mod_ref[0, :, 4 * D_MODEL:5 * D_MODEL]
    gate2 = mod_ref[0, :, 5 * D_MODEL:6 * D_MODEL]
    h = (x * (1.0 + scale2) + shift2).astype(BF16)
    f = jnp.zeros_like(x)
    for k0 in range(0, D_FF, FF_TILE):
        a = jnp.dot(h, w1_ref[:, k0:k0 + FF_TILE], preferred_element_type=F32)
        a = jnp.maximum(a, 0.0)
        f = f + _dot(a * a, w2_ref[k0:k0 + FF_TILE, :])
    o_ref[0] = _layer_norm(DN_ALPHA * x + gate2 * f, g_ref[...], b_ref[...])


def _ffn(x, mods, mod_row0, tm, w1, w2, g, b):
    s_n, l_n, _ = x.shape
    tok = lambda s, j: (s, j, 0)
    return pl.pallas_call(
        _ffn_kernel,
        grid=(s_n, l_n // tm),
        in_specs=[pl.BlockSpec((1, tm, D_MODEL), tok),
                  pl.BlockSpec((1, 1, 6 * D_MODEL), lambda s, j: (mod_row0 + s, 0, 0)),
                  _const_spec(w1.shape), _const_spec(w2.shape),
                  _const_spec(g.shape), _const_spec(b.shape)],
        out_specs=pl.BlockSpec((1, tm, D_MODEL), tok),
        out_shape=jax.ShapeDtypeStruct((s_n, l_n, D_MODEL), F32),
        compiler_params=_params(48, 2),
        name="ffn",
    )(x, mods, w1, w2, g, b)


def _grid_pos_tables(n_tokens):
    rows = n_tokens // GRID_W
    quarter = D_MODEL // 4
    omega = 1.0 / (POS_BASE ** (jnp.arange(quarter, dtype=F32) / quarter))
    r = jnp.arange(rows, dtype=F32)[:, None] * omega
    col = jnp.arange(GRID_W, dtype=F32)[:, None] * omega
    emb_r = jnp.concatenate([jnp.sin(r), jnp.cos(r)], axis=-1)
    emb_c = jnp.concatenate([jnp.sin(col), jnp.cos(col)], axis=-1)
    return emb_r, emb_c


def _dt_lanes(v, axis):
    pad = [(0, 0)] * v.ndim
    pad[axis] = (0, DT_PAD - DT_COPIES * DT_GROUP)
    return jnp.pad(jnp.concatenate([v] * DT_COPIES, axis=axis), pad)


def _head_expanders():
    lane = jnp.arange(LANES)[:, None]
    head = jnp.arange(D_SSD)[None, :] // SSD_HEAD_DIM
    valid = lane < DT_COPIES * DT_GROUP
    e_f = jnp.logical_and(valid, lane % DT_GROUP == head)
    e_b = jnp.logical_and(valid, lane % DT_GROUP == head + SSD_HEADS)
    return e_f.astype(BF16), e_b.astype(BF16)


def _stream(x, pos, mods, mod_row0, seq_len, h0f, h0b, prm, tm):
    xa, dtp, hbs, hb = _front(x, pos, mods, h0b, prm["w_front"], prm["conv_w"], prm["conv_b"],
                              prm["dt_bias"], prm["a_neg"], prm["e_b"], mod_row0, tm, seq_len)
    x1, hf = _main(x, pos, xa, dtp, hbs, h0f, mods, mod_row0, tm, seq_len, prm["main"])
    y = _ffn(x1, mods, mod_row0, tm, prm["w_ff1"], prm["w_ff2"], prm["ln2_g"], prm["ln2_b"])
    return y, hf, hb


def kernel(x_prompt, x_sample, state_ssd_fwd, state_ssd_bwd, c, c_ctx, w_ada, b_ada, w_in, conv_w, conv_b, dt_bias_fwd, dt_bias_bwd, a_log_fwd, a_log_bwd, d_skip, ssd_norm_w, sgu_ln_g, sgu_ln_b, w_spatial, b_spatial, w_proj_a, w_proj_b, w_out, ln1_g, ln1_b, w_ff1, w_ff2, ln2_g, ln2_b):
    batch, seq, _ = x_prompt.shape
    dec_batch, dec_seq, _ = x_sample.shape
    l = 0
    row = lambda v: v.reshape(1, -1)

    cond = jnp.zeros((SUBLANES, D_MODEL), F32).at[0].set(c_ctx).at[1:1 + dec_batch].set(c)
    mods = _ada(cond, w_ada[l], b_ada[l]).reshape(SUBLANES, 1, 6 * D_MODEL)

    o1 = D_SSD
    o2 = o1 + SSD_CONV_DIM
    o3 = o2 + DT_GROUP
    w = w_in[l]
    w_front = jnp.concatenate([w[:, o1:o2], _dt_lanes(w[:, o2:o3], 1)], axis=1).astype(BF16)
    w_main = jnp.concatenate([w[:, :o1], w[:, o3:]], axis=1).astype(BF16)
    e_f, e_b = _head_expanders()
    dt_bias = row(_dt_lanes(jnp.concatenate([dt_bias_fwd[l], dt_bias_bwd[l]]), 0))
    a_neg = row(_dt_lanes(-jnp.exp(jnp.concatenate([a_log_fwd[l], a_log_bwd[l]])), 0))

    prm = dict(
        w_front=w_front, conv_w=conv_w[l], conv_b=row(conv_b[l]),
        dt_bias=dt_bias, a_neg=a_neg, e_b=e_b,
        main=(w_main, dt_bias, a_neg, row(jnp.repeat(d_skip[l], SSD_HEAD_DIM)),
              row(ssd_norm_w[l]), e_f, e_b, row(sgu_ln_g[l]), row(sgu_ln_b[l]),
              w_spatial[l].astype(BF16), jnp.repeat(b_spatial[l].T, SGU_GROUP_DIM, axis=1),
              w_proj_a[l].astype(BF16), w_proj_b[l].astype(BF16), w_out[l].astype(BF16),
              row(ln1_g[l]), row(ln1_b[l])),
        w_ff1=w_ff1[l].astype(BF16), w_ff2=w_ff2[l].astype(BF16),
        ln2_g=row(ln2_g[l]), ln2_b=row(ln2_b[l]),
    )

    tm = 256
    xp, hf, hb = _stream(x_prompt.reshape(1, batch * seq, D_MODEL), None, mods, 0, seq,
                         None, None, prm, tm)
    pos = _grid_pos_tables(dec_seq)
    h0f = state_ssd_fwd[:, l].reshape(dec_batch, D_SSD, SSD_STATE)
    h0b = state_ssd_bwd[:, l].reshape(dec_batch, D_SSD, SSD_STATE)
    xs, _, _ = _stream(x_sample, pos, mods, 1, dec_seq, h0f, h0b, prm, tm)

    state_shape = (batch, DEPTH, SSD_HEADS, SSD_HEAD_DIM, SSD_STATE)
    return (xp.reshape(batch, seq, D_MODEL), xs,
            hf.reshape(state_shape), hb.reshape(state_shape))
```

```python
import functools
import math

import jax
import jax.numpy as jnp
from jax import lax
from jax.experimental import pallas as pl
from jax.experimental.pallas import tpu as pltpu

D_MODEL = 1024
GRID_W = 64
POS_BASE = 10000.0
D_SSD = 1024
SSD_HEAD_DIM = 64
SSD_HEADS = D_SSD // SSD_HEAD_DIM
SSD_GROUPS = 2
SSD_STATE = 128
SSD_CHUNK = 128
SSD_CONV_DIM = D_SSD + 2 * SSD_GROUPS * SSD_STATE
D_SGU = 1024
SGU_GROUPS = 8
SGU_GROUP_DIM = D_SGU // SGU_GROUPS
SGU_CHUNK = 128
D_FF = 4 * D_MODEL
DEPTH = 1
DN_ALPHA = (2.0 * DEPTH) ** 0.25
EPS = 1e-5

LANES = 128
SUBLANES = 8
BF16_ROWS = 16

DT_GROUP = 2 * SSD_HEADS
DT_COPIES = 3
DT_PAD = LANES
HEAD_PAIRS = SSD_HEADS // 2
PAIRS_PER_GROUP = HEAD_PAIRS // SSD_GROUPS
GROUP_COLS = D_SSD // SSD_GROUPS
BC_COLS = SSD_GROUPS * SSD_STATE

BF16 = jnp.bfloat16
F32 = jnp.float32
HIGHEST = lax.Precision.HIGHEST


def _dot(a, b):
    return jnp.dot(a.astype(BF16), b.astype(BF16), preferred_element_type=F32)


def _dot_exact(a, b):
    return jnp.dot(a, b, precision=HIGHEST, preferred_element_type=F32)


def _layer_norm(x, g, b):
    mu = jnp.mean(x, axis=-1, keepdims=True)
    xc = x - mu
    var = jnp.mean(xc * xc, axis=-1, keepdims=True)
    return xc * lax.rsqrt(var + EPS) * g + b


def _softplus(x):
    return jnp.maximum(x, 0.0) + jnp.log1p(jnp.exp(-jnp.abs(x)))


def _gelu(x):
    return 0.5 * x * (1.0 + lax.erf(x * (1.0 / math.sqrt(2.0))))


def _const_spec(shape):
    zeros = (0,) * len(shape)
    return pl.BlockSpec(shape, lambda *_: zeros, pipeline_mode=pl.Buffered(1))


def _params(vmem_mb, n_axes):
    return pltpu.CompilerParams(
        dimension_semantics=("arbitrary",) * n_axes,
        vmem_limit_bytes=vmem_mb * 1024 * 1024)


def _ada_kernel(c_ref, w_ref, b_ref, o_ref):
    c = c_ref[...]
    o_ref[...] = _dot_exact(c * jax.nn.sigmoid(c), w_ref[...]) + b_ref[...]


def _ada(cond, w_ada, b_ada):
    n = w_ada.shape[1]
    tn = 1024
    return pl.pallas_call(
        _ada_kernel,
        grid=(n // tn,),
        in_specs=[pl.BlockSpec((SUBLANES, D_MODEL), lambda j: (0, 0)),
                  pl.BlockSpec((D_MODEL, tn), lambda j: (0, j)),
                  pl.BlockSpec((1, tn), lambda j: (0, j))],
        out_specs=pl.BlockSpec((SUBLANES, tn), lambda j: (0, j)),
        out_shape=jax.ShapeDtypeStruct((SUBLANES, n), F32),
        compiler_params=_params(32, 1),
        name="ada",
    )(cond, w_ada, b_ada.reshape(1, n))


IN_NTILE = 512
HALO = BF16_ROWS


def _pos_rows(embr_ref, embc_ref, grid_row0, n_rows):
    left = [jnp.broadcast_to(embr_ref[pl.ds(grid_row0 + i, 1), :], (GRID_W, D_MODEL // 2))
            for i in range(n_rows)]
    right = [embc_ref[...]] * n_rows
    return jnp.concatenate([jnp.concatenate(left, axis=0), jnp.concatenate(right, axis=0)],
                           axis=1)


def _pos_halo(embr_ref, embc_ref, grid_row, col0):
    left = jnp.broadcast_to(embr_ref[pl.ds(grid_row, 1), :], (HALO, D_MODEL // 2))
    return jnp.concatenate([left, embc_ref[col0:col0 + HALO, :]], axis=1)


def _block_diag_pair(xp):
    lane = lax.broadcasted_iota(jnp.int32, xp.shape, 1)
    first = lane < SSD_HEAD_DIM
    return jnp.concatenate([jnp.where(first, xp, 0.0), jnp.where(first, 0.0, xp)], axis=0)


def _split_terms(v):
    lane = lax.broadcasted_iota(jnp.int32, v.shape, 1)
    hi = v.astype(BF16).astype(F32)
    r1 = v - hi
    mid = r1.astype(BF16).astype(F32)
    lo = r1 - mid
    return jnp.where(lane < DT_GROUP, hi, jnp.where(lane < 2 * DT_GROUP, mid, lo)).astype(BF16)


def _expand(v, e_ref):
    return jnp.dot(_split_terms(v), e_ref[...], preferred_element_type=F32)


def _chunk_decay(dtp, dtbias_ref, aneg_ref):
    q = dtp.shape[0]
    dt_all = _softplus(dtp + dtbias_ref[...])
    a_all = dt_all * aneg_ref[...]
    ii = lax.broadcasted_iota(jnp.int32, (q, q), 0)
    jj = lax.broadcasted_iota(jnp.int32, (q, q), 1)
    acum = _dot_exact((ii >= jj).astype(F32), a_all)
    atot = acum[q - 1:q, :]
    rcum = atot - acum + a_all
    return dt_all, a_all, acum, atot, rcum, rcum[0:1, :]


def _carry_update(h_old, w_tok, tot_row, e_ref, xh, b_all):
    wx = (_expand(w_tok, e_ref) * xh).astype(BF16)
    s = jnp.concatenate(
        [jnp.dot(b_all[:, g * SSD_STATE:(g + 1) * SSD_STATE].T.astype(BF16),
                 wx[:, g * GROUP_COLS:(g + 1) * GROUP_COLS], preferred_element_type=F32)
         for g in range(SSD_GROUPS)], axis=1)
    decay = jnp.exp(_expand(jnp.broadcast_to(tot_row, (BF16_ROWS, LANES)), e_ref))[0:1, :]
    return decay * h_old + s


def _ssd_chunk(xh, b_all, c_all, dtp, hf_in, hb_in, dtbias_ref, aneg_ref, dskip_ref, ef_ref,
               eb_ref, between_pairs=lambda: None):
    q = SSD_CHUNK
    dt_all, a_all, acum, atot, rcum, _ = _chunk_decay(dtp, dtbias_ref, aneg_ref)
    ii = lax.broadcasted_iota(jnp.int32, (q, q), 0)
    jj = lax.broadcasted_iota(jnp.int32, (q, q), 1)
    lower = ii >= jj
    diag = ii == jj
    dt_t = dt_all.T
    a_t = a_all.T
    acum_t = _dot_exact(a_t, (ii <= jj).astype(F32))
    rcum_t = acum_t[:, q - 1:q] - acum_t + a_t

    ef = jnp.exp(_expand(acum, ef_ref))
    eb = jnp.exp(_expand(rcum, eb_ref))
    y_off = []
    cb = []
    for g in range(SSD_GROUPS):
        gs = slice(g * GROUP_COLS, (g + 1) * GROUP_COLS)
        c_g = c_all[:, g * SSD_STATE:(g + 1) * SSD_STATE].astype(BF16)
        b_g = b_all[:, g * SSD_STATE:(g + 1) * SSD_STATE].astype(BF16)
        cb.append(lax.dot_general(c_g, b_g, (((1,), (1,)), ((), ())),
                                  preferred_element_type=F32))
        y_off.append(ef[:, gs] * jnp.dot(c_g, hf_in[:, gs].astype(BF16),
                                         preferred_element_type=F32)
                     + eb[:, gs] * jnp.dot(c_g, hb_in[:, gs].astype(BF16),
                                           preferred_element_type=F32))
    y_off = jnp.concatenate(y_off, axis=1)

    y_cols = []
    for pr in range(HEAD_PAIRS):
        g = pr // PAIRS_PER_GROUP
        m_parts = []
        for h in (2 * pr, 2 * pr + 1):
            hb_lane = SSD_HEADS + h
            colf = jnp.broadcast_to(acum[:, h:h + 1], (q, q))
            colb = jnp.broadcast_to(rcum[:, hb_lane:hb_lane + 1], (q, q))
            seg = jnp.where(lower, colf - acum_t[h:h + 1, :],
                            colb - rcum_t[hb_lane:hb_lane + 1, :])
            dt_b_row = dt_t[hb_lane:hb_lane + 1, :]
            w = (jnp.exp(seg) * jnp.where(lower, dt_t[h:h + 1, :], dt_b_row)
                 + jnp.where(diag, dt_b_row, 0.0))
            m_parts.append(cb[g] * w)
        sl = slice(pr * LANES, (pr + 1) * LANES)
        y_cols.append(_dot(jnp.concatenate(m_parts, axis=1), _block_diag_pair(xh[:, sl])))
        between_pairs()
    y = jnp.concatenate(y_cols, axis=1) + y_off + dskip_ref[...] * xh
    h_new = _carry_update(hf_in, dt_all * jnp.exp(atot - acum), atot, ef_ref, xh, b_all)
    return y, h_new


def _tiling(x, tm, seq_len):
    s_n, l_n, _ = x.shape
    return s_n, l_n, l_n // tm, seq_len // tm, l_n // seq_len


FRONT_SEGS = (SSD_CONV_DIM, DT_PAD)
FRONT_NTILE = 256


def _front_kernel(*refs, has_pos, has_h0, tiles_per_seq, n_tiles):
    x_ref, xp_ref, xn_ref = refs[:3]
    refs = refs[3:]
    if has_pos:
        embr_ref, embc_ref = refs[:2]
        refs = refs[2:]
    if has_h0:
        h0b_ref = refs[0]
        refs = refs[1:]
    mod_ref, w_ref, convw_ref, convb_ref, dtbias_ref, aneg_ref, eb_ref = refs[:7]
    xa_ref, dt_ref, hbs_ref, hfb_ref, hb_s, xb_s, dts_s = refs[7:]
    shift1 = mod_ref[0, :, 0:D_MODEL]
    scale1 = mod_ref[0, :, D_MODEL:2 * D_MODEL]
    tm = x_ref.shape[1]
    n_ext = tm + 2 * HALO
    q = SSD_CHUNK
    j = pl.program_id(1)
    tile = n_tiles - 1 - jnp.minimum(j, n_tiles - 1)
    t = tile % tiles_per_seq
    t_scan = jnp.minimum(n_tiles - j, n_tiles - 1) % tiles_per_seq
    slot = j % 2
    scan_slot = 1 - slot

    @pl.when(j == 0)
    def _():
        xb_s[1] = jnp.zeros(xb_s.shape[1:], F32)
        dts_s[1] = jnp.zeros(dts_s.shape[1:], F32)

    def init_state():
        if has_h0:
            hb_s[...] = h0b_ref[0].T
        else:
            hb_s[...] = jnp.zeros_like(hb_s)

    if tiles_per_seq > 1:
        pl.when(t_scan == tiles_per_seq - 1)(init_state)
    else:
        init_state()

    def modulate(x, pos):
        if pos is not None:
            x = x + pos
        return (x * (1.0 + scale1) + shift1).astype(BF16)

    pos = pos_prev = pos_next = None
    if has_pos:
        rows_per_tile = tm // GRID_W
        n_grid_rows = embr_ref.shape[0]
        r0 = tile * rows_per_tile
        pos = _pos_rows(embr_ref, embc_ref, r0, rows_per_tile)
        pos_prev = _pos_halo(embr_ref, embc_ref, jnp.maximum(r0 - 1, 0), GRID_W - HALO)
        pos_next = _pos_halo(embr_ref, embc_ref,
                             jnp.minimum(r0 + rows_per_tile, n_grid_rows - 1), 0)
    h = modulate(x_ref[0], pos)
    h_ext = jnp.concatenate([modulate(xp_ref[0], pos_prev), h, modulate(xn_ref[0], pos_next)],
                            axis=0)

    row = lax.broadcasted_iota(jnp.int32, (tm, 1), 0)
    zero_prev = row == 0
    zero_next = row == tm - 1
    if tiles_per_seq > 1:
        zero_prev = jnp.logical_and(zero_prev, t == 0)
        zero_next = jnp.logical_and(zero_next, t == tiles_per_seq - 1)

    def project():
        dt = jnp.dot(h, w_ref[:, SSD_CONV_DIM:], preferred_element_type=F32)
        dt_ref[0] = dt
        dts_s[slot] = dt
        yield
        for c0 in range(0, SSD_CONV_DIM, FRONT_NTILE):
            c1 = c0 + FRONT_NTILE
            p = jnp.dot(h_ext, w_ref[:, c0:c1], preferred_element_type=F32)
            cur = p[HALO:HALO + tm]
            prv = jnp.where(zero_prev, 0.0, pltpu.roll(p, 1, 0)[HALO:HALO + tm])
            nxt = jnp.where(zero_next, 0.0, pltpu.roll(p, n_ext - 1, 0)[HALO:HALO + tm])
            xc = (convw_ref[0:1, c0:c1] * prv + convw_ref[1:2, c0:c1] * cur
                  + convw_ref[2:3, c0:c1] * nxt + convb_ref[:, c0:c1])
            xa = xc * jax.nn.sigmoid(xc)
            xa_ref[0, :, c0:c1] = xa
            if c0 < D_SSD + BC_COLS:
                xb_s[slot, :, c0:c1] = xa
            yield

    final = []

    def scan():
        h_state = hb_s[...]
        for ch in reversed(range(tm // q)):
            rows = slice(ch * q, (ch + 1) * q)
            hbs_ref[0, ch] = h_state
            dt_all, _, _, _, rcum, rtot = _chunk_decay(dts_s[scan_slot, rows, :], dtbias_ref,
                                                       aneg_ref)
            yield
            w_exp = _expand(dt_all * jnp.exp(rtot - rcum), eb_ref)
            decay = jnp.exp(_expand(jnp.broadcast_to(rtot, (BF16_ROWS, LANES)), eb_ref))[0:1, :]
            yield
            wx = (w_exp * xb_s[scan_slot, rows, 0:D_SSD]).astype(BF16)
            b_all = xb_s[scan_slot, rows, D_SSD:D_SSD + BC_COLS]
            parts = []
            for g in range(SSD_GROUPS):
                parts.append(jnp.dot(b_all[:, g * SSD_STATE:(g + 1) * SSD_STATE].T.astype(BF16),
                                     wx[:, g * GROUP_COLS:(g + 1) * GROUP_COLS],
                                     preferred_element_type=F32))
                yield
            h_state = decay * h_state + jnp.concatenate(parts, axis=1)
        hb_s[...] = h_state
        final.append(h_state)

    stages = [project(), scan()]
    while stages:
        for gen in list(stages):
            if next(gen, StopIteration) is StopIteration:
                stages.remove(gen)

    def final_state():
        hfb_ref[0] = final[0].T

    if tiles_per_seq > 1:
        pl.when(t_scan == 0)(final_state)
    else:
        final_state()


def _front(x, pos, mods, h0b, w_front, convw, convb, dtbias, aneg, e_b, mod_row0, tm, seq_len):
    s_n, l_n, n_tiles, tiles_per_seq, seqs_per_row = _tiling(x, tm, seq_len)
    has_pos = pos is not None
    has_h0 = h0b is not None
    hb = tm // HALO
    n_hb = l_n // HALO
    cpt = tm // SSD_CHUNK
    fwd = lambda j: n_tiles - 1 - jnp.minimum(j, n_tiles - 1)
    scanned = lambda j: jnp.minimum(n_tiles - j, n_tiles - 1)
    tok = lambda s, j: (s, fwd(j), 0)
    seq_of = lambda s, j: (s * seqs_per_row + scanned(j) // tiles_per_seq, 0, 0)
    in_specs = [pl.BlockSpec((1, tm, D_MODEL), tok),
                pl.BlockSpec((1, HALO, D_MODEL),
                             lambda s, j: (s, jnp.maximum(fwd(j) * hb - 1, 0), 0)),
                pl.BlockSpec((1, HALO, D_MODEL),
                             lambda s, j: (s, jnp.minimum((fwd(j) + 1) * hb, n_hb - 1), 0))]
    args = [x, x, x]
    if has_pos:
        in_specs += [_const_spec(pos[0].shape), _const_spec(pos[1].shape)]
        args += list(pos)
    state_spec = pl.BlockSpec((1, D_SSD, SSD_STATE), seq_of)
    if has_h0:
        in_specs.append(state_spec)
        args.append(h0b)
    in_specs.append(pl.BlockSpec((1, 1, 6 * D_MODEL), lambda s, j: (mod_row0 + s, 0, 0)))
    args.append(mods)
    for a in (w_front, convw, convb, dtbias, aneg, e_b):
        in_specs.append(_const_spec(a.shape))
        args.append(a)
    return pl.pallas_call(
        functools.partial(_front_kernel, has_pos=has_pos, has_h0=has_h0,
                          tiles_per_seq=tiles_per_seq, n_tiles=n_tiles),
        grid=(s_n, n_tiles + 1),
        in_specs=in_specs,
        out_specs=[pl.BlockSpec((1, tm, w), tok) for w in FRONT_SEGS]
        + [pl.BlockSpec((1, cpt, SSD_STATE, D_SSD), lambda s, j: (s, scanned(j), 0, 0)),
           state_spec],
        out_shape=[jax.ShapeDtypeStruct((s_n, l_n, w), F32) for w in FRONT_SEGS]
        + [jax.ShapeDtypeStruct((s_n, l_n // SSD_CHUNK, SSD_STATE, D_SSD), F32),
           jax.ShapeDtypeStruct((s_n * seqs_per_row, D_SSD, SSD_STATE), F32)],
        scratch_shapes=[pltpu.VMEM((SSD_STATE, D_SSD), F32),
                        pltpu.VMEM((2, tm, D_SSD + BC_COLS), F32),
                        pltpu.VMEM((2, tm, DT_PAD), F32)],
        compiler_params=_params(48, 2),
        name="front",
    )(*args)


MAIN_Z, MAIN_U, MAIN_V, MAIN_GA, MAIN_GB = (i * D_MODEL for i in range(5))
MAIN_COLS = 5 * D_MODEL


def _main_kernel(*refs, has_pos, has_h0, tiles_per_seq):
    x_ref, xa_ref, dtp_ref, hbs_ref = refs[:4]
    refs = refs[4:]
    if has_pos:
        embr_ref, embc_ref = refs[:2]
        refs = refs[2:]
    if has_h0:
        h0f_ref = refs[0]
        refs = refs[1:]
    (mod_ref, w_ref, dtbias_ref, aneg_ref, dskip_ref, normw_ref, ef_ref, eb_ref, lng_ref,
     lnb_ref, wsp_ref, bsp_ref, wpa_ref, wpb_ref, wout_ref, ln1g_ref, ln1b_ref) = refs[:17]
    o_ref, hff_ref, hf_s = refs[17:]
    tm = x_ref.shape[1]
    q = SSD_CHUNK
    tile = pl.program_id(1)
    t = tile % tiles_per_seq

    x = x_ref[0]
    if has_pos:
        rows_per_tile = tm // GRID_W
        x = x + _pos_rows(embr_ref, embc_ref, tile * rows_per_tile, rows_per_tile)
    shift1 = mod_ref[0, :, 0:D_MODEL]
    scale1 = mod_ref[0, :, D_MODEL:2 * D_MODEL]
    gate1 = mod_ref[0, :, 2 * D_MODEL:3 * D_MODEL]
    h = (x * (1.0 + scale1) + shift1).astype(BF16)

    col_tiles = {}
    pending = list(range(0, MAIN_COLS, IN_NTILE))
    n_slots = (tm // q) * HEAD_PAIRS
    n_fill = (MAIN_COLS - MAIN_U) // IN_NTILE
    slot = [0]

    def project_next():
        c = pending.pop(0)
        col_tiles[c] = jnp.dot(h, w_ref[:, c:c + IN_NTILE], preferred_element_type=F32)

    def between_pairs():
        slot[0] += 1
        done = n_fill - len(pending)
        for _ in range(-(-slot[0] * n_fill // n_slots) - done):
            project_next()

    def in_proj(c0):
        while pending and pending[0] < c0 + D_MODEL:
            project_next()
        return jnp.concatenate([col_tiles.pop(c) for c in range(c0, c0 + D_MODEL, IN_NTILE)],
                               axis=1)

    def init_state():
        if has_h0:
            hf_s[...] = h0f_ref[0].T
        else:
            hf_s[...] = jnp.zeros_like(hf_s)

    if tiles_per_seq > 1:
        pl.when(t == 0)(init_state)
    else:
        init_state()
    h_state = hf_s[...]
    zz = in_proj(MAIN_Z)
    zz = zz * jax.nn.sigmoid(zz)
    ya = []
    for ch in range(tm // q):
        rows = slice(ch * q, (ch + 1) * q)
        y, h_state = _ssd_chunk(
            xa_ref[0, rows, 0:D_SSD], xa_ref[0, rows, D_SSD:D_SSD + BC_COLS],
            xa_ref[0, rows, D_SSD + BC_COLS:], dtp_ref[0, rows, :], h_state, hbs_ref[0, ch],
            dtbias_ref, aneg_ref, dskip_ref, ef_ref, eb_ref, between_pairs)
        yg = y * zz[rows]
        ms = jnp.mean(yg * yg, axis=-1, keepdims=True)
        ya.append(yg * lax.rsqrt(ms + EPS) * normw_ref[...])
    hf_s[...] = h_state

    def final_state():
        hff_ref[0] = h_state.T

    if tiles_per_seq > 1:
        pl.when(t == tiles_per_seq - 1)(final_state)
    else:
        final_state()
    ya = jnp.concatenate(ya, axis=0)

    u = _gelu(in_proj(MAIN_U))
    v = _layer_norm(_gelu(in_proj(MAIN_V)), lng_ref[...], lnb_ref[...])
    rows_out = []
    for ch in range(tm // SGU_CHUNK):
        vc = v[ch * SGU_CHUNK:(ch + 1) * SGU_CHUNK, :]
        cols = [_dot(wsp_ref[g], vc[:, g * SGU_GROUP_DIM:(g + 1) * SGU_GROUP_DIM])
                for g in range(SGU_GROUPS)]
        rows_out.append(jnp.concatenate(cols, axis=1) + bsp_ref[...])
    yb = u * jnp.concatenate(rows_out, axis=0)

    merged = (jax.nn.sigmoid(in_proj(MAIN_GA)) * _dot(ya, wpa_ref[...])
              + jax.nn.sigmoid(in_proj(MAIN_GB)) * _dot(yb, wpb_ref[...]))
    o = _dot(merged, wout_ref[...])
    o_ref[0] = _layer_norm(DN_ALPHA * x + gate1 * o, ln1g_ref[...], ln1b_ref[...])


def _main(x, pos, xa, dtp, hbs, h0f, mods, mod_row0, tm, seq_len, consts):
    s_n, l_n, n_tiles, tiles_per_seq, seqs_per_row = _tiling(x, tm, seq_len)
    has_pos = pos is not None
    has_h0 = h0f is not None
    cpt = tm // SSD_CHUNK
    tok = lambda s, j: (s, j, 0)
    in_specs = [pl.BlockSpec((1, tm, D_MODEL), tok),
                pl.BlockSpec((1, tm, SSD_CONV_DIM), tok),
                pl.BlockSpec((1, tm, DT_PAD), tok),
                pl.BlockSpec((1, cpt, SSD_STATE, D_SSD), lambda s, j: (s, j, 0, 0))]
    args = [x, xa, dtp, hbs]
    if has_pos:
        in_specs += [_const_spec(pos[0].shape), _const_spec(pos[1].shape)]
        args += list(pos)
    state_spec = pl.BlockSpec((1, D_SSD, SSD_STATE),
                              lambda s, j: (s * seqs_per_row + j // tiles_per_seq, 0, 0))
    if has_h0:
        in_specs.append(state_spec)
        args.append(h0f)
    in_specs.append(pl.BlockSpec((1, 1, 6 * D_MODEL), lambda s, j: (mod_row0 + s, 0, 0)))
    args.append(mods)
    for a in consts:
        in_specs.append(_const_spec(a.shape))
        args.append(a)
    return pl.pallas_call(
        functools.partial(_main_kernel, has_pos=has_pos, has_h0=has_h0,
                          tiles_per_seq=tiles_per_seq),
        grid=(s_n, n_tiles),
        in_specs=in_specs,
        out_specs=[pl.BlockSpec((1, tm, D_MODEL), tok), state_spec],
        out_shape=[jax.ShapeDtypeStruct((s_n, l_n, D_MODEL), F32),
                   jax.ShapeDtypeStruct((s_n * seqs_per_row, D_SSD, SSD_STATE), F32)],
        scratch_shapes=[pltpu.VMEM((SSD_STATE, D_SSD), F32)],
        compiler_params=_params(56, 2),
        name="main",
    )(*args)


FF_TILE = 1024


def _ffn_kernel(x_ref, mod_ref, w1_ref, w2_ref, g_ref, b_ref, o_ref):
    x = x_ref[0]
    shift2 = mod_ref[0, :, 3 * D_MODEL:4 * D_MODEL]
    scale2 = mod_ref[0, :, 4 * D_MODEL:5 * D_MODEL]
    gate2 = mod_ref[0, :, 5 * D_MODEL:6 * D_MODEL]
    h = (x * (1.0 + scale2) + shift2).astype(BF16)
    f = jnp.zeros_like(x)
    for k0 in range(0, D_FF, FF_TILE):
        a = jnp.dot(h, w1_ref[:, k0:k0 + FF_TILE], preferred_element_type=F32)
        a = jnp.maximum(a, 0.0)
        f = f + _dot(a * a, w2_ref[k0:k0 + FF_TILE, :])
    o_ref[0] = _layer_norm(DN_ALPHA * x + gate2 * f, g_ref[...], b_ref[...])


def _ffn(x, mods, mod_row0, tm, w1, w2, g, b):
    s_n, l_n, _ = x.shape
    tok = lambda s, j: (s, j, 0)
    return pl.pallas_call(
        _ffn_kernel,
        grid=(s_n, l_n // tm),
        in_specs=[pl.BlockSpec((1, tm, D_MODEL), tok),
                  pl.BlockSpec((1, 1, 6 * D_MODEL), lambda s, j: (mod_row0 + s, 0, 0)),
                  _const_spec(w1.shape), _const_spec(w2.shape),
                  _const_spec(g.shape), _const_spec(b.shape)],
        out_specs=pl.BlockSpec((1, tm, D_MODEL), tok),
        out_shape=jax.ShapeDtypeStruct((s_n, l_n, D_MODEL), F32),
        compiler_params=_params(48, 2),
        name="ffn",
    )(x, mods, w1, w2, g, b)


def _grid_pos_tables(n_tokens):
    rows = n_tokens // GRID_W
    quarter = D_MODEL // 4
    omega = 1.0 / (POS_BASE ** (jnp.arange(quarter, dtype=F32) / quarter))
    r = jnp.arange(rows, dtype=F32)[:, None] * omega
    col = jnp.arange(GRID_W, dtype=F32)[:, None] * omega
    emb_r = jnp.concatenate([jnp.sin(r), jnp.cos(r)], axis=-1)
    emb_c = jnp.concatenate([jnp.sin(col), jnp.cos(col)], axis=-1)
    return emb_r, emb_c


def _dt_lanes(v, axis):
    pad = [(0, 0)] * v.ndim
    pad[axis] = (0, DT_PAD - DT_COPIES * DT_GROUP)
    return jnp.pad(jnp.concatenate([v] * DT_COPIES, axis=axis), pad)


def _head_expanders():
    lane = jnp.arange(LANES)[:, None]
    head = jnp.arange(D_SSD)[None, :] // SSD_HEAD_DIM
    valid = lane < DT_COPIES * DT_GROUP
    e_f = jnp.logical_and(valid, lane % DT_GROUP == head)
    e_b = jnp.logical_and(valid, lane % DT_GROUP == head + SSD_HEADS)
    return e_f.astype(BF16), e_b.astype(BF16)


def _stream(x, pos, mods, mod_row0, seq_len, h0f, h0b, prm, tm):
    xa, dtp, hbs, hb = _front(x, pos, mods, h0b, prm["w_front"], prm["conv_w"], prm["conv_b"],
                              prm["dt_bias"], prm["a_neg"], prm["e_b"], mod_row0, tm, seq_len)
    x1, hf = _main(x, pos, xa, dtp, hbs, h0f, mods, mod_row0, tm, seq_len, prm["main"])
    y = _ffn(x1, mods, mod_row0, tm, prm["w_ff1"], prm["w_ff2"], prm["ln2_g"], prm["ln2_b"])
    return y, hf, hb


def kernel(x_prompt, x_sample, state_ssd_fwd, state_ssd_bwd, c, c_ctx, w_ada, b_ada, w_in, conv_w, conv_b, dt_bias_fwd, dt_bias_bwd, a_log_fwd, a_log_bwd, d_skip, ssd_norm_w, sgu_ln_g, sgu_ln_b, w_spatial, b_spatial, w_proj_a, w_proj_b, w_out, ln1_g, ln1_b, w_ff1, w_ff2, ln2_g, ln2_b):
    batch, seq, _ = x_prompt.shape
    dec_batch, dec_seq, _ = x_sample.shape
    l = 0
    row = lambda v: v.reshape(1, -1)

    cond = jnp.zeros((SUBLANES, D_MODEL), F32).at[0].set(c_ctx).at[1:1 + dec_batch].set(c)
    mods = _ada(cond, w_ada[l], b_ada[l]).reshape(SUBLANES, 1, 6 * D_MODEL)

    o1 = D_SSD
    o2 = o1 + SSD_CONV_DIM
    o3 = o2 + DT_GROUP
    w = w_in[l]
    w_front = jnp.concatenate([w[:, o1:o2], _dt_lanes(w[:, o2:o3], 1)], axis=1).astype(BF16)
    w_main = jnp.concatenate([w[:, :o1], w[:, o3:]], axis=1).astype(BF16)
    e_f, e_b = _head_expanders()
    dt_bias = row(_dt_lanes(jnp.concatenate([dt_bias_fwd[l], dt_bias_bwd[l]]), 0))
    a_neg = row(_dt_lanes(-jnp.exp(jnp.concatenate([a_log_fwd[l], a_log_bwd[l]])), 0))

    prm = dict(
        w_front=w_front, conv_w=conv_w[l], conv_b=row(conv_b[l]),
        dt_bias=dt_bias, a_neg=a_neg, e_b=e_b,
        main=(w_main, dt_bias, a_neg, row(jnp.repeat(d_skip[l], SSD_HEAD_DIM)),
              row(ssd_norm_w[l]), e_f, e_b, row(sgu_ln_g[l]), row(sgu_ln_b[l]),
              w_spatial[l].astype(BF16), jnp.repeat(b_spatial[l].T, SGU_GROUP_DIM, axis=1),
              w_proj_a[l].astype(BF16), w_proj_b[l].astype(BF16), w_out[l].astype(BF16),
              row(ln1_g[l]), row(ln1_b[l])),
        w_ff1=w_ff1[l].astype(BF16), w_ff2=w_ff2[l].astype(BF16),
        ln2_g=row(ln2_g[l]), ln2_b=row(ln2_b[l]),
    )

    tm = 256
    xp, hf, hb = _stream(x_prompt.reshape(1, batch * seq, D_MODEL), None, mods, 0, seq,
                         None, None, prm, tm)
    pos = _grid_pos_tables(dec_seq)
    h0f = state_ssd_fwd[:, l].reshape(dec_batch, D_SSD, SSD_STATE)
    h0b = state_ssd_bwd[:, l].reshape(dec_batch, D_SSD, SSD_STATE)
    xs, _, _ = _stream(x_sample, pos, mods, 1, dec_seq, h0f, h0b, prm, tm)

    state_shape = (batch, DEPTH, SSD_HEADS, SSD_HEAD_DIM, SSD_STATE)
    return (xp.reshape(batch, seq, D_MODEL), xs,
            hf.reshape(state_shape), hb.reshape(state_shape))
```

```python
import functools
import math

import jax
import jax.numpy as jnp
from jax import lax
from jax.experimental import pallas as pl
from jax.experimental.pallas import tpu as pltpu

D_MODEL = 1024
GRID_W = 64
POS_BASE = 10000.0
D_SSD = 1024
SSD_HEAD_DIM = 64
SSD_HEADS = D_SSD // SSD_HEAD_DIM
SSD_GROUPS = 2
SSD_STATE = 128
SSD_CHUNK = 128
SSD_CONV_DIM = D_SSD + 2 * SSD_GROUPS * SSD_STATE
D_SGU = 1024
SGU_GROUPS = 8
SGU_GROUP_DIM = D_SGU // SGU_GROUPS
SGU_CHUNK = 128
D_FF = 4 * D_MODEL
DEPTH = 1
DN_ALPHA = (2.0 * DEPTH) ** 0.25
EPS = 1e-5

LANES = 128
SUBLANES = 8
BF16_ROWS = 16

DT_GROUP = 2 * SSD_HEADS
DT_COPIES = 3
DT_PAD = LANES
HEAD_PAIRS = SSD_HEADS // 2
PAIRS_PER_GROUP = HEAD_PAIRS // SSD_GROUPS
GROUP_COLS = D_SSD // SSD_GROUPS
BC_COLS = SSD_GROUPS * SSD_STATE

BF16 = jnp.bfloat16
F32 = jnp.float32
HIGHEST = lax.Precision.HIGHEST


def _dot(a, b):
    return jnp.dot(a.astype(BF16), b.astype(BF16), preferred_element_type=F32)


def _dot_exact(a, b):
    return jnp.dot(a, b, precision=HIGHEST, preferred_element_type=F32)


def _layer_norm(x, g, b):
    mu = jnp.mean(x, axis=-1, keepdims=True)
    xc = x - mu
    var = jnp.mean(xc * xc, axis=-1, keepdims=True)
    return xc * lax.rsqrt(var + EPS) * g + b


def _softplus(x):
    return jnp.maximum(x, 0.0) + jnp.log1p(jnp.exp(-jnp.abs(x)))


def _gelu(x):
    return 0.5 * x * (1.0 + lax.erf(x * (1.0 / math.sqrt(2.0))))


def _const_spec(shape):
    zeros = (0,) * len(shape)
    return pl.BlockSpec(shape, lambda *_: zeros, pipeline_mode=pl.Buffered(1))


def _params(vmem_mb, n_axes):
    return pltpu.CompilerParams(
        dimension_semantics=("arbitrary",) * n_axes,
        vmem_limit_bytes=vmem_mb * 1024 * 1024)


def _ada_kernel(c_ref, w_ref, b_ref, o_ref):
    c = c_ref[...]
    o_ref[...] = _dot_exact(c * jax.nn.sigmoid(c), w_ref[...]) + b_ref[...]


def _ada(cond, w_ada, b_ada):
    n = w_ada.shape[1]
    tn = 1024
    return pl.pallas_call(
        _ada_kernel,
        grid=(n // tn,),
        in_specs=[pl.BlockSpec((SUBLANES, D_MODEL), lambda j: (0, 0)),
                  pl.BlockSpec((D_MODEL, tn), lambda j: (0, j)),
                  pl.BlockSpec((1, tn), lambda j: (0, j))],
        out_specs=pl.BlockSpec((SUBLANES, tn), lambda j: (0, j)),
        out_shape=jax.ShapeDtypeStruct((SUBLANES, n), F32),
        compiler_params=_params(32, 1),
        name="ada",
    )(cond, w_ada, b_ada.reshape(1, n))


W_FRONT_COLS = SSD_CONV_DIM + DT_PAD
W_MAIN_COLS = 5 * D_MODEL
W_SPLIT_ROWS = 256


def _w_in_split_kernel(w_ref, wf_ref, wm_ref):
    o1 = D_SSD
    o2 = o1 + SSD_CONV_DIM
    o3 = o2 + DT_GROUP
    w = w_ref[0]
    dt = w[:, o2:o3]
    pad = jnp.zeros((dt.shape[0], DT_PAD - DT_COPIES * DT_GROUP), F32)
    wf_ref[:, 0:SSD_CONV_DIM] = w[:, o1:o2].astype(BF16)
    wf_ref[:, SSD_CONV_DIM:] = jnp.concatenate([dt] * DT_COPIES + [pad], axis=1).astype(BF16)
    wm_ref[:, 0:D_SSD] = w[:, 0:o1].astype(BF16)
    wm_ref[:, D_SSD:] = w[:, o3:].astype(BF16)


def _w_in_split(w_in):
    n_in = w_in.shape[-1]
    return pl.pallas_call(
        _w_in_split_kernel,
        grid=(D_MODEL // W_SPLIT_ROWS,),
        in_specs=[pl.BlockSpec((1, W_SPLIT_ROWS, n_in), lambda i: (0, i, 0))],
        out_specs=[pl.BlockSpec((W_SPLIT_ROWS, W_FRONT_COLS), lambda i: (i, 0)),
                   pl.BlockSpec((W_SPLIT_ROWS, W_MAIN_COLS), lambda i: (i, 0))],
        out_shape=[jax.ShapeDtypeStruct((D_MODEL, W_FRONT_COLS), BF16),
                   jax.ShapeDtypeStruct((D_MODEL, W_MAIN_COLS), BF16)],
        compiler_params=_params(40, 1),
        name="w_in_split",
    )(w_in)


IN_NTILE = 512
HALO = BF16_ROWS


def _pos_rows(embr_ref, embc_ref, grid_row0, n_rows):
    left = [jnp.broadcast_to(embr_ref[pl.ds(grid_row0 + i, 1), :], (GRID_W, D_MODEL // 2))
            for i in range(n_rows)]
    right = [embc_ref[...]] * n_rows
    return jnp.concatenate([jnp.concatenate(left, axis=0), jnp.concatenate(right, axis=0)],
                           axis=1)


def _pos_halo(embr_ref, embc_ref, grid_row, col0):
    left = jnp.broadcast_to(embr_ref[pl.ds(grid_row, 1), :], (HALO, D_MODEL // 2))
    return jnp.concatenate([left, embc_ref[col0:col0 + HALO, :]], axis=1)


def _block_diag_pair(xp):
    lane = lax.broadcasted_iota(jnp.int32, xp.shape, 1)
    first = lane < SSD_HEAD_DIM
    return jnp.concatenate([jnp.where(first, xp, 0.0), jnp.where(first, 0.0, xp)], axis=0)


def _split_terms(v):
    lane = lax.broadcasted_iota(jnp.int32, v.shape, 1)
    hi = v.astype(BF16).astype(F32)
    r1 = v - hi
    mid = r1.astype(BF16).astype(F32)
    lo = r1 - mid
    return jnp.where(lane < DT_GROUP, hi, jnp.where(lane < 2 * DT_GROUP, mid, lo)).astype(BF16)


def _expand(v, e_ref):
    return jnp.dot(_split_terms(v), e_ref[...], preferred_element_type=F32)


def _chunk_decay(dtp, dtbias_ref, aneg_ref):
    q = dtp.shape[0]
    dt_all = _softplus(dtp + dtbias_ref[...])
    a_all = dt_all * aneg_ref[...]
    ii = lax.broadcasted_iota(jnp.int32, (q, q), 0)
    jj = lax.broadcasted_iota(jnp.int32, (q, q), 1)
    acum = _dot_exact((ii >= jj).astype(F32), a_all)
    atot = acum[q - 1:q, :]
    rcum = atot - acum + a_all
    return dt_all, a_all, acum, atot, rcum, rcum[0:1, :]


def _carry_update(h_old, w_tok, tot_row, e_ref, xh, b_all):
    wx = (_expand(w_tok, e_ref) * xh).astype(BF16)
    s = jnp.concatenate(
        [jnp.dot(b_all[:, g * SSD_STATE:(g + 1) * SSD_STATE].T.astype(BF16),
                 wx[:, g * GROUP_COLS:(g + 1) * GROUP_COLS], preferred_element_type=F32)
         for g in range(SSD_GROUPS)], axis=1)
    decay = jnp.exp(_expand(jnp.broadcast_to(tot_row, (BF16_ROWS, LANES)), e_ref))[0:1, :]
    return decay * h_old + s


def _ssd_chunk(xh, b_all, c_all, dtp, hf_in, hb_in, dtbias_ref, aneg_ref, dskip_ref, ef_ref,
               eb_ref, between_pairs=lambda: None):
    q = SSD_CHUNK
    dt_all, a_all, acum, atot, rcum, _ = _chunk_decay(dtp, dtbias_ref, aneg_ref)
    ii = lax.broadcasted_iota(jnp.int32, (q, q), 0)
    jj = lax.broadcasted_iota(jnp.int32, (q, q), 1)
    lower = ii >= jj
    diag = ii == jj
    dt_t = dt_all.T
    a_t = a_all.T
    acum_t = _dot_exact(a_t, (ii <= jj).astype(F32))
    rcum_t = acum_t[:, q - 1:q] - acum_t + a_t

    ef = jnp.exp(_expand(acum, ef_ref))
    eb = jnp.exp(_expand(rcum, eb_ref))
    y_off = []
    cb = []
    for g in range(SSD_GROUPS):
        gs = slice(g * GROUP_COLS, (g + 1) * GROUP_COLS)
        c_g = c_all[:, g * SSD_STATE:(g + 1) * SSD_STATE].astype(BF16)
        b_g = b_all[:, g * SSD_STATE:(g + 1) * SSD_STATE].astype(BF16)
        cb.append(lax.dot_general(c_g, b_g, (((1,), (1,)), ((), ())),
                                  preferred_element_type=F32))
        y_off.append(ef[:, gs] * jnp.dot(c_g, hf_in[:, gs].astype(BF16),
                                         preferred_element_type=F32)
                     + eb[:, gs] * jnp.dot(c_g, hb_in[:, gs].astype(BF16),
                                           preferred_element_type=F32))
    y_off = jnp.concatenate(y_off, axis=1)

    y_cols = []
    for pr in range(HEAD_PAIRS):
        g = pr // PAIRS_PER_GROUP
        m_parts = []
        for h in (2 * pr, 2 * pr + 1):
            hb_lane = SSD_HEADS + h
            colf = jnp.broadcast_to(acum[:, h:h + 1], (q, q))
            colb = jnp.broadcast_to(rcum[:, hb_lane:hb_lane + 1], (q, q))
            seg = jnp.where(lower, colf - acum_t[h:h + 1, :],
                            colb - rcum_t[hb_lane:hb_lane + 1, :])
            dt_b_row = dt_t[hb_lane:hb_lane + 1, :]
            w = (jnp.exp(seg) * jnp.where(lower, dt_t[h:h + 1, :], dt_b_row)
                 + jnp.where(diag, dt_b_row, 0.0))
            m_parts.append(cb[g] * w)
        sl = slice(pr * LANES, (pr + 1) * LANES)
        y_cols.append(_dot(jnp.concatenate(m_parts, axis=1), _block_diag_pair(xh[:, sl])))
        between_pairs()
    y = jnp.concatenate(y_cols, axis=1) + y_off + dskip_ref[...] * xh
    h_new = _carry_update(hf_in, dt_all * jnp.exp(atot - acum), atot, ef_ref, xh, b_all)
    return y, h_new


def _tiling(x, tm, seq_len):
    s_n, l_n, _ = x.shape
    return s_n, l_n, l_n // tm, seq_len // tm, l_n // seq_len


FRONT_SEGS = (SSD_CONV_DIM, DT_PAD)
FRONT_NTILE = 256


def _front_kernel(*refs, has_pos, has_h0, tiles_per_seq, n_tiles):
    x_ref, xp_ref, xn_ref = refs[:3]
    refs = refs[3:]
    if has_pos:
        embr_ref, embc_ref = refs[:2]
        refs = refs[2:]
    if has_h0:
        h0b_ref = refs[0]
        refs = refs[1:]
    mod_ref, w_ref, convw_ref, convb_ref, dtbias_ref, aneg_ref, eb_ref = refs[:7]
    xa_ref, dt_ref, hbs_ref, hfb_ref, hb_s, xb_s, dts_s = refs[7:]
    shift1 = mod_ref[0, :, 0:D_MODEL]
    scale1 = mod_ref[0, :, D_MODEL:2 * D_MODEL]
    tm = x_ref.shape[1]
    n_ext = tm + 2 * HALO
    q = SSD_CHUNK
    j = pl.program_id(1)
    tile = n_tiles - 1 - jnp.minimum(j, n_tiles - 1)
    t = tile % tiles_per_seq
    t_scan = jnp.minimum(n_tiles - j, n_tiles - 1) % tiles_per_seq
    slot = j % 2
    scan_slot = 1 - slot

    @pl.when(j == 0)
    def _():
        xb_s[1] = jnp.zeros(xb_s.shape[1:], F32)
        dts_s[1] = jnp.zeros(dts_s.shape[1:], F32)

    def init_state():
        if has_h0:
            hb_s[...] = h0b_ref[0].T
        else:
            hb_s[...] = jnp.zeros_like(hb_s)

    if tiles_per_seq > 1:
        pl.when(t_scan == tiles_per_seq - 1)(init_state)
    else:
        init_state()

    def modulate(x, pos):
        if pos is not None:
            x = x + pos
        return (x * (1.0 + scale1) + shift1).astype(BF16)

    pos = pos_prev = pos_next = None
    if has_pos:
        rows_per_tile = tm // GRID_W
        n_grid_rows = embr_ref.shape[0]
        r0 = tile * rows_per_tile
        pos = _pos_rows(embr_ref, embc_ref, r0, rows_per_tile)
        pos_prev = _pos_halo(embr_ref, embc_ref, jnp.maximum(r0 - 1, 0), GRID_W - HALO)
        pos_next = _pos_halo(embr_ref, embc_ref,
                             jnp.minimum(r0 + rows_per_tile, n_grid_rows - 1), 0)
    h = modulate(x_ref[0], pos)
    h_ext = jnp.concatenate([modulate(xp_ref[0], pos_prev), h, modulate(xn_ref[0], pos_next)],
                            axis=0)

    row = lax.broadcasted_iota(jnp.int32, (tm, 1), 0)
    zero_prev = row == 0
    zero_next = row == tm - 1
    if tiles_per_seq > 1:
        zero_prev = jnp.logical_and(zero_prev, t == 0)
        zero_next = jnp.logical_and(zero_next, t == tiles_per_seq - 1)

    def project():
        dt = jnp.dot(h, w_ref[:, SSD_CONV_DIM:], preferred_element_type=F32)
        dt_ref[0] = dt
        dts_s[slot] = dt
        yield
        for c0 in range(0, SSD_CONV_DIM, FRONT_NTILE):
            c1 = c0 + FRONT_NTILE
            p = jnp.dot(h_ext, w_ref[:, c0:c1], preferred_element_type=F32)
            cur = p[HALO:HALO + tm]
            prv = jnp.where(zero_prev, 0.0, pltpu.roll(p, 1, 0)[HALO:HALO + tm])
            nxt = jnp.where(zero_next, 0.0, pltpu.roll(p, n_ext - 1, 0)[HALO:HALO + tm])
            xc = (convw_ref[0:1, c0:c1] * prv + convw_ref[1:2, c0:c1] * cur
                  + convw_ref[2:3, c0:c1] * nxt + convb_ref[:, c0:c1])
            xa = xc * jax.nn.sigmoid(xc)
            xa_ref[0, :, c0:c1] = xa
            if c0 < D_SSD + BC_COLS:
                xb_s[slot, :, c0:c1] = xa
            yield

    final = []

    def scan():
        h_state = hb_s[...]
        for ch in reversed(range(tm // q)):
            rows = slice(ch * q, (ch + 1) * q)
            hbs_ref[0, ch] = h_state
            dt_all, _, _, _, rcum, rtot = _chunk_decay(dts_s[scan_slot, rows, :], dtbias_ref,
                                                       aneg_ref)
            yield
            w_exp = _expand(dt_all * jnp.exp(rtot - rcum), eb_ref)
            decay = jnp.exp(_expand(jnp.broadcast_to(rtot, (BF16_ROWS, LANES)), eb_ref))[0:1, :]
            yield
            wx = (w_exp * xb_s[scan_slot, rows, 0:D_SSD]).astype(BF16)
            b_all = xb_s[scan_slot, rows, D_SSD:D_SSD + BC_COLS]
            parts = []
            for g in range(SSD_GROUPS):
                parts.append(jnp.dot(b_all[:, g * SSD_STATE:(g + 1) * SSD_STATE].T.astype(BF16),
                                     wx[:, g * GROUP_COLS:(g + 1) * GROUP_COLS],
                                     preferred_element_type=F32))
                yield
            h_state = decay * h_state + jnp.concatenate(parts, axis=1)
        hb_s[...] = h_state
        final.append(h_state)

    stages = [project(), scan()]
    while stages:
        for gen in list(stages):
            if next(gen, StopIteration) is StopIteration:
                stages.remove(gen)

    def final_state():
        hfb_ref[0] = final[0].T

    if tiles_per_seq > 1:
        pl.when(t_scan == 0)(final_state)
    else:
        final_state()


def _front(x, pos, mods, h0b, w_front, convw, convb, dtbias, aneg, e_b, mod_row0, tm, seq_len):
    s_n, l_n, n_tiles, tiles_per_seq, seqs_per_row = _tiling(x, tm, seq_len)
    has_pos = pos is not None
    has_h0 = h0b is not None
    hb = tm // HALO
    n_hb = l_n // HALO
    cpt = tm // SSD_CHUNK
    fwd = lambda j: n_tiles - 1 - jnp.minimum(j, n_tiles - 1)
    scanned = lambda j: jnp.minimum(n_tiles - j, n_tiles - 1)
    tok = lambda s, j: (s, fwd(j), 0)
    seq_of = lambda s, j: (s * seqs_per_row + scanned(j) // tiles_per_seq, 0, 0)
    in_specs = [pl.BlockSpec((1, tm, D_MODEL), tok),
                pl.BlockSpec((1, HALO, D_MODEL),
                             lambda s, j: (s, jnp.maximum(fwd(j) * hb - 1, 0), 0)),
                pl.BlockSpec((1, HALO, D_MODEL),
                             lambda s, j: (s, jnp.minimum((fwd(j) + 1) * hb, n_hb - 1), 0))]
    args = [x, x, x]
    if has_pos:
        in_specs += [_const_spec(pos[0].shape), _const_spec(pos[1].shape)]
        args += list(pos)
    state_spec = pl.BlockSpec((1, D_SSD, SSD_STATE), seq_of)
    if has_h0:
        in_specs.append(state_spec)
        args.append(h0b)
    in_specs.append(pl.BlockSpec((1, 1, 6 * D_MODEL), lambda s, j: (mod_row0 + s, 0, 0)))
    args.append(mods)
    for a in (w_front, convw, convb, dtbias, aneg, e_b):
        in_specs.append(_const_spec(a.shape))
        args.append(a)
    return pl.pallas_call(
        functools.partial(_front_kernel, has_pos=has_pos, has_h0=has_h0,
                          tiles_per_seq=tiles_per_seq, n_tiles=n_tiles),
        grid=(s_n, n_tiles + 1),
        in_specs=in_specs,
        out_specs=[pl.BlockSpec((1, tm, w), tok) for w in FRONT_SEGS]
        + [pl.BlockSpec((1, cpt, SSD_STATE, D_SSD), lambda s, j: (s, scanned(j), 0, 0)),
           state_spec],
        out_shape=[jax.ShapeDtypeStruct((s_n, l_n, w), F32) for w in FRONT_SEGS]
        + [jax.ShapeDtypeStruct((s_n, l_n // SSD_CHUNK, SSD_STATE, D_SSD), F32),
           jax.ShapeDtypeStruct((s_n * seqs_per_row, D_SSD, SSD_STATE), F32)],
        scratch_shapes=[pltpu.VMEM((SSD_STATE, D_SSD), F32),
                        pltpu.VMEM((2, tm, D_SSD + BC_COLS), F32),
                        pltpu.VMEM((2, tm, DT_PAD), F32)],
        compiler_params=_params(48, 2),
        name="front",
    )(*args)


MAIN_Z, MAIN_U, MAIN_V, MAIN_GA, MAIN_GB = (i * D_MODEL for i in range(5))
MAIN_COLS = 5 * D_MODEL


def _main_kernel(*refs, has_pos, has_h0, tiles_per_seq):
    x_ref, xa_ref, dtp_ref, hbs_ref = refs[:4]
    refs = refs[4:]
    if has_pos:
        embr_ref, embc_ref = refs[:2]
        refs = refs[2:]
    if has_h0:
        h0f_ref = refs[0]
        refs = refs[1:]
    (mod_ref, w_ref, dtbias_ref, aneg_ref, dskip_ref, normw_ref, ef_ref, eb_ref, lng_ref,
     lnb_ref, wsp_ref, bsp_ref, wpa_ref, wpb_ref, wout_ref, ln1g_ref, ln1b_ref) = refs[:17]
    o_ref, hff_ref, hf_s = refs[17:]
    tm = x_ref.shape[1]
    q = SSD_CHUNK
    tile = pl.program_id(1)
    t = tile % tiles_per_seq

    x = x_ref[0]
    if has_pos:
        rows_per_tile = tm // GRID_W
        x = x + _pos_rows(embr_ref, embc_ref, tile * rows_per_tile, rows_per_tile)
    shift1 = mod_ref[0, :, 0:D_MODEL]
    scale1 = mod_ref[0, :, D_MODEL:2 * D_MODEL]
    gate1 = mod_ref[0, :, 2 * D_MODEL:3 * D_MODEL]
    h = (x * (1.0 + scale1) + shift1).astype(BF16)

    col_tiles = {}
    pending = list(range(0, MAIN_COLS, IN_NTILE))
    n_slots = (tm // q) * HEAD_PAIRS
    n_fill = (MAIN_COLS - MAIN_U) // IN_NTILE
    slot = [0]

    def project_next():
        c = pending.pop(0)
        col_tiles[c] = jnp.dot(h, w_ref[:, c:c + IN_NTILE], preferred_element_type=F32)

    def between_pairs():
        slot[0] += 1
        done = n_fill - len(pending)
        for _ in range(-(-slot[0] * n_fill // n_slots) - done):
            project_next()

    def in_proj(c0):
        while pending and pending[0] < c0 + D_MODEL:
            project_next()
        return jnp.concatenate([col_tiles.pop(c) for c in range(c0, c0 + D_MODEL, IN_NTILE)],
                               axis=1)

    def init_state():
        if has_h0:
            hf_s[...] = h0f_ref[0].T
        else:
            hf_s[...] = jnp.zeros_like(hf_s)

    if tiles_per_seq > 1:
        pl.when(t == 0)(init_state)
    else:
        init_state()
    h_state = hf_s[...]
    zz = in_proj(MAIN_Z)
    zz = zz * jax.nn.sigmoid(zz)
    ya = []
    for ch in range(tm // q):
        rows = slice(ch * q, (ch + 1) * q)
        y, h_state = _ssd_chunk(
            xa_ref[0, rows, 0:D_SSD], xa_ref[0, rows, D_SSD:D_SSD + BC_COLS],
            xa_ref[0, rows, D_SSD + BC_COLS:], dtp_ref[0, rows, :], h_state, hbs_ref[0, ch],
            dtbias_ref, aneg_ref, dskip_ref, ef_ref, eb_ref, between_pairs)
        yg = y * zz[rows]
        ms = jnp.mean(yg * yg, axis=-1, keepdims=True)
        ya.append(yg * lax.rsqrt(ms + EPS) * normw_ref[...])
    hf_s[...] = h_state

    def final_state():
        hff_ref[0] = h_state.T

    if tiles_per_seq > 1:
        pl.when(t == tiles_per_seq - 1)(final_state)
    else:
        final_state()
    ya = jnp.concatenate(ya, axis=0)

    u = _gelu(in_proj(MAIN_U))
    v = _layer_norm(_gelu(in_proj(MAIN_V)), lng_ref[...], lnb_ref[...])
    rows_out = []
    for ch in range(tm // SGU_CHUNK):
        vc = v[ch * SGU_CHUNK:(ch + 1) * SGU_CHUNK, :]
        cols = [_dot(wsp_ref[g], vc[:, g * SGU_GROUP_DIM:(g + 1) * SGU_GROUP_DIM])
                for g in range(SGU_GROUPS)]
        rows_out.append(jnp.concatenate(cols, axis=1) + bsp_ref[...])
    yb = u * jnp.concatenate(rows_out, axis=0)

    merged = (jax.nn.sigmoid(in_proj(MAIN_GA)) * _dot(ya, wpa_ref[...])
              + jax.nn.sigmoid(in_proj(MAIN_GB)) * _dot(yb, wpb_ref[...]))
    o = _dot(merged, wout_ref[...])
    o_ref[0] = _layer_norm(DN_ALPHA * x + gate1 * o, ln1g_ref[...], ln1b_ref[...])


def _main(x, pos, xa, dtp, hbs, h0f, mods, mod_row0, tm, seq_len, consts):
    s_n, l_n, n_tiles, tiles_per_seq, seqs_per_row = _tiling(x, tm, seq_len)
    has_pos = pos is not None
    has_h0 = h0f is not None
    cpt = tm // SSD_CHUNK
    tok = lambda s, j: (s, j, 0)
    in_specs = [pl.BlockSpec((1, tm, D_MODEL), tok),
                pl.BlockSpec((1, tm, SSD_CONV_DIM), tok),
                pl.BlockSpec((1, tm, DT_PAD), tok),
                pl.BlockSpec((1, cpt, SSD_STATE, D_SSD), lambda s, j: (s, j, 0, 0))]
    args = [x, xa, dtp, hbs]
    if has_pos:
        in_specs += [_const_spec(pos[0].shape), _const_spec(pos[1].shape)]
        args += list(pos)
    state_spec = pl.BlockSpec((1, D_SSD, SSD_STATE),
                              lambda s, j: (s * seqs_per_row + j // tiles_per_seq, 0, 0))
    if has_h0:
        in_specs.append(state_spec)
        args.append(h0f)
    in_specs.append(pl.BlockSpec((1, 1, 6 * D_MODEL), lambda s, j: (mod_row0 + s, 0, 0)))
    args.append(mods)
    for a in consts:
        in_specs.append(_const_spec(a.shape))
        args.append(a)
    return pl.pallas_call(
        functools.partial(_main_kernel, has_pos=has_pos, has_h0=has_h0,
                          tiles_per_seq=tiles_per_seq),
        grid=(s_n, n_tiles),
        in_specs=in_specs,
        out_specs=[pl.BlockSpec((1, tm, D_MODEL), tok), state_spec],
        out_shape=[jax.ShapeDtypeStruct((s_n, l_n, D_MODEL), F32),
                   jax.ShapeDtypeStruct((s_n * seqs_per_row, D_SSD, SSD_STATE), F32)],
        scratch_shapes=[pltpu.VMEM((SSD_STATE, D_SSD), F32)],
        compiler_params=_params(56, 2),
        name="main",
    )(*args)


FF_TILE = 1024
FFN_TM = 512


def _ffn_kernel(x_ref, mod_ref, w1_ref, w2_ref, g_ref, b_ref, o_ref):
    x = x_ref[0]
    shift2 = mod_ref[0, :, 3 * D_MODEL:4 * D_MODEL]
    scale2 = mod_ref[0, :, 4 * D_MODEL:5 * D_MODEL]
    gate2 = mod_ref[0, :, 5 * D_MODEL:6 * D_MODEL]
    h = (x * (1.0 + scale2) + shift2).astype(BF16)
    f = jnp.zeros_like(x)
    for k0 in range(0, D_FF, FF_TILE):
        a = jnp.dot(h, w1_ref[:, k0:k0 + FF_TILE], preferred_element_type=F32)
        a = jnp.maximum(a, 0.0)
        f = f + _dot(a * a, w2_ref[k0:k0 + FF_TILE, :])
    o_ref[0] = _layer_norm(DN_ALPHA * x + gate2 * f, g_ref[...], b_ref[...])


def _ffn(x, mods, mod_row0, tm, w1, w2, g, b):
    s_n, l_n, _ = x.shape
    tok = lambda s, j: (s, j, 0)
    return pl.pallas_call(
        _ffn_kernel,
        grid=(s_n, l_n // tm),
        in_specs=[pl.BlockSpec((1, tm, D_MODEL), tok),
                  pl.BlockSpec((1, 1, 6 * D_MODEL), lambda s, j: (mod_row0 + s, 0, 0)),
                  _const_spec(w1.shape), _const_spec(w2.shape),
                  _const_spec(g.shape), _const_spec(b.shape)],
        out_specs=pl.BlockSpec((1, tm, D_MODEL), tok),
        out_shape=jax.ShapeDtypeStruct((s_n, l_n, D_MODEL), F32),
        compiler_params=_params(48, 2),
        name="ffn",
    )(x, mods, w1, w2, g, b)


def _grid_pos_tables(n_tokens):
    rows = n_tokens // GRID_W
    quarter = D_MODEL // 4
    omega = 1.0 / (POS_BASE ** (jnp.arange(quarter, dtype=F32) / quarter))
    r = jnp.arange(rows, dtype=F32)[:, None] * omega
    col = jnp.arange(GRID_W, dtype=F32)[:, None] * omega
    emb_r = jnp.concatenate([jnp.sin(r), jnp.cos(r)], axis=-1)
    emb_c = jnp.concatenate([jnp.sin(col), jnp.cos(col)], axis=-1)
    return emb_r, emb_c


def _dt_lanes(v, axis):
    pad = [(0, 0)] * v.ndim
    pad[axis] = (0, DT_PAD - DT_COPIES * DT_GROUP)
    return jnp.pad(jnp.concatenate([v] * DT_COPIES, axis=axis), pad)


def _head_expanders():
    lane = jnp.arange(LANES)[:, None]
    head = jnp.arange(D_SSD)[None, :] // SSD_HEAD_DIM
    valid = lane < DT_COPIES * DT_GROUP
    e_f = jnp.logical_and(valid, lane % DT_GROUP == head)
    e_b = jnp.logical_and(valid, lane % DT_GROUP == head + SSD_HEADS)
    return e_f.astype(BF16), e_b.astype(BF16)


def _stream(x, pos, mods, mod_row0, seq_len, h0f, h0b, prm, tm):
    xa, dtp, hbs, hb = _front(x, pos, mods, h0b, prm["w_front"], prm["conv_w"], prm["conv_b"],
                              prm["dt_bias"], prm["a_neg"], prm["e_b"], mod_row0, tm, seq_len)
    x1, hf = _main(x, pos, xa, dtp, hbs, h0f, mods, mod_row0, tm, seq_len, prm["main"])
    y = _ffn(x1, mods, mod_row0, FFN_TM, prm["w_ff1"], prm["w_ff2"], prm["ln2_g"], prm["ln2_b"])
    return y, hf, hb


def kernel(x_prompt, x_sample, state_ssd_fwd, state_ssd_bwd, c, c_ctx, w_ada, b_ada, w_in, conv_w, conv_b, dt_bias_fwd, dt_bias_bwd, a_log_fwd, a_log_bwd, d_skip, ssd_norm_w, sgu_ln_g, sgu_ln_b, w_spatial, b_spatial, w_proj_a, w_proj_b, w_out, ln1_g, ln1_b, w_ff1, w_ff2, ln2_g, ln2_b):
    batch, seq, _ = x_prompt.shape
    dec_batch, dec_seq, _ = x_sample.shape
    l = 0
    row = lambda v: v.reshape(1, -1)

    cond = jnp.zeros((SUBLANES, D_MODEL), F32).at[0].set(c_ctx).at[1:1 + dec_batch].set(c)
    mods = _ada(cond, w_ada[l], b_ada[l]).reshape(SUBLANES, 1, 6 * D_MODEL)

    w_front, w_main = _w_in_split(w_in[l:l + 1])
    e_f, e_b = _head_expanders()
    dt_bias = row(_dt_lanes(jnp.concatenate([dt_bias_fwd[l], dt_bias_bwd[l]]), 0))
    a_neg = row(_dt_lanes(-jnp.exp(jnp.concatenate([a_log_fwd[l], a_log_bwd[l]])), 0))

    prm = dict(
        w_front=w_front, conv_w=conv_w[l], conv_b=row(conv_b[l]),
        dt_bias=dt_bias, a_neg=a_neg, e_b=e_b,
        main=(w_main, dt_bias, a_neg, row(jnp.repeat(d_skip[l], SSD_HEAD_DIM)),
              row(ssd_norm_w[l]), e_f, e_b, row(sgu_ln_g[l]), row(sgu_ln_b[l]),
              w_spatial[l].astype(BF16), jnp.repeat(b_spatial[l].T, SGU_GROUP_DIM, axis=1),
              w_proj_a[l].astype(BF16), w_proj_b[l].astype(BF16), w_out[l].astype(BF16),
              row(ln1_g[l]), row(ln1_b[l])),
        w_ff1=w_ff1[l].astype(BF16), w_ff2=w_ff2[l].astype(BF16),
        ln2_g=row(ln2_g[l]), ln2_b=row(ln2_b[l]),
    )

    tm = 256
    xp, hf, hb = _stream(x_prompt.reshape(1, batch * seq, D_MODEL), None, mods, 0, seq,
                         None, None, prm, tm)
    pos = _grid_pos_tables(dec_seq)
    h0f = state_ssd_fwd[:, l].reshape(dec_batch, D_SSD, SSD_STATE)
    h0b = state_ssd_bwd[:, l].reshape(dec_batch, D_SSD, SSD_STATE)
    xs, _, _ = _stream(x_sample, pos, mods, 1, dec_seq, h0f, h0b, prm, tm)

    state_shape = (batch, DEPTH, SSD_HEADS, SSD_HEAD_DIM, SSD_STATE)
    return (xp.reshape(batch, seq, D_MODEL), xs,
            hf.reshape(state_shape), hb.reshape(state_shape))
```

```python
import functools
import math

import jax
import jax.numpy as jnp
from jax import lax
from jax.experimental import pallas as pl
from jax.experimental.pallas import tpu as pltpu

D_MODEL = 1024
GRID_W = 64
POS_BASE = 10000.0
D_SSD = 1024
SSD_HEAD_DIM = 64
SSD_HEADS = D_SSD // SSD_HEAD_DIM
SSD_GROUPS = 2
SSD_STATE = 128
SSD_CHUNK = 128
SSD_CONV_DIM = D_SSD + 2 * SSD_GROUPS * SSD_STATE
D_SGU = 1024
SGU_GROUPS = 8
SGU_GROUP_DIM = D_SGU // SGU_GROUPS
SGU_CHUNK = 128
D_FF = 4 * D_MODEL
DEPTH = 1
DN_ALPHA = (2.0 * DEPTH) ** 0.25
EPS = 1e-5

LANES = 128
SUBLANES = 8
BF16_ROWS = 16

DT_GROUP = 2 * SSD_HEADS
DT_COPIES = 3
DT_PAD = LANES
HEAD_PAIRS = SSD_HEADS // 2
PAIRS_PER_GROUP = HEAD_PAIRS // SSD_GROUPS
GROUP_COLS = D_SSD // SSD_GROUPS
BC_COLS = SSD_GROUPS * SSD_STATE

BF16 = jnp.bfloat16
F32 = jnp.float32
HIGHEST = lax.Precision.HIGHEST


def _dot(a, b):
    return jnp.dot(a.astype(BF16), b.astype(BF16), preferred_element_type=F32)


def _dot_exact(a, b):
    return jnp.dot(a, b, precision=HIGHEST, preferred_element_type=F32)


def _layer_norm(x, g, b):
    mu = jnp.mean(x, axis=-1, keepdims=True)
    xc = x - mu
    var = jnp.mean(xc * xc, axis=-1, keepdims=True)
    return xc * lax.rsqrt(var + EPS) * g + b


def _softplus(x):
    return jnp.maximum(x, 0.0) + jnp.log1p(jnp.exp(-jnp.abs(x)))


def _gelu(x):
    return 0.5 * x * (1.0 + lax.erf(x * (1.0 / math.sqrt(2.0))))


def _sigmoid(x):
    return 0.5 * jnp.tanh(0.5 * x) + 0.5


def _silu(x):
    half = 0.5 * x
    return half * jnp.tanh(half) + half


def _const_spec(shape):
    zeros = (0,) * len(shape)
    return pl.BlockSpec(shape, lambda *_: zeros, pipeline_mode=pl.Buffered(1))


def _params(vmem_mb, n_axes):
    return pltpu.CompilerParams(
        dimension_semantics=("arbitrary",) * n_axes,
        vmem_limit_bytes=vmem_mb * 1024 * 1024)


def _ada_kernel(c_ref, w_ref, b_ref, o_ref):
    o_ref[...] = _dot_exact(_silu(c_ref[...]), w_ref[...]) + b_ref[...]


def _ada(cond, w_ada, b_ada):
    n = w_ada.shape[1]
    tn = 1024
    return pl.pallas_call(
        _ada_kernel,
        grid=(n // tn,),
        in_specs=[pl.BlockSpec((SUBLANES, D_MODEL), lambda j: (0, 0)),
                  pl.BlockSpec((D_MODEL, tn), lambda j: (0, j)),
                  pl.BlockSpec((1, tn), lambda j: (0, j))],
        out_specs=pl.BlockSpec((SUBLANES, tn), lambda j: (0, j)),
        out_shape=jax.ShapeDtypeStruct((SUBLANES, n), F32),
        compiler_params=_params(32, 1),
        name="ada",
    )(cond, w_ada, b_ada.reshape(1, n))


W_FRONT_COLS = SSD_CONV_DIM + DT_PAD
W_MAIN_COLS = 5 * D_MODEL
W_SPLIT_ROWS = 256


def _w_in_split_kernel(w_ref, wf_ref, wm_ref):
    o1 = D_SSD
    o2 = o1 + SSD_CONV_DIM
    o3 = o2 + DT_GROUP
    w = w_ref[0]
    dt = w[:, o2:o3]
    pad = jnp.zeros((dt.shape[0], DT_PAD - DT_COPIES * DT_GROUP), F32)
    wf_ref[:, 0:SSD_CONV_DIM] = w[:, o1:o2].astype(BF16)
    wf_ref[:, SSD_CONV_DIM:] = jnp.concatenate([dt] * DT_COPIES + [pad], axis=1).astype(BF16)
    wm_ref[:, 0:D_SSD] = w[:, 0:o1].astype(BF16)
    wm_ref[:, D_SSD:] = w[:, o3:].astype(BF16)


def _w_in_split(w_in):
    n_in = w_in.shape[-1]
    return pl.pallas_call(
        _w_in_split_kernel,
        grid=(D_MODEL // W_SPLIT_ROWS,),
        in_specs=[pl.BlockSpec((1, W_SPLIT_ROWS, n_in), lambda i: (0, i, 0))],
        out_specs=[pl.BlockSpec((W_SPLIT_ROWS, W_FRONT_COLS), lambda i: (i, 0)),
                   pl.BlockSpec((W_SPLIT_ROWS, W_MAIN_COLS), lambda i: (i, 0))],
        out_shape=[jax.ShapeDtypeStruct((D_MODEL, W_FRONT_COLS), BF16),
                   jax.ShapeDtypeStruct((D_MODEL, W_MAIN_COLS), BF16)],
        compiler_params=_params(40, 1),
        name="w_in_split",
    )(w_in)


IN_NTILE = 512
HALO = BF16_ROWS


def _add_pos(x_ref, embr_ref, embc_ref, grid_row0):
    n_rows = x_ref.shape[1] // GRID_W
    col = embc_ref[...]
    blocks = []
    for i in range(n_rows):
        row = embr_ref[grid_row0 + i]
        row = jnp.concatenate([row] * (GRID_W // SUBLANES), axis=0)
        blocks.append(x_ref[0, i * GRID_W:(i + 1) * GRID_W, :]
                      + jnp.concatenate([row, col], axis=1))
    return jnp.concatenate(blocks, axis=0)


def _pos_halo(embr_ref, embc_ref, grid_row, col0):
    left = jnp.concatenate([embr_ref[grid_row]] * (HALO // SUBLANES), axis=0)
    return jnp.concatenate([left, embc_ref[col0:col0 + HALO, :]], axis=1)


def _block_diag_pair(xp):
    lane = lax.broadcasted_iota(jnp.int32, xp.shape, 1)
    first = lane < SSD_HEAD_DIM
    return jnp.concatenate([jnp.where(first, xp, 0.0), jnp.where(first, 0.0, xp)], axis=0)


def _split_terms(v):
    lane = lax.broadcasted_iota(jnp.int32, v.shape, 1)
    hi = v.astype(BF16).astype(F32)
    r1 = v - hi
    mid = r1.astype(BF16).astype(F32)
    lo = r1 - mid
    return jnp.where(lane < DT_GROUP, hi, jnp.where(lane < 2 * DT_GROUP, mid, lo)).astype(BF16)


def _expand(v, e_ref):
    return jnp.dot(_split_terms(v), e_ref[...], preferred_element_type=F32)


def _chunk_decay(dtp, dtbias_ref, aneg_ref):
    q = dtp.shape[0]
    dt_all = _softplus(dtp + dtbias_ref[...])
    a_all = dt_all * aneg_ref[...]
    ii = lax.broadcasted_iota(jnp.int32, (q, q), 0)
    jj = lax.broadcasted_iota(jnp.int32, (q, q), 1)
    acum = _dot_exact((ii >= jj).astype(F32), a_all)
    atot = acum[q - 1:q, :]
    rcum = atot - acum + a_all
    return dt_all, a_all, acum, atot, rcum, rcum[0:1, :]


def _carry_update(h_old, w_tok, tot_row, e_ref, xh, b_all):
    wx = (_expand(w_tok, e_ref) * xh).astype(BF16)
    s = jnp.concatenate(
        [jnp.dot(b_all[:, g * SSD_STATE:(g + 1) * SSD_STATE].T.astype(BF16),
                 wx[:, g * GROUP_COLS:(g + 1) * GROUP_COLS], preferred_element_type=F32)
         for g in range(SSD_GROUPS)], axis=1)
    decay = jnp.exp(_expand(jnp.broadcast_to(tot_row, (BF16_ROWS, LANES)), e_ref))[0:1, :]
    return decay * h_old + s


def _ssd_chunk(xh, b_all, c_all, dtp, hf_in, hb_in, dtbias_ref, aneg_ref, dskip_ref, ef_ref,
               eb_ref, between_pairs=lambda: None):
    q = SSD_CHUNK
    dt_all, a_all, acum, atot, rcum, _ = _chunk_decay(dtp, dtbias_ref, aneg_ref)
    ii = lax.broadcasted_iota(jnp.int32, (q, q), 0)
    jj = lax.broadcasted_iota(jnp.int32, (q, q), 1)
    lower = ii >= jj
    diag = ii == jj
    dt_t = dt_all.T
    a_t = a_all.T
    acum_t = _dot_exact(a_t, (ii <= jj).astype(F32))
    rcum_t = acum_t[:, q - 1:q] - acum_t + a_t

    ef = jnp.exp(_expand(acum, ef_ref))
    eb = jnp.exp(_expand(rcum, eb_ref))
    y_off = []
    cb = []
    for g in range(SSD_GROUPS):
        gs = slice(g * GROUP_COLS, (g + 1) * GROUP_COLS)
        c_g = c_all[:, g * SSD_STATE:(g + 1) * SSD_STATE].astype(BF16)
        b_g = b_all[:, g * SSD_STATE:(g + 1) * SSD_STATE].astype(BF16)
        cb.append(lax.dot_general(c_g, b_g, (((1,), (1,)), ((), ())),
                                  preferred_element_type=F32))
        y_off.append(ef[:, gs] * jnp.dot(c_g, hf_in[:, gs].astype(BF16),
                                         preferred_element_type=F32)
                     + eb[:, gs] * jnp.dot(c_g, hb_in[:, gs].astype(BF16),
                                           preferred_element_type=F32))
    y_off = jnp.concatenate(y_off, axis=1)

    y_cols = []
    for pr in range(HEAD_PAIRS):
        g = pr // PAIRS_PER_GROUP
        m_parts = []
        for h in (2 * pr, 2 * pr + 1):
            hb_lane = SSD_HEADS + h
            colf = jnp.broadcast_to(acum[:, h:h + 1], (q, q))
            colb = jnp.broadcast_to(rcum[:, hb_lane:hb_lane + 1], (q, q))
            seg = jnp.where(lower, colf - acum_t[h:h + 1, :],
                            colb - rcum_t[hb_lane:hb_lane + 1, :])
            dt_b_row = dt_t[hb_lane:hb_lane + 1, :]
            w = (jnp.exp(seg) * jnp.where(lower, dt_t[h:h + 1, :], dt_b_row)
                 + jnp.where(diag, dt_b_row, 0.0))
            m_parts.append(cb[g] * w)
        sl = slice(pr * LANES, (pr + 1) * LANES)
        y_cols.append(_dot(jnp.concatenate(m_parts, axis=1), _block_diag_pair(xh[:, sl])))
        between_pairs()
    y = jnp.concatenate(y_cols, axis=1) + y_off + dskip_ref[...] * xh
    h_new = _carry_update(hf_in, dt_all * jnp.exp(atot - acum), atot, ef_ref, xh, b_all)
    return y, h_new


def _tiling(x, tm, seq_len):
    s_n, l_n, _ = x.shape
    return s_n, l_n, l_n // tm, seq_len // tm, l_n // seq_len


FRONT_SEGS = (SSD_CONV_DIM, DT_PAD)
FRONT_NTILE = 256


def _front_kernel(*refs, has_pos, has_h0, tiles_per_seq, n_tiles):
    x_ref, xp_ref, xn_ref = refs[:3]
    refs = refs[3:]
    if has_pos:
        embr_ref, embc_ref = refs[:2]
        refs = refs[2:]
    if has_h0:
        h0b_ref = refs[0]
        refs = refs[1:]
    mod_ref, w_ref, convw_ref, convb_ref, dtbias_ref, aneg_ref, eb_ref = refs[:7]
    xa_ref, dt_ref, hbs_ref, hfb_ref, hb_s, xb_s, dts_s = refs[7:]
    shift1 = mod_ref[0, :, 0:D_MODEL]
    scale1 = mod_ref[0, :, D_MODEL:2 * D_MODEL]
    tm = x_ref.shape[1]
    n_ext = tm + 2 * HALO
    q = SSD_CHUNK
    j = pl.program_id(1)
    tile = n_tiles - 1 - jnp.minimum(j, n_tiles - 1)
    t = tile % tiles_per_seq
    t_scan = jnp.minimum(n_tiles - j, n_tiles - 1) % tiles_per_seq
    slot = j % 2
    scan_slot = 1 - slot

    @pl.when(j == 0)
    def _():
        xb_s[1] = jnp.zeros(xb_s.shape[1:], F32)
        dts_s[1] = jnp.zeros(dts_s.shape[1:], F32)

    def init_state():
        if has_h0:
            hb_s[...] = h0b_ref[0].T
        else:
            hb_s[...] = jnp.zeros_like(hb_s)

    if tiles_per_seq > 1:
        pl.when(t_scan == tiles_per_seq - 1)(init_state)
    else:
        init_state()

    def modulate(x):
        return (x * (1.0 + scale1) + shift1).astype(BF16)

    if has_pos:
        rows_per_tile = tm // GRID_W
        n_grid_rows = embr_ref.shape[0]
        r0 = tile * rows_per_tile
        x = _add_pos(x_ref, embr_ref, embc_ref, r0)
        x_prev = xp_ref[0] + _pos_halo(embr_ref, embc_ref, jnp.maximum(r0 - 1, 0),
                                       GRID_W - HALO)
        x_next = xn_ref[0] + _pos_halo(embr_ref, embc_ref,
                                       jnp.minimum(r0 + rows_per_tile, n_grid_rows - 1), 0)
    else:
        x, x_prev, x_next = x_ref[0], xp_ref[0], xn_ref[0]
    h = modulate(x)
    h_ext = jnp.concatenate([modulate(x_prev), h, modulate(x_next)], axis=0)

    row = lax.broadcasted_iota(jnp.int32, (tm, 1), 0)
    zero_prev = row == 0
    zero_next = row == tm - 1
    if tiles_per_seq > 1:
        zero_prev = jnp.logical_and(zero_prev, t == 0)
        zero_next = jnp.logical_and(zero_next, t == tiles_per_seq - 1)

    def project():
        dt = jnp.dot(h, w_ref[:, SSD_CONV_DIM:], preferred_element_type=F32)
        dt_ref[0] = dt
        dts_s[slot] = dt
        yield
        for c0 in range(0, SSD_CONV_DIM, FRONT_NTILE):
            c1 = c0 + FRONT_NTILE
            p = jnp.dot(h_ext, w_ref[:, c0:c1], preferred_element_type=F32)
            cur = p[HALO:HALO + tm]
            prv = jnp.where(zero_prev, 0.0, pltpu.roll(p, 1, 0)[HALO:HALO + tm])
            nxt = jnp.where(zero_next, 0.0, pltpu.roll(p, n_ext - 1, 0)[HALO:HALO + tm])
            xc = (convw_ref[0:1, c0:c1] * prv + convw_ref[1:2, c0:c1] * cur
                  + convw_ref[2:3, c0:c1] * nxt + convb_ref[:, c0:c1])
            xa = _silu(xc)
            xa_ref[0, :, c0:c1] = xa
            if c0 < D_SSD + BC_COLS:
                xb_s[slot, :, c0:c1] = xa
            yield

    final = []

    def scan():
        h_state = hb_s[...]
        for ch in reversed(range(tm // q)):
            rows = slice(ch * q, (ch + 1) * q)
            hbs_ref[0, ch] = h_state
            dt_all, _, _, _, rcum, rtot = _chunk_decay(dts_s[scan_slot, rows, :], dtbias_ref,
                                                       aneg_ref)
            yield
            w_exp = _expand(dt_all * jnp.exp(rtot - rcum), eb_ref)
            decay = jnp.exp(_expand(jnp.broadcast_to(rtot, (BF16_ROWS, LANES)), eb_ref))[0:1, :]
            yield
            wx = (w_exp * xb_s[scan_slot, rows, 0:D_SSD]).astype(BF16)
            b_all = xb_s[scan_slot, rows, D_SSD:D_SSD + BC_COLS]
            parts = []
            for g in range(SSD_GROUPS):
                parts.append(jnp.dot(b_all[:, g * SSD_STATE:(g + 1) * SSD_STATE].T.astype(BF16),
                                     wx[:, g * GROUP_COLS:(g + 1) * GROUP_COLS],
                                     preferred_element_type=F32))
                yield
            h_state = decay * h_state + jnp.concatenate(parts, axis=1)
        hb_s[...] = h_state
        final.append(h_state)

    stages = [project(), scan()]
    while stages:
        for gen in list(stages):
            if next(gen, StopIteration) is StopIteration:
                stages.remove(gen)

    def final_state():
        hfb_ref[0] = final[0].T

    if tiles_per_seq > 1:
        pl.when(t_scan == 0)(final_state)
    else:
        final_state()


def _front(x, pos, mods, h0b, w_front, convw, convb, dtbias, aneg, e_b, mod_row0, tm, seq_len):
    s_n, l_n, n_tiles, tiles_per_seq, seqs_per_row = _tiling(x, tm, seq_len)
    has_pos = pos is not None
    has_h0 = h0b is not None
    hb = tm // HALO
    n_hb = l_n // HALO
    cpt = tm // SSD_CHUNK
    fwd = lambda j: n_tiles - 1 - jnp.minimum(j, n_tiles - 1)
    scanned = lambda j: jnp.minimum(n_tiles - j, n_tiles - 1)
    tok = lambda s, j: (s, fwd(j), 0)
    seq_of = lambda s, j: (s * seqs_per_row + scanned(j) // tiles_per_seq, 0, 0)
    in_specs = [pl.BlockSpec((1, tm, D_MODEL), tok),
                pl.BlockSpec((1, HALO, D_MODEL),
                             lambda s, j: (s, jnp.maximum(fwd(j) * hb - 1, 0), 0)),
                pl.BlockSpec((1, HALO, D_MODEL),
                             lambda s, j: (s, jnp.minimum((fwd(j) + 1) * hb, n_hb - 1), 0))]
    args = [x, x, x]
    if has_pos:
        in_specs += [_const_spec(pos[0].shape), _const_spec(pos[1].shape)]
        args += list(pos)
    state_spec = pl.BlockSpec((1, D_SSD, SSD_STATE), seq_of)
    if has_h0:
        in_specs.append(state_spec)
        args.append(h0b)
    in_specs.append(pl.BlockSpec((1, 1, 6 * D_MODEL), lambda s, j: (mod_row0 + s, 0, 0)))
    args.append(mods)
    for a in (w_front, convw, convb, dtbias, aneg, e_b):
        in_specs.append(_const_spec(a.shape))
        args.append(a)
    return pl.pallas_call(
        functools.partial(_front_kernel, has_pos=has_pos, has_h0=has_h0,
                          tiles_per_seq=tiles_per_seq, n_tiles=n_tiles),
        grid=(s_n, n_tiles + 1),
        in_specs=in_specs,
        out_specs=[pl.BlockSpec((1, tm, w), tok) for w in FRONT_SEGS]
        + [pl.BlockSpec((1, cpt, SSD_STATE, D_SSD), lambda s, j: (s, scanned(j), 0, 0)),
           state_spec],
        out_shape=[jax.ShapeDtypeStruct((s_n, l_n, w), F32) for w in FRONT_SEGS]
        + [jax.ShapeDtypeStruct((s_n, l_n // SSD_CHUNK, SSD_STATE, D_SSD), F32),
           jax.ShapeDtypeStruct((s_n * seqs_per_row, D_SSD, SSD_STATE), F32)],
        scratch_shapes=[pltpu.VMEM((SSD_STATE, D_SSD), F32),
                        pltpu.VMEM((2, tm, D_SSD + BC_COLS), F32),
                        pltpu.VMEM((2, tm, DT_PAD), F32)],
        compiler_params=_params(48, 2),
        name="front",
    )(*args)


MAIN_Z, MAIN_U, MAIN_V, MAIN_GA, MAIN_GB = (i * D_MODEL for i in range(5))
MAIN_COLS = 5 * D_MODEL


def _main_kernel(*refs, has_pos, has_h0, tiles_per_seq):
    x_ref, xa_ref, dtp_ref, hbs_ref = refs[:4]
    refs = refs[4:]
    if has_pos:
        embr_ref, embc_ref = refs[:2]
        refs = refs[2:]
    if has_h0:
        h0f_ref = refs[0]
        refs = refs[1:]
    (mod_ref, w_ref, dtbias_ref, aneg_ref, dskip_ref, normw_ref, ef_ref, eb_ref, lng_ref,
     lnb_ref, wsp_ref, bsp_ref, wpa_ref, wpb_ref, wout_ref, ln1g_ref, ln1b_ref) = refs[:17]
    o_ref, hff_ref, hf_s = refs[17:]
    tm = x_ref.shape[1]
    q = SSD_CHUNK
    tile = pl.program_id(1)
    t = tile % tiles_per_seq

    if has_pos:
        x = _add_pos(x_ref, embr_ref, embc_ref, tile * (tm // GRID_W))
    else:
        x = x_ref[0]
    shift1 = mod_ref[0, :, 0:D_MODEL]
    scale1 = mod_ref[0, :, D_MODEL:2 * D_MODEL]
    gate1 = mod_ref[0, :, 2 * D_MODEL:3 * D_MODEL]
    h = (x * (1.0 + scale1) + shift1).astype(BF16)

    col_tiles = {}
    pending = list(range(0, MAIN_COLS, IN_NTILE))
    n_slots = (tm // q) * HEAD_PAIRS
    n_fill = (MAIN_COLS - MAIN_U) // IN_NTILE
    slot = [0]

    def project_next():
        c = pending.pop(0)
        col_tiles[c] = jnp.dot(h, w_ref[:, c:c + IN_NTILE], preferred_element_type=F32)

    def between_pairs():
        slot[0] += 1
        done = n_fill - len(pending)
        for _ in range(-(-slot[0] * n_fill // n_slots) - done):
            project_next()

    def in_proj(c0):
        while pending and pending[0] < c0 + D_MODEL:
            project_next()
        return jnp.concatenate([col_tiles.pop(c) for c in range(c0, c0 + D_MODEL, IN_NTILE)],
                               axis=1)

    def init_state():
        if has_h0:
            hf_s[...] = h0f_ref[0].T
        else:
            hf_s[...] = jnp.zeros_like(hf_s)

    if tiles_per_seq > 1:
        pl.when(t == 0)(init_state)
    else:
        init_state()
    h_state = hf_s[...]
    zz = in_proj(MAIN_Z)
    zz = _silu(zz)
    ya = []
    for ch in range(tm // q):
        rows = slice(ch * q, (ch + 1) * q)
        y, h_state = _ssd_chunk(
            xa_ref[0, rows, 0:D_SSD], xa_ref[0, rows, D_SSD:D_SSD + BC_COLS],
            xa_ref[0, rows, D_SSD + BC_COLS:], dtp_ref[0, rows, :], h_state, hbs_ref[0, ch],
            dtbias_ref, aneg_ref, dskip_ref, ef_ref, eb_ref, between_pairs)
        yg = y * zz[rows]
        ms = jnp.mean(yg * yg, axis=-1, keepdims=True)
        ya.append(yg * lax.rsqrt(ms + EPS) * normw_ref[...])
    hf_s[...] = h_state

    def final_state():
        hff_ref[0] = h_state.T

    if tiles_per_seq > 1:
        pl.when(t == tiles_per_seq - 1)(final_state)
    else:
        final_state()
    ya = jnp.concatenate(ya, axis=0)

    u = _gelu(in_proj(MAIN_U))
    v = _layer_norm(_gelu(in_proj(MAIN_V)), lng_ref[...], lnb_ref[...])
    rows_out = []
    for ch in range(tm // SGU_CHUNK):
        vc = v[ch * SGU_CHUNK:(ch + 1) * SGU_CHUNK, :]
        cols = [_dot(wsp_ref[g], vc[:, g * SGU_GROUP_DIM:(g + 1) * SGU_GROUP_DIM])
                for g in range(SGU_GROUPS)]
        rows_out.append(jnp.concatenate(cols, axis=1) + bsp_ref[...])
    yb = u * jnp.concatenate(rows_out, axis=0)

    merged = (_sigmoid(in_proj(MAIN_GA)) * _dot(ya, wpa_ref[...])
              + _sigmoid(in_proj(MAIN_GB)) * _dot(yb, wpb_ref[...]))
    o = _dot(merged, wout_ref[...])
    o_ref[0] = _layer_norm(DN_ALPHA * x + gate1 * o, ln1g_ref[...], ln1b_ref[...])


def _main(x, pos, xa, dtp, hbs, h0f, mods, mod_row0, tm, seq_len, consts):
    s_n, l_n, n_tiles, tiles_per_seq, seqs_per_row = _tiling(x, tm, seq_len)
    has_pos = pos is not None
    has_h0 = h0f is not None
    cpt = tm // SSD_CHUNK
    tok = lambda s, j: (s, j, 0)
    in_specs = [pl.BlockSpec((1, tm, D_MODEL), tok),
                pl.BlockSpec((1, tm, SSD_CONV_DIM), tok),
                pl.BlockSpec((1, tm, DT_PAD), tok),
                pl.BlockSpec((1, cpt, SSD_STATE, D_SSD), lambda s, j: (s, j, 0, 0))]
    args = [x, xa, dtp, hbs]
    if has_pos:
        in_specs += [_const_spec(pos[0].shape), _const_spec(pos[1].shape)]
        args += list(pos)
    state_spec = pl.BlockSpec((1, D_SSD, SSD_STATE),
                              lambda s, j: (s * seqs_per_row + j // tiles_per_seq, 0, 0))
    if has_h0:
        in_specs.append(state_spec)
        args.append(h0f)
    in_specs.append(pl.BlockSpec((1, 1, 6 * D_MODEL), lambda s, j: (mod_row0 + s, 0, 0)))
    args.append(mods)
    for a in consts:
        in_specs.append(_const_spec(a.shape))
        args.append(a)
    return pl.pallas_call(
        functools.partial(_main_kernel, has_pos=has_pos, has_h0=has_h0,
                          tiles_per_seq=tiles_per_seq),
        grid=(s_n, n_tiles),
        in_specs=in_specs,
        out_specs=[pl.BlockSpec((1, tm, D_MODEL), tok), state_spec],
        out_shape=[jax.ShapeDtypeStruct((s_n, l_n, D_MODEL), F32),
                   jax.ShapeDtypeStruct((s_n * seqs_per_row, D_SSD, SSD_STATE), F32)],
        scratch_shapes=[pltpu.VMEM((SSD_STATE, D_SSD), F32)],
        compiler_params=_params(56, 2),
        name="main",
    )(*args)


FF_TILE = 1024
FFN_TM = 512


def _ffn_kernel(x_ref, mod_ref, w1_ref, w2_ref, g_ref, b_ref, o_ref):
    x = x_ref[0]
    shift2 = mod_ref[0, :, 3 * D_MODEL:4 * D_MODEL]
    scale2 = mod_ref[0, :, 4 * D_MODEL:5 * D_MODEL]
    gate2 = mod_ref[0, :, 5 * D_MODEL:6 * D_MODEL]
    h = (x * (1.0 + scale2) + shift2).astype(BF16)
    f = jnp.zeros_like(x)
    for k0 in range(0, D_FF, FF_TILE):
        a = jnp.dot(h, w1_ref[:, k0:k0 + FF_TILE], preferred_element_type=F32)
        a = jnp.maximum(a, 0.0)
        f = f + _dot(a * a, w2_ref[k0:k0 + FF_TILE, :])
    o_ref[0] = _layer_norm(DN_ALPHA * x + gate2 * f, g_ref[...], b_ref[...])


def _ffn(x, mods, mod_row0, tm, w1, w2, g, b):
    s_n, l_n, _ = x.shape
    tok = lambda s, j: (s, j, 0)
    return pl.pallas_call(
        _ffn_kernel,
        grid=(s_n, l_n // tm),
        in_specs=[pl.BlockSpec((1, tm, D_MODEL), tok),
                  pl.BlockSpec((1, 1, 6 * D_MODEL), lambda s, j: (mod_row0 + s, 0, 0)),
                  _const_spec(w1.shape), _const_spec(w2.shape),
                  _const_spec(g.shape), _const_spec(b.shape)],
        out_specs=pl.BlockSpec((1, tm, D_MODEL), tok),
        out_shape=jax.ShapeDtypeStruct((s_n, l_n, D_MODEL), F32),
        compiler_params=_params(48, 2),
        name="ffn",
    )(x, mods, w1, w2, g, b)


def _grid_pos_tables(n_tokens):
    rows = n_tokens // GRID_W
    quarter = D_MODEL // 4
    omega = 1.0 / (POS_BASE ** (jnp.arange(quarter, dtype=F32) / quarter))
    r = jnp.arange(rows, dtype=F32)[:, None] * omega
    col = jnp.arange(GRID_W, dtype=F32)[:, None] * omega
    emb_r = jnp.concatenate([jnp.sin(r), jnp.cos(r)], axis=-1)
    emb_c = jnp.concatenate([jnp.sin(col), jnp.cos(col)], axis=-1)
    return jnp.broadcast_to(emb_r[:, None, :], (rows, SUBLANES, D_MODEL // 2)), emb_c


def _dt_lanes(v, axis):
    pad = [(0, 0)] * v.ndim
    pad[axis] = (0, DT_PAD - DT_COPIES * DT_GROUP)
    return jnp.pad(jnp.concatenate([v] * DT_COPIES, axis=axis), pad)


def _head_expanders():
    lane = jnp.arange(LANES)[:, None]
    head = jnp.arange(D_SSD)[None, :] // SSD_HEAD_DIM
    valid = lane < DT_COPIES * DT_GROUP
    e_f = jnp.logical_and(valid, lane % DT_GROUP == head)
    e_b = jnp.logical_and(valid, lane % DT_GROUP == head + SSD_HEADS)
    return e_f.astype(BF16), e_b.astype(BF16)


def _stream(x, pos, mods, mod_row0, seq_len, h0f, h0b, prm, tm):
    xa, dtp, hbs, hb = _front(x, pos, mods, h0b, prm["w_front"], prm["conv_w"], prm["conv_b"],
                              prm["dt_bias"], prm["a_neg"], prm["e_b"], mod_row0, tm, seq_len)
    x1, hf = _main(x, pos, xa, dtp, hbs, h0f, mods, mod_row0, tm, seq_len, prm["main"])
    y = _ffn(x1, mods, mod_row0, FFN_TM, prm["w_ff1"], prm["w_ff2"], prm["ln2_g"], prm["ln2_b"])
    return y, hf, hb


def kernel(x_prompt, x_sample, state_ssd_fwd, state_ssd_bwd, c, c_ctx, w_ada, b_ada, w_in, conv_w, conv_b, dt_bias_fwd, dt_bias_bwd, a_log_fwd, a_log_bwd, d_skip, ssd_norm_w, sgu_ln_g, sgu_ln_b, w_spatial, b_spatial, w_proj_a, w_proj_b, w_out, ln1_g, ln1_b, w_ff1, w_ff2, ln2_g, ln2_b):
    batch, seq, _ = x_prompt.shape
    dec_batch, dec_seq, _ = x_sample.shape
    l = 0
    row = lambda v: v.reshape(1, -1)

    cond = jnp.zeros((SUBLANES, D_MODEL), F32).at[0].set(c_ctx).at[1:1 + dec_batch].set(c)
    mods = _ada(cond, w_ada[l], b_ada[l]).reshape(SUBLANES, 1, 6 * D_MODEL)

    w_front, w_main = _w_in_split(w_in[l:l + 1])
    e_f, e_b = _head_expanders()
    dt_bias = row(_dt_lanes(jnp.concatenate([dt_bias_fwd[l], dt_bias_bwd[l]]), 0))
    a_neg = row(_dt_lanes(-jnp.exp(jnp.concatenate([a_log_fwd[l], a_log_bwd[l]])), 0))

    prm = dict(
        w_front=w_front, conv_w=conv_w[l], conv_b=row(conv_b[l]),
        dt_bias=dt_bias, a_neg=a_neg, e_b=e_b,
        main=(w_main, dt_bias, a_neg, row(jnp.repeat(d_skip[l], SSD_HEAD_DIM)),
              row(ssd_norm_w[l]), e_f, e_b, row(sgu_ln_g[l]), row(sgu_ln_b[l]),
              w_spatial[l].astype(BF16), jnp.repeat(b_spatial[l].T, SGU_GROUP_DIM, axis=1),
              w_proj_a[l].astype(BF16), w_proj_b[l].astype(BF16), w_out[l].astype(BF16),
              row(ln1_g[l]), row(ln1_b[l])),
        w_ff1=w_ff1[l].astype(BF16), w_ff2=w_ff2[l].astype(BF16),
        ln2_g=row(ln2_g[l]), ln2_b=row(ln2_b[l]),
    )

    tm = 256
    xp, hf, hb = _stream(x_prompt.reshape(1, batch * seq, D_MODEL), None, mods, 0, seq,
                         None, None, prm, tm)
    pos = _grid_pos_tables(dec_seq)
    h0f = state_ssd_fwd[:, l].reshape(dec_batch, D_SSD, SSD_STATE)
    h0b = state_ssd_bwd[:, l].reshape(dec_batch, D_SSD, SSD_STATE)
    xs, _, _ = _stream(x_sample, pos, mods, 1, dec_seq, h0f, h0b, prm, tm)

    state_shape = (batch, DEPTH, SSD_HEADS, SSD_HEAD_DIM, SSD_STATE)
    return (xp.reshape(batch, seq, D_MODEL), xs,
            hf.reshape(state_shape), hb.reshape(state_shape))
```

```python
import functools
import math

import jax
import jax.numpy as jnp
from jax import lax
from jax.experimental import pallas as pl
from jax.experimental.pallas import tpu as pltpu

D_MODEL = 1024
GRID_W = 64
POS_BASE = 10000.0
D_SSD = 1024
SSD_HEAD_DIM = 64
SSD_HEADS = D_SSD // SSD_HEAD_DIM
SSD_GROUPS = 2
SSD_STATE = 128
SSD_CHUNK = 128
SSD_CONV_DIM = D_SSD + 2 * SSD_GROUPS * SSD_STATE
D_SGU = 1024
SGU_GROUPS = 8
SGU_GROUP_DIM = D_SGU // SGU_GROUPS
SGU_CHUNK = 128
D_FF = 4 * D_MODEL
DEPTH = 1
DN_ALPHA = (2.0 * DEPTH) ** 0.25
EPS = 1e-5

LANES = 128
SUBLANES = 8
BF16_ROWS = 16

DT_GROUP = 2 * SSD_HEADS
DT_COPIES = 3
DT_PAD = LANES
HEAD_PAIRS = SSD_HEADS // 2
PAIRS_PER_GROUP = HEAD_PAIRS // SSD_GROUPS
GROUP_COLS = D_SSD // SSD_GROUPS
BC_COLS = SSD_GROUPS * SSD_STATE

BF16 = jnp.bfloat16
F32 = jnp.float32
HIGHEST = lax.Precision.HIGHEST


def _dot(a, b):
    return jnp.dot(a.astype(BF16), b.astype(BF16), preferred_element_type=F32)


def _dot_exact(a, b):
    return jnp.dot(a, b, precision=HIGHEST, preferred_element_type=F32)


def _layer_norm(x, g, b):
    mu = jnp.mean(x, axis=-1, keepdims=True)
    xc = x - mu
    var = jnp.mean(xc * xc, axis=-1, keepdims=True)
    return xc * lax.rsqrt(var + EPS) * g + b


def _softplus(x):
    return jnp.maximum(x, 0.0) + jnp.log1p(jnp.exp(-jnp.abs(x)))


def _gelu(x):
    return 0.5 * x * (1.0 + lax.erf(x * (1.0 / math.sqrt(2.0))))


def _sigmoid(x):
    return 0.5 * jnp.tanh(0.5 * x) + 0.5


def _silu(x):
    half = 0.5 * x
    return half * jnp.tanh(half) + half


def _const_spec(shape):
    zeros = (0,) * len(shape)
    return pl.BlockSpec(shape, lambda *_: zeros, pipeline_mode=pl.Buffered(1))


def _params(vmem_mb, n_axes):
    return pltpu.CompilerParams(
        dimension_semantics=("arbitrary",) * n_axes,
        vmem_limit_bytes=vmem_mb * 1024 * 1024)


def _ada_kernel(c_ref, w_ref, b_ref, o_ref):
    o_ref[...] = _dot_exact(_silu(c_ref[...]), w_ref[...]) + b_ref[...]


def _ada(cond, w_ada, b_ada):
    n = w_ada.shape[1]
    tn = 2048
    return pl.pallas_call(
        _ada_kernel,
        grid=(n // tn,),
        in_specs=[pl.BlockSpec((SUBLANES, D_MODEL), lambda j: (0, 0)),
                  pl.BlockSpec((D_MODEL, tn), lambda j: (0, j)),
                  pl.BlockSpec((1, tn), lambda j: (0, j))],
        out_specs=pl.BlockSpec((SUBLANES, tn), lambda j: (0, j)),
        out_shape=jax.ShapeDtypeStruct((SUBLANES, n), F32),
        compiler_params=_params(32, 1),
        name="ada",
    )(cond, w_ada, b_ada.reshape(1, n))


W_COL_TILE = 512


def _w_cols_kernel(src_ref, o_ref):
    o_ref[...] = src_ref[...].T.astype(BF16)


def _w_cols(w_t, col_start, n_tiles):
    d = w_t.shape[1]
    return pl.pallas_call(
        _w_cols_kernel,
        grid=(n_tiles,),
        in_specs=[pl.BlockSpec((pl.Element(W_COL_TILE), pl.Element(d)),
                               lambda i: (col_start(i) * DT_GROUP, 0))],
        out_specs=pl.BlockSpec((d, W_COL_TILE), lambda i: (0, i)),
        out_shape=jax.ShapeDtypeStruct((d, n_tiles * W_COL_TILE), BF16),
        compiler_params=_params(32, 1),
        name="w_cols",
    )(w_t)


IN_NTILE = 512
HALO = BF16_ROWS


def _add_pos(x_ref, embr_ref, embc_ref, grid_row0):
    n_rows = x_ref.shape[1] // GRID_W
    col = embc_ref[...]
    blocks = []
    for i in range(n_rows):
        row = embr_ref[grid_row0 + i]
        row = jnp.concatenate([row] * (GRID_W // SUBLANES), axis=0)
        blocks.append(x_ref[0, i * GRID_W:(i + 1) * GRID_W, :]
                      + jnp.concatenate([row, col], axis=1))
    return jnp.concatenate(blocks, axis=0)


def _pos_halo(embr_ref, embc_ref, grid_row, col0):
    left = jnp.concatenate([embr_ref[grid_row]] * (HALO // SUBLANES), axis=0)
    return jnp.concatenate([left, embc_ref[col0:col0 + HALO, :]], axis=1)


def _block_diag_pair(xp):
    lane = lax.broadcasted_iota(jnp.int32, xp.shape, 1)
    first = lane < SSD_HEAD_DIM
    return jnp.concatenate([jnp.where(first, xp, 0.0), jnp.where(first, 0.0, xp)], axis=0)


def _split_terms(v):
    lane = lax.broadcasted_iota(jnp.int32, v.shape, 1)
    hi = v.astype(BF16).astype(F32)
    r1 = v - hi
    mid = r1.astype(BF16).astype(F32)
    lo = r1 - mid
    return jnp.where(lane < DT_GROUP, hi, jnp.where(lane < 2 * DT_GROUP, mid, lo)).astype(BF16)


def _expand(v, e_ref):
    return jnp.dot(_split_terms(v), e_ref[...], preferred_element_type=F32)


def _chunk_decay(dtp, dtbias_ref, aneg_ref):
    q = dtp.shape[0]
    dt_all = _softplus(dtp + dtbias_ref[...])
    a_all = dt_all * aneg_ref[...]
    ii = lax.broadcasted_iota(jnp.int32, (q, q), 0)
    jj = lax.broadcasted_iota(jnp.int32, (q, q), 1)
    acum = _dot_exact((ii >= jj).astype(F32), a_all)
    atot = acum[q - 1:q, :]
    rcum = atot - acum + a_all
    return dt_all, a_all, acum, atot, rcum, rcum[0:1, :]


def _carry_update(h_old, w_tok, tot_row, e_ref, xh, b_all):
    wx = (_expand(w_tok, e_ref) * xh).astype(BF16)
    s = jnp.concatenate(
        [jnp.dot(b_all[:, g * SSD_STATE:(g + 1) * SSD_STATE].T.astype(BF16),
                 wx[:, g * GROUP_COLS:(g + 1) * GROUP_COLS], preferred_element_type=F32)
         for g in range(SSD_GROUPS)], axis=1)
    decay = jnp.exp(_expand(jnp.broadcast_to(tot_row, (BF16_ROWS, LANES)), e_ref))[0:1, :]
    return decay * h_old + s


def _ssd_chunk(xh, b_all, c_all, dtp, hf_in, hb_in, dtbias_ref, aneg_ref, dskip_ref, ef_ref,
               eb_ref, between_pairs=lambda: None):
    q = SSD_CHUNK
    dt_all, a_all, acum, atot, rcum, _ = _chunk_decay(dtp, dtbias_ref, aneg_ref)
    ii = lax.broadcasted_iota(jnp.int32, (q, q), 0)
    jj = lax.broadcasted_iota(jnp.int32, (q, q), 1)
    lower = ii >= jj
    diag = ii == jj
    dt_t = dt_all.T
    a_t = a_all.T
    acum_t = _dot_exact(a_t, (ii <= jj).astype(F32))
    rcum_t = acum_t[:, q - 1:q] - acum_t + a_t

    ef = jnp.exp(_expand(acum, ef_ref))
    eb = jnp.exp(_expand(rcum, eb_ref))
    y_off = []
    cb = []
    for g in range(SSD_GROUPS):
        gs = slice(g * GROUP_COLS, (g + 1) * GROUP_COLS)
        c_g = c_all[:, g * SSD_STATE:(g + 1) * SSD_STATE].astype(BF16)
        b_g = b_all[:, g * SSD_STATE:(g + 1) * SSD_STATE].astype(BF16)
        cb.append(lax.dot_general(c_g, b_g, (((1,), (1,)), ((), ())),
                                  preferred_element_type=F32))
        y_off.append(ef[:, gs] * jnp.dot(c_g, hf_in[:, gs].astype(BF16),
                                         preferred_element_type=F32)
                     + eb[:, gs] * jnp.dot(c_g, hb_in[:, gs].astype(BF16),
                                           preferred_element_type=F32))
    y_off = jnp.concatenate(y_off, axis=1)

    y_cols = []
    for pr in range(HEAD_PAIRS):
        g = pr // PAIRS_PER_GROUP
        m_parts = []
        for h in (2 * pr, 2 * pr + 1):
            hb_lane = SSD_HEADS + h
            colf = jnp.broadcast_to(acum[:, h:h + 1], (q, q))
            colb = jnp.broadcast_to(rcum[:, hb_lane:hb_lane + 1], (q, q))
            seg = jnp.where(lower, colf - acum_t[h:h + 1, :],
                            colb - rcum_t[hb_lane:hb_lane + 1, :])
            dt_b_row = dt_t[hb_lane:hb_lane + 1, :]
            w = (jnp.exp(seg) * jnp.where(lower, dt_t[h:h + 1, :], dt_b_row)
                 + jnp.where(diag, dt_b_row, 0.0))
            m_parts.append(cb[g] * w)
        sl = slice(pr * LANES, (pr + 1) * LANES)
        y_cols.append(_dot(jnp.concatenate(m_parts, axis=1), _block_diag_pair(xh[:, sl])))
        between_pairs()
    y = jnp.concatenate(y_cols, axis=1) + y_off + dskip_ref[...] * xh
    h_new = _carry_update(hf_in, dt_all * jnp.exp(atot - acum), atot, ef_ref, xh, b_all)
    return y, h_new


def _tiling(x, tm, seq_len):
    s_n, l_n, _ = x.shape
    return s_n, l_n, l_n // tm, seq_len // tm, l_n // seq_len


FRONT_SEGS = (SSD_CONV_DIM, DT_PAD)
FRONT_NTILE = 256


def _front_kernel(*refs, has_pos, has_h0, tiles_per_seq, n_tiles):
    x_ref, xp_ref, xn_ref = refs[:3]
    refs = refs[3:]
    if has_pos:
        embr_ref, embc_ref = refs[:2]
        refs = refs[2:]
    if has_h0:
        h0b_ref = refs[0]
        refs = refs[1:]
    mod_ref, w_ref, wdt_ref, convw_ref, convb_ref, dtbias_ref, aneg_ref, eb_ref = refs[:8]
    xa_ref, dt_ref, hbs_ref, hfb_ref, hb_s, xb_s, dts_s = refs[8:]
    shift1 = mod_ref[0, :, 0:D_MODEL]
    scale1 = mod_ref[0, :, D_MODEL:2 * D_MODEL]
    tm = x_ref.shape[1]
    n_ext = tm + 2 * HALO
    q = SSD_CHUNK
    j = pl.program_id(1)
    tile = n_tiles - 1 - jnp.minimum(j, n_tiles - 1)
    t = tile % tiles_per_seq
    t_scan = jnp.minimum(n_tiles - j, n_tiles - 1) % tiles_per_seq
    slot = j % 2
    scan_slot = 1 - slot

    @pl.when(j == 0)
    def _():
        xb_s[1] = jnp.zeros(xb_s.shape[1:], F32)
        dts_s[1] = jnp.zeros(dts_s.shape[1:], F32)

    def init_state():
        if has_h0:
            hb_s[...] = h0b_ref[0].T
        else:
            hb_s[...] = jnp.zeros_like(hb_s)

    if tiles_per_seq > 1:
        pl.when(t_scan == tiles_per_seq - 1)(init_state)
    else:
        init_state()

    def modulate(x):
        return (x * (1.0 + scale1) + shift1).astype(BF16)

    if has_pos:
        rows_per_tile = tm // GRID_W
        n_grid_rows = embr_ref.shape[0]
        r0 = tile * rows_per_tile
        x = _add_pos(x_ref, embr_ref, embc_ref, r0)
        x_prev = xp_ref[0] + _pos_halo(embr_ref, embc_ref, jnp.maximum(r0 - 1, 0),
                                       GRID_W - HALO)
        x_next = xn_ref[0] + _pos_halo(embr_ref, embc_ref,
                                       jnp.minimum(r0 + rows_per_tile, n_grid_rows - 1), 0)
    else:
        x, x_prev, x_next = x_ref[0], xp_ref[0], xn_ref[0]
    h = modulate(x)
    h_ext = jnp.concatenate([modulate(x_prev), h, modulate(x_next)], axis=0)

    row = lax.broadcasted_iota(jnp.int32, (tm, 1), 0)
    zero_prev = row == 0
    zero_next = row == tm - 1
    if tiles_per_seq > 1:
        zero_prev = jnp.logical_and(zero_prev, t == 0)
        zero_next = jnp.logical_and(zero_next, t == tiles_per_seq - 1)

    def project():
        dt = jnp.dot(h, wdt_ref[...], preferred_element_type=F32)
        dt_ref[0] = dt
        dts_s[slot] = dt
        yield
        for c0 in range(0, SSD_CONV_DIM, FRONT_NTILE):
            c1 = c0 + FRONT_NTILE
            p = jnp.dot(h_ext, w_ref[:, c0:c1], preferred_element_type=F32)
            cur = p[HALO:HALO + tm]
            prv = jnp.where(zero_prev, 0.0, pltpu.roll(p, 1, 0)[HALO:HALO + tm])
            nxt = jnp.where(zero_next, 0.0, pltpu.roll(p, n_ext - 1, 0)[HALO:HALO + tm])
            xc = (convw_ref[0:1, c0:c1] * prv + convw_ref[1:2, c0:c1] * cur
                  + convw_ref[2:3, c0:c1] * nxt + convb_ref[:, c0:c1])
            xa = _silu(xc)
            xa_ref[0, :, c0:c1] = xa
            if c0 < D_SSD + BC_COLS:
                xb_s[slot, :, c0:c1] = xa
            yield

    final = []

    def scan():
        h_state = hb_s[...]
        for ch in reversed(range(tm // q)):
            rows = slice(ch * q, (ch + 1) * q)
            hbs_ref[0, ch] = h_state
            dt_all, _, _, _, rcum, rtot = _chunk_decay(dts_s[scan_slot, rows, :], dtbias_ref,
                                                       aneg_ref)
            yield
            w_exp = _expand(dt_all * jnp.exp(rtot - rcum), eb_ref)
            decay = jnp.exp(_expand(jnp.broadcast_to(rtot, (BF16_ROWS, LANES)), eb_ref))[0:1, :]
            yield
            wx = (w_exp * xb_s[scan_slot, rows, 0:D_SSD]).astype(BF16)
            b_all = xb_s[scan_slot, rows, D_SSD:D_SSD + BC_COLS]
            parts = []
            for g in range(SSD_GROUPS):
                parts.append(jnp.dot(b_all[:, g * SSD_STATE:(g + 1) * SSD_STATE].T.astype(BF16),
                                     wx[:, g * GROUP_COLS:(g + 1) * GROUP_COLS],
                                     preferred_element_type=F32))
                yield
            h_state = decay * h_state + jnp.concatenate(parts, axis=1)
        hb_s[...] = h_state
        final.append(h_state)

    stages = [project(), scan()]
    while stages:
        for gen in list(stages):
            if next(gen, StopIteration) is StopIteration:
                stages.remove(gen)

    def final_state():
        hfb_ref[0] = final[0].T

    if tiles_per_seq > 1:
        pl.when(t_scan == 0)(final_state)
    else:
        final_state()


def _front(x, pos, mods, h0b, w_xbc, w_dt, convw, convb, dtbias, aneg, e_b, mod_row0, tm,
           seq_len):
    s_n, l_n, n_tiles, tiles_per_seq, seqs_per_row = _tiling(x, tm, seq_len)
    has_pos = pos is not None
    has_h0 = h0b is not None
    hb = tm // HALO
    n_hb = l_n // HALO
    cpt = tm // SSD_CHUNK
    fwd = lambda j: n_tiles - 1 - jnp.minimum(j, n_tiles - 1)
    scanned = lambda j: jnp.minimum(n_tiles - j, n_tiles - 1)
    tok = lambda s, j: (s, fwd(j), 0)
    seq_of = lambda s, j: (s * seqs_per_row + scanned(j) // tiles_per_seq, 0, 0)
    in_specs = [pl.BlockSpec((1, tm, D_MODEL), tok),
                pl.BlockSpec((1, HALO, D_MODEL),
                             lambda s, j: (s, jnp.maximum(fwd(j) * hb - 1, 0), 0)),
                pl.BlockSpec((1, HALO, D_MODEL),
                             lambda s, j: (s, jnp.minimum((fwd(j) + 1) * hb, n_hb - 1), 0))]
    args = [x, x, x]
    if has_pos:
        in_specs += [_const_spec(pos[0].shape), _const_spec(pos[1].shape)]
        args += list(pos)
    state_spec = pl.BlockSpec((1, D_SSD, SSD_STATE), seq_of)
    if has_h0:
        in_specs.append(state_spec)
        args.append(h0b)
    in_specs.append(pl.BlockSpec((1, 1, 6 * D_MODEL), lambda s, j: (mod_row0 + s, 0, 0)))
    args.append(mods)
    for a in (w_xbc, w_dt, convw, convb, dtbias, aneg, e_b):
        in_specs.append(_const_spec(a.shape))
        args.append(a)
    return pl.pallas_call(
        functools.partial(_front_kernel, has_pos=has_pos, has_h0=has_h0,
                          tiles_per_seq=tiles_per_seq, n_tiles=n_tiles),
        grid=(s_n, n_tiles + 1),
        in_specs=in_specs,
        out_specs=[pl.BlockSpec((1, tm, w), tok) for w in FRONT_SEGS]
        + [pl.BlockSpec((1, cpt, SSD_STATE, D_SSD), lambda s, j: (s, scanned(j), 0, 0)),
           state_spec],
        out_shape=[jax.ShapeDtypeStruct((s_n, l_n, w), F32) for w in FRONT_SEGS]
        + [jax.ShapeDtypeStruct((s_n, l_n // SSD_CHUNK, SSD_STATE, D_SSD), F32),
           jax.ShapeDtypeStruct((s_n * seqs_per_row, D_SSD, SSD_STATE), F32)],
        scratch_shapes=[pltpu.VMEM((SSD_STATE, D_SSD), F32),
                        pltpu.VMEM((2, tm, D_SSD + BC_COLS), F32),
                        pltpu.VMEM((2, tm, DT_PAD), F32)],
        compiler_params=_params(48, 2),
        name="front",
    )(*args)


MAIN_Z, MAIN_U, MAIN_V, MAIN_GA, MAIN_GB = (i * D_MODEL for i in range(5))
MAIN_COLS = 5 * D_MODEL


def _main_kernel(*refs, has_pos, has_h0, tiles_per_seq):
    x_ref, xa_ref, dtp_ref, hbs_ref = refs[:4]
    refs = refs[4:]
    if has_pos:
        embr_ref, embc_ref = refs[:2]
        refs = refs[2:]
    if has_h0:
        h0f_ref = refs[0]
        refs = refs[1:]
    (mod_ref, w_ref, dtbias_ref, aneg_ref, dskip_ref, normw_ref, ef_ref, eb_ref, lng_ref,
     lnb_ref, wsp_ref, bsp_ref, wpa_ref, wpb_ref, wout_ref, ln1g_ref, ln1b_ref) = refs[:17]
    o_ref, hff_ref, hf_s = refs[17:]
    tm = x_ref.shape[1]
    q = SSD_CHUNK
    tile = pl.program_id(1)
    t = tile % tiles_per_seq

    if has_pos:
        x = _add_pos(x_ref, embr_ref, embc_ref, tile * (tm // GRID_W))
    else:
        x = x_ref[0]
    shift1 = mod_ref[0, :, 0:D_MODEL]
    scale1 = mod_ref[0, :, D_MODEL:2 * D_MODEL]
    gate1 = mod_ref[0, :, 2 * D_MODEL:3 * D_MODEL]
    h = (x * (1.0 + scale1) + shift1).astype(BF16)

    col_tiles = {}
    pending = list(range(0, MAIN_COLS, IN_NTILE))
    n_slots = (tm // q) * HEAD_PAIRS
    n_fill = (MAIN_COLS - MAIN_U) // IN_NTILE
    slot = [0]

    def project_next():
        c = pending.pop(0)
        col_tiles[c] = jnp.dot(h, w_ref[:, c:c + IN_NTILE], preferred_element_type=F32)

    def between_pairs():
        slot[0] += 1
        done = n_fill - len(pending)
        for _ in range(-(-slot[0] * n_fill // n_slots) - done):
            project_next()

    def in_proj(c0):
        while pending and pending[0] < c0 + D_MODEL:
            project_next()
        return jnp.concatenate([col_tiles.pop(c) for c in range(c0, c0 + D_MODEL, IN_NTILE)],
                               axis=1)

    def init_state():
        if has_h0:
            hf_s[...] = h0f_ref[0].T
        else:
            hf_s[...] = jnp.zeros_like(hf_s)

    if tiles_per_seq > 1:
        pl.when(t == 0)(init_state)
    else:
        init_state()
    h_state = hf_s[...]
    zz = in_proj(MAIN_Z)
    zz = _silu(zz)
    ya = []
    for ch in range(tm // q):
        rows = slice(ch * q, (ch + 1) * q)
        y, h_state = _ssd_chunk(
            xa_ref[0, rows, 0:D_SSD], xa_ref[0, rows, D_SSD:D_SSD + BC_COLS],
            xa_ref[0, rows, D_SSD + BC_COLS:], dtp_ref[0, rows, :], h_state, hbs_ref[0, ch],
            dtbias_ref, aneg_ref, dskip_ref, ef_ref, eb_ref, between_pairs)
        yg = y * zz[rows]
        ms = jnp.mean(yg * yg, axis=-1, keepdims=True)
        ya.append(yg * lax.rsqrt(ms + EPS) * normw_ref[...])
    hf_s[...] = h_state

    def final_state():
        hff_ref[0] = h_state.T

    if tiles_per_seq > 1:
        pl.when(t == tiles_per_seq - 1)(final_state)
    else:
        final_state()
    ya = jnp.concatenate(ya, axis=0)

    u = _gelu(in_proj(MAIN_U))
    v = _layer_norm(_gelu(in_proj(MAIN_V)), lng_ref[...], lnb_ref[...])
    rows_out = []
    for ch in range(tm // SGU_CHUNK):
        vc = v[ch * SGU_CHUNK:(ch + 1) * SGU_CHUNK, :]
        cols = [_dot(wsp_ref[g], vc[:, g * SGU_GROUP_DIM:(g + 1) * SGU_GROUP_DIM])
                for g in range(SGU_GROUPS)]
        rows_out.append(jnp.concatenate(cols, axis=1) + bsp_ref[...])
    yb = u * jnp.concatenate(rows_out, axis=0)

    merged = (_sigmoid(in_proj(MAIN_GA)) * _dot(ya, wpa_ref[...])
              + _sigmoid(in_proj(MAIN_GB)) * _dot(yb, wpb_ref[...]))
    o = _dot(merged, wout_ref[...])
    o_ref[0] = _layer_norm(DN_ALPHA * x + gate1 * o, ln1g_ref[...], ln1b_ref[...])


def _main(x, pos, xa, dtp, hbs, h0f, mods, mod_row0, tm, seq_len, consts):
    s_n, l_n, n_tiles, tiles_per_seq, seqs_per_row = _tiling(x, tm, seq_len)
    has_pos = pos is not None
    has_h0 = h0f is not None
    cpt = tm // SSD_CHUNK
    tok = lambda s, j: (s, j, 0)
    in_specs = [pl.BlockSpec((1, tm, D_MODEL), tok),
                pl.BlockSpec((1, tm, SSD_CONV_DIM), tok),
                pl.BlockSpec((1, tm, DT_PAD), tok),
                pl.BlockSpec((1, cpt, SSD_STATE, D_SSD), lambda s, j: (s, j, 0, 0))]
    args = [x, xa, dtp, hbs]
    if has_pos:
        in_specs += [_const_spec(pos[0].shape), _const_spec(pos[1].shape)]
        args += list(pos)
    state_spec = pl.BlockSpec((1, D_SSD, SSD_STATE),
                              lambda s, j: (s * seqs_per_row + j // tiles_per_seq, 0, 0))
    if has_h0:
        in_specs.append(state_spec)
        args.append(h0f)
    in_specs.append(pl.BlockSpec((1, 1, 6 * D_MODEL), lambda s, j: (mod_row0 + s, 0, 0)))
    args.append(mods)
    for a in consts:
        in_specs.append(_const_spec(a.shape))
        args.append(a)
    return pl.pallas_call(
        functools.partial(_main_kernel, has_pos=has_pos, has_h0=has_h0,
                          tiles_per_seq=tiles_per_seq),
        grid=(s_n, n_tiles),
        in_specs=in_specs,
        out_specs=[pl.BlockSpec((1, tm, D_MODEL), tok), state_spec],
        out_shape=[jax.ShapeDtypeStruct((s_n, l_n, D_MODEL), F32),
                   jax.ShapeDtypeStruct((s_n * seqs_per_row, D_SSD, SSD_STATE), F32)],
        scratch_shapes=[pltpu.VMEM((SSD_STATE, D_SSD), F32)],
        compiler_params=_params(56, 2),
        name="main",
    )(*args)


FF_TILE = 1024
FFN_TM = 1024


def _ffn_kernel(x_ref, mod_ref, w1_ref, w2_ref, g_ref, b_ref, o_ref):
    x = x_ref[0]
    shift2 = mod_ref[0, :, 3 * D_MODEL:4 * D_MODEL]
    scale2 = mod_ref[0, :, 4 * D_MODEL:5 * D_MODEL]
    gate2 = mod_ref[0, :, 5 * D_MODEL:6 * D_MODEL]
    h = (x * (1.0 + scale2) + shift2).astype(BF16)
    f = jnp.zeros_like(x)
    for k0 in range(0, D_FF, FF_TILE):
        a = jnp.dot(h, w1_ref[:, k0:k0 + FF_TILE], preferred_element_type=F32)
        a = jnp.maximum(a, 0.0)
        f = f + _dot(a * a, w2_ref[k0:k0 + FF_TILE, :])
    o_ref[0] = _layer_norm(DN_ALPHA * x + gate2 * f, g_ref[...], b_ref[...])


def _ffn(x, mods, mod_row0, tm, w1, w2, g, b):
    s_n, l_n, _ = x.shape
    tok = lambda s, j: (s, j, 0)
    return pl.pallas_call(
        _ffn_kernel,
        grid=(s_n, l_n // tm),
        in_specs=[pl.BlockSpec((1, tm, D_MODEL), tok),
                  pl.BlockSpec((1, 1, 6 * D_MODEL), lambda s, j: (mod_row0 + s, 0, 0)),
                  _const_spec(w1.shape), _const_spec(w2.shape),
                  _const_spec(g.shape), _const_spec(b.shape)],
        out_specs=pl.BlockSpec((1, tm, D_MODEL), tok),
        out_shape=jax.ShapeDtypeStruct((s_n, l_n, D_MODEL), F32),
        compiler_params=_params(48, 2),
        name="ffn",
    )(x, mods, w1, w2, g, b)


def _grid_pos_tables(n_tokens):
    rows = n_tokens // GRID_W
    quarter = D_MODEL // 4
    omega = 1.0 / (POS_BASE ** (jnp.arange(quarter, dtype=F32) / quarter))
    r = jnp.arange(rows, dtype=F32)[:, None] * omega
    col = jnp.arange(GRID_W, dtype=F32)[:, None] * omega
    emb_r = jnp.concatenate([jnp.sin(r), jnp.cos(r)], axis=-1)
    emb_c = jnp.concatenate([jnp.sin(col), jnp.cos(col)], axis=-1)
    return jnp.broadcast_to(emb_r[:, None, :], (rows, SUBLANES, D_MODEL // 2)), emb_c


def _dt_lanes(v, axis):
    pad = [(0, 0)] * v.ndim
    pad[axis] = (0, DT_PAD - DT_COPIES * DT_GROUP)
    return jnp.pad(jnp.concatenate([v] * DT_COPIES, axis=axis), pad)


def _head_expanders():
    lane = jnp.arange(LANES)[:, None]
    head = jnp.arange(D_SSD)[None, :] // SSD_HEAD_DIM
    valid = lane < DT_COPIES * DT_GROUP
    e_f = jnp.logical_and(valid, lane % DT_GROUP == head)
    e_b = jnp.logical_and(valid, lane % DT_GROUP == head + SSD_HEADS)
    return e_f.astype(BF16), e_b.astype(BF16)


def _stream(x, pos, mods, mod_row0, seq_len, h0f, h0b, prm, tm):
    xa, dtp, hbs, hb = _front(x, pos, mods, h0b, prm["w_xbc"], prm["w_dt"], prm["conv_w"],
                              prm["conv_b"],
                              prm["dt_bias"], prm["a_neg"], prm["e_b"], mod_row0, tm, seq_len)
    x1, hf = _main(x, pos, xa, dtp, hbs, h0f, mods, mod_row0, tm, seq_len, prm["main"])
    y = _ffn(x1, mods, mod_row0, FFN_TM, prm["w_ff1"], prm["w_ff2"], prm["ln2_g"], prm["ln2_b"])
    return y, hf, hb


def kernel(x_prompt, x_sample, state_ssd_fwd, state_ssd_bwd, c, c_ctx, w_ada, b_ada, w_in, conv_w, conv_b, dt_bias_fwd, dt_bias_bwd, a_log_fwd, a_log_bwd, d_skip, ssd_norm_w, sgu_ln_g, sgu_ln_b, w_spatial, b_spatial, w_proj_a, w_proj_b, w_out, ln1_g, ln1_b, w_ff1, w_ff2, ln2_g, ln2_b):
    batch, seq, _ = x_prompt.shape
    dec_batch, dec_seq, _ = x_sample.shape
    l = 0
    row = lambda v: v.reshape(1, -1)

    cond = jnp.zeros((SUBLANES, D_MODEL), F32).at[0].set(c_ctx).at[1:1 + dec_batch].set(c)
    mods = _ada(cond, w_ada[l], b_ada[l]).reshape(SUBLANES, 1, 6 * D_MODEL)

    o1 = D_SSD
    o2 = o1 + SSD_CONV_DIM
    o3 = o2 + DT_GROUP
    w_t = jnp.swapaxes(w_in[l], 0, 1)
    step = W_COL_TILE // DT_GROUP
    w_xbc = _w_cols(w_t, lambda i: o1 // DT_GROUP + i * step, SSD_CONV_DIM // W_COL_TILE)
    z_tiles = o1 // W_COL_TILE
    w_main = _w_cols(
        w_t, lambda i: jnp.where(i < z_tiles, i * step, o3 // DT_GROUP + (i - z_tiles) * step),
        MAIN_COLS // W_COL_TILE)
    w_dt = _dt_lanes(w_in[l][:, o2:o3], 1).astype(BF16)
    e_f, e_b = _head_expanders()
    dt_bias = row(_dt_lanes(jnp.concatenate([dt_bias_fwd[l], dt_bias_bwd[l]]), 0))
    a_neg = row(_dt_lanes(-jnp.exp(jnp.concatenate([a_log_fwd[l], a_log_bwd[l]])), 0))

    prm = dict(
        w_xbc=w_xbc, w_dt=w_dt, conv_w=conv_w[l], conv_b=row(conv_b[l]),
        dt_bias=dt_bias, a_neg=a_neg, e_b=e_b,
        main=(w_main, dt_bias, a_neg, row(jnp.repeat(d_skip[l], SSD_HEAD_DIM)),
              row(ssd_norm_w[l]), e_f, e_b, row(sgu_ln_g[l]), row(sgu_ln_b[l]),
              w_spatial[l].astype(BF16), jnp.repeat(b_spatial[l].T, SGU_GROUP_DIM, axis=1),
              w_proj_a[l].astype(BF16), w_proj_b[l].astype(BF16), w_out[l].astype(BF16),
              row(ln1_g[l]), row(ln1_b[l])),
        w_ff1=w_ff1[l].astype(BF16), w_ff2=w_ff2[l].astype(BF16),
        ln2_g=row(ln2_g[l]), ln2_b=row(ln2_b[l]),
    )

    tm = 256
    xp, hf, hb = _stream(x_prompt.reshape(1, batch * seq, D_MODEL), None, mods, 0, seq,
                         None, None, prm, tm)
    pos = _grid_pos_tables(dec_seq)
    h0f = state_ssd_fwd[:, l].reshape(dec_batch, D_SSD, SSD_STATE)
    h0b = state_ssd_bwd[:, l].reshape(dec_batch, D_SSD, SSD_STATE)
    xs, _, _ = _stream(x_sample, pos, mods, 1, dec_seq, h0f, h0b, prm, tm)

    state_shape = (batch, DEPTH, SSD_HEADS, SSD_HEAD_DIM, SSD_STATE)
    return (xp.reshape(batch, seq, D_MODEL), xs,
            hf.reshape(state_shape), hb.reshape(state_shape))
```

```python
import functools
import math

import jax
import jax.numpy as jnp
from jax import lax
from jax.experimental import pallas as pl
from jax.experimental.pallas import tpu as pltpu

D_MODEL = 1024
GRID_W = 64
POS_BASE = 10000.0
D_SSD = 1024
SSD_HEAD_DIM = 64
SSD_HEADS = D_SSD // SSD_HEAD_DIM
SSD_GROUPS = 2
SSD_STATE = 128
SSD_CHUNK = 128
SSD_CONV_DIM = D_SSD + 2 * SSD_GROUPS * SSD_STATE
D_SGU = 1024
SGU_GROUPS = 8
SGU_GROUP_DIM = D_SGU // SGU_GROUPS
SGU_CHUNK = 128
D_FF = 4 * D_MODEL
DEPTH = 1
DN_ALPHA = (2.0 * DEPTH) ** 0.25
EPS = 1e-5

LANES = 128
SUBLANES = 8
BF16_ROWS = 16

DT_GROUP = 2 * SSD_HEADS
DT_COPIES = 3
DT_PAD = LANES
HEAD_PAIRS = SSD_HEADS // 2
PAIRS_PER_GROUP = HEAD_PAIRS // SSD_GROUPS
GROUP_COLS = D_SSD // SSD_GROUPS
BC_COLS = SSD_GROUPS * SSD_STATE

BF16 = jnp.bfloat16
F32 = jnp.float32
HIGHEST = lax.Precision.HIGHEST


def _dot(a, b):
    return jnp.dot(a.astype(BF16), b.astype(BF16), preferred_element_type=F32)


def _dot_exact(a, b):
    return jnp.dot(a, b, precision=HIGHEST, preferred_element_type=F32)


def _layer_norm(x, g, b):
    mu = jnp.mean(x, axis=-1, keepdims=True)
    xc = x - mu
    var = jnp.mean(xc * xc, axis=-1, keepdims=True)
    return xc * lax.rsqrt(var + EPS) * g + b


def _softplus(x):
    return jnp.maximum(x, 0.0) + jnp.log1p(jnp.exp(-jnp.abs(x)))


def _gelu(x):
    return 0.5 * x * (1.0 + lax.erf(x * (1.0 / math.sqrt(2.0))))


def _sigmoid(x):
    return 0.5 * jnp.tanh(0.5 * x) + 0.5


def _silu(x):
    half = 0.5 * x
    return half * jnp.tanh(half) + half


def _const_spec(shape):
    zeros = (0,) * len(shape)
    return pl.BlockSpec(shape, lambda *_: zeros, pipeline_mode=pl.Buffered(1))


def _params(vmem_mb, n_axes):
    return pltpu.CompilerParams(
        dimension_semantics=("arbitrary",) * n_axes,
        vmem_limit_bytes=vmem_mb * 1024 * 1024)


def _ada_kernel(c_ref, w_ref, b_ref, o_ref):
    o_ref[...] = _dot_exact(_silu(c_ref[...]), w_ref[...]) + b_ref[...]


def _ada(cond, w_ada, b_ada):
    n = w_ada.shape[1]
    tn = 2048
    return pl.pallas_call(
        _ada_kernel,
        grid=(n // tn,),
        in_specs=[pl.BlockSpec((SUBLANES, D_MODEL), lambda j: (0, 0)),
                  pl.BlockSpec((D_MODEL, tn), lambda j: (0, j)),
                  pl.BlockSpec((1, tn), lambda j: (0, j))],
        out_specs=pl.BlockSpec((SUBLANES, tn), lambda j: (0, j)),
        out_shape=jax.ShapeDtypeStruct((SUBLANES, n), F32),
        compiler_params=_params(32, 1),
        name="ada",
    )(cond, w_ada, b_ada.reshape(1, n))


W_COL_TILE = 512


def _w_cols_kernel(src_ref, o_ref):
    o_ref[...] = src_ref[...].T.astype(BF16)


def _w_dt_kernel(src_ref, o_ref):
    t = src_ref[...].T
    lane = lax.broadcasted_iota(jnp.int32, t.shape, 1)
    dt = jnp.where(lane < DT_GROUP, t, 0.0)
    o_ref[...] = (dt + pltpu.roll(dt, DT_GROUP, 1) + pltpu.roll(dt, 2 * DT_GROUP, 1)).astype(BF16)


def _w_dt(w_t, col0):
    d = w_t.shape[1]
    return pl.pallas_call(
        _w_dt_kernel,
        grid=(1,),
        in_specs=[pl.BlockSpec((LANES, d), lambda i: (col0 // LANES, 0))],
        out_specs=pl.BlockSpec((d, LANES), lambda i: (0, 0)),
        out_shape=jax.ShapeDtypeStruct((d, LANES), BF16),
        name="w_dt",
    )(w_t)


def _w_cols(w_t, col_start, n_tiles):
    d = w_t.shape[1]
    return pl.pallas_call(
        _w_cols_kernel,
        grid=(n_tiles,),
        in_specs=[pl.BlockSpec((pl.Element(W_COL_TILE), pl.Element(d)),
                               lambda i: (col_start(i) * DT_GROUP, 0))],
        out_specs=pl.BlockSpec((d, W_COL_TILE), lambda i: (0, i)),
        out_shape=jax.ShapeDtypeStruct((d, n_tiles * W_COL_TILE), BF16),
        compiler_params=_params(32, 1),
        name="w_cols",
    )(w_t)


IN_NTILE = 512
HALO = BF16_ROWS


def _add_pos(x_ref, embr_ref, embc_ref, grid_row0):
    n_rows = x_ref.shape[1] // GRID_W
    col = embc_ref[...]
    blocks = []
    for i in range(n_rows):
        row = embr_ref[grid_row0 + i]
        row = jnp.concatenate([row] * (GRID_W // SUBLANES), axis=0)
        blocks.append(x_ref[0, i * GRID_W:(i + 1) * GRID_W, :]
                      + jnp.concatenate([row, col], axis=1))
    return jnp.concatenate(blocks, axis=0)


def _pos_halo(embr_ref, embc_ref, grid_row, col0):
    left = jnp.concatenate([embr_ref[grid_row]] * (HALO // SUBLANES), axis=0)
    return jnp.concatenate([left, embc_ref[col0:col0 + HALO, :]], axis=1)


def _block_diag_pair(xp):
    lane = lax.broadcasted_iota(jnp.int32, xp.shape, 1)
    first = lane < SSD_HEAD_DIM
    return jnp.concatenate([jnp.where(first, xp, 0.0), jnp.where(first, 0.0, xp)], axis=0)


def _split_terms(v):
    lane = lax.broadcasted_iota(jnp.int32, v.shape, 1)
    hi = v.astype(BF16).astype(F32)
    r1 = v - hi
    mid = r1.astype(BF16).astype(F32)
    lo = r1 - mid
    return jnp.where(lane < DT_GROUP, hi, jnp.where(lane < 2 * DT_GROUP, mid, lo)).astype(BF16)


def _expand(v, e_ref):
    return jnp.dot(_split_terms(v), e_ref[...], preferred_element_type=F32)


def _chunk_decay(dtp, dtbias_ref, aneg_ref):
    q = dtp.shape[0]
    dt_all = _softplus(dtp + dtbias_ref[...])
    a_all = dt_all * aneg_ref[...]
    ii = lax.broadcasted_iota(jnp.int32, (q, q), 0)
    jj = lax.broadcasted_iota(jnp.int32, (q, q), 1)
    acum = _dot_exact((ii >= jj).astype(F32), a_all)
    atot = acum[q - 1:q, :]
    rcum = atot - acum + a_all
    return dt_all, a_all, acum, atot, rcum, rcum[0:1, :]


def _carry_update(h_old, w_tok, tot_row, e_ref, xh, b_all):
    wx = (_expand(w_tok, e_ref) * xh).astype(BF16)
    s = jnp.concatenate(
        [jnp.dot(b_all[:, g * SSD_STATE:(g + 1) * SSD_STATE].T.astype(BF16),
                 wx[:, g * GROUP_COLS:(g + 1) * GROUP_COLS], preferred_element_type=F32)
         for g in range(SSD_GROUPS)], axis=1)
    decay = jnp.exp(_expand(jnp.broadcast_to(tot_row, (BF16_ROWS, LANES)), e_ref))[0:1, :]
    return decay * h_old + s


def _ssd_chunk(xh, b_all, c_all, dtp, hf_in, hb_in, dtbias_ref, aneg_ref, dskip_ref, ef_ref,
               eb_ref, between_pairs=lambda: None):
    q = SSD_CHUNK
    dt_all, a_all, acum, atot, rcum, _ = _chunk_decay(dtp, dtbias_ref, aneg_ref)
    ii = lax.broadcasted_iota(jnp.int32, (q, q), 0)
    jj = lax.broadcasted_iota(jnp.int32, (q, q), 1)
    lower = ii >= jj
    diag = ii == jj
    dt_t = dt_all.T
    a_t = a_all.T
    acum_t = _dot_exact(a_t, (ii <= jj).astype(F32))
    rcum_t = acum_t[:, q - 1:q] - acum_t + a_t

    ef = jnp.exp(_expand(acum, ef_ref))
    eb = jnp.exp(_expand(rcum, eb_ref))
    y_off = []
    cb = []
    for g in range(SSD_GROUPS):
        gs = slice(g * GROUP_COLS, (g + 1) * GROUP_COLS)
        c_g = c_all[:, g * SSD_STATE:(g + 1) * SSD_STATE].astype(BF16)
        b_g = b_all[:, g * SSD_STATE:(g + 1) * SSD_STATE].astype(BF16)
        cb.append(lax.dot_general(c_g, b_g, (((1,), (1,)), ((), ())),
                                  preferred_element_type=F32))
        y_off.append(ef[:, gs] * jnp.dot(c_g, hf_in[:, gs].astype(BF16),
                                         preferred_element_type=F32)
                     + eb[:, gs] * jnp.dot(c_g, hb_in[:, gs].astype(BF16),
                                           preferred_element_type=F32))
    y_off = jnp.concatenate(y_off, axis=1)

    y_cols = []
    for pr in range(HEAD_PAIRS):
        g = pr // PAIRS_PER_GROUP
        m_parts = []
        for h in (2 * pr, 2 * pr + 1):
            hb_lane = SSD_HEADS + h
            colf = jnp.broadcast_to(acum[:, h:h + 1], (q, q))
            colb = jnp.broadcast_to(rcum[:, hb_lane:hb_lane + 1], (q, q))
            seg = jnp.where(lower, colf - acum_t[h:h + 1, :],
                            colb - rcum_t[hb_lane:hb_lane + 1, :])
            dt_b_row = dt_t[hb_lane:hb_lane + 1, :]
            w = (jnp.exp(seg) * jnp.where(lower, dt_t[h:h + 1, :], dt_b_row)
                 + jnp.where(diag, dt_b_row, 0.0))
            m_parts.append(cb[g] * w)
        sl = slice(pr * LANES, (pr + 1) * LANES)
        y_cols.append(_dot(jnp.concatenate(m_parts, axis=1), _block_diag_pair(xh[:, sl])))
        between_pairs()
    y = jnp.concatenate(y_cols, axis=1) + y_off + dskip_ref[...] * xh
    h_new = _carry_update(hf_in, dt_all * jnp.exp(atot - acum), atot, ef_ref, xh, b_all)
    return y, h_new


def _tiling(x, tm, seq_len):
    s_n, l_n, _ = x.shape
    return s_n, l_n, l_n // tm, seq_len // tm, l_n // seq_len


FRONT_SEGS = (SSD_CONV_DIM, DT_PAD)
FRONT_NTILE = 256


def _front_kernel(*refs, has_pos, has_h0, tiles_per_seq, n_tiles):
    x_ref, xp_ref, xn_ref = refs[:3]
    refs = refs[3:]
    if has_pos:
        embr_ref, embc_ref = refs[:2]
        refs = refs[2:]
    if has_h0:
        h0b_ref = refs[0]
        refs = refs[1:]
    (mod_ref, w_ref, wdt_ref, convw_ref, convb_ref, dtbias_ref, aneg_ref, eb_ref, wuv_ref,
     lng_ref, lnb_ref, wsp_ref, bsp_ref) = refs[:13]
    xa_ref, dt_ref, hbs_ref, hfb_ref, yb_ref, hb_s, xb_s, dts_s = refs[13:]
    shift1 = mod_ref[0, :, 0:D_MODEL]
    scale1 = mod_ref[0, :, D_MODEL:2 * D_MODEL]
    tm = x_ref.shape[1]
    n_ext = tm + 2 * HALO
    q = SSD_CHUNK
    j = pl.program_id(1)
    tile = n_tiles - 1 - jnp.minimum(j, n_tiles - 1)
    t = tile % tiles_per_seq
    t_scan = jnp.minimum(n_tiles - j, n_tiles - 1) % tiles_per_seq
    slot = j % 2
    scan_slot = 1 - slot

    @pl.when(j == 0)
    def _():
        xb_s[1] = jnp.zeros(xb_s.shape[1:], F32)
        dts_s[1] = jnp.zeros(dts_s.shape[1:], F32)

    def init_state():
        if has_h0:
            hb_s[...] = h0b_ref[0].T
        else:
            hb_s[...] = jnp.zeros_like(hb_s)

    if tiles_per_seq > 1:
        pl.when(t_scan == tiles_per_seq - 1)(init_state)
    else:
        init_state()

    def modulate(x):
        return (x * (1.0 + scale1) + shift1).astype(BF16)

    if has_pos:
        rows_per_tile = tm // GRID_W
        n_grid_rows = embr_ref.shape[0]
        r0 = tile * rows_per_tile
        x = _add_pos(x_ref, embr_ref, embc_ref, r0)
        x_prev = xp_ref[0] + _pos_halo(embr_ref, embc_ref, jnp.maximum(r0 - 1, 0),
                                       GRID_W - HALO)
        x_next = xn_ref[0] + _pos_halo(embr_ref, embc_ref,
                                       jnp.minimum(r0 + rows_per_tile, n_grid_rows - 1), 0)
    else:
        x, x_prev, x_next = x_ref[0], xp_ref[0], xn_ref[0]
    h = modulate(x)
    h_ext = jnp.concatenate([modulate(x_prev), h, modulate(x_next)], axis=0)

    row = lax.broadcasted_iota(jnp.int32, (tm, 1), 0)
    zero_prev = row == 0
    zero_next = row == tm - 1
    if tiles_per_seq > 1:
        zero_prev = jnp.logical_and(zero_prev, t == 0)
        zero_next = jnp.logical_and(zero_next, t == tiles_per_seq - 1)

    def project():
        dt = jnp.dot(h, wdt_ref[...], preferred_element_type=F32)
        dt_ref[0] = dt
        dts_s[slot] = dt
        yield
        for c0 in range(0, SSD_CONV_DIM, FRONT_NTILE):
            c1 = c0 + FRONT_NTILE
            p = jnp.dot(h_ext, w_ref[:, c0:c1], preferred_element_type=F32)
            cur = p[HALO:HALO + tm]
            prv = jnp.where(zero_prev, 0.0, pltpu.roll(p, 1, 0)[HALO:HALO + tm])
            nxt = jnp.where(zero_next, 0.0, pltpu.roll(p, n_ext - 1, 0)[HALO:HALO + tm])
            xc = (convw_ref[0:1, c0:c1] * prv + convw_ref[1:2, c0:c1] * cur
                  + convw_ref[2:3, c0:c1] * nxt + convb_ref[:, c0:c1])
            xa = _silu(xc)
            xa_ref[0, :, c0:c1] = xa
            if c0 < D_SSD + BC_COLS:
                xb_s[slot, :, c0:c1] = xa
            yield

    final = []

    def scan():
        h_state = hb_s[...]
        for ch in reversed(range(tm // q)):
            rows = slice(ch * q, (ch + 1) * q)
            hbs_ref[0, ch] = h_state
            dt_all, _, _, _, rcum, rtot = _chunk_decay(dts_s[scan_slot, rows, :], dtbias_ref,
                                                       aneg_ref)
            yield
            w_exp = _expand(dt_all * jnp.exp(rtot - rcum), eb_ref)
            decay = jnp.exp(_expand(jnp.broadcast_to(rtot, (BF16_ROWS, LANES)), eb_ref))[0:1, :]
            yield
            wx = (w_exp * xb_s[scan_slot, rows, 0:D_SSD]).astype(BF16)
            b_all = xb_s[scan_slot, rows, D_SSD:D_SSD + BC_COLS]
            parts = []
            for g in range(SSD_GROUPS):
                parts.append(jnp.dot(b_all[:, g * SSD_STATE:(g + 1) * SSD_STATE].T.astype(BF16),
                                     wx[:, g * GROUP_COLS:(g + 1) * GROUP_COLS],
                                     preferred_element_type=F32))
                yield
            h_state = decay * h_state + jnp.concatenate(parts, axis=1)
        hb_s[...] = h_state
        final.append(h_state)

    def gate_mlp():
        uv = []
        for c in range(0, 2 * D_SGU, IN_NTILE):
            uv.append(_gelu(jnp.dot(h, wuv_ref[:, c:c + IN_NTILE], preferred_element_type=F32)))
            yield
        n_u = D_SGU // IN_NTILE
        u = jnp.concatenate(uv[:n_u], axis=1)
        v = _layer_norm(jnp.concatenate(uv[n_u:], axis=1), lng_ref[...], lnb_ref[...])
        yield
        for ch in range(tm // SGU_CHUNK):
            rows = slice(ch * SGU_CHUNK, (ch + 1) * SGU_CHUNK)
            cols = [_dot(wsp_ref[g], v[rows, g * SGU_GROUP_DIM:(g + 1) * SGU_GROUP_DIM])
                    for g in range(SGU_GROUPS)]
            s = jnp.concatenate(cols, axis=1) + bsp_ref[...]
            yb_ref[0, rows, :] = (u[rows] * s).astype(BF16)
            yield

    stages = [project(), scan(), gate_mlp()]
    while stages:
        for gen in list(stages):
            if next(gen, StopIteration) is StopIteration:
                stages.remove(gen)

    def final_state():
        hfb_ref[0] = final[0].T

    if tiles_per_seq > 1:
        pl.when(t_scan == 0)(final_state)
    else:
        final_state()


def _front(x, pos, mods, h0b, w_xbc, w_dt, convw, convb, dtbias, aneg, e_b, gate_consts,
           mod_row0, tm, seq_len):
    s_n, l_n, n_tiles, tiles_per_seq, seqs_per_row = _tiling(x, tm, seq_len)
    has_pos = pos is not None
    has_h0 = h0b is not None
    hb = tm // HALO
    n_hb = l_n // HALO
    cpt = tm // SSD_CHUNK
    fwd = lambda j: n_tiles - 1 - jnp.minimum(j, n_tiles - 1)
    scanned = lambda j: jnp.minimum(n_tiles - j, n_tiles - 1)
    tok = lambda s, j: (s, fwd(j), 0)
    seq_of = lambda s, j: (s * seqs_per_row + scanned(j) // tiles_per_seq, 0, 0)
    in_specs = [pl.BlockSpec((1, tm, D_MODEL), tok),
                pl.BlockSpec((1, HALO, D_MODEL),
                             lambda s, j: (s, jnp.maximum(fwd(j) * hb - 1, 0), 0)),
                pl.BlockSpec((1, HALO, D_MODEL),
                             lambda s, j: (s, jnp.minimum((fwd(j) + 1) * hb, n_hb - 1), 0))]
    args = [x, x, x]
    if has_pos:
        in_specs += [_const_spec(pos[0].shape), _const_spec(pos[1].shape)]
        args += list(pos)
    state_spec = pl.BlockSpec((1, D_SSD, SSD_STATE), seq_of)
    if has_h0:
        in_specs.append(state_spec)
        args.append(h0b)
    in_specs.append(pl.BlockSpec((1, 1, 6 * D_MODEL), lambda s, j: (mod_row0 + s, 0, 0)))
    args.append(mods)
    for a in (w_xbc, w_dt, convw, convb, dtbias, aneg, e_b) + tuple(gate_consts):
        in_specs.append(_const_spec(a.shape))
        args.append(a)
    return pl.pallas_call(
        functools.partial(_front_kernel, has_pos=has_pos, has_h0=has_h0,
                          tiles_per_seq=tiles_per_seq, n_tiles=n_tiles),
        grid=(s_n, n_tiles + 1),
        in_specs=in_specs,
        out_specs=[pl.BlockSpec((1, tm, w), tok) for w in FRONT_SEGS]
        + [pl.BlockSpec((1, cpt, SSD_STATE, D_SSD), lambda s, j: (s, scanned(j), 0, 0)),
           state_spec, pl.BlockSpec((1, tm, D_SGU), tok)],
        out_shape=[jax.ShapeDtypeStruct((s_n, l_n, w), F32) for w in FRONT_SEGS]
        + [jax.ShapeDtypeStruct((s_n, l_n // SSD_CHUNK, SSD_STATE, D_SSD), F32),
           jax.ShapeDtypeStruct((s_n * seqs_per_row, D_SSD, SSD_STATE), F32),
           jax.ShapeDtypeStruct((s_n, l_n, D_SGU), BF16)],
        scratch_shapes=[pltpu.VMEM((SSD_STATE, D_SSD), F32),
                        pltpu.VMEM((2, tm, D_SSD + BC_COLS), F32),
                        pltpu.VMEM((2, tm, DT_PAD), F32)],
        compiler_params=_params(48, 2),
        name="front",
    )(*args)


MAIN_Z, MAIN_GA, MAIN_GB = (i * D_MODEL for i in range(3))
MAIN_COLS = 3 * D_MODEL


def _main_kernel(*refs, has_pos, has_h0, tiles_per_seq):
    x_ref, xa_ref, dtp_ref, hbs_ref, yb_ref = refs[:5]
    refs = refs[5:]
    if has_pos:
        embr_ref, embc_ref = refs[:2]
        refs = refs[2:]
    if has_h0:
        h0f_ref = refs[0]
        refs = refs[1:]
    (mod_ref, w_ref, dtbias_ref, aneg_ref, dskip_ref, normw_ref, ef_ref, eb_ref, wpa_ref,
     wpb_ref, wout_ref, ln1g_ref, ln1b_ref) = refs[:13]
    o_ref, hff_ref, hf_s = refs[13:]
    tm = x_ref.shape[1]
    q = SSD_CHUNK
    tile = pl.program_id(1)
    t = tile % tiles_per_seq

    if has_pos:
        x = _add_pos(x_ref, embr_ref, embc_ref, tile * (tm // GRID_W))
    else:
        x = x_ref[0]
    shift1 = mod_ref[0, :, 0:D_MODEL]
    scale1 = mod_ref[0, :, D_MODEL:2 * D_MODEL]
    gate1 = mod_ref[0, :, 2 * D_MODEL:3 * D_MODEL]
    h = (x * (1.0 + scale1) + shift1).astype(BF16)

    def in_proj_tile(c):
        return jnp.dot(h, w_ref[:, c:c + IN_NTILE], preferred_element_type=F32)

    def proj_b_tile(c):
        return jnp.dot(yb_ref[0], wpb_ref[:, c:c + IN_NTILE], preferred_element_type=F32)

    col_tiles = {}
    pending = ([("in", c, in_proj_tile) for c in range(MAIN_GA, MAIN_COLS, IN_NTILE)]
               + [("pb", c, proj_b_tile) for c in range(0, D_MODEL, IN_NTILE)])
    n_slots = (tm // q) * HEAD_PAIRS
    n_fill = len(pending)
    slot = [0]

    def between_pairs():
        slot[0] += 1
        done = n_fill - len(pending)
        for _ in range(-(-slot[0] * n_fill // n_slots) - done):
            kind, c, fn = pending.pop(0)
            col_tiles[kind, c] = fn(c)

    def gathered(kind, c0):
        return jnp.concatenate([col_tiles.pop((kind, c)) for c in range(c0, c0 + D_MODEL, IN_NTILE)],
                               axis=1)

    def init_state():
        if has_h0:
            hf_s[...] = h0f_ref[0].T
        else:
            hf_s[...] = jnp.zeros_like(hf_s)

    if tiles_per_seq > 1:
        pl.when(t == 0)(init_state)
    else:
        init_state()
    h_state = hf_s[...]
    zz = _silu(jnp.concatenate([in_proj_tile(c) for c in range(MAIN_Z, MAIN_GA, IN_NTILE)],
                               axis=1))
    ya = []
    for ch in range(tm // q):
        rows = slice(ch * q, (ch + 1) * q)
        y, h_state = _ssd_chunk(
            xa_ref[0, rows, 0:D_SSD], xa_ref[0, rows, D_SSD:D_SSD + BC_COLS],
            xa_ref[0, rows, D_SSD + BC_COLS:], dtp_ref[0, rows, :], h_state, hbs_ref[0, ch],
            dtbias_ref, aneg_ref, dskip_ref, ef_ref, eb_ref, between_pairs)
        yg = y * zz[rows]
        ms = jnp.mean(yg * yg, axis=-1, keepdims=True)
        ya.append(yg * lax.rsqrt(ms + EPS) * normw_ref[...])
    hf_s[...] = h_state

    def final_state():
        hff_ref[0] = h_state.T

    if tiles_per_seq > 1:
        pl.when(t == tiles_per_seq - 1)(final_state)
    else:
        final_state()
    ya = jnp.concatenate(ya, axis=0)

    merged = (_sigmoid(gathered("in", MAIN_GA)) * _dot(ya, wpa_ref[...])
              + _sigmoid(gathered("in", MAIN_GB)) * gathered("pb", 0))
    o = _dot(merged, wout_ref[...])
    o_ref[0] = _layer_norm(DN_ALPHA * x + gate1 * o, ln1g_ref[...], ln1b_ref[...])


def _main(x, pos, xa, dtp, hbs, yb, h0f, mods, mod_row0, tm, seq_len, consts):
    s_n, l_n, n_tiles, tiles_per_seq, seqs_per_row = _tiling(x, tm, seq_len)
    has_pos = pos is not None
    has_h0 = h0f is not None
    cpt = tm // SSD_CHUNK
    tok = lambda s, j: (s, j, 0)
    in_specs = [pl.BlockSpec((1, tm, D_MODEL), tok),
                pl.BlockSpec((1, tm, SSD_CONV_DIM), tok),
                pl.BlockSpec((1, tm, DT_PAD), tok),
                pl.BlockSpec((1, cpt, SSD_STATE, D_SSD), lambda s, j: (s, j, 0, 0)),
                pl.BlockSpec((1, tm, D_SGU), tok)]
    args = [x, xa, dtp, hbs, yb]
    if has_pos:
        in_specs += [_const_spec(pos[0].shape), _const_spec(pos[1].shape)]
        args += list(pos)
    state_spec = pl.BlockSpec((1, D_SSD, SSD_STATE),
                              lambda s, j: (s * seqs_per_row + j // tiles_per_seq, 0, 0))
    if has_h0:
        in_specs.append(state_spec)
        args.append(h0f)
    in_specs.append(pl.BlockSpec((1, 1, 6 * D_MODEL), lambda s, j: (mod_row0 + s, 0, 0)))
    args.append(mods)
    for a in consts:
        in_specs.append(_const_spec(a.shape))
        args.append(a)
    return pl.pallas_call(
        functools.partial(_main_kernel, has_pos=has_pos, has_h0=has_h0,
                          tiles_per_seq=tiles_per_seq),
        grid=(s_n, n_tiles),
        in_specs=in_specs,
        out_specs=[pl.BlockSpec((1, tm, D_MODEL), tok), state_spec],
        out_shape=[jax.ShapeDtypeStruct((s_n, l_n, D_MODEL), F32),
                   jax.ShapeDtypeStruct((s_n * seqs_per_row, D_SSD, SSD_STATE), F32)],
        scratch_shapes=[pltpu.VMEM((SSD_STATE, D_SSD), F32)],
        compiler_params=_params(56, 2),
        name="main",
    )(*args)


FF_TILE = 1024
FFN_TM = 1024


def _ffn_kernel(x_ref, mod_ref, w1_ref, w2_ref, g_ref, b_ref, o_ref):
    x = x_ref[0]
    shift2 = mod_ref[0, :, 3 * D_MODEL:4 * D_MODEL]
    scale2 = mod_ref[0, :, 4 * D_MODEL:5 * D_MODEL]
    gate2 = mod_ref[0, :, 5 * D_MODEL:6 * D_MODEL]
    h = (x * (1.0 + scale2) + shift2).astype(BF16)
    f = jnp.zeros_like(x)
    for k0 in range(0, D_FF, FF_TILE):
        a = jnp.dot(h, w1_ref[:, k0:k0 + FF_TILE], preferred_element_type=F32)
        a = jnp.maximum(a, 0.0)
        f = f + _dot(a * a, w2_ref[k0:k0 + FF_TILE, :])
    o_ref[0] = _layer_norm(DN_ALPHA * x + gate2 * f, g_ref[...], b_ref[...])


def _ffn(x, mods, mod_row0, tm, w1, w2, g, b):
    s_n, l_n, _ = x.shape
    tok = lambda s, j: (s, j, 0)
    return pl.pallas_call(
        _ffn_kernel,
        grid=(s_n, l_n // tm),
        in_specs=[pl.BlockSpec((1, tm, D_MODEL), tok),
                  pl.BlockSpec((1, 1, 6 * D_MODEL), lambda s, j: (mod_row0 + s, 0, 0)),
                  _const_spec(w1.shape), _const_spec(w2.shape),
                  _const_spec(g.shape), _const_spec(b.shape)],
        out_specs=pl.BlockSpec((1, tm, D_MODEL), tok),
        out_shape=jax.ShapeDtypeStruct((s_n, l_n, D_MODEL), F32),
        compiler_params=_params(48, 2),
        name="ffn",
    )(x, mods, w1, w2, g, b)


def _grid_pos_tables(n_tokens):
    rows = n_tokens // GRID_W
    quarter = D_MODEL // 4
    omega = 1.0 / (POS_BASE ** (jnp.arange(quarter, dtype=F32) / quarter))
    r = jnp.arange(rows, dtype=F32)[:, None] * omega
    col = jnp.arange(GRID_W, dtype=F32)[:, None] * omega
    emb_r = jnp.concatenate([jnp.sin(r), jnp.cos(r)], axis=-1)
    emb_c = jnp.concatenate([jnp.sin(col), jnp.cos(col)], axis=-1)
    return jnp.broadcast_to(emb_r[:, None, :], (rows, SUBLANES, D_MODEL // 2)), emb_c


def _dt_lanes(v, axis):
    pad = [(0, 0)] * v.ndim
    pad[axis] = (0, DT_PAD - DT_COPIES * DT_GROUP)
    return jnp.pad(jnp.concatenate([v] * DT_COPIES, axis=axis), pad)


def _head_expanders():
    lane = jnp.arange(LANES)[:, None]
    head = jnp.arange(D_SSD)[None, :] // SSD_HEAD_DIM
    valid = lane < DT_COPIES * DT_GROUP
    e_f = jnp.logical_and(valid, lane % DT_GROUP == head)
    e_b = jnp.logical_and(valid, lane % DT_GROUP == head + SSD_HEADS)
    return e_f.astype(BF16), e_b.astype(BF16)


def _stream(x, pos, mods, mod_row0, seq_len, h0f, h0b, prm, tm):
    xa, dtp, hbs, hb, yb = _front(x, pos, mods, h0b, prm["w_xbc"], prm["w_dt"], prm["conv_w"],
                                  prm["conv_b"], prm["dt_bias"], prm["a_neg"], prm["e_b"],
                                  prm["gate"], mod_row0, tm, seq_len)
    x1, hf = _main(x, pos, xa, dtp, hbs, yb, h0f, mods, mod_row0, tm, seq_len, prm["main"])
    y = _ffn(x1, mods, mod_row0, FFN_TM, prm["w_ff1"], prm["w_ff2"], prm["ln2_g"], prm["ln2_b"])
    return y, hf, hb


def kernel(x_prompt, x_sample, state_ssd_fwd, state_ssd_bwd, c, c_ctx, w_ada, b_ada, w_in, conv_w, conv_b, dt_bias_fwd, dt_bias_bwd, a_log_fwd, a_log_bwd, d_skip, ssd_norm_w, sgu_ln_g, sgu_ln_b, w_spatial, b_spatial, w_proj_a, w_proj_b, w_out, ln1_g, ln1_b, w_ff1, w_ff2, ln2_g, ln2_b):
    batch, seq, _ = x_prompt.shape
    dec_batch, dec_seq, _ = x_sample.shape
    l = 0
    row = lambda v: v.reshape(1, -1)

    cond = jnp.zeros((SUBLANES, D_MODEL), F32).at[0].set(c_ctx).at[1:1 + dec_batch].set(c)
    mods = _ada(cond, w_ada[l], b_ada[l]).reshape(SUBLANES, 1, 6 * D_MODEL)

    o1 = D_SSD
    o2 = o1 + SSD_CONV_DIM
    o3 = o2 + DT_GROUP
    o4 = o3 + 2 * D_SGU
    w_t = jnp.swapaxes(w_in[l], 0, 1)
    step = W_COL_TILE // DT_GROUP
    w_xbc = _w_cols(w_t, lambda i: o1 // DT_GROUP + i * step, SSD_CONV_DIM // W_COL_TILE)
    w_uv = _w_cols(w_t, lambda i: o3 // DT_GROUP + i * step, 2 * D_SGU // W_COL_TILE)
    z_tiles = o1 // W_COL_TILE
    w_main = _w_cols(
        w_t, lambda i: jnp.where(i < z_tiles, i * step, o4 // DT_GROUP + (i - z_tiles) * step),
        MAIN_COLS // W_COL_TILE)
    assert o2 % LANES == 0
    w_dt = _w_dt(w_t, o2)
    e_f, e_b = _head_expanders()
    dt_bias = row(_dt_lanes(jnp.concatenate([dt_bias_fwd[l], dt_bias_bwd[l]]), 0))
    a_neg = row(_dt_lanes(-jnp.exp(jnp.concatenate([a_log_fwd[l], a_log_bwd[l]])), 0))

    prm = dict(
        w_xbc=w_xbc, w_dt=w_dt, conv_w=conv_w[l], conv_b=row(conv_b[l]),
        dt_bias=dt_bias, a_neg=a_neg, e_b=e_b,
        gate=(w_uv, row(sgu_ln_g[l]), row(sgu_ln_b[l]), w_spatial[l].astype(BF16),
              jnp.repeat(b_spatial[l].T, SGU_GROUP_DIM, axis=1)),
        main=(w_main, dt_bias, a_neg, row(jnp.repeat(d_skip[l], SSD_HEAD_DIM)),
              row(ssd_norm_w[l]), e_f, e_b,
              w_proj_a[l].astype(BF16), w_proj_b[l].astype(BF16), w_out[l].astype(BF16),
              row(ln1_g[l]), row(ln1_b[l])),
        w_ff1=w_ff1[l].astype(BF16), w_ff2=w_ff2[l].astype(BF16),
        ln2_g=row(ln2_g[l]), ln2_b=row(ln2_b[l]),
    )

    tm = 256
    xp, hf, hb = _stream(x_prompt.reshape(1, batch * seq, D_MODEL), None, mods, 0, seq,
                         None, None, prm, tm)
    pos = _grid_pos_tables(dec_seq)
    h0f = state_ssd_fwd[:, l].reshape(dec_batch, D_SSD, SSD_STATE)
    h0b = state_ssd_bwd[:, l].reshape(dec_batch, D_SSD, SSD_STATE)
    xs, _, _ = _stream(x_sample, pos, mods, 1, dec_seq, h0f, h0b, prm, tm)

    state_shape = (batch, DEPTH, SSD_HEADS, SSD_HEAD_DIM, SSD_STATE)
    return (xp.reshape(batch, seq, D_MODEL), xs,
            hf.reshape(state_shape), hb.reshape(state_shape))
```

```python
import functools
import math

import jax
import jax.numpy as jnp
from jax import lax
from jax.experimental import pallas as pl
from jax.experimental.pallas import tpu as pltpu

D_MODEL = 1024
GRID_W = 64
POS_BASE = 10000.0
D_SSD = 1024
SSD_HEAD_DIM = 64
SSD_HEADS = D_SSD // SSD_HEAD_DIM
SSD_GROUPS = 2
SSD_STATE = 128
SSD_CHUNK = 128
SSD_CONV_DIM = D_SSD + 2 * SSD_GROUPS * SSD_STATE
D_SGU = 1024
SGU_GROUPS = 8
SGU_GROUP_DIM = D_SGU // SGU_GROUPS
SGU_CHUNK = 128
D_FF = 4 * D_MODEL
DEPTH = 1
DN_ALPHA = (2.0 * DEPTH) ** 0.25
EPS = 1e-5

LANES = 128
SUBLANES = 8
BF16_ROWS = 16

DT_GROUP = 2 * SSD_HEADS
DT_COPIES = 3
DT_PAD = LANES
HEAD_PAIRS = SSD_HEADS // 2
PAIRS_PER_GROUP = HEAD_PAIRS // SSD_GROUPS
GROUP_COLS = D_SSD // SSD_GROUPS
BC_COLS = SSD_GROUPS * SSD_STATE

BF16 = jnp.bfloat16
F32 = jnp.float32
HIGHEST = lax.Precision.HIGHEST


def _dot(a, b):
    return jnp.dot(a.astype(BF16), b.astype(BF16), preferred_element_type=F32)


def _dot_exact(a, b):
    return jnp.dot(a, b, precision=HIGHEST, preferred_element_type=F32)


def _layer_norm(x, g, b):
    mu = jnp.mean(x, axis=-1, keepdims=True)
    xc = x - mu
    var = jnp.mean(xc * xc, axis=-1, keepdims=True)
    return xc * lax.rsqrt(var + EPS) * g + b


def _softplus(x):
    return jnp.maximum(x, 0.0) + jnp.log1p(jnp.exp(-jnp.abs(x)))


def _gelu(x):
    return 0.5 * x * (1.0 + lax.erf(x * (1.0 / math.sqrt(2.0))))


def _sigmoid(x):
    return 0.5 * jnp.tanh(0.5 * x) + 0.5


def _silu(x):
    half = 0.5 * x
    return half * jnp.tanh(half) + half


def _const_spec(shape):
    zeros = (0,) * len(shape)
    return pl.BlockSpec(shape, lambda *_: zeros, pipeline_mode=pl.Buffered(1))


def _params(vmem_mb, n_axes):
    return pltpu.CompilerParams(
        dimension_semantics=("arbitrary",) * n_axes,
        vmem_limit_bytes=vmem_mb * 1024 * 1024)


def _ada_kernel(c_ref, w_ref, b_ref, o_ref):
    o_ref[...] = _dot_exact(_silu(c_ref[...]), w_ref[...]) + b_ref[...]


def _ada(cond, w_ada, b_ada):
    n = w_ada.shape[1]
    tn = 2048
    return pl.pallas_call(
        _ada_kernel,
        grid=(n // tn,),
        in_specs=[pl.BlockSpec((SUBLANES, D_MODEL), lambda j: (0, 0)),
                  pl.BlockSpec((D_MODEL, tn), lambda j: (0, j)),
                  pl.BlockSpec((1, tn), lambda j: (0, j))],
        out_specs=pl.BlockSpec((SUBLANES, tn), lambda j: (0, j)),
        out_shape=jax.ShapeDtypeStruct((SUBLANES, n), F32),
        compiler_params=_params(32, 1),
        name="ada",
    )(cond, w_ada, b_ada.reshape(1, n))


W_COL_TILE = 512


def _w_cols_kernel(src_ref, o_ref):
    o_ref[...] = src_ref[...].T.astype(BF16)


def _w_dt_kernel(src_ref, o_ref):
    t = src_ref[...].T
    lane = lax.broadcasted_iota(jnp.int32, t.shape, 1)
    dt = jnp.where(lane < DT_GROUP, t, 0.0)
    o_ref[...] = (dt + pltpu.roll(dt, DT_GROUP, 1) + pltpu.roll(dt, 2 * DT_GROUP, 1)).astype(BF16)


def _w_dt(w_t, col0):
    d = w_t.shape[1]
    return pl.pallas_call(
        _w_dt_kernel,
        grid=(1,),
        in_specs=[pl.BlockSpec((LANES, d), lambda i: (col0 // LANES, 0))],
        out_specs=pl.BlockSpec((d, LANES), lambda i: (0, 0)),
        out_shape=jax.ShapeDtypeStruct((d, LANES), BF16),
        name="w_dt",
    )(w_t)


def _w_cols(w_t, col_start, n_tiles):
    d = w_t.shape[1]
    return pl.pallas_call(
        _w_cols_kernel,
        grid=(n_tiles,),
        in_specs=[pl.BlockSpec((pl.Element(W_COL_TILE), pl.Element(d)),
                               lambda i: (col_start(i) * DT_GROUP, 0))],
        out_specs=pl.BlockSpec((d, W_COL_TILE), lambda i: (0, i)),
        out_shape=jax.ShapeDtypeStruct((d, n_tiles * W_COL_TILE), BF16),
        compiler_params=_params(32, 1),
        name="w_cols",
    )(w_t)


IN_NTILE = 512
HALO = BF16_ROWS


def _add_pos(x_ref, embr_ref, embc_ref, grid_row0):
    n_rows = x_ref.shape[1] // GRID_W
    col = embc_ref[...]
    blocks = []
    for i in range(n_rows):
        row = embr_ref[grid_row0 + i]
        row = jnp.concatenate([row] * (GRID_W // SUBLANES), axis=0)
        blocks.append(x_ref[0, i * GRID_W:(i + 1) * GRID_W, :]
                      + jnp.concatenate([row, col], axis=1))
    return jnp.concatenate(blocks, axis=0)


def _pos_halo(embr_ref, embc_ref, grid_row, col0):
    left = jnp.concatenate([embr_ref[grid_row]] * (HALO // SUBLANES), axis=0)
    return jnp.concatenate([left, embc_ref[col0:col0 + HALO, :]], axis=1)


def _block_diag_pair(xp):
    lane = lax.broadcasted_iota(jnp.int32, xp.shape, 1)
    first = lane < SSD_HEAD_DIM
    return jnp.concatenate([jnp.where(first, xp, 0.0), jnp.where(first, 0.0, xp)], axis=0)


def _split_terms(v):
    lane = lax.broadcasted_iota(jnp.int32, v.shape, 1)
    hi = v.astype(BF16).astype(F32)
    r1 = v - hi
    mid = r1.astype(BF16).astype(F32)
    lo = r1 - mid
    return jnp.where(lane < DT_GROUP, hi, jnp.where(lane < 2 * DT_GROUP, mid, lo)).astype(BF16)


def _expand(v, e_ref):
    return jnp.dot(_split_terms(v), e_ref[...], preferred_element_type=F32)


def _chunk_decay(dtp, dtbias_ref, aneg_ref):
    q = dtp.shape[0]
    dt_all = _softplus(dtp + dtbias_ref[...])
    a_all = dt_all * aneg_ref[...]
    ii = lax.broadcasted_iota(jnp.int32, (q, q), 0)
    jj = lax.broadcasted_iota(jnp.int32, (q, q), 1)
    acum = _dot_exact((ii >= jj).astype(F32), a_all)
    atot = acum[q - 1:q, :]
    rcum = atot - acum + a_all
    return dt_all, a_all, acum, atot, rcum, rcum[0:1, :]


def _carry_update(h_old, w_tok, tot_row, e_ref, xh, b_all):
    wx = (_expand(w_tok, e_ref) * xh).astype(BF16)
    s = jnp.concatenate(
        [jnp.dot(b_all[:, g * SSD_STATE:(g + 1) * SSD_STATE].T.astype(BF16),
                 wx[:, g * GROUP_COLS:(g + 1) * GROUP_COLS], preferred_element_type=F32)
         for g in range(SSD_GROUPS)], axis=1)
    decay = jnp.exp(_expand(jnp.broadcast_to(tot_row, (BF16_ROWS, LANES)), e_ref))[0:1, :]
    return decay * h_old + s


def _ssd_chunk(xh, b_all, c_all, dtp, hf_in, hb_in, dtbias_ref, aneg_ref, dskip_ref, ef_ref,
               eb_ref, between_pairs=lambda: None):
    q = SSD_CHUNK
    dt_all, a_all, acum, atot, rcum, _ = _chunk_decay(dtp, dtbias_ref, aneg_ref)
    ii = lax.broadcasted_iota(jnp.int32, (q, q), 0)
    jj = lax.broadcasted_iota(jnp.int32, (q, q), 1)
    lower = ii >= jj
    diag = ii == jj
    dt_t = dt_all.T
    acum_t = acum.T
    rcum_t = rcum.T

    ef = jnp.exp(_expand(acum, ef_ref))
    eb = jnp.exp(_expand(rcum, eb_ref))
    y_off = []
    cb = []
    for g in range(SSD_GROUPS):
        gs = slice(g * GROUP_COLS, (g + 1) * GROUP_COLS)
        c_g = c_all[:, g * SSD_STATE:(g + 1) * SSD_STATE].astype(BF16)
        b_g = b_all[:, g * SSD_STATE:(g + 1) * SSD_STATE].astype(BF16)
        cb.append(lax.dot_general(c_g, b_g, (((1,), (1,)), ((), ())),
                                  preferred_element_type=F32))
        y_off.append(ef[:, gs] * jnp.dot(c_g, hf_in[:, gs].astype(BF16),
                                         preferred_element_type=F32)
                     + eb[:, gs] * jnp.dot(c_g, hb_in[:, gs].astype(BF16),
                                           preferred_element_type=F32))
    y_off = jnp.concatenate(y_off, axis=1)

    y_cols = []
    for pr in range(HEAD_PAIRS):
        g = pr // PAIRS_PER_GROUP
        m_parts = []
        for h in (2 * pr, 2 * pr + 1):
            hb_lane = SSD_HEADS + h
            colf = jnp.broadcast_to(acum[:, h:h + 1], (q, q))
            colb = jnp.broadcast_to(rcum[:, hb_lane:hb_lane + 1], (q, q))
            seg = jnp.where(lower, colf - acum_t[h:h + 1, :],
                            colb - rcum_t[hb_lane:hb_lane + 1, :])
            dt_b_row = dt_t[hb_lane:hb_lane + 1, :]
            w = (jnp.exp(seg) * jnp.where(lower, dt_t[h:h + 1, :], dt_b_row)
                 + jnp.where(diag, dt_b_row, 0.0))
            m_parts.append(cb[g] * w)
        sl = slice(pr * LANES, (pr + 1) * LANES)
        y_cols.append(_dot(jnp.concatenate(m_parts, axis=1), _block_diag_pair(xh[:, sl])))
        between_pairs()
    y = jnp.concatenate(y_cols, axis=1) + y_off + dskip_ref[...] * xh
    h_new = _carry_update(hf_in, dt_all * jnp.exp(atot - acum), atot, ef_ref, xh, b_all)
    return y, h_new


def _tiling(x, tm, seq_len):
    s_n, l_n, _ = x.shape
    return s_n, l_n, l_n // tm, seq_len // tm, l_n // seq_len


FRONT_SEGS = (SSD_CONV_DIM, DT_PAD)
FRONT_NTILE = 256


def _front_kernel(*refs, has_pos, has_h0, tiles_per_seq, n_tiles):
    x_ref, xp_ref, xn_ref = refs[:3]
    refs = refs[3:]
    if has_pos:
        embr_ref, embc_ref = refs[:2]
        refs = refs[2:]
    if has_h0:
        h0b_ref = refs[0]
        refs = refs[1:]
    mod_ref, w_ref, wdt_ref, convw_ref, convb_ref, dtbias_ref, aneg_ref, eb_ref = refs[:8]
    xa_ref, dt_ref, hbs_ref, hfb_ref, hb_s, xb_s, dts_s = refs[8:]
    shift1 = mod_ref[0, :, 0:D_MODEL]
    scale1 = mod_ref[0, :, D_MODEL:2 * D_MODEL]
    tm = x_ref.shape[1]
    n_ext = tm + 2 * HALO
    q = SSD_CHUNK
    j = pl.program_id(1)
    tile = n_tiles - 1 - jnp.minimum(j, n_tiles - 1)
    t = tile % tiles_per_seq
    t_scan = jnp.minimum(n_tiles - j, n_tiles - 1) % tiles_per_seq
    slot = j % 2
    scan_slot = 1 - slot

    @pl.when(j == 0)
    def _():
        xb_s[1] = jnp.zeros(xb_s.shape[1:], F32)
        dts_s[1] = jnp.zeros(dts_s.shape[1:], F32)

    def init_state():
        if has_h0:
            hb_s[...] = h0b_ref[0].T
        else:
            hb_s[...] = jnp.zeros_like(hb_s)

    if tiles_per_seq > 1:
        pl.when(t_scan == tiles_per_seq - 1)(init_state)
    else:
        init_state()

    def modulate(x):
        return (x * (1.0 + scale1) + shift1).astype(BF16)

    if has_pos:
        rows_per_tile = tm // GRID_W
        n_grid_rows = embr_ref.shape[0]
        r0 = tile * rows_per_tile
        x = _add_pos(x_ref, embr_ref, embc_ref, r0)
        x_prev = xp_ref[0] + _pos_halo(embr_ref, embc_ref, jnp.maximum(r0 - 1, 0),
                                       GRID_W - HALO)
        x_next = xn_ref[0] + _pos_halo(embr_ref, embc_ref,
                                       jnp.minimum(r0 + rows_per_tile, n_grid_rows - 1), 0)
    else:
        x, x_prev, x_next = x_ref[0], xp_ref[0], xn_ref[0]
    h = modulate(x)
    h_ext = jnp.concatenate([modulate(x_prev), h, modulate(x_next)], axis=0)

    row = lax.broadcasted_iota(jnp.int32, (tm, 1), 0)
    zero_prev = row == 0
    zero_next = row == tm - 1
    if tiles_per_seq > 1:
        zero_prev = jnp.logical_and(zero_prev, t == 0)
        zero_next = jnp.logical_and(zero_next, t == tiles_per_seq - 1)

    def project():
        dt = jnp.dot(h, wdt_ref[...], preferred_element_type=F32)
        dt_ref[0] = dt
        dts_s[slot] = dt
        yield
        for c0 in range(0, SSD_CONV_DIM, FRONT_NTILE):
            c1 = c0 + FRONT_NTILE
            p = jnp.dot(h_ext, w_ref[:, c0:c1], preferred_element_type=F32)
            cur = p[HALO:HALO + tm]
            prv = jnp.where(zero_prev, 0.0, pltpu.roll(p, 1, 0)[HALO:HALO + tm])
            nxt = jnp.where(zero_next, 0.0, pltpu.roll(p, n_ext - 1, 0)[HALO:HALO + tm])
            xc = (convw_ref[0:1, c0:c1] * prv + convw_ref[1:2, c0:c1] * cur
                  + convw_ref[2:3, c0:c1] * nxt + convb_ref[:, c0:c1])
            xa = _silu(xc)
            xa_ref[0, :, c0:c1] = xa
            if c0 < D_SSD + BC_COLS:
                xb_s[slot, :, c0:c1] = xa
            yield

    final = []

    def scan():
        h_state = hb_s[...]
        for ch in reversed(range(tm // q)):
            rows = slice(ch * q, (ch + 1) * q)
            hbs_ref[0, ch] = h_state
            dt_all, _, _, _, rcum, rtot = _chunk_decay(dts_s[scan_slot, rows, :], dtbias_ref,
                                                       aneg_ref)
            yield
            w_exp = _expand(dt_all * jnp.exp(rtot - rcum), eb_ref)
            decay = jnp.exp(_expand(jnp.broadcast_to(rtot, (BF16_ROWS, LANES)), eb_ref))[0:1, :]
            yield
            wx = (w_exp * xb_s[scan_slot, rows, 0:D_SSD]).astype(BF16)
            b_all = xb_s[scan_slot, rows, D_SSD:D_SSD + BC_COLS]
            parts = []
            for g in range(SSD_GROUPS):
                parts.append(jnp.dot(b_all[:, g * SSD_STATE:(g + 1) * SSD_STATE].T.astype(BF16),
                                     wx[:, g * GROUP_COLS:(g + 1) * GROUP_COLS],
                                     preferred_element_type=F32))
                yield
            h_state = decay * h_state + jnp.concatenate(parts, axis=1)
        hb_s[...] = h_state
        final.append(h_state)

    stages = [project(), scan()]
    while stages:
        for gen in list(stages):
            if next(gen, StopIteration) is StopIteration:
                stages.remove(gen)

    def final_state():
        hfb_ref[0] = final[0].T

    if tiles_per_seq > 1:
        pl.when(t_scan == 0)(final_state)
    else:
        final_state()


def _front(x, pos, mods, h0b, w_xbc, w_dt, convw, convb, dtbias, aneg, e_b, mod_row0, tm,
           seq_len):
    s_n, l_n, n_tiles, tiles_per_seq, seqs_per_row = _tiling(x, tm, seq_len)
    has_pos = pos is not None
    has_h0 = h0b is not None
    hb = tm // HALO
    n_hb = l_n // HALO
    cpt = tm // SSD_CHUNK
    fwd = lambda j: n_tiles - 1 - jnp.minimum(j, n_tiles - 1)
    scanned = lambda j: jnp.minimum(n_tiles - j, n_tiles - 1)
    tok = lambda s, j: (s, fwd(j), 0)
    seq_of = lambda s, j: (s * seqs_per_row + scanned(j) // tiles_per_seq, 0, 0)
    in_specs = [pl.BlockSpec((1, tm, D_MODEL), tok),
                pl.BlockSpec((1, HALO, D_MODEL),
                             lambda s, j: (s, jnp.maximum(fwd(j) * hb - 1, 0), 0)),
                pl.BlockSpec((1, HALO, D_MODEL),
                             lambda s, j: (s, jnp.minimum((fwd(j) + 1) * hb, n_hb - 1), 0))]
    args = [x, x, x]
    if has_pos:
        in_specs += [_const_spec(pos[0].shape), _const_spec(pos[1].shape)]
        args += list(pos)
    state_spec = pl.BlockSpec((1, D_SSD, SSD_STATE), seq_of)
    if has_h0:
        in_specs.append(state_spec)
        args.append(h0b)
    in_specs.append(pl.BlockSpec((1, 1, 6 * D_MODEL), lambda s, j: (mod_row0 + s, 0, 0)))
    args.append(mods)
    for a in (w_xbc, w_dt, convw, convb, dtbias, aneg, e_b):
        in_specs.append(_const_spec(a.shape))
        args.append(a)
    return pl.pallas_call(
        functools.partial(_front_kernel, has_pos=has_pos, has_h0=has_h0,
                          tiles_per_seq=tiles_per_seq, n_tiles=n_tiles),
        grid=(s_n, n_tiles + 1),
        in_specs=in_specs,
        out_specs=[pl.BlockSpec((1, tm, w), tok) for w in FRONT_SEGS]
        + [pl.BlockSpec((1, cpt, SSD_STATE, D_SSD), lambda s, j: (s, scanned(j), 0, 0)),
           state_spec],
        out_shape=[jax.ShapeDtypeStruct((s_n, l_n, w), F32) for w in FRONT_SEGS]
        + [jax.ShapeDtypeStruct((s_n, l_n // SSD_CHUNK, SSD_STATE, D_SSD), F32),
           jax.ShapeDtypeStruct((s_n * seqs_per_row, D_SSD, SSD_STATE), F32)],
        scratch_shapes=[pltpu.VMEM((SSD_STATE, D_SSD), F32),
                        pltpu.VMEM((2, tm, D_SSD + BC_COLS), F32),
                        pltpu.VMEM((2, tm, DT_PAD), F32)],
        compiler_params=_params(48, 2),
        name="front",
    )(*args)


MAIN_Z, MAIN_U, MAIN_V, MAIN_GA, MAIN_GB = (i * D_MODEL for i in range(5))
MAIN_COLS = 5 * D_MODEL


def _main_kernel(*refs, has_pos, has_h0, tiles_per_seq):
    x_ref, xa_ref, dtp_ref, hbs_ref = refs[:4]
    refs = refs[4:]
    if has_pos:
        embr_ref, embc_ref = refs[:2]
        refs = refs[2:]
    if has_h0:
        h0f_ref = refs[0]
        refs = refs[1:]
    (mod_ref, w_ref, dtbias_ref, aneg_ref, dskip_ref, normw_ref, ef_ref, eb_ref, lng_ref,
     lnb_ref, wsp_ref, bsp_ref, wpa_ref, wpb_ref, wout_ref, ln1g_ref, ln1b_ref) = refs[:17]
    o_ref, hff_ref, hf_s = refs[17:]
    tm = x_ref.shape[1]
    q = SSD_CHUNK
    tile = pl.program_id(1)
    t = tile % tiles_per_seq

    if has_pos:
        x = _add_pos(x_ref, embr_ref, embc_ref, tile * (tm // GRID_W))
    else:
        x = x_ref[0]
    shift1 = mod_ref[0, :, 0:D_MODEL]
    scale1 = mod_ref[0, :, D_MODEL:2 * D_MODEL]
    gate1 = mod_ref[0, :, 2 * D_MODEL:3 * D_MODEL]
    h = (x * (1.0 + scale1) + shift1).astype(BF16)

    col_tiles = {}
    pending = list(range(0, MAIN_COLS, IN_NTILE))
    n_slots = (tm // q) * HEAD_PAIRS
    n_fill = (MAIN_COLS - MAIN_U) // IN_NTILE
    slot = [0]

    def project_next():
        c = pending.pop(0)
        col_tiles[c] = jnp.dot(h, w_ref[:, c:c + IN_NTILE], preferred_element_type=F32)

    def between_pairs():
        slot[0] += 1
        done = n_fill - len(pending)
        for _ in range(-(-slot[0] * n_fill // n_slots) - done):
            project_next()

    def in_proj(c0):
        while pending and pending[0] < c0 + D_MODEL:
            project_next()
        return jnp.concatenate([col_tiles.pop(c) for c in range(c0, c0 + D_MODEL, IN_NTILE)],
                               axis=1)

    def init_state():
        if has_h0:
            hf_s[...] = h0f_ref[0].T
        else:
            hf_s[...] = jnp.zeros_like(hf_s)

    if tiles_per_seq > 1:
        pl.when(t == 0)(init_state)
    else:
        init_state()
    h_state = hf_s[...]
    zz = in_proj(MAIN_Z)
    zz = _silu(zz)
    ya = []
    for ch in range(tm // q):
        rows = slice(ch * q, (ch + 1) * q)
        y, h_state = _ssd_chunk(
            xa_ref[0, rows, 0:D_SSD], xa_ref[0, rows, D_SSD:D_SSD + BC_COLS],
            xa_ref[0, rows, D_SSD + BC_COLS:], dtp_ref[0, rows, :], h_state, hbs_ref[0, ch],
            dtbias_ref, aneg_ref, dskip_ref, ef_ref, eb_ref, between_pairs)
        yg = y * zz[rows]
        ms = jnp.mean(yg * yg, axis=-1, keepdims=True)
        ya.append(yg * lax.rsqrt(ms + EPS) * normw_ref[...])
    hf_s[...] = h_state

    def final_state():
        hff_ref[0] = h_state.T

    if tiles_per_seq > 1:
        pl.when(t == tiles_per_seq - 1)(final_state)
    else:
        final_state()
    ya = jnp.concatenate(ya, axis=0)
    proj_a = _dot(ya, wpa_ref[...])

    u = _gelu(in_proj(MAIN_U))
    v = _layer_norm(_gelu(in_proj(MAIN_V)), lng_ref[...], lnb_ref[...])
    rows_out = []
    for ch in range(tm // SGU_CHUNK):
        vc = v[ch * SGU_CHUNK:(ch + 1) * SGU_CHUNK, :]
        cols = [_dot(wsp_ref[g], vc[:, g * SGU_GROUP_DIM:(g + 1) * SGU_GROUP_DIM])
                for g in range(SGU_GROUPS)]
        rows_out.append(jnp.concatenate(cols, axis=1) + bsp_ref[...])
    yb = u * jnp.concatenate(rows_out, axis=0)

    merged = (_sigmoid(in_proj(MAIN_GA)) * proj_a
              + _sigmoid(in_proj(MAIN_GB)) * _dot(yb, wpb_ref[...]))
    o = _dot(merged, wout_ref[...])
    o_ref[0] = _layer_norm(DN_ALPHA * x + gate1 * o, ln1g_ref[...], ln1b_ref[...])


def _main(x, pos, xa, dtp, hbs, h0f, mods, mod_row0, tm, seq_len, consts):
    s_n, l_n, n_tiles, tiles_per_seq, seqs_per_row = _tiling(x, tm, seq_len)
    has_pos = pos is not None
    has_h0 = h0f is not None
    cpt = tm // SSD_CHUNK
    tok = lambda s, j: (s, j, 0)
    in_specs = [pl.BlockSpec((1, tm, D_MODEL), tok),
                pl.BlockSpec((1, tm, SSD_CONV_DIM), tok),
                pl.BlockSpec((1, tm, DT_PAD), tok),
                pl.BlockSpec((1, cpt, SSD_STATE, D_SSD), lambda s, j: (s, j, 0, 0))]
    args = [x, xa, dtp, hbs]
    if has_pos:
        in_specs += [_const_spec(pos[0].shape), _const_spec(pos[1].shape)]
        args += list(pos)
    state_spec = pl.BlockSpec((1, D_SSD, SSD_STATE),
                              lambda s, j: (s * seqs_per_row + j // tiles_per_seq, 0, 0))
    if has_h0:
        in_specs.append(state_spec)
        args.append(h0f)
    in_specs.append(pl.BlockSpec((1, 1, 6 * D_MODEL), lambda s, j: (mod_row0 + s, 0, 0)))
    args.append(mods)
    for a in consts:
        in_specs.append(_const_spec(a.shape))
        args.append(a)
    return pl.pallas_call(
        functools.partial(_main_kernel, has_pos=has_pos, has_h0=has_h0,
                          tiles_per_seq=tiles_per_seq),
        grid=(s_n, n_tiles),
        in_specs=in_specs,
        out_specs=[pl.BlockSpec((1, tm, D_MODEL), tok), state_spec],
        out_shape=[jax.ShapeDtypeStruct((s_n, l_n, D_MODEL), F32),
                   jax.ShapeDtypeStruct((s_n * seqs_per_row, D_SSD, SSD_STATE), F32)],
        scratch_shapes=[pltpu.VMEM((SSD_STATE, D_SSD), F32)],
        compiler_params=_params(56, 2),
        name="main",
    )(*args)


FF_TILE = 1024
FFN_TM = 1024


def _ffn_kernel(x_ref, mod_ref, w1_ref, w2_ref, g_ref, b_ref, o_ref):
    x = x_ref[0]
    shift2 = mod_ref[0, :, 3 * D_MODEL:4 * D_MODEL]
    scale2 = mod_ref[0, :, 4 * D_MODEL:5 * D_MODEL]
    gate2 = mod_ref[0, :, 5 * D_MODEL:6 * D_MODEL]
    h = (x * (1.0 + scale2) + shift2).astype(BF16)
    f = jnp.zeros_like(x)
    for k0 in range(0, D_FF, FF_TILE):
        a = jnp.dot(h, w1_ref[:, k0:k0 + FF_TILE], preferred_element_type=F32)
        a = jnp.maximum(a, 0.0)
        f = f + _dot(a * a, w2_ref[k0:k0 + FF_TILE, :])
    o_ref[0] = _layer_norm(DN_ALPHA * x + gate2 * f, g_ref[...], b_ref[...])


def _ffn(x, mods, mod_row0, tm, w1, w2, g, b):
    s_n, l_n, _ = x.shape
    tok = lambda s, j: (s, j, 0)
    return pl.pallas_call(
        _ffn_kernel,
        grid=(s_n, l_n // tm),
        in_specs=[pl.BlockSpec((1, tm, D_MODEL), tok),
                  pl.BlockSpec((1, 1, 6 * D_MODEL), lambda s, j: (mod_row0 + s, 0, 0)),
                  _const_spec(w1.shape), _const_spec(w2.shape),
                  _const_spec(g.shape), _const_spec(b.shape)],
        out_specs=pl.BlockSpec((1, tm, D_MODEL), tok),
        out_shape=jax.ShapeDtypeStruct((s_n, l_n, D_MODEL), F32),
        compiler_params=_params(48, 2),
        name="ffn",
    )(x, mods, w1, w2, g, b)


def _grid_pos_tables(n_tokens):
    rows = n_tokens // GRID_W
    quarter = D_MODEL // 4
    omega = 1.0 / (POS_BASE ** (jnp.arange(quarter, dtype=F32) / quarter))
    r = jnp.arange(rows, dtype=F32)[:, None] * omega
    col = jnp.arange(GRID_W, dtype=F32)[:, None] * omega
    emb_r = jnp.concatenate([jnp.sin(r), jnp.cos(r)], axis=-1)
    emb_c = jnp.concatenate([jnp.sin(col), jnp.cos(col)], axis=-1)
    return jnp.broadcast_to(emb_r[:, None, :], (rows, SUBLANES, D_MODEL // 2)), emb_c


def _dt_lanes(v, axis):
    pad = [(0, 0)] * v.ndim
    pad[axis] = (0, DT_PAD - DT_COPIES * DT_GROUP)
    return jnp.pad(jnp.concatenate([v] * DT_COPIES, axis=axis), pad)


def _head_expanders():
    lane = jnp.arange(LANES)[:, None]
    head = jnp.arange(D_SSD)[None, :] // SSD_HEAD_DIM
    valid = lane < DT_COPIES * DT_GROUP
    e_f = jnp.logical_and(valid, lane % DT_GROUP == head)
    e_b = jnp.logical_and(valid, lane % DT_GROUP == head + SSD_HEADS)
    return e_f.astype(BF16), e_b.astype(BF16)


def _stream(x, pos, mods, mod_row0, seq_len, h0f, h0b, prm, tm):
    xa, dtp, hbs, hb = _front(x, pos, mods, h0b, prm["w_xbc"], prm["w_dt"], prm["conv_w"],
                              prm["conv_b"],
                              prm["dt_bias"], prm["a_neg"], prm["e_b"], mod_row0, tm, seq_len)
    x1, hf = _main(x, pos, xa, dtp, hbs, h0f, mods, mod_row0, tm, seq_len, prm["main"])
    y = _ffn(x1, mods, mod_row0, FFN_TM, prm["w_ff1"], prm["w_ff2"], prm["ln2_g"], prm["ln2_b"])
    return y, hf, hb


def kernel(x_prompt, x_sample, state_ssd_fwd, state_ssd_bwd, c, c_ctx, w_ada, b_ada, w_in, conv_w, conv_b, dt_bias_fwd, dt_bias_bwd, a_log_fwd, a_log_bwd, d_skip, ssd_norm_w, sgu_ln_g, sgu_ln_b, w_spatial, b_spatial, w_proj_a, w_proj_b, w_out, ln1_g, ln1_b, w_ff1, w_ff2, ln2_g, ln2_b):
    batch, seq, _ = x_prompt.shape
    dec_batch, dec_seq, _ = x_sample.shape
    l = 0
    row = lambda v: v.reshape(1, -1)

    cond = jnp.zeros((SUBLANES, D_MODEL), F32).at[0].set(c_ctx).at[1:1 + dec_batch].set(c)
    mods = _ada(cond, w_ada[l], b_ada[l]).reshape(SUBLANES, 1, 6 * D_MODEL)

    o1 = D_SSD
    o2 = o1 + SSD_CONV_DIM
    o3 = o2 + DT_GROUP
    w_t = jnp.swapaxes(w_in[l], 0, 1)
    step = W_COL_TILE // DT_GROUP
    w_xbc = _w_cols(w_t, lambda i: o1 // DT_GROUP + i * step, SSD_CONV_DIM // W_COL_TILE)
    z_tiles = o1 // W_COL_TILE
    w_main = _w_cols(
        w_t, lambda i: jnp.where(i < z_tiles, i * step, o3 // DT_GROUP + (i - z_tiles) * step),
        MAIN_COLS // W_COL_TILE)
    assert o2 % LANES == 0
    w_dt = _w_dt(w_t, o2)
    e_f, e_b = _head_expanders()
    dt_bias = row(_dt_lanes(jnp.concatenate([dt_bias_fwd[l], dt_bias_bwd[l]]), 0))
    a_neg = row(_dt_lanes(-jnp.exp(jnp.concatenate([a_log_fwd[l], a_log_bwd[l]])), 0))

    prm = dict(
        w_xbc=w_xbc, w_dt=w_dt, conv_w=conv_w[l], conv_b=row(conv_b[l]),
        dt_bias=dt_bias, a_neg=a_neg, e_b=e_b,
        main=(w_main, dt_bias, a_neg, row(jnp.repeat(d_skip[l], SSD_HEAD_DIM)),
              row(ssd_norm_w[l]), e_f, e_b, row(sgu_ln_g[l]), row(sgu_ln_b[l]),
              w_spatial[l].astype(BF16), jnp.repeat(b_spatial[l].T, SGU_GROUP_DIM, axis=1),
              w_proj_a[l].astype(BF16), w_proj_b[l].astype(BF16), w_out[l].astype(BF16),
              row(ln1_g[l]), row(ln1_b[l])),
        w_ff1=w_ff1[l].astype(BF16), w_ff2=w_ff2[l].astype(BF16),
        ln2_g=row(ln2_g[l]), ln2_b=row(ln2_b[l]),
    )

    tm = 256
    xp, hf, hb = _stream(x_prompt.reshape(1, batch * seq, D_MODEL), None, mods, 0, seq,
                         None, None, prm, tm)
    pos = _grid_pos_tables(dec_seq)
    h0f = state_ssd_fwd[:, l].reshape(dec_batch, D_SSD, SSD_STATE)
    h0b = state_ssd_bwd[:, l].reshape(dec_batch, D_SSD, SSD_STATE)
    xs, _, _ = _stream(x_sample, pos, mods, 1, dec_seq, h0f, h0b, prm, tm)

    state_shape = (batch, DEPTH, SSD_HEADS, SSD_HEAD_DIM, SSD_STATE)
    return (xp.reshape(batch, seq, D_MODEL), xs,
            hf.reshape(state_shape), hb.reshape(state_shape))
```

```python
import functools
import math

import jax
import jax.numpy as jnp
from jax import lax
from jax.experimental import pallas as pl
from jax.experimental.pallas import tpu as pltpu

D_MODEL = 1024
GRID_W = 64
POS_BASE = 10000.0
D_SSD = 1024
SSD_HEAD_DIM = 64
SSD_HEADS = D_SSD // SSD_HEAD_DIM
SSD_GROUPS = 2
SSD_STATE = 128
SSD_CHUNK = 128
SSD_CONV_DIM = D_SSD + 2 * SSD_GROUPS * SSD_STATE
D_SGU = 1024
SGU_GROUPS = 8
SGU_GROUP_DIM = D_SGU // SGU_GROUPS
SGU_CHUNK = 128
D_FF = 4 * D_MODEL
DEPTH = 1
DN_ALPHA = (2.0 * DEPTH) ** 0.25
EPS = 1e-5

LANES = 128
SUBLANES = 8
BF16_ROWS = 16

DT_GROUP = 2 * SSD_HEADS
DT_COPIES = 3
DT_PAD = LANES
HEAD_PAIRS = SSD_HEADS // 2
PAIRS_PER_GROUP = HEAD_PAIRS // SSD_GROUPS
GROUP_COLS = D_SSD // SSD_GROUPS
BC_COLS = SSD_GROUPS * SSD_STATE

BF16 = jnp.bfloat16
F32 = jnp.float32
HIGHEST = lax.Precision.HIGHEST


def _dot(a, b):
    return jnp.dot(a.astype(BF16), b.astype(BF16), preferred_element_type=F32)


def _dot_exact(a, b):
    return jnp.dot(a, b, precision=HIGHEST, preferred_element_type=F32)


def _layer_norm(x, g, b):
    mu = jnp.mean(x, axis=-1, keepdims=True)
    xc = x - mu
    var = jnp.mean(xc * xc, axis=-1, keepdims=True)
    return xc * lax.rsqrt(var + EPS) * g + b


def _softplus(x):
    return jnp.maximum(x, 0.0) + jnp.log1p(jnp.exp(-jnp.abs(x)))


def _gelu(x):
    return 0.5 * x * (1.0 + lax.erf(x * (1.0 / math.sqrt(2.0))))


def _sigmoid(x):
    return 0.5 * jnp.tanh(0.5 * x) + 0.5


def _silu(x):
    half = 0.5 * x
    return half * jnp.tanh(half) + half


def _const_spec(shape):
    zeros = (0,) * len(shape)
    return pl.BlockSpec(shape, lambda *_: zeros, pipeline_mode=pl.Buffered(1))


def _params(vmem_mb, n_axes):
    return pltpu.CompilerParams(
        dimension_semantics=("arbitrary",) * n_axes,
        vmem_limit_bytes=vmem_mb * 1024 * 1024)


def _ada_kernel(c_ref, w_ref, b_ref, o_ref):
    o_ref[...] = _dot_exact(_silu(c_ref[...]), w_ref[...]) + b_ref[...]


def _ada(cond, w_ada, b_ada):
    n = w_ada.shape[1]
    tn = 2048
    return pl.pallas_call(
        _ada_kernel,
        grid=(n // tn,),
        in_specs=[pl.BlockSpec((SUBLANES, D_MODEL), lambda j: (0, 0)),
                  pl.BlockSpec((D_MODEL, tn), lambda j: (0, j)),
                  pl.BlockSpec((1, tn), lambda j: (0, j))],
        out_specs=pl.BlockSpec((SUBLANES, tn), lambda j: (0, j)),
        out_shape=jax.ShapeDtypeStruct((SUBLANES, n), F32),
        compiler_params=_params(32, 1),
        name="ada",
    )(cond, w_ada, b_ada.reshape(1, n))


W_COL_TILE = 512


def _w_cols_kernel(src_ref, o_ref):
    o_ref[...] = src_ref[...].T.astype(BF16)


def _w_dt_kernel(src_ref, o_ref):
    t = src_ref[...].T
    lane = lax.broadcasted_iota(jnp.int32, t.shape, 1)
    dt = jnp.where(lane < DT_GROUP, t, 0.0)
    o_ref[...] = (dt + pltpu.roll(dt, DT_GROUP, 1) + pltpu.roll(dt, 2 * DT_GROUP, 1)).astype(BF16)


def _w_dt(w_t, col0):
    d = w_t.shape[1]
    return pl.pallas_call(
        _w_dt_kernel,
        grid=(1,),
        in_specs=[pl.BlockSpec((LANES, d), lambda i: (col0 // LANES, 0))],
        out_specs=pl.BlockSpec((d, LANES), lambda i: (0, 0)),
        out_shape=jax.ShapeDtypeStruct((d, LANES), BF16),
        name="w_dt",
    )(w_t)


def _w_cols(w_t, col_start, n_tiles):
    d = w_t.shape[1]
    return pl.pallas_call(
        _w_cols_kernel,
        grid=(n_tiles,),
        in_specs=[pl.BlockSpec((pl.Element(W_COL_TILE), pl.Element(d)),
                               lambda i: (col_start(i) * DT_GROUP, 0))],
        out_specs=pl.BlockSpec((d, W_COL_TILE), lambda i: (0, i)),
        out_shape=jax.ShapeDtypeStruct((d, n_tiles * W_COL_TILE), BF16),
        compiler_params=_params(32, 1),
        name="w_cols",
    )(w_t)


IN_NTILE = 512
HALO = BF16_ROWS


def _add_pos(x_ref, embr_ref, embc_ref, grid_row0):
    n_rows = x_ref.shape[1] // GRID_W
    col = embc_ref[...]
    blocks = []
    for i in range(n_rows):
        row = embr_ref[grid_row0 + i]
        row = jnp.concatenate([row] * (GRID_W // SUBLANES), axis=0)
        blocks.append(x_ref[0, i * GRID_W:(i + 1) * GRID_W, :]
                      + jnp.concatenate([row, col], axis=1))
    return jnp.concatenate(blocks, axis=0)


def _pos_halo(embr_ref, embc_ref, grid_row, col0):
    left = jnp.concatenate([embr_ref[grid_row]] * (HALO // SUBLANES), axis=0)
    return jnp.concatenate([left, embc_ref[col0:col0 + HALO, :]], axis=1)


def _block_diag_pair(xp):
    lane = lax.broadcasted_iota(jnp.int32, xp.shape, 1)
    first = lane < SSD_HEAD_DIM
    return jnp.concatenate([jnp.where(first, xp, 0.0), jnp.where(first, 0.0, xp)], axis=0)


def _split_terms(v):
    lane = lax.broadcasted_iota(jnp.int32, v.shape, 1)
    hi = v.astype(BF16).astype(F32)
    r1 = v - hi
    mid = r1.astype(BF16).astype(F32)
    lo = r1 - mid
    return jnp.where(lane < DT_GROUP, hi, jnp.where(lane < 2 * DT_GROUP, mid, lo)).astype(BF16)


def _expand(v, e_ref):
    return jnp.dot(_split_terms(v), e_ref[...], preferred_element_type=F32)


def _chunk_decay(dtp, dtbias_ref, aneg_ref):
    q = dtp.shape[0]
    dt_all = _softplus(dtp + dtbias_ref[...])
    a_all = dt_all * aneg_ref[...]
    ii = lax.broadcasted_iota(jnp.int32, (q, q), 0)
    jj = lax.broadcasted_iota(jnp.int32, (q, q), 1)
    acum = _dot_exact((ii >= jj).astype(F32), a_all)
    atot = acum[q - 1:q, :]
    rcum = atot - acum + a_all
    return dt_all, a_all, acum, atot, rcum, rcum[0:1, :]


def _carry_update(h_old, w_tok, tot_row, e_ref, xh, b_all):
    wx = (_expand(w_tok, e_ref) * xh).astype(BF16)
    s = jnp.concatenate(
        [jnp.dot(b_all[:, g * SSD_STATE:(g + 1) * SSD_STATE].T.astype(BF16),
                 wx[:, g * GROUP_COLS:(g + 1) * GROUP_COLS], preferred_element_type=F32)
         for g in range(SSD_GROUPS)], axis=1)
    decay = jnp.exp(_expand(jnp.broadcast_to(tot_row, (BF16_ROWS, LANES)), e_ref))[0:1, :]
    return decay * h_old + s


def _ssd_chunk(xh, b_all, c_all, dtp, hf_in, hb_in, dtbias_ref, aneg_ref, dskip_ref, ef_ref,
               eb_ref, between_pairs=lambda: None):
    q = SSD_CHUNK
    dt_all, a_all, acum, atot, rcum, _ = _chunk_decay(dtp, dtbias_ref, aneg_ref)
    ii = lax.broadcasted_iota(jnp.int32, (q, q), 0)
    jj = lax.broadcasted_iota(jnp.int32, (q, q), 1)
    lower = ii >= jj
    diag = ii == jj
    dt_t = dt_all.T
    acum_t = acum.T
    rcum_t = rcum.T

    ef = jnp.exp(_expand(acum, ef_ref))
    eb = jnp.exp(_expand(rcum, eb_ref))
    y_off = []
    cb = []
    for g in range(SSD_GROUPS):
        gs = slice(g * GROUP_COLS, (g + 1) * GROUP_COLS)
        c_g = c_all[:, g * SSD_STATE:(g + 1) * SSD_STATE].astype(BF16)
        b_g = b_all[:, g * SSD_STATE:(g + 1) * SSD_STATE].astype(BF16)
        cb.append(lax.dot_general(c_g, b_g, (((1,), (1,)), ((), ())),
                                  preferred_element_type=F32))
        y_off.append(ef[:, gs] * jnp.dot(c_g, hf_in[:, gs].astype(BF16),
                                         preferred_element_type=F32)
                     + eb[:, gs] * jnp.dot(c_g, hb_in[:, gs].astype(BF16),
                                           preferred_element_type=F32))
    y_off = jnp.concatenate(y_off, axis=1)

    y_cols = []
    for pr in range(HEAD_PAIRS):
        g = pr // PAIRS_PER_GROUP
        m_parts = []
        for h in (2 * pr, 2 * pr + 1):
            hb_lane = SSD_HEADS + h
            colf = jnp.broadcast_to(acum[:, h:h + 1], (q, q))
            colb = jnp.broadcast_to(rcum[:, hb_lane:hb_lane + 1], (q, q))
            seg = jnp.where(lower, colf - acum_t[h:h + 1, :],
                            colb - rcum_t[hb_lane:hb_lane + 1, :])
            dt_b_row = dt_t[hb_lane:hb_lane + 1, :]
            w = (jnp.exp(seg) * jnp.where(lower, dt_t[h:h + 1, :], dt_b_row)
                 + jnp.where(diag, dt_b_row, 0.0))
            m_parts.append(cb[g] * w)
        sl = slice(pr * LANES, (pr + 1) * LANES)
        y_cols.append(_dot(jnp.concatenate(m_parts, axis=1), _block_diag_pair(xh[:, sl])))
        between_pairs()
    y = jnp.concatenate(y_cols, axis=1) + y_off + dskip_ref[...] * xh
    h_new = _carry_update(hf_in, dt_all * jnp.exp(atot - acum), atot, ef_ref, xh, b_all)
    return y, h_new


def _tiling(x, tm, seq_len):
    s_n, l_n, _ = x.shape
    return s_n, l_n, l_n // tm, seq_len // tm, l_n // seq_len


FRONT_SEGS = (SSD_CONV_DIM, DT_PAD)
FRONT_NTILE = 256


def _front_kernel(*refs, has_pos, has_h0, tiles_per_seq, n_tiles):
    x_ref, xp_ref, xn_ref = refs[:3]
    refs = refs[3:]
    if has_pos:
        embr_ref, embc_ref = refs[:2]
        refs = refs[2:]
    if has_h0:
        h0b_ref = refs[0]
        refs = refs[1:]
    mod_ref, w_ref, wdt_ref, convw_ref, convb_ref, dtbias_ref, aneg_ref, eb_ref = refs[:8]
    xa_ref, dt_ref, hbs_ref, hfb_ref, hb_s, xb_s, dts_s = refs[8:]
    shift1 = mod_ref[0, :, 0:D_MODEL]
    scale1 = mod_ref[0, :, D_MODEL:2 * D_MODEL]
    tm = x_ref.shape[1]
    n_ext = tm + 2 * HALO
    q = SSD_CHUNK
    j = pl.program_id(1)
    tile = n_tiles - 1 - jnp.minimum(j, n_tiles - 1)
    t = tile % tiles_per_seq
    t_scan = jnp.minimum(n_tiles - j, n_tiles - 1) % tiles_per_seq
    slot = j % 2
    scan_slot = 1 - slot

    def init_state():
        if has_h0:
            hb_s[...] = h0b_ref[0].T
        else:
            hb_s[...] = jnp.zeros_like(hb_s)

    def modulate(x):
        return (x * (1.0 + scale1) + shift1).astype(BF16)

    def project():
        if has_pos:
            rows_per_tile = tm // GRID_W
            n_grid_rows = embr_ref.shape[0]
            r0 = tile * rows_per_tile
            x = _add_pos(x_ref, embr_ref, embc_ref, r0)
            x_prev = xp_ref[0] + _pos_halo(embr_ref, embc_ref, jnp.maximum(r0 - 1, 0),
                                           GRID_W - HALO)
            x_next = xn_ref[0] + _pos_halo(embr_ref, embc_ref,
                                           jnp.minimum(r0 + rows_per_tile, n_grid_rows - 1), 0)
        else:
            x, x_prev, x_next = x_ref[0], xp_ref[0], xn_ref[0]
        h = modulate(x)
        h_ext = jnp.concatenate([modulate(x_prev), h, modulate(x_next)], axis=0)

        row = lax.broadcasted_iota(jnp.int32, (tm, 1), 0)
        zero_prev = row == 0
        zero_next = row == tm - 1
        if tiles_per_seq > 1:
            zero_prev = jnp.logical_and(zero_prev, t == 0)
            zero_next = jnp.logical_and(zero_next, t == tiles_per_seq - 1)

        dt = jnp.dot(h, wdt_ref[...], preferred_element_type=F32)
        dt_ref[0] = dt
        dts_s[slot] = dt
        yield
        for c0 in range(0, SSD_CONV_DIM, FRONT_NTILE):
            c1 = c0 + FRONT_NTILE
            p = jnp.dot(h_ext, w_ref[:, c0:c1], preferred_element_type=F32)
            cur = p[HALO:HALO + tm]
            prv = jnp.where(zero_prev, 0.0, pltpu.roll(p, 1, 0)[HALO:HALO + tm])
            nxt = jnp.where(zero_next, 0.0, pltpu.roll(p, n_ext - 1, 0)[HALO:HALO + tm])
            xc = (convw_ref[0:1, c0:c1] * prv + convw_ref[1:2, c0:c1] * cur
                  + convw_ref[2:3, c0:c1] * nxt + convb_ref[:, c0:c1])
            xa = _silu(xc)
            xa_ref[0, :, c0:c1] = xa
            if c0 < D_SSD + BC_COLS:
                xb_s[slot, :, c0:c1] = xa
            yield

    def scan():
        h_state = hb_s[...]
        for ch in reversed(range(tm // q)):
            rows = slice(ch * q, (ch + 1) * q)
            hbs_ref[0, ch] = h_state
            dt_all, _, _, _, rcum, rtot = _chunk_decay(dts_s[scan_slot, rows, :], dtbias_ref,
                                                       aneg_ref)
            yield
            w_exp = _expand(dt_all * jnp.exp(rtot - rcum), eb_ref)
            decay = jnp.exp(_expand(jnp.broadcast_to(rtot, (BF16_ROWS, LANES)), eb_ref))[0:1, :]
            yield
            wx = (w_exp * xb_s[scan_slot, rows, 0:D_SSD]).astype(BF16)
            b_all = xb_s[scan_slot, rows, D_SSD:D_SSD + BC_COLS]
            parts = []
            for g in range(SSD_GROUPS):
                parts.append(jnp.dot(b_all[:, g * SSD_STATE:(g + 1) * SSD_STATE].T.astype(BF16),
                                     wx[:, g * GROUP_COLS:(g + 1) * GROUP_COLS],
                                     preferred_element_type=F32))
                yield
            h_state = decay * h_state + jnp.concatenate(parts, axis=1)
        hb_s[...] = h_state

        def final_state():
            hfb_ref[0] = h_state.T

        if tiles_per_seq > 1:
            pl.when(t_scan == 0)(final_state)
        else:
            final_state()

    def run(*stage_fns):
        if scan in stage_fns:
            if tiles_per_seq > 1:
                pl.when(t_scan == tiles_per_seq - 1)(init_state)
            else:
                init_state()
        stages = [fn() for fn in stage_fns]
        while stages:
            for gen in list(stages):
                if next(gen, StopIteration) is StopIteration:
                    stages.remove(gen)

    pl.when(j == 0)(functools.partial(run, project))
    pl.when(jnp.logical_and(j > 0, j < n_tiles))(functools.partial(run, project, scan))
    pl.when(j == n_tiles)(functools.partial(run, scan))


def _front(x, pos, mods, h0b, w_xbc, w_dt, convw, convb, dtbias, aneg, e_b, mod_row0, tm,
           seq_len):
    s_n, l_n, n_tiles, tiles_per_seq, seqs_per_row = _tiling(x, tm, seq_len)
    has_pos = pos is not None
    has_h0 = h0b is not None
    hb = tm // HALO
    n_hb = l_n // HALO
    cpt = tm // SSD_CHUNK
    fwd = lambda j: n_tiles - 1 - jnp.minimum(j, n_tiles - 1)
    scanned = lambda j: jnp.minimum(n_tiles - j, n_tiles - 1)
    tok = lambda s, j: (s, fwd(j), 0)
    seq_of = lambda s, j: (s * seqs_per_row + scanned(j) // tiles_per_seq, 0, 0)
    in_specs = [pl.BlockSpec((1, tm, D_MODEL), tok),
                pl.BlockSpec((1, HALO, D_MODEL),
                             lambda s, j: (s, jnp.maximum(fwd(j) * hb - 1, 0), 0)),
                pl.BlockSpec((1, HALO, D_MODEL),
                             lambda s, j: (s, jnp.minimum((fwd(j) + 1) * hb, n_hb - 1), 0))]
    args = [x, x, x]
    if has_pos:
        in_specs += [_const_spec(pos[0].shape), _const_spec(pos[1].shape)]
        args += list(pos)
    state_spec = pl.BlockSpec((1, D_SSD, SSD_STATE), seq_of)
    if has_h0:
        in_specs.append(state_spec)
        args.append(h0b)
    in_specs.append(pl.BlockSpec((1, 1, 6 * D_MODEL), lambda s, j: (mod_row0 + s, 0, 0)))
    args.append(mods)
    for a in (w_xbc, w_dt, convw, convb, dtbias, aneg, e_b):
        in_specs.append(_const_spec(a.shape))
        args.append(a)
    return pl.pallas_call(
        functools.partial(_front_kernel, has_pos=has_pos, has_h0=has_h0,
                          tiles_per_seq=tiles_per_seq, n_tiles=n_tiles),
        grid=(s_n, n_tiles + 1),
        in_specs=in_specs,
        out_specs=[pl.BlockSpec((1, tm, w), tok) for w in FRONT_SEGS]
        + [pl.BlockSpec((1, cpt, SSD_STATE, D_SSD), lambda s, j: (s, scanned(j), 0, 0)),
           state_spec],
        out_shape=[jax.ShapeDtypeStruct((s_n, l_n, w), F32) for w in FRONT_SEGS]
        + [jax.ShapeDtypeStruct((s_n, l_n // SSD_CHUNK, SSD_STATE, D_SSD), F32),
           jax.ShapeDtypeStruct((s_n * seqs_per_row, D_SSD, SSD_STATE), F32)],
        scratch_shapes=[pltpu.VMEM((SSD_STATE, D_SSD), F32),
                        pltpu.VMEM((2, tm, D_SSD + BC_COLS), F32),
                        pltpu.VMEM((2, tm, DT_PAD), F32)],
        compiler_params=_params(48, 2),
        name="front",
    )(*args)


MAIN_Z, MAIN_U, MAIN_V, MAIN_GA, MAIN_GB = (i * D_MODEL for i in range(5))
MAIN_COLS = 5 * D_MODEL


def _main_kernel(*refs, has_pos, has_h0, tiles_per_seq):
    x_ref, xa_ref, dtp_ref, hbs_ref = refs[:4]
    refs = refs[4:]
    if has_pos:
        embr_ref, embc_ref = refs[:2]
        refs = refs[2:]
    if has_h0:
        h0f_ref = refs[0]
        refs = refs[1:]
    (mod_ref, w_ref, dtbias_ref, aneg_ref, dskip_ref, normw_ref, ef_ref, eb_ref, lng_ref,
     lnb_ref, wsp_ref, bsp_ref, wpa_ref, wpb_ref, wout_ref, ln1g_ref, ln1b_ref) = refs[:17]
    o_ref, hff_ref, hf_s = refs[17:]
    tm = x_ref.shape[1]
    q = SSD_CHUNK
    tile = pl.program_id(1)
    t = tile % tiles_per_seq

    if has_pos:
        x = _add_pos(x_ref, embr_ref, embc_ref, tile * (tm // GRID_W))
    else:
        x = x_ref[0]
    shift1 = mod_ref[0, :, 0:D_MODEL]
    scale1 = mod_ref[0, :, D_MODEL:2 * D_MODEL]
    gate1 = mod_ref[0, :, 2 * D_MODEL:3 * D_MODEL]
    h = (x * (1.0 + scale1) + shift1).astype(BF16)

    col_tiles = {}
    pending = list(range(0, MAIN_COLS, IN_NTILE))
    n_slots = (tm // q) * HEAD_PAIRS
    n_fill = (MAIN_COLS - MAIN_U) // IN_NTILE
    slot = [0]

    def project_next():
        c = pending.pop(0)
        col_tiles[c] = jnp.dot(h, w_ref[:, c:c + IN_NTILE], preferred_element_type=F32)

    def between_pairs():
        slot[0] += 1
        done = n_fill - len(pending)
        for _ in range(-(-slot[0] * n_fill // n_slots) - done):
            project_next()

    def in_proj(c0):
        while pending and pending[0] < c0 + D_MODEL:
            project_next()
        return jnp.concatenate([col_tiles.pop(c) for c in range(c0, c0 + D_MODEL, IN_NTILE)],
                               axis=1)

    def init_state():
        if has_h0:
            hf_s[...] = h0f_ref[0].T
        else:
            hf_s[...] = jnp.zeros_like(hf_s)

    if tiles_per_seq > 1:
        pl.when(t == 0)(init_state)
    else:
        init_state()
    h_state = hf_s[...]
    zz = in_proj(MAIN_Z)
    zz = _silu(zz)
    ya = []
    for ch in range(tm // q):
        rows = slice(ch * q, (ch + 1) * q)
        y, h_state = _ssd_chunk(
            xa_ref[0, rows, 0:D_SSD], xa_ref[0, rows, D_SSD:D_SSD + BC_COLS],
            xa_ref[0, rows, D_SSD + BC_COLS:], dtp_ref[0, rows, :], h_state, hbs_ref[0, ch],
            dtbias_ref, aneg_ref, dskip_ref, ef_ref, eb_ref, between_pairs)
        yg = y * zz[rows]
        ms = jnp.mean(yg * yg, axis=-1, keepdims=True)
        ya.append(yg * lax.rsqrt(ms + EPS) * normw_ref[...])
    hf_s[...] = h_state

    def final_state():
        hff_ref[0] = h_state.T

    if tiles_per_seq > 1:
        pl.when(t == tiles_per_seq - 1)(final_state)
    else:
        final_state()
    ya = jnp.concatenate(ya, axis=0)
    proj_a = _dot(ya, wpa_ref[...])

    u = _gelu(in_proj(MAIN_U))
    v = _layer_norm(_gelu(in_proj(MAIN_V)), lng_ref[...], lnb_ref[...])
    rows_out = []
    for ch in range(tm // SGU_CHUNK):
        vc = v[ch * SGU_CHUNK:(ch + 1) * SGU_CHUNK, :]
        cols = [_dot(wsp_ref[g], vc[:, g * SGU_GROUP_DIM:(g + 1) * SGU_GROUP_DIM])
                for g in range(SGU_GROUPS)]
        rows_out.append(jnp.concatenate(cols, axis=1) + bsp_ref[...])
    yb = u * jnp.concatenate(rows_out, axis=0)

    merged = (_sigmoid(in_proj(MAIN_GA)) * proj_a
              + _sigmoid(in_proj(MAIN_GB)) * _dot(yb, wpb_ref[...]))
    o = _dot(merged, wout_ref[...])
    o_ref[0] = _layer_norm(DN_ALPHA * x + gate1 * o, ln1g_ref[...], ln1b_ref[...])


def _main(x, pos, xa, dtp, hbs, h0f, mods, mod_row0, tm, seq_len, consts):
    s_n, l_n, n_tiles, tiles_per_seq, seqs_per_row = _tiling(x, tm, seq_len)
    has_pos = pos is not None
    has_h0 = h0f is not None
    cpt = tm // SSD_CHUNK
    tok = lambda s, j: (s, j, 0)
    in_specs = [pl.BlockSpec((1, tm, D_MODEL), tok),
                pl.BlockSpec((1, tm, SSD_CONV_DIM), tok),
                pl.BlockSpec((1, tm, DT_PAD), tok),
                pl.BlockSpec((1, cpt, SSD_STATE, D_SSD), lambda s, j: (s, j, 0, 0))]
    args = [x, xa, dtp, hbs]
    if has_pos:
        in_specs += [_const_spec(pos[0].shape), _const_spec(pos[1].shape)]
        args += list(pos)
    state_spec = pl.BlockSpec((1, D_SSD, SSD_STATE),
                              lambda s, j: (s * seqs_per_row + j // tiles_per_seq, 0, 0))
    if has_h0:
        in_specs.append(state_spec)
        args.append(h0f)
    in_specs.append(pl.BlockSpec((1, 1, 6 * D_MODEL), lambda s, j: (mod_row0 + s, 0, 0)))
    args.append(mods)
    for a in consts:
        in_specs.append(_const_spec(a.shape))
        args.append(a)
    return pl.pallas_call(
        functools.partial(_main_kernel, has_pos=has_pos, has_h0=has_h0,
                          tiles_per_seq=tiles_per_seq),
        grid=(s_n, n_tiles),
        in_specs=in_specs,
        out_specs=[pl.BlockSpec((1, tm, D_MODEL), tok), state_spec],
        out_shape=[jax.ShapeDtypeStruct((s_n, l_n, D_MODEL), F32),
                   jax.ShapeDtypeStruct((s_n * seqs_per_row, D_SSD, SSD_STATE), F32)],
        scratch_shapes=[pltpu.VMEM((SSD_STATE, D_SSD), F32)],
        compiler_params=_params(56, 2),
        name="main",
    )(*args)


FF_TILE = 1024
FFN_TM = 1024


def _ffn_kernel(x_ref, mod_ref, w1_ref, w2_ref, g_ref, b_ref, o_ref):
    x = x_ref[0]
    shift2 = mod_ref[0, :, 3 * D_MODEL:4 * D_MODEL]
    scale2 = mod_ref[0, :, 4 * D_MODEL:5 * D_MODEL]
    gate2 = mod_ref[0, :, 5 * D_MODEL:6 * D_MODEL]
    h = (x * (1.0 + scale2) + shift2).astype(BF16)
    f = jnp.zeros_like(x)
    for k0 in range(0, D_FF, FF_TILE):
        a = jnp.dot(h, w1_ref[:, k0:k0 + FF_TILE], preferred_element_type=F32)
        a = jnp.maximum(a, 0.0)
        f = f + _dot(a * a, w2_ref[k0:k0 + FF_TILE, :])
    o_ref[0] = _layer_norm(DN_ALPHA * x + gate2 * f, g_ref[...], b_ref[...])


def _ffn(x, mods, mod_row0, tm, w1, w2, g, b):
    s_n, l_n, _ = x.shape
    tok = lambda s, j: (s, j, 0)
    return pl.pallas_call(
        _ffn_kernel,
        grid=(s_n, l_n // tm),
        in_specs=[pl.BlockSpec((1, tm, D_MODEL), tok),
                  pl.BlockSpec((1, 1, 6 * D_MODEL), lambda s, j: (mod_row0 + s, 0, 0)),
                  _const_spec(w1.shape), _const_spec(w2.shape),
                  _const_spec(g.shape), _const_spec(b.shape)],
        out_specs=pl.BlockSpec((1, tm, D_MODEL), tok),
        out_shape=jax.ShapeDtypeStruct((s_n, l_n, D_MODEL), F32),
        compiler_params=_params(48, 2),
        name="ffn",
    )(x, mods, w1, w2, g, b)


def _grid_pos_tables(n_tokens):
    rows = n_tokens // GRID_W
    quarter = D_MODEL // 4
    omega = 1.0 / (POS_BASE ** (jnp.arange(quarter, dtype=F32) / quarter))
    r = jnp.arange(rows, dtype=F32)[:, None] * omega
    col = jnp.arange(GRID_W, dtype=F32)[:, None] * omega
    emb_r = jnp.concatenate([jnp.sin(r), jnp.cos(r)], axis=-1)
    emb_c = jnp.concatenate([jnp.sin(col), jnp.cos(col)], axis=-1)
    return jnp.broadcast_to(emb_r[:, None, :], (rows, SUBLANES, D_MODEL // 2)), emb_c


def _dt_lanes(v, axis):
    pad = [(0, 0)] * v.ndim
    pad[axis] = (0, DT_PAD - DT_COPIES * DT_GROUP)
    return jnp.pad(jnp.concatenate([v] * DT_COPIES, axis=axis), pad)


def _head_expanders():
    lane = jnp.arange(LANES)[:, None]
    head = jnp.arange(D_SSD)[None, :] // SSD_HEAD_DIM
    valid = lane < DT_COPIES * DT_GROUP
    e_f = jnp.logical_and(valid, lane % DT_GROUP == head)
    e_b = jnp.logical_and(valid, lane % DT_GROUP == head + SSD_HEADS)
    return e_f.astype(BF16), e_b.astype(BF16)


def _stream(x, pos, mods, mod_row0, seq_len, h0f, h0b, prm, tm):
    xa, dtp, hbs, hb = _front(x, pos, mods, h0b, prm["w_xbc"], prm["w_dt"], prm["conv_w"],
                              prm["conv_b"],
                              prm["dt_bias"], prm["a_neg"], prm["e_b"], mod_row0, tm, seq_len)
    x1, hf = _main(x, pos, xa, dtp, hbs, h0f, mods, mod_row0, tm, seq_len, prm["main"])
    y = _ffn(x1, mods, mod_row0, FFN_TM, prm["w_ff1"], prm["w_ff2"], prm["ln2_g"], prm["ln2_b"])
    return y, hf, hb


def kernel(x_prompt, x_sample, state_ssd_fwd, state_ssd_bwd, c, c_ctx, w_ada, b_ada, w_in, conv_w, conv_b, dt_bias_fwd, dt_bias_bwd, a_log_fwd, a_log_bwd, d_skip, ssd_norm_w, sgu_ln_g, sgu_ln_b, w_spatial, b_spatial, w_proj_a, w_proj_b, w_out, ln1_g, ln1_b, w_ff1, w_ff2, ln2_g, ln2_b):
    batch, seq, _ = x_prompt.shape
    dec_batch, dec_seq, _ = x_sample.shape
    l = 0
    row = lambda v: v.reshape(1, -1)

    cond = jnp.zeros((SUBLANES, D_MODEL), F32).at[0].set(c_ctx).at[1:1 + dec_batch].set(c)
    mods = _ada(cond, w_ada[l], b_ada[l]).reshape(SUBLANES, 1, 6 * D_MODEL)

    o1 = D_SSD
    o2 = o1 + SSD_CONV_DIM
    o3 = o2 + DT_GROUP
    w_t = jnp.swapaxes(w_in[l], 0, 1)
    step = W_COL_TILE // DT_GROUP
    w_xbc = _w_cols(w_t, lambda i: o1 // DT_GROUP + i * step, SSD_CONV_DIM // W_COL_TILE)
    z_tiles = o1 // W_COL_TILE
    w_main = _w_cols(
        w_t, lambda i: jnp.where(i < z_tiles, i * step, o3 // DT_GROUP + (i - z_tiles) * step),
        MAIN_COLS // W_COL_TILE)
    assert o2 % LANES == 0
    w_dt = _w_dt(w_t, o2)
    e_f, e_b = _head_expanders()
    dt_bias = row(_dt_lanes(jnp.concatenate([dt_bias_fwd[l], dt_bias_bwd[l]]), 0))
    a_neg = row(_dt_lanes(-jnp.exp(jnp.concatenate([a_log_fwd[l], a_log_bwd[l]])), 0))

    prm = dict(
        w_xbc=w_xbc, w_dt=w_dt, conv_w=conv_w[l], conv_b=row(conv_b[l]),
        dt_bias=dt_bias, a_neg=a_neg, e_b=e_b,
        main=(w_main, dt_bias, a_neg, row(jnp.repeat(d_skip[l], SSD_HEAD_DIM)),
              row(ssd_norm_w[l]), e_f, e_b, row(sgu_ln_g[l]), row(sgu_ln_b[l]),
              w_spatial[l].astype(BF16), jnp.repeat(b_spatial[l].T, SGU_GROUP_DIM, axis=1),
              w_proj_a[l].astype(BF16), w_proj_b[l].astype(BF16), w_out[l].astype(BF16),
              row(ln1_g[l]), row(ln1_b[l])),
        w_ff1=w_ff1[l].astype(BF16), w_ff2=w_ff2[l].astype(BF16),
        ln2_g=row(ln2_g[l]), ln2_b=row(ln2_b[l]),
    )

    tm = 256
    xp, hf, hb = _stream(x_prompt.reshape(1, batch * seq, D_MODEL), None, mods, 0, seq,
                         None, None, prm, tm)
    pos = _grid_pos_tables(dec_seq)
    h0f = state_ssd_fwd[:, l].reshape(dec_batch, D_SSD, SSD_STATE)
    h0b = state_ssd_bwd[:, l].reshape(dec_batch, D_SSD, SSD_STATE)
    xs, _, _ = _stream(x_sample, pos, mods, 1, dec_seq, h0f, h0b, prm, tm)

    state_shape = (batch, DEPTH, SSD_HEADS, SSD_HEAD_DIM, SSD_STATE)
    return (xp.reshape(batch, seq, D_MODEL), xs,
            hf.reshape(state_shape), hb.reshape(state_shape))
```

```python
import functools
import math

import jax
import jax.numpy as jnp
from jax import lax
from jax.experimental import pallas as pl
from jax.experimental.pallas import tpu as pltpu

D_MODEL = 1024
GRID_W = 64
POS_BASE = 10000.0
D_SSD = 1024
SSD_HEAD_DIM = 64
SSD_HEADS = D_SSD // SSD_HEAD_DIM
SSD_GROUPS = 2
SSD_STATE = 128
SSD_CHUNK = 128
SSD_CONV_DIM = D_SSD + 2 * SSD_GROUPS * SSD_STATE
D_SGU = 1024
SGU_GROUPS = 8
SGU_GROUP_DIM = D_SGU // SGU_GROUPS
SGU_CHUNK = 128
D_FF = 4 * D_MODEL
DEPTH = 1
DN_ALPHA = (2.0 * DEPTH) ** 0.25
EPS = 1e-5

LANES = 128
SUBLANES = 8
BF16_ROWS = 16

DT_GROUP = 2 * SSD_HEADS
DT_COPIES = 3
DT_PAD = LANES
HEAD_PAIRS = SSD_HEADS // 2
PAIRS_PER_GROUP = HEAD_PAIRS // SSD_GROUPS
GROUP_COLS = D_SSD // SSD_GROUPS
BC_COLS = SSD_GROUPS * SSD_STATE

BF16 = jnp.bfloat16
F32 = jnp.float32
HIGHEST = lax.Precision.HIGHEST


def _dot(a, b):
    return jnp.dot(a.astype(BF16), b.astype(BF16), preferred_element_type=F32)


def _dot_exact(a, b):
    return jnp.dot(a, b, precision=HIGHEST, preferred_element_type=F32)


def _layer_norm(x, g, b):
    mu = jnp.mean(x, axis=-1, keepdims=True)
    xc = x - mu
    var = jnp.mean(xc * xc, axis=-1, keepdims=True)
    return xc * lax.rsqrt(var + EPS) * g + b


def _softplus(x):
    return jnp.maximum(x, 0.0) + jnp.log1p(jnp.exp(-jnp.abs(x)))


def _gelu(x):
    return 0.5 * x * (1.0 + lax.erf(x * (1.0 / math.sqrt(2.0))))


def _sigmoid(x):
    return 0.5 * jnp.tanh(0.5 * x) + 0.5


def _silu(x):
    half = 0.5 * x
    return half * jnp.tanh(half) + half


def _const_spec(shape):
    zeros = (0,) * len(shape)
    return pl.BlockSpec(shape, lambda *_: zeros, pipeline_mode=pl.Buffered(1))


def _params(vmem_mb, n_axes):
    return pltpu.CompilerParams(
        dimension_semantics=("arbitrary",) * n_axes,
        vmem_limit_bytes=vmem_mb * 1024 * 1024)


def _ada_kernel(c_ref, w_ref, b_ref, o_ref):
    c = _silu(c_ref[...])
    c_hi = c.astype(BF16)
    c_lo = (c - c_hi.astype(F32)).astype(BF16)
    w = w_ref[...]
    w_hi = w.astype(BF16)
    w_lo = (w - w_hi.astype(F32)).astype(BF16)
    rows = c.shape[0]
    by_hi = jnp.dot(jnp.concatenate([c_hi, c_lo], axis=0), w_hi, preferred_element_type=F32)
    by_lo = jnp.dot(c_hi, w_lo, preferred_element_type=F32)
    o_ref[...] = by_hi[:rows] + by_hi[rows:] + by_lo + b_ref[...]


def _ada(cond, w_ada, b_ada):
    n = w_ada.shape[1]
    tn = 2048
    return pl.pallas_call(
        _ada_kernel,
        grid=(n // tn,),
        in_specs=[pl.BlockSpec((SUBLANES, D_MODEL), lambda j: (0, 0)),
                  pl.BlockSpec((D_MODEL, tn), lambda j: (0, j)),
                  pl.BlockSpec((1, tn), lambda j: (0, j))],
        out_specs=pl.BlockSpec((SUBLANES, tn), lambda j: (0, j)),
        out_shape=jax.ShapeDtypeStruct((SUBLANES, n), F32),
        compiler_params=_params(32, 1),
        name="ada",
    )(cond, w_ada, b_ada.reshape(1, n))


W_COL_TILE = 512


def _w_cols_kernel(src_ref, o_ref):
    o_ref[...] = src_ref[...].T.astype(BF16)


def _w_dt_kernel(src_ref, o_ref):
    t = src_ref[...].T
    lane = lax.broadcasted_iota(jnp.int32, t.shape, 1)
    dt = jnp.where(lane < DT_GROUP, t, 0.0)
    o_ref[...] = (dt + pltpu.roll(dt, DT_GROUP, 1) + pltpu.roll(dt, 2 * DT_GROUP, 1)).astype(BF16)


def _w_dt(w_t, col0):
    d = w_t.shape[1]
    return pl.pallas_call(
        _w_dt_kernel,
        grid=(1,),
        in_specs=[pl.BlockSpec((LANES, d), lambda i: (col0 // LANES, 0))],
        out_specs=pl.BlockSpec((d, LANES), lambda i: (0, 0)),
        out_shape=jax.ShapeDtypeStruct((d, LANES), BF16),
        name="w_dt",
    )(w_t)


def _w_cols(w_t, col_start, n_tiles):
    d = w_t.shape[1]
    return pl.pallas_call(
        _w_cols_kernel,
        grid=(n_tiles,),
        in_specs=[pl.BlockSpec((pl.Element(W_COL_TILE), pl.Element(d)),
                               lambda i: (col_start(i) * DT_GROUP, 0))],
        out_specs=pl.BlockSpec((d, W_COL_TILE), lambda i: (0, i)),
        out_shape=jax.ShapeDtypeStruct((d, n_tiles * W_COL_TILE), BF16),
        compiler_params=_params(32, 1),
        name="w_cols",
    )(w_t)


IN_NTILE = 512
HALO = BF16_ROWS


def _add_pos(x_ref, embr_ref, embc_ref, grid_row0):
    n_rows = x_ref.shape[1] // GRID_W
    col = embc_ref[...]
    blocks = []
    for i in range(n_rows):
        row = embr_ref[grid_row0 + i]
        row = jnp.concatenate([row] * (GRID_W // SUBLANES), axis=0)
        blocks.append(x_ref[0, i * GRID_W:(i + 1) * GRID_W, :]
                      + jnp.concatenate([row, col], axis=1))
    return jnp.concatenate(blocks, axis=0)


def _pos_halo(embr_ref, embc_ref, grid_row, col0):
    left = jnp.concatenate([embr_ref[grid_row]] * (HALO // SUBLANES), axis=0)
    return jnp.concatenate([left, embc_ref[col0:col0 + HALO, :]], axis=1)


def _block_diag_pair(xp):
    lane = lax.broadcasted_iota(jnp.int32, xp.shape, 1)
    first = lane < SSD_HEAD_DIM
    return jnp.concatenate([jnp.where(first, xp, 0.0), jnp.where(first, 0.0, xp)], axis=0)


def _split_terms(v):
    lane = lax.broadcasted_iota(jnp.int32, v.shape, 1)
    hi = v.astype(BF16).astype(F32)
    r1 = v - hi
    mid = r1.astype(BF16).astype(F32)
    lo = r1 - mid
    return jnp.where(lane < DT_GROUP, hi, jnp.where(lane < 2 * DT_GROUP, mid, lo)).astype(BF16)


def _expand(v, e_ref):
    return jnp.dot(_split_terms(v), e_ref[...], preferred_element_type=F32)


def _chunk_decay(dtp, dtbias_ref, aneg_ref):
    q = dtp.shape[0]
    dt_all = _softplus(dtp + dtbias_ref[...])
    a_all = dt_all * aneg_ref[...]
    ii = lax.broadcasted_iota(jnp.int32, (q, q), 0)
    jj = lax.broadcasted_iota(jnp.int32, (q, q), 1)
    acum = _dot_exact((ii >= jj).astype(F32), a_all)
    atot = acum[q - 1:q, :]
    rcum = atot - acum + a_all
    return dt_all, a_all, acum, atot, rcum, rcum[0:1, :]


def _carry_update(h_old, w_tok, tot_row, e_ref, xh, b_all):
    wx = (_expand(w_tok, e_ref) * xh).astype(BF16)
    s = jnp.concatenate(
        [jnp.dot(b_all[:, g * SSD_STATE:(g + 1) * SSD_STATE].T.astype(BF16),
                 wx[:, g * GROUP_COLS:(g + 1) * GROUP_COLS], preferred_element_type=F32)
         for g in range(SSD_GROUPS)], axis=1)
    decay = jnp.exp(_expand(jnp.broadcast_to(tot_row, (BF16_ROWS, LANES)), e_ref))[0:1, :]
    return decay * h_old + s


def _ssd_chunk(xh, b_all, c_all, dtp, hf_in, hb_in, dtbias_ref, aneg_ref, dskip_ref, ef_ref,
               eb_ref, between_pairs=lambda: None):
    q = SSD_CHUNK
    dt_all, a_all, acum, atot, rcum, _ = _chunk_decay(dtp, dtbias_ref, aneg_ref)
    ii = lax.broadcasted_iota(jnp.int32, (q, q), 0)
    jj = lax.broadcasted_iota(jnp.int32, (q, q), 1)
    lower = ii >= jj
    diag = ii == jj
    dt_t = dt_all.T
    acum_t = acum.T
    rcum_t = rcum.T

    ef = jnp.exp(_expand(acum, ef_ref))
    eb = jnp.exp(_expand(rcum, eb_ref))
    y_off = []
    cb = []
    for g in range(SSD_GROUPS):
        gs = slice(g * GROUP_COLS, (g + 1) * GROUP_COLS)
        c_g = c_all[:, g * SSD_STATE:(g + 1) * SSD_STATE].astype(BF16)
        b_g = b_all[:, g * SSD_STATE:(g + 1) * SSD_STATE].astype(BF16)
        cb.append(lax.dot_general(c_g, b_g, (((1,), (1,)), ((), ())),
                                  preferred_element_type=F32))
        y_off.append(ef[:, gs] * jnp.dot(c_g, hf_in[:, gs].astype(BF16),
                                         preferred_element_type=F32)
                     + eb[:, gs] * jnp.dot(c_g, hb_in[:, gs].astype(BF16),
                                           preferred_element_type=F32))
    y_off = jnp.concatenate(y_off, axis=1)

    y_cols = []
    for pr in range(HEAD_PAIRS):
        g = pr // PAIRS_PER_GROUP
        m_parts = []
        for h in (2 * pr, 2 * pr + 1):
            hb_lane = SSD_HEADS + h
            colf = jnp.broadcast_to(acum[:, h:h + 1], (q, q))
            colb = jnp.broadcast_to(rcum[:, hb_lane:hb_lane + 1], (q, q))
            seg = jnp.where(lower, colf - acum_t[h:h + 1, :],
                            colb - rcum_t[hb_lane:hb_lane + 1, :])
            dt_b_row = dt_t[hb_lane:hb_lane + 1, :]
            w = (jnp.exp(seg) * jnp.where(lower, dt_t[h:h + 1, :], dt_b_row)
                 + jnp.where(diag, dt_b_row, 0.0))
            m_parts.append(cb[g] * w)
        sl = slice(pr * LANES, (pr + 1) * LANES)
        y_cols.append(_dot(jnp.concatenate(m_parts, axis=1), _block_diag_pair(xh[:, sl])))
        between_pairs()
    y = jnp.concatenate(y_cols, axis=1) + y_off + dskip_ref[...] * xh
    h_new = _carry_update(hf_in, dt_all * jnp.exp(atot - acum), atot, ef_ref, xh, b_all)
    return y, h_new


def _tiling(x, tm, seq_len):
    s_n, l_n, _ = x.shape
    return s_n, l_n, l_n // tm, seq_len // tm, l_n // seq_len


FRONT_SEGS = (SSD_CONV_DIM, DT_PAD)
FRONT_NTILE = 256


def _front_kernel(*refs, has_pos, has_h0, tiles_per_seq, n_tiles):
    x_ref, xp_ref, xn_ref = refs[:3]
    refs = refs[3:]
    if has_pos:
        embr_ref, embc_ref = refs[:2]
        refs = refs[2:]
    if has_h0:
        h0b_ref = refs[0]
        refs = refs[1:]
    mod_ref, w_ref, wdt_ref, convw_ref, convb_ref, dtbias_ref, aneg_ref, eb_ref = refs[:8]
    xa_ref, dt_ref, hbs_ref, hfb_ref, hb_s, xb_s, dts_s = refs[8:]
    shift1 = mod_ref[0, :, 0:D_MODEL]
    scale1 = mod_ref[0, :, D_MODEL:2 * D_MODEL]
    tm = x_ref.shape[1]
    n_ext = tm + 2 * HALO
    q = SSD_CHUNK
    j = pl.program_id(1)
    tile = n_tiles - 1 - jnp.minimum(j, n_tiles - 1)
    t = tile % tiles_per_seq
    t_scan = jnp.minimum(n_tiles - j, n_tiles - 1) % tiles_per_seq
    slot = j % 2
    scan_slot = 1 - slot

    def init_state():
        if has_h0:
            hb_s[...] = h0b_ref[0].T
        else:
            hb_s[...] = jnp.zeros_like(hb_s)

    def modulate(x):
        return (x * (1.0 + scale1) + shift1).astype(BF16)

    def project():
        if has_pos:
            rows_per_tile = tm // GRID_W
            n_grid_rows = embr_ref.shape[0]
            r0 = tile * rows_per_tile
            x = _add_pos(x_ref, embr_ref, embc_ref, r0)
            x_prev = xp_ref[0] + _pos_halo(embr_ref, embc_ref, jnp.maximum(r0 - 1, 0),
                                           GRID_W - HALO)
            x_next = xn_ref[0] + _pos_halo(embr_ref, embc_ref,
                                           jnp.minimum(r0 + rows_per_tile, n_grid_rows - 1), 0)
        else:
            x, x_prev, x_next = x_ref[0], xp_ref[0], xn_ref[0]
        h = modulate(x)
        h_ext = jnp.concatenate([modulate(x_prev), h, modulate(x_next)], axis=0)

        row = lax.broadcasted_iota(jnp.int32, (tm, 1), 0)
        zero_prev = row == 0
        zero_next = row == tm - 1
        if tiles_per_seq > 1:
            zero_prev = jnp.logical_and(zero_prev, t == 0)
            zero_next = jnp.logical_and(zero_next, t == tiles_per_seq - 1)

        dt = jnp.dot(h, wdt_ref[...], preferred_element_type=F32)
        dt_ref[0] = dt
        dts_s[slot] = dt
        yield
        for c0 in range(0, SSD_CONV_DIM, FRONT_NTILE):
            c1 = c0 + FRONT_NTILE
            p = jnp.dot(h_ext, w_ref[:, c0:c1], preferred_element_type=F32)
            cur = p[HALO:HALO + tm]
            prv = jnp.where(zero_prev, 0.0, pltpu.roll(p, 1, 0)[HALO:HALO + tm])
            nxt = jnp.where(zero_next, 0.0, pltpu.roll(p, n_ext - 1, 0)[HALO:HALO + tm])
            xc = (convw_ref[0:1, c0:c1] * prv + convw_ref[1:2, c0:c1] * cur
                  + convw_ref[2:3, c0:c1] * nxt + convb_ref[:, c0:c1])
            xa = _silu(xc)
            xa_ref[0, :, c0:c1] = xa
            if c0 < D_SSD + BC_COLS:
                xb_s[slot, :, c0:c1] = xa
            yield

    def scan():
        h_state = hb_s[...]
        for ch in reversed(range(tm // q)):
            rows = slice(ch * q, (ch + 1) * q)
            hbs_ref[0, ch] = h_state
            dt_all, _, _, _, rcum, rtot = _chunk_decay(dts_s[scan_slot, rows, :], dtbias_ref,
                                                       aneg_ref)
            yield
            w_exp = _expand(dt_all * jnp.exp(rtot - rcum), eb_ref)
            decay = jnp.exp(_expand(jnp.broadcast_to(rtot, (BF16_ROWS, LANES)), eb_ref))[0:1, :]
            yield
            wx = (w_exp * xb_s[scan_slot, rows, 0:D_SSD]).astype(BF16)
            b_all = xb_s[scan_slot, rows, D_SSD:D_SSD + BC_COLS]
            parts = []
            for g in range(SSD_GROUPS):
                parts.append(jnp.dot(b_all[:, g * SSD_STATE:(g + 1) * SSD_STATE].T.astype(BF16),
                                     wx[:, g * GROUP_COLS:(g + 1) * GROUP_COLS],
                                     preferred_element_type=F32))
                yield
            h_state = decay * h_state + jnp.concatenate(parts, axis=1)
        hb_s[...] = h_state

        def final_state():
            hfb_ref[0] = h_state.T

        if tiles_per_seq > 1:
            pl.when(t_scan == 0)(final_state)
        else:
            final_state()

    def run(*stage_fns):
        if scan in stage_fns:
            if tiles_per_seq > 1:
                pl.when(t_scan == tiles_per_seq - 1)(init_state)
            else:
                init_state()
        stages = [fn() for fn in stage_fns]
        while stages:
            for gen in list(stages):
                if next(gen, StopIteration) is StopIteration:
                    stages.remove(gen)

    pl.when(j == 0)(functools.partial(run, project))
    pl.when(jnp.logical_and(j > 0, j < n_tiles))(functools.partial(run, project, scan))
    pl.when(j == n_tiles)(functools.partial(run, scan))


def _front(x, pos, mods, h0b, w_xbc, w_dt, convw, convb, dtbias, aneg, e_b, mod_row0, tm,
           seq_len):
    s_n, l_n, n_tiles, tiles_per_seq, seqs_per_row = _tiling(x, tm, seq_len)
    has_pos = pos is not None
    has_h0 = h0b is not None
    hb = tm // HALO
    n_hb = l_n // HALO
    cpt = tm // SSD_CHUNK
    fwd = lambda j: n_tiles - 1 - jnp.minimum(j, n_tiles - 1)
    scanned = lambda j: jnp.minimum(n_tiles - j, n_tiles - 1)
    tok = lambda s, j: (s, fwd(j), 0)
    seq_of = lambda s, j: (s * seqs_per_row + scanned(j) // tiles_per_seq, 0, 0)
    in_specs = [pl.BlockSpec((1, tm, D_MODEL), tok),
                pl.BlockSpec((1, HALO, D_MODEL),
                             lambda s, j: (s, jnp.maximum(fwd(j) * hb - 1, 0), 0)),
                pl.BlockSpec((1, HALO, D_MODEL),
                             lambda s, j: (s, jnp.minimum((fwd(j) + 1) * hb, n_hb - 1), 0))]
    args = [x, x, x]
    if has_pos:
        in_specs += [_const_spec(pos[0].shape), _const_spec(pos[1].shape)]
        args += list(pos)
    state_spec = pl.BlockSpec((1, D_SSD, SSD_STATE), seq_of)
    if has_h0:
        in_specs.append(state_spec)
        args.append(h0b)
    in_specs.append(pl.BlockSpec((1, 1, 6 * D_MODEL), lambda s, j: (mod_row0 + s, 0, 0)))
    args.append(mods)
    for a in (w_xbc, w_dt, convw, convb, dtbias, aneg, e_b):
        in_specs.append(_const_spec(a.shape))
        args.append(a)
    return pl.pallas_call(
        functools.partial(_front_kernel, has_pos=has_pos, has_h0=has_h0,
                          tiles_per_seq=tiles_per_seq, n_tiles=n_tiles),
        grid=(s_n, n_tiles + 1),
        in_specs=in_specs,
        out_specs=[pl.BlockSpec((1, tm, w), tok) for w in FRONT_SEGS]
        + [pl.BlockSpec((1, cpt, SSD_STATE, D_SSD), lambda s, j: (s, scanned(j), 0, 0)),
           state_spec],
        out_shape=[jax.ShapeDtypeStruct((s_n, l_n, w), F32) for w in FRONT_SEGS]
        + [jax.ShapeDtypeStruct((s_n, l_n // SSD_CHUNK, SSD_STATE, D_SSD), F32),
           jax.ShapeDtypeStruct((s_n * seqs_per_row, D_SSD, SSD_STATE), F32)],
        scratch_shapes=[pltpu.VMEM((SSD_STATE, D_SSD), F32),
                        pltpu.VMEM((2, tm, D_SSD + BC_COLS), F32),
                        pltpu.VMEM((2, tm, DT_PAD), F32)],
        compiler_params=_params(48, 2),
        name="front",
    )(*args)


MAIN_Z, MAIN_U, MAIN_V, MAIN_GA, MAIN_GB = (i * D_MODEL for i in range(5))
MAIN_COLS = 5 * D_MODEL


def _main_kernel(*refs, has_pos, has_h0, tiles_per_seq):
    x_ref, xa_ref, dtp_ref, hbs_ref = refs[:4]
    refs = refs[4:]
    if has_pos:
        embr_ref, embc_ref = refs[:2]
        refs = refs[2:]
    if has_h0:
        h0f_ref = refs[0]
        refs = refs[1:]
    (mod_ref, w_ref, dtbias_ref, aneg_ref, dskip_ref, normw_ref, ef_ref, eb_ref, lng_ref,
     lnb_ref, wsp_ref, bsp_ref, wpa_ref, wpb_ref, wout_ref, ln1g_ref, ln1b_ref) = refs[:17]
    o_ref, hff_ref, hf_s = refs[17:]
    tm = x_ref.shape[1]
    q = SSD_CHUNK
    tile = pl.program_id(1)
    t = tile % tiles_per_seq

    if has_pos:
        x = _add_pos(x_ref, embr_ref, embc_ref, tile * (tm // GRID_W))
    else:
        x = x_ref[0]
    shift1 = mod_ref[0, :, 0:D_MODEL]
    scale1 = mod_ref[0, :, D_MODEL:2 * D_MODEL]
    gate1 = mod_ref[0, :, 2 * D_MODEL:3 * D_MODEL]
    h = (x * (1.0 + scale1) + shift1).astype(BF16)

    col_tiles = {}
    pending = list(range(0, MAIN_COLS, IN_NTILE))
    n_slots = (tm // q) * HEAD_PAIRS
    n_fill = (MAIN_COLS - MAIN_U) // IN_NTILE
    slot = [0]

    def project_next():
        c = pending.pop(0)
        col_tiles[c] = jnp.dot(h, w_ref[:, c:c + IN_NTILE], preferred_element_type=F32)

    def between_pairs():
        slot[0] += 1
        done = n_fill - len(pending)
        for _ in range(-(-slot[0] * n_fill // n_slots) - done):
            project_next()

    def in_proj(c0):
        while pending and pending[0] < c0 + D_MODEL:
            project_next()
        return jnp.concatenate([col_tiles.pop(c) for c in range(c0, c0 + D_MODEL, IN_NTILE)],
                               axis=1)

    def init_state():
        if has_h0:
            hf_s[...] = h0f_ref[0].T
        else:
            hf_s[...] = jnp.zeros_like(hf_s)

    if tiles_per_seq > 1:
        pl.when(t == 0)(init_state)
    else:
        init_state()
    h_state = hf_s[...]
    zz = in_proj(MAIN_Z)
    zz = _silu(zz)
    ya = []
    for ch in range(tm // q):
        rows = slice(ch * q, (ch + 1) * q)
        y, h_state = _ssd_chunk(
            xa_ref[0, rows, 0:D_SSD], xa_ref[0, rows, D_SSD:D_SSD + BC_COLS],
            xa_ref[0, rows, D_SSD + BC_COLS:], dtp_ref[0, rows, :], h_state, hbs_ref[0, ch],
            dtbias_ref, aneg_ref, dskip_ref, ef_ref, eb_ref, between_pairs)
        yg = y * zz[rows]
        ms = jnp.mean(yg * yg, axis=-1, keepdims=True)
        ya.append(yg * lax.rsqrt(ms + EPS) * normw_ref[...])
    hf_s[...] = h_state

    def final_state():
        hff_ref[0] = h_state.T

    if tiles_per_seq > 1:
        pl.when(t == tiles_per_seq - 1)(final_state)
    else:
        final_state()
    ya = jnp.concatenate(ya, axis=0)
    proj_a = _dot(ya, wpa_ref[...])

    u = _gelu(in_proj(MAIN_U))
    v = _layer_norm(_gelu(in_proj(MAIN_V)), lng_ref[...], lnb_ref[...])
    rows_out = []
    for ch in range(tm // SGU_CHUNK):
        vc = v[ch * SGU_CHUNK:(ch + 1) * SGU_CHUNK, :]
        cols = [_dot(wsp_ref[g], vc[:, g * SGU_GROUP_DIM:(g + 1) * SGU_GROUP_DIM])
                for g in range(SGU_GROUPS)]
        rows_out.append(jnp.concatenate(cols, axis=1) + bsp_ref[...])
    yb = u * jnp.concatenate(rows_out, axis=0)

    merged = (_sigmoid(in_proj(MAIN_GA)) * proj_a
              + _sigmoid(in_proj(MAIN_GB)) * _dot(yb, wpb_ref[...]))
    o = _dot(merged, wout_ref[...])
    o_ref[0] = _layer_norm(DN_ALPHA * x + gate1 * o, ln1g_ref[...], ln1b_ref[...])


def _main(x, pos, xa, dtp, hbs, h0f, mods, mod_row0, tm, seq_len, consts):
    s_n, l_n, n_tiles, tiles_per_seq, seqs_per_row = _tiling(x, tm, seq_len)
    has_pos = pos is not None
    has_h0 = h0f is not None
    cpt = tm // SSD_CHUNK
    tok = lambda s, j: (s, j, 0)
    in_specs = [pl.BlockSpec((1, tm, D_MODEL), tok),
                pl.BlockSpec((1, tm, SSD_CONV_DIM), tok),
                pl.BlockSpec((1, tm, DT_PAD), tok),
                pl.BlockSpec((1, cpt, SSD_STATE, D_SSD), lambda s, j: (s, j, 0, 0))]
    args = [x, xa, dtp, hbs]
    if has_pos:
        in_specs += [_const_spec(pos[0].shape), _const_spec(pos[1].shape)]
        args += list(pos)
    state_spec = pl.BlockSpec((1, D_SSD, SSD_STATE),
                              lambda s, j: (s * seqs_per_row + j // tiles_per_seq, 0, 0))
    if has_h0:
        in_specs.append(state_spec)
        args.append(h0f)
    in_specs.append(pl.BlockSpec((1, 1, 6 * D_MODEL), lambda s, j: (mod_row0 + s, 0, 0)))
    args.append(mods)
    for a in consts:
        in_specs.append(_const_spec(a.shape))
        args.append(a)
    return pl.pallas_call(
        functools.partial(_main_kernel, has_pos=has_pos, has_h0=has_h0,
                          tiles_per_seq=tiles_per_seq),
        grid=(s_n, n_tiles),
        in_specs=in_specs,
        out_specs=[pl.BlockSpec((1, tm, D_MODEL), tok), state_spec],
        out_shape=[jax.ShapeDtypeStruct((s_n, l_n, D_MODEL), F32),
                   jax.ShapeDtypeStruct((s_n * seqs_per_row, D_SSD, SSD_STATE), F32)],
        scratch_shapes=[pltpu.VMEM((SSD_STATE, D_SSD), F32)],
        compiler_params=_params(56, 2),
        name="main",
    )(*args)


FF_TILE = 1024
FFN_TM = 1024


def _ffn_kernel(x_ref, mod_ref, w1_ref, w2_ref, g_ref, b_ref, o_ref):
    x = x_ref[0]
    shift2 = mod_ref[0, :, 3 * D_MODEL:4 * D_MODEL]
    scale2 = mod_ref[0, :, 4 * D_MODEL:5 * D_MODEL]
    gate2 = mod_ref[0, :, 5 * D_MODEL:6 * D_MODEL]
    h = (x * (1.0 + scale2) + shift2).astype(BF16)
    f = jnp.zeros_like(x)
    for k0 in range(0, D_FF, FF_TILE):
        a = jnp.dot(h, w1_ref[:, k0:k0 + FF_TILE], preferred_element_type=F32)
        a = jnp.maximum(a, 0.0)
        f = f + _dot(a * a, w2_ref[k0:k0 + FF_TILE, :])
    o_ref[0] = _layer_norm(DN_ALPHA * x + gate2 * f, g_ref[...], b_ref[...])


def _ffn(x, mods, mod_row0, tm, w1, w2, g, b):
    s_n, l_n, _ = x.shape
    tok = lambda s, j: (s, j, 0)
    return pl.pallas_call(
        _ffn_kernel,
        grid=(s_n, l_n // tm),
        in_specs=[pl.BlockSpec((1, tm, D_MODEL), tok),
                  pl.BlockSpec((1, 1, 6 * D_MODEL), lambda s, j: (mod_row0 + s, 0, 0)),
                  _const_spec(w1.shape), _const_spec(w2.shape),
                  _const_spec(g.shape), _const_spec(b.shape)],
        out_specs=pl.BlockSpec((1, tm, D_MODEL), tok),
        out_shape=jax.ShapeDtypeStruct((s_n, l_n, D_MODEL), F32),
        compiler_params=_params(48, 2),
        name="ffn",
    )(x, mods, w1, w2, g, b)


def _grid_pos_tables(n_tokens):
    rows = n_tokens // GRID_W
    quarter = D_MODEL // 4
    omega = 1.0 / (POS_BASE ** (jnp.arange(quarter, dtype=F32) / quarter))
    r = jnp.arange(rows, dtype=F32)[:, None] * omega
    col = jnp.arange(GRID_W, dtype=F32)[:, None] * omega
    emb_r = jnp.concatenate([jnp.sin(r), jnp.cos(r)], axis=-1)
    emb_c = jnp.concatenate([jnp.sin(col), jnp.cos(col)], axis=-1)
    return jnp.broadcast_to(emb_r[:, None, :], (rows, SUBLANES, D_MODEL // 2)), emb_c


def _dt_lanes(v, axis):
    pad = [(0, 0)] * v.ndim
    pad[axis] = (0, DT_PAD - DT_COPIES * DT_GROUP)
    return jnp.pad(jnp.concatenate([v] * DT_COPIES, axis=axis), pad)


def _head_expanders():
    lane = jnp.arange(LANES)[:, None]
    head = jnp.arange(D_SSD)[None, :] // SSD_HEAD_DIM
    valid = lane < DT_COPIES * DT_GROUP
    e_f = jnp.logical_and(valid, lane % DT_GROUP == head)
    e_b = jnp.logical_and(valid, lane % DT_GROUP == head + SSD_HEADS)
    return e_f.astype(BF16), e_b.astype(BF16)


def _stream(x, pos, mods, mod_row0, seq_len, h0f, h0b, prm, tm):
    xa, dtp, hbs, hb = _front(x, pos, mods, h0b, prm["w_xbc"], prm["w_dt"], prm["conv_w"],
                              prm["conv_b"],
                              prm["dt_bias"], prm["a_neg"], prm["e_b"], mod_row0, tm, seq_len)
    x1, hf = _main(x, pos, xa, dtp, hbs, h0f, mods, mod_row0, tm, seq_len, prm["main"])
    y = _ffn(x1, mods, mod_row0, FFN_TM, prm["w_ff1"], prm["w_ff2"], prm["ln2_g"], prm["ln2_b"])
    return y, hf, hb


def kernel(x_prompt, x_sample, state_ssd_fwd, state_ssd_bwd, c, c_ctx, w_ada, b_ada, w_in, conv_w, conv_b, dt_bias_fwd, dt_bias_bwd, a_log_fwd, a_log_bwd, d_skip, ssd_norm_w, sgu_ln_g, sgu_ln_b, w_spatial, b_spatial, w_proj_a, w_proj_b, w_out, ln1_g, ln1_b, w_ff1, w_ff2, ln2_g, ln2_b):
    batch, seq, _ = x_prompt.shape
    dec_batch, dec_seq, _ = x_sample.shape
    l = 0
    row = lambda v: v.reshape(1, -1)

    cond = jnp.zeros((SUBLANES, D_MODEL), F32).at[0].set(c_ctx).at[1:1 + dec_batch].set(c)
    mods = _ada(cond, w_ada[l], b_ada[l]).reshape(SUBLANES, 1, 6 * D_MODEL)

    o1 = D_SSD
    o2 = o1 + SSD_CONV_DIM
    o3 = o2 + DT_GROUP
    w_t = jnp.swapaxes(w_in[l], 0, 1)
    step = W_COL_TILE // DT_GROUP
    w_xbc = _w_cols(w_t, lambda i: o1 // DT_GROUP + i * step, SSD_CONV_DIM // W_COL_TILE)
    z_tiles = o1 // W_COL_TILE
    w_main = _w_cols(
        w_t, lambda i: jnp.where(i < z_tiles, i * step, o3 // DT_GROUP + (i - z_tiles) * step),
        MAIN_COLS // W_COL_TILE)
    assert o2 % LANES == 0
    w_dt = _w_dt(w_t, o2)
    e_f, e_b = _head_expanders()
    dt_bias = row(_dt_lanes(jnp.concatenate([dt_bias_fwd[l], dt_bias_bwd[l]]), 0))
    a_neg = row(_dt_lanes(-jnp.exp(jnp.concatenate([a_log_fwd[l], a_log_bwd[l]])), 0))

    prm = dict(
        w_xbc=w_xbc, w_dt=w_dt, conv_w=conv_w[l], conv_b=row(conv_b[l]),
        dt_bias=dt_bias, a_neg=a_neg, e_b=e_b,
        main=(w_main, dt_bias, a_neg, row(jnp.repeat(d_skip[l], SSD_HEAD_DIM)),
              row(ssd_norm_w[l]), e_f, e_b, row(sgu_ln_g[l]), row(sgu_ln_b[l]),
              w_spatial[l].astype(BF16), jnp.repeat(b_spatial[l].T, SGU_GROUP_DIM, axis=1),
              w_proj_a[l].astype(BF16), w_proj_b[l].astype(BF16), w_out[l].astype(BF16),
              row(ln1_g[l]), row(ln1_b[l])),
        w_ff1=w_ff1[l].astype(BF16), w_ff2=w_ff2[l].astype(BF16),
        ln2_g=row(ln2_g[l]), ln2_b=row(ln2_b[l]),
    )

    tm = 256
    xp, hf, hb = _stream(x_prompt.reshape(1, batch * seq, D_MODEL), None, mods, 0, seq,
                         None, None, prm, tm)
    pos = _grid_pos_tables(dec_seq)
    h0f = state_ssd_fwd[:, l].reshape(dec_batch, D_SSD, SSD_STATE)
    h0b = state_ssd_bwd[:, l].reshape(dec_batch, D_SSD, SSD_STATE)
    xs, _, _ = _stream(x_sample, pos, mods, 1, dec_seq, h0f, h0b, prm, tm)

    state_shape = (batch, DEPTH, SSD_HEADS, SSD_HEAD_DIM, SSD_STATE)
    return (xp.reshape(batch, seq, D_MODEL), xs,
            hf.reshape(state_shape), hb.reshape(state_shape))
```

```python
import functools
import math

import jax
import jax.numpy as jnp
from jax import lax
from jax.experimental import pallas as pl
from jax.experimental.pallas import tpu as pltpu

D_MODEL = 1024
GRID_W = 64
POS_BASE = 10000.0
D_SSD = 1024
SSD_HEAD_DIM = 64
SSD_HEADS = D_SSD // SSD_HEAD_DIM
SSD_GROUPS = 2
SSD_STATE = 128
SSD_CHUNK = 128
SSD_CONV_DIM = D_SSD + 2 * SSD_GROUPS * SSD_STATE
D_SGU = 1024
SGU_GROUPS = 8
SGU_GROUP_DIM = D_SGU // SGU_GROUPS
SGU_CHUNK = 128
D_FF = 4 * D_MODEL
DEPTH = 1
DN_ALPHA = (2.0 * DEPTH) ** 0.25
EPS = 1e-5

LANES = 128
SUBLANES = 8
BF16_ROWS = 16
V7X_VMEM_MIB = 64
VMEM_MIB_SMALL = V7X_VMEM_MIB // 2
VMEM_MIB_FRONT = 48
VMEM_MIB_MAIN = 56
VMEM_MIB_FFN = 48

DT_GROUP = 2 * SSD_HEADS
DT_COPIES = 3
DT_PAD = LANES
HEAD_PAIRS = SSD_HEADS // 2
PAIRS_PER_GROUP = HEAD_PAIRS // SSD_GROUPS
GROUP_COLS = D_SSD // SSD_GROUPS
BC_COLS = SSD_GROUPS * SSD_STATE

BF16 = jnp.bfloat16
F32 = jnp.float32
HIGHEST = lax.Precision.HIGHEST


def _dot(a, b):
    return jnp.dot(a.astype(BF16), b.astype(BF16), preferred_element_type=F32)


def _dot_exact(a, b):
    return jnp.dot(a, b, precision=HIGHEST, preferred_element_type=F32)


def _layer_norm(x, g, b):
    mu = jnp.mean(x, axis=-1, keepdims=True)
    xc = x - mu
    var = jnp.mean(xc * xc, axis=-1, keepdims=True)
    return xc * lax.rsqrt(var + EPS) * g + b


def _softplus(x):
    return jnp.maximum(x, 0.0) + jnp.log1p(jnp.exp(-jnp.abs(x)))


def _gelu(x):
    return 0.5 * x * (1.0 + lax.erf(x * (1.0 / math.sqrt(2.0))))


def _sigmoid(x):
    return 0.5 * jnp.tanh(0.5 * x) + 0.5


def _silu(x):
    half = 0.5 * x
    return half * jnp.tanh(half) + half


def _const_spec(shape):
    zeros = (0,) * len(shape)
    return pl.BlockSpec(shape, lambda *_: zeros, pipeline_mode=pl.Buffered(1))


def _params(vmem_mib, n_axes):
    return pltpu.CompilerParams(
        dimension_semantics=("arbitrary",) * n_axes,
        vmem_limit_bytes=vmem_mib * 1024 * 1024)


def _ada_kernel(c_ref, w_ref, b_ref, o_ref):
    c = _silu(c_ref[...])
    c_hi = c.astype(BF16)
    c_lo = (c - c_hi.astype(F32)).astype(BF16)
    w = w_ref[...]
    w_hi = w.astype(BF16)
    w_lo = (w - w_hi.astype(F32)).astype(BF16)
    rows = c.shape[0]
    by_hi = jnp.dot(jnp.concatenate([c_hi, c_lo], axis=0), w_hi, preferred_element_type=F32)
    by_lo = jnp.dot(c_hi, w_lo, preferred_element_type=F32)
    o_ref[...] = by_hi[:rows] + by_hi[rows:] + by_lo + b_ref[...]


def _ada(cond, w_ada, b_ada):
    n = w_ada.shape[1]
    tn = 2048
    return pl.pallas_call(
        _ada_kernel,
        grid=(n // tn,),
        in_specs=[pl.BlockSpec((SUBLANES, D_MODEL), lambda j: (0, 0)),
                  pl.BlockSpec((D_MODEL, tn), lambda j: (0, j)),
                  pl.BlockSpec((1, tn), lambda j: (0, j))],
        out_specs=pl.BlockSpec((SUBLANES, tn), lambda j: (0, j)),
        out_shape=jax.ShapeDtypeStruct((SUBLANES, n), F32),
        compiler_params=_params(VMEM_MIB_SMALL, 1),
        name="ada",
    )(cond, w_ada, b_ada.reshape(1, n))


W_COL_TILE = 512


def _w_cols_kernel(src_ref, o_ref):
    o_ref[...] = src_ref[...].T.astype(BF16)


def _w_dt_kernel(src_ref, o_ref):
    t = src_ref[...].T
    lane = lax.broadcasted_iota(jnp.int32, t.shape, 1)
    dt = jnp.where(lane < DT_GROUP, t, 0.0)
    o_ref[...] = (dt + pltpu.roll(dt, DT_GROUP, 1) + pltpu.roll(dt, 2 * DT_GROUP, 1)).astype(BF16)


def _w_dt(w_t, col0):
    d = w_t.shape[1]
    return pl.pallas_call(
        _w_dt_kernel,
        grid=(1,),
        in_specs=[pl.BlockSpec((LANES, d), lambda i: (col0 // LANES, 0))],
        out_specs=pl.BlockSpec((d, LANES), lambda i: (0, 0)),
        out_shape=jax.ShapeDtypeStruct((d, LANES), BF16),
        name="w_dt",
    )(w_t)


def _w_cols(w_t, col_start, n_tiles):
    d = w_t.shape[1]
    return pl.pallas_call(
        _w_cols_kernel,
        grid=(n_tiles,),
        in_specs=[pl.BlockSpec((pl.Element(W_COL_TILE), pl.Element(d)),
                               lambda i: (col_start(i) * DT_GROUP, 0))],
        out_specs=pl.BlockSpec((d, W_COL_TILE), lambda i: (0, i)),
        out_shape=jax.ShapeDtypeStruct((d, n_tiles * W_COL_TILE), BF16),
        compiler_params=_params(VMEM_MIB_SMALL, 1),
        name="w_cols",
    )(w_t)


IN_NTILE = 512
HALO = BF16_ROWS


def _add_pos(x_ref, embr_ref, embc_ref, grid_row0):
    n_rows = x_ref.shape[1] // GRID_W
    col = embc_ref[...]
    blocks = []
    for i in range(n_rows):
        row = embr_ref[grid_row0 + i]
        row = jnp.concatenate([row] * (GRID_W // SUBLANES), axis=0)
        blocks.append(x_ref[0, i * GRID_W:(i + 1) * GRID_W, :]
                      + jnp.concatenate([row, col], axis=1))
    return jnp.concatenate(blocks, axis=0)


def _pos_halo(embr_ref, embc_ref, grid_row, col0):
    left = jnp.concatenate([embr_ref[grid_row]] * (HALO // SUBLANES), axis=0)
    return jnp.concatenate([left, embc_ref[col0:col0 + HALO, :]], axis=1)


def _block_diag_pair(xp):
    lane = lax.broadcasted_iota(jnp.int32, xp.shape, 1)
    first = lane < SSD_HEAD_DIM
    return jnp.concatenate([jnp.where(first, xp, 0.0), jnp.where(first, 0.0, xp)], axis=0)


def _split_terms(v):
    lane = lax.broadcasted_iota(jnp.int32, v.shape, 1)
    hi = v.astype(BF16).astype(F32)
    r1 = v - hi
    mid = r1.astype(BF16).astype(F32)
    lo = r1 - mid
    return jnp.where(lane < DT_GROUP, hi, jnp.where(lane < 2 * DT_GROUP, mid, lo)).astype(BF16)


def _expand(v, e_ref):
    return jnp.dot(_split_terms(v), e_ref[...], preferred_element_type=F32)


def _chunk_decay(dtp, dtbias_ref, aneg_ref, on_mxu=True):
    q = dtp.shape[0]
    dt_all = _softplus(dtp + dtbias_ref[...])
    a_all = dt_all * aneg_ref[...]
    if on_mxu:
        ii = lax.broadcasted_iota(jnp.int32, (q, q), 0)
        jj = lax.broadcasted_iota(jnp.int32, (q, q), 1)
        acum = _dot_exact((ii >= jj).astype(F32), a_all)
    else:
        row = lax.broadcasted_iota(jnp.int32, a_all.shape, 0)
        acum = a_all
        shift = 1
        while shift < q:
            acum = acum + jnp.where(row >= shift, pltpu.roll(acum, shift, 0), 0.0)
            shift *= 2
    atot = acum[q - 1:q, :]
    rcum = atot - acum + a_all
    return dt_all, a_all, acum, atot, rcum, rcum[0:1, :]


def _ssd_chunk(xh, b_all, c_all, dtp, hf_in, hb_in, dtbias_ref, aneg_ref, dskip_ref, ef_ref,
               between_pairs=lambda: None):
    q = SSD_CHUNK
    dt_all, a_all, acum, atot, rcum, _ = _chunk_decay(dtp, dtbias_ref, aneg_ref, on_mxu=False)
    ii = lax.broadcasted_iota(jnp.int32, (q, q), 0)
    jj = lax.broadcasted_iota(jnp.int32, (q, q), 1)
    lower = ii >= jj
    diag = ii == jj
    dt_t = dt_all.T
    acum_t = acum.T
    rcum_t = rcum.T

    off_f, off_b, cb = [], [], []
    for g in range(SSD_GROUPS):
        gs = slice(g * GROUP_COLS, (g + 1) * GROUP_COLS)
        c_g = c_all[:, g * SSD_STATE:(g + 1) * SSD_STATE].astype(BF16)
        b_g = b_all[:, g * SSD_STATE:(g + 1) * SSD_STATE].astype(BF16)
        cb.append(lax.dot_general(c_g, b_g, (((1,), (1,)), ((), ())),
                                  preferred_element_type=F32))
        off_f.append(jnp.dot(c_g, hf_in[:, gs].astype(BF16), preferred_element_type=F32))
        off_b.append(jnp.dot(c_g, hb_in[:, gs].astype(BF16), preferred_element_type=F32))
    off_f = jnp.concatenate(off_f, axis=1)
    off_b = jnp.concatenate(off_b, axis=1)

    first_head = lax.broadcasted_iota(jnp.int32, (q, LANES), 1) < SSD_HEAD_DIM
    w_tok = dt_all * jnp.exp(atot - acum)
    y_cols, wx = [], []
    for pr in range(HEAD_PAIRS):
        g = pr // PAIRS_PER_GROUP
        m_parts, col_f, col_b, col_w = [], [], [], []
        for h in (2 * pr, 2 * pr + 1):
            hb_lane = SSD_HEADS + h
            colf = jnp.broadcast_to(acum[:, h:h + 1], (q, q))
            colb = jnp.broadcast_to(rcum[:, hb_lane:hb_lane + 1], (q, q))
            col_w.append(jnp.broadcast_to(w_tok[:, h:h + 1], (q, q)))
            seg = jnp.where(lower, colf - acum_t[h:h + 1, :],
                            colb - rcum_t[hb_lane:hb_lane + 1, :])
            dt_b_row = dt_t[hb_lane:hb_lane + 1, :]
            w = (jnp.exp(seg) * jnp.where(lower, dt_t[h:h + 1, :], dt_b_row)
                 + jnp.where(diag, dt_b_row, 0.0))
            m_parts.append(cb[g] * w)
            col_f.append(colf)
            col_b.append(colb)
        sl = slice(pr * LANES, (pr + 1) * LANES)
        xp = xh[:, sl]
        y_cols.append(_dot(jnp.concatenate(m_parts, axis=1), _block_diag_pair(xp))
                      + jnp.exp(jnp.where(first_head, col_f[0], col_f[1])) * off_f[:, sl]
                      + jnp.exp(jnp.where(first_head, col_b[0], col_b[1])) * off_b[:, sl]
                      + dskip_ref[:, sl] * xp)
        wx.append((jnp.where(first_head, col_w[0], col_w[1]) * xp).astype(BF16))
        between_pairs()
    y = jnp.concatenate(y_cols, axis=1)

    wx = jnp.concatenate(wx, axis=1)
    s = jnp.concatenate(
        [jnp.dot(b_all[:, g * SSD_STATE:(g + 1) * SSD_STATE].T.astype(BF16),
                 wx[:, g * GROUP_COLS:(g + 1) * GROUP_COLS], preferred_element_type=F32)
         for g in range(SSD_GROUPS)], axis=1)
    decay = jnp.exp(_expand(jnp.broadcast_to(atot, (BF16_ROWS, LANES)), ef_ref))[0:1, :]
    return y, decay * hf_in + s


def _tiling(x, tm, seq_len):
    s_n, l_n, _ = x.shape
    return s_n, l_n, l_n // tm, seq_len // tm, l_n // seq_len


FRONT_SEGS = (SSD_CONV_DIM, DT_PAD)
FRONT_NTILE = 256


def _front_kernel(*refs, has_pos, has_h0, tiles_per_seq, n_tiles):
    x_ref, xp_ref, xn_ref = refs[:3]
    refs = refs[3:]
    if has_pos:
        embr_ref, embc_ref = refs[:2]
        refs = refs[2:]
    if has_h0:
        h0b_ref = refs[0]
        refs = refs[1:]
    mod_ref, w_ref, wdt_ref, convw_ref, convb_ref, dtbias_ref, aneg_ref, eb_ref = refs[:8]
    xa_ref, dt_ref, hbs_ref, hfb_ref, hb_s, xb_s, dts_s = refs[8:]
    shift1 = mod_ref[0, :, 0:D_MODEL]
    scale1 = mod_ref[0, :, D_MODEL:2 * D_MODEL]
    tm = x_ref.shape[1]
    n_ext = tm + 2 * HALO
    q = SSD_CHUNK
    j = pl.program_id(1)
    tile = n_tiles - 1 - jnp.minimum(j, n_tiles - 1)
    t = tile % tiles_per_seq
    t_scan = jnp.minimum(n_tiles - j, n_tiles - 1) % tiles_per_seq
    slot = j % 2
    scan_slot = 1 - slot

    def init_state():
        if has_h0:
            hb_s[...] = h0b_ref[0].T
        else:
            hb_s[...] = jnp.zeros_like(hb_s)

    def modulate(x):
        return (x * (1.0 + scale1) + shift1).astype(BF16)

    def project():
        if has_pos:
            rows_per_tile = tm // GRID_W
            n_grid_rows = embr_ref.shape[0]
            r0 = tile * rows_per_tile
            x = _add_pos(x_ref, embr_ref, embc_ref, r0)
            x_prev = xp_ref[0] + _pos_halo(embr_ref, embc_ref, jnp.maximum(r0 - 1, 0),
                                           GRID_W - HALO)
            x_next = xn_ref[0] + _pos_halo(embr_ref, embc_ref,
                                           jnp.minimum(r0 + rows_per_tile, n_grid_rows - 1), 0)
        else:
            x, x_prev, x_next = x_ref[0], xp_ref[0], xn_ref[0]
        h = modulate(x)
        h_ext = jnp.concatenate([modulate(x_prev), h, modulate(x_next)], axis=0)

        row = lax.broadcasted_iota(jnp.int32, (tm, 1), 0)
        zero_prev = row == 0
        zero_next = row == tm - 1
        if tiles_per_seq > 1:
            zero_prev = jnp.logical_and(zero_prev, t == 0)
            zero_next = jnp.logical_and(zero_next, t == tiles_per_seq - 1)

        dt = jnp.dot(h, wdt_ref[...], preferred_element_type=F32)
        dt_ref[0] = dt
        dts_s[slot] = dt
        yield
        for c0 in range(0, SSD_CONV_DIM, FRONT_NTILE):
            c1 = c0 + FRONT_NTILE
            p = jnp.dot(h_ext, w_ref[:, c0:c1], preferred_element_type=F32)
            cur = p[HALO:HALO + tm]
            prv = jnp.where(zero_prev, 0.0, pltpu.roll(p, 1, 0)[HALO:HALO + tm])
            nxt = jnp.where(zero_next, 0.0, pltpu.roll(p, n_ext - 1, 0)[HALO:HALO + tm])
            xc = (convw_ref[0:1, c0:c1] * prv + convw_ref[1:2, c0:c1] * cur
                  + convw_ref[2:3, c0:c1] * nxt + convb_ref[:, c0:c1])
            xa = _silu(xc)
            xa_ref[0, :, c0:c1] = xa
            if c0 < D_SSD + BC_COLS:
                xb_s[slot, :, c0:c1] = xa
            yield

    def scan():
        h_state = hb_s[...]
        for ch in reversed(range(tm // q)):
            rows = slice(ch * q, (ch + 1) * q)
            hbs_ref[0, ch] = h_state
            dt_all, _, _, _, rcum, rtot = _chunk_decay(dts_s[scan_slot, rows, :], dtbias_ref,
                                                       aneg_ref)
            yield
            w_exp = _expand(dt_all * jnp.exp(rtot - rcum), eb_ref)
            decay = jnp.exp(_expand(jnp.broadcast_to(rtot, (BF16_ROWS, LANES)), eb_ref))[0:1, :]
            yield
            wx = (w_exp * xb_s[scan_slot, rows, 0:D_SSD]).astype(BF16)
            b_all = xb_s[scan_slot, rows, D_SSD:D_SSD + BC_COLS]
            parts = []
            for g in range(SSD_GROUPS):
                parts.append(jnp.dot(b_all[:, g * SSD_STATE:(g + 1) * SSD_STATE].T.astype(BF16),
                                     wx[:, g * GROUP_COLS:(g + 1) * GROUP_COLS],
                                     preferred_element_type=F32))
                yield
            h_state = decay * h_state + jnp.concatenate(parts, axis=1)
        hb_s[...] = h_state

        def final_state():
            hfb_ref[0] = h_state.T

        if tiles_per_seq > 1:
            pl.when(t_scan == 0)(final_state)
        else:
            final_state()

    def run(*stage_fns):
        if scan in stage_fns:
            if tiles_per_seq > 1:
                pl.when(t_scan == tiles_per_seq - 1)(init_state)
            else:
                init_state()
        stages = [fn() for fn in stage_fns]
        while stages:
            for gen in list(stages):
                if next(gen, StopIteration) is StopIteration:
                    stages.remove(gen)

    pl.when(j == 0)(functools.partial(run, project))
    pl.when(jnp.logical_and(j > 0, j < n_tiles))(functools.partial(run, project, scan))
    pl.when(j == n_tiles)(functools.partial(run, scan))


def _front(x, pos, mods, h0b, w_xbc, w_dt, convw, convb, dtbias, aneg, e_b, mod_row0, tm,
           seq_len):
    s_n, l_n, n_tiles, tiles_per_seq, seqs_per_row = _tiling(x, tm, seq_len)
    has_pos = pos is not None
    has_h0 = h0b is not None
    hb = tm // HALO
    n_hb = l_n // HALO
    cpt = tm // SSD_CHUNK
    fwd = lambda j: n_tiles - 1 - jnp.minimum(j, n_tiles - 1)
    scanned = lambda j: jnp.minimum(n_tiles - j, n_tiles - 1)
    tok = lambda s, j: (s, fwd(j), 0)
    seq_of = lambda s, j: (s * seqs_per_row + scanned(j) // tiles_per_seq, 0, 0)
    in_specs = [pl.BlockSpec((1, tm, D_MODEL), tok),
                pl.BlockSpec((1, HALO, D_MODEL),
                             lambda s, j: (s, jnp.maximum(fwd(j) * hb - 1, 0), 0)),
                pl.BlockSpec((1, HALO, D_MODEL),
                             lambda s, j: (s, jnp.minimum((fwd(j) + 1) * hb, n_hb - 1), 0))]
    args = [x, x, x]
    if has_pos:
        in_specs += [_const_spec(pos[0].shape), _const_spec(pos[1].shape)]
        args += list(pos)
    state_spec = pl.BlockSpec((1, D_SSD, SSD_STATE), seq_of)
    if has_h0:
        in_specs.append(state_spec)
        args.append(h0b)
    in_specs.append(pl.BlockSpec((1, 1, 6 * D_MODEL), lambda s, j: (mod_row0 + s, 0, 0)))
    args.append(mods)
    for a in (w_xbc, w_dt, convw, convb, dtbias, aneg, e_b):
        in_specs.append(_const_spec(a.shape))
        args.append(a)
    return pl.pallas_call(
        functools.partial(_front_kernel, has_pos=has_pos, has_h0=has_h0,
                          tiles_per_seq=tiles_per_seq, n_tiles=n_tiles),
        grid=(s_n, n_tiles + 1),
        in_specs=in_specs,
        out_specs=[pl.BlockSpec((1, tm, w), tok) for w in FRONT_SEGS]
        + [pl.BlockSpec((1, cpt, SSD_STATE, D_SSD), lambda s, j: (s, scanned(j), 0, 0)),
           state_spec],
        out_shape=[jax.ShapeDtypeStruct((s_n, l_n, w), F32) for w in FRONT_SEGS]
        + [jax.ShapeDtypeStruct((s_n, l_n // SSD_CHUNK, SSD_STATE, D_SSD), F32),
           jax.ShapeDtypeStruct((s_n * seqs_per_row, D_SSD, SSD_STATE), F32)],
        scratch_shapes=[pltpu.VMEM((SSD_STATE, D_SSD), F32),
                        pltpu.VMEM((2, tm, D_SSD + BC_COLS), F32),
                        pltpu.VMEM((2, tm, DT_PAD), F32)],
        compiler_params=_params(VMEM_MIB_FRONT, 2),
        name="front",
    )(*args)


MAIN_Z, MAIN_U, MAIN_V, MAIN_GA, MAIN_GB = (i * D_MODEL for i in range(5))
MAIN_COLS = 5 * D_MODEL


def _main_kernel(*refs, has_pos, has_h0, tiles_per_seq):
    x_ref, xa_ref, dtp_ref, hbs_ref = refs[:4]
    refs = refs[4:]
    if has_pos:
        embr_ref, embc_ref = refs[:2]
        refs = refs[2:]
    if has_h0:
        h0f_ref = refs[0]
        refs = refs[1:]
    (mod_ref, w_ref, dtbias_ref, aneg_ref, dskip_ref, normw_ref, ef_ref, lng_ref, lnb_ref,
     wsp_ref, bsp_ref, wpa_ref, wpb_ref, wout_ref, ln1g_ref, ln1b_ref) = refs[:16]
    o_ref, hff_ref, hf_s = refs[16:]
    tm = x_ref.shape[1]
    q = SSD_CHUNK
    tile = pl.program_id(1)
    t = tile % tiles_per_seq

    if has_pos:
        x = _add_pos(x_ref, embr_ref, embc_ref, tile * (tm // GRID_W))
    else:
        x = x_ref[0]
    shift1 = mod_ref[0, :, 0:D_MODEL]
    scale1 = mod_ref[0, :, D_MODEL:2 * D_MODEL]
    gate1 = mod_ref[0, :, 2 * D_MODEL:3 * D_MODEL]
    h = (x * (1.0 + scale1) + shift1).astype(BF16)

    col_tiles = {}
    pending = list(range(0, MAIN_COLS, IN_NTILE))
    n_slots = (tm // q) * HEAD_PAIRS
    n_fill = len(pending)
    slot = [0]

    def project_next():
        c = pending.pop(0)
        act = _silu if c < MAIN_U else (_gelu if c < MAIN_GA else _sigmoid)
        col_tiles[c] = act(jnp.dot(h, w_ref[:, c:c + IN_NTILE], preferred_element_type=F32))

    def between_pairs():
        slot[0] += 1
        done = n_fill - len(pending)
        for _ in range(-(-slot[0] * n_fill // n_slots) - done):
            project_next()

    def in_proj(c0):
        while pending and pending[0] < c0 + D_MODEL:
            project_next()
        return jnp.concatenate([col_tiles.pop(c) for c in range(c0, c0 + D_MODEL, IN_NTILE)],
                               axis=1)

    def init_state():
        if has_h0:
            hf_s[...] = h0f_ref[0].T
        else:
            hf_s[...] = jnp.zeros_like(hf_s)

    if tiles_per_seq > 1:
        pl.when(t == 0)(init_state)
    else:
        init_state()
    h_state = hf_s[...]
    ya = []
    for ch in range(tm // q):
        rows = slice(ch * q, (ch + 1) * q)
        y, h_state = _ssd_chunk(
            xa_ref[0, rows, 0:D_SSD], xa_ref[0, rows, D_SSD:D_SSD + BC_COLS],
            xa_ref[0, rows, D_SSD + BC_COLS:], dtp_ref[0, rows, :], h_state, hbs_ref[0, ch],
            dtbias_ref, aneg_ref, dskip_ref, ef_ref, between_pairs)
        if ch == 0:
            zz = in_proj(MAIN_Z)
        yg = y * zz[rows]
        ms = jnp.mean(yg * yg, axis=-1, keepdims=True)
        ya.append(yg * lax.rsqrt(ms + EPS) * normw_ref[...])
    hf_s[...] = h_state

    def final_state():
        hff_ref[0] = h_state.T

    if tiles_per_seq > 1:
        pl.when(t == tiles_per_seq - 1)(final_state)
    else:
        final_state()
    ya = jnp.concatenate(ya, axis=0)
    proj_a = _dot(ya, wpa_ref[...])

    u = in_proj(MAIN_U)
    v = _layer_norm(in_proj(MAIN_V), lng_ref[...], lnb_ref[...])
    n_ch = tm // SGU_CHUNK
    vb = v.astype(BF16)
    per_group = []
    for g in range(SGU_GROUPS):
        gs = slice(g * SGU_GROUP_DIM, (g + 1) * SGU_GROUP_DIM)
        rhs = jnp.concatenate([vb[ch * SGU_CHUNK:(ch + 1) * SGU_CHUNK, gs] for ch in range(n_ch)],
                              axis=1)
        per_group.append(jnp.dot(wsp_ref[g], rhs, preferred_element_type=F32))
    rows_out = []
    for ch in range(n_ch):
        cs = slice(ch * SGU_GROUP_DIM, (ch + 1) * SGU_GROUP_DIM)
        rows_out.append(jnp.concatenate([p[:, cs] for p in per_group], axis=1) + bsp_ref[...])
    yb = u * jnp.concatenate(rows_out, axis=0)

    merged = in_proj(MAIN_GA) * proj_a + in_proj(MAIN_GB) * _dot(yb, wpb_ref[...])
    o = _dot(merged, wout_ref[...])
    o_ref[0] = _layer_norm(DN_ALPHA * x + gate1 * o, ln1g_ref[...], ln1b_ref[...])


def _main(x, pos, xa, dtp, hbs, h0f, mods, mod_row0, tm, seq_len, consts):
    s_n, l_n, n_tiles, tiles_per_seq, seqs_per_row = _tiling(x, tm, seq_len)
    has_pos = pos is not None
    has_h0 = h0f is not None
    cpt = tm // SSD_CHUNK
    tok = lambda s, j: (s, j, 0)
    in_specs = [pl.BlockSpec((1, tm, D_MODEL), tok),
                pl.BlockSpec((1, tm, SSD_CONV_DIM), tok),
                pl.BlockSpec((1, tm, DT_PAD), tok),
                pl.BlockSpec((1, cpt, SSD_STATE, D_SSD), lambda s, j: (s, j, 0, 0))]
    args = [x, xa, dtp, hbs]
    if has_pos:
        in_specs += [_const_spec(pos[0].shape), _const_spec(pos[1].shape)]
        args += list(pos)
    state_spec = pl.BlockSpec((1, D_SSD, SSD_STATE),
                              lambda s, j: (s * seqs_per_row + j // tiles_per_seq, 0, 0))
    if has_h0:
        in_specs.append(state_spec)
        args.append(h0f)
    in_specs.append(pl.BlockSpec((1, 1, 6 * D_MODEL), lambda s, j: (mod_row0 + s, 0, 0)))
    args.append(mods)
    for a in consts:
        in_specs.append(_const_spec(a.shape))
        args.append(a)
    return pl.pallas_call(
        functools.partial(_main_kernel, has_pos=has_pos, has_h0=has_h0,
                          tiles_per_seq=tiles_per_seq),
        grid=(s_n, n_tiles),
        in_specs=in_specs,
        out_specs=[pl.BlockSpec((1, tm, D_MODEL), tok), state_spec],
        out_shape=[jax.ShapeDtypeStruct((s_n, l_n, D_MODEL), F32),
                   jax.ShapeDtypeStruct((s_n * seqs_per_row, D_SSD, SSD_STATE), F32)],
        scratch_shapes=[pltpu.VMEM((SSD_STATE, D_SSD), F32)],
        compiler_params=_params(VMEM_MIB_MAIN, 2),
        name="main",
    )(*args)


FF_TILE = 1024
FFN_TM = 1024


def _ffn_kernel(x_ref, mod_ref, w1_ref, w2_ref, g_ref, b_ref, o_ref):
    x = x_ref[0]
    shift2 = mod_ref[0, :, 3 * D_MODEL:4 * D_MODEL]
    scale2 = mod_ref[0, :, 4 * D_MODEL:5 * D_MODEL]
    gate2 = mod_ref[0, :, 5 * D_MODEL:6 * D_MODEL]
    h = (x * (1.0 + scale2) + shift2).astype(BF16)
    f = jnp.zeros_like(x)
    for k0 in range(0, D_FF, FF_TILE):
        a = jnp.dot(h, w1_ref[:, k0:k0 + FF_TILE], preferred_element_type=F32)
        a = jnp.maximum(a, 0.0)
        f = f + _dot(a * a, w2_ref[k0:k0 + FF_TILE, :])
    o_ref[0] = _layer_norm(DN_ALPHA * x + gate2 * f, g_ref[...], b_ref[...])


def _ffn(x, mods, mod_row0, tm, w1, w2, g, b):
    s_n, l_n, _ = x.shape
    tok = lambda s, j: (s, j, 0)
    return pl.pallas_call(
        _ffn_kernel,
        grid=(s_n, l_n // tm),
        in_specs=[pl.BlockSpec((1, tm, D_MODEL), tok),
                  pl.BlockSpec((1, 1, 6 * D_MODEL), lambda s, j: (mod_row0 + s, 0, 0)),
                  _const_spec(w1.shape), _const_spec(w2.shape),
                  _const_spec(g.shape), _const_spec(b.shape)],
        out_specs=pl.BlockSpec((1, tm, D_MODEL), tok),
        out_shape=jax.ShapeDtypeStruct((s_n, l_n, D_MODEL), F32),
        compiler_params=_params(VMEM_MIB_FFN, 2),
        name="ffn",
    )(x, mods, w1, w2, g, b)


def _grid_pos_tables(n_tokens):
    rows = n_tokens // GRID_W
    quarter = D_MODEL // 4
    omega = 1.0 / (POS_BASE ** (jnp.arange(quarter, dtype=F32) / quarter))
    r = jnp.arange(rows, dtype=F32)[:, None] * omega
    col = jnp.arange(GRID_W, dtype=F32)[:, None] * omega
    emb_r = jnp.concatenate([jnp.sin(r), jnp.cos(r)], axis=-1)
    emb_c = jnp.concatenate([jnp.sin(col), jnp.cos(col)], axis=-1)
    return jnp.broadcast_to(emb_r[:, None, :], (rows, SUBLANES, D_MODEL // 2)), emb_c


def _dt_lanes(v, axis):
    pad = [(0, 0)] * v.ndim
    pad[axis] = (0, DT_PAD - DT_COPIES * DT_GROUP)
    return jnp.pad(jnp.concatenate([v] * DT_COPIES, axis=axis), pad)


def _head_expanders():
    lane = jnp.arange(LANES)[:, None]
    head = jnp.arange(D_SSD)[None, :] // SSD_HEAD_DIM
    valid = lane < DT_COPIES * DT_GROUP
    e_f = jnp.logical_and(valid, lane % DT_GROUP == head)
    e_b = jnp.logical_and(valid, lane % DT_GROUP == head + SSD_HEADS)
    return e_f.astype(BF16), e_b.astype(BF16)


def _stream(x, pos, mods, mod_row0, seq_len, h0f, h0b, prm, tm):
    xa, dtp, hbs, hb = _front(x, pos, mods, h0b, prm["w_xbc"], prm["w_dt"], prm["conv_w"],
                              prm["conv_b"],
                              prm["dt_bias"], prm["a_neg"], prm["e_b"], mod_row0, tm, seq_len)
    x1, hf = _main(x, pos, xa, dtp, hbs, h0f, mods, mod_row0, tm, seq_len, prm["main"])
    y = _ffn(x1, mods, mod_row0, FFN_TM, prm["w_ff1"], prm["w_ff2"], prm["ln2_g"], prm["ln2_b"])
    return y, hf, hb


def kernel(x_prompt, x_sample, state_ssd_fwd, state_ssd_bwd, c, c_ctx, w_ada, b_ada, w_in, conv_w, conv_b, dt_bias_fwd, dt_bias_bwd, a_log_fwd, a_log_bwd, d_skip, ssd_norm_w, sgu_ln_g, sgu_ln_b, w_spatial, b_spatial, w_proj_a, w_proj_b, w_out, ln1_g, ln1_b, w_ff1, w_ff2, ln2_g, ln2_b):
    batch, seq, _ = x_prompt.shape
    dec_batch, dec_seq, _ = x_sample.shape
    l = 0
    row = lambda v: v.reshape(1, -1)

    cond = jnp.zeros((SUBLANES, D_MODEL), F32).at[0].set(c_ctx).at[1:1 + dec_batch].set(c)
    mods = _ada(cond, w_ada[l], b_ada[l]).reshape(SUBLANES, 1, 6 * D_MODEL)

    o1 = D_SSD
    o2 = o1 + SSD_CONV_DIM
    o3 = o2 + DT_GROUP
    w_t = jnp.swapaxes(w_in[l], 0, 1)
    step = W_COL_TILE // DT_GROUP
    w_xbc = _w_cols(w_t, lambda i: o1 // DT_GROUP + i * step, SSD_CONV_DIM // W_COL_TILE)
    z_tiles = o1 // W_COL_TILE
    w_main = _w_cols(
        w_t, lambda i: jnp.where(i < z_tiles, i * step, o3 // DT_GROUP + (i - z_tiles) * step),
        MAIN_COLS // W_COL_TILE)
    assert o2 % LANES == 0
    w_dt = _w_dt(w_t, o2)
    e_f, e_b = _head_expanders()
    dt_bias = row(_dt_lanes(jnp.concatenate([dt_bias_fwd[l], dt_bias_bwd[l]]), 0))
    a_neg = row(_dt_lanes(-jnp.exp(jnp.concatenate([a_log_fwd[l], a_log_bwd[l]])), 0))

    prm = dict(
        w_xbc=w_xbc, w_dt=w_dt, conv_w=conv_w[l], conv_b=row(conv_b[l]),
        dt_bias=dt_bias, a_neg=a_neg, e_b=e_b,
        main=(w_main, dt_bias, a_neg, row(jnp.repeat(d_skip[l], SSD_HEAD_DIM)),
              row(ssd_norm_w[l]), e_f, row(sgu_ln_g[l]), row(sgu_ln_b[l]),
              w_spatial[l].astype(BF16), jnp.repeat(b_spatial[l].T, SGU_GROUP_DIM, axis=1),
              w_proj_a[l].astype(BF16), w_proj_b[l].astype(BF16), w_out[l].astype(BF16),
              row(ln1_g[l]), row(ln1_b[l])),
        w_ff1=w_ff1[l].astype(BF16), w_ff2=w_ff2[l].astype(BF16),
        ln2_g=row(ln2_g[l]), ln2_b=row(ln2_b[l]),
    )

    tm = 256
    xp, hf, hb = _stream(x_prompt.reshape(1, batch * seq, D_MODEL), None, mods, 0, seq,
                         None, None, prm, tm)
    pos = _grid_pos_tables(dec_seq)
    h0f = state_ssd_fwd[:, l].reshape(dec_batch, D_SSD, SSD_STATE)
    h0b = state_ssd_bwd[:, l].reshape(dec_batch, D_SSD, SSD_STATE)
    xs, _, _ = _stream(x_sample, pos, mods, 1, dec_seq, h0f, h0b, prm, tm)

    state_shape = (batch, DEPTH, SSD_HEADS, SSD_HEAD_DIM, SSD_STATE)
    return (xp.reshape(batch, seq, D_MODEL), xs,
            hf.reshape(state_shape), hb.reshape(state_shape))
```

```python
import functools
import math

import jax
import jax.numpy as jnp
from jax import lax
from jax.experimental import pallas as pl
from jax.experimental.pallas import tpu as pltpu

D_MODEL = 1024
GRID_W = 64
POS_BASE = 10000.0
D_SSD = 1024
SSD_HEAD_DIM = 64
SSD_HEADS = D_SSD // SSD_HEAD_DIM
SSD_GROUPS = 2
SSD_STATE = 128
SSD_CHUNK = 128
SSD_CONV_DIM = D_SSD + 2 * SSD_GROUPS * SSD_STATE
D_SGU = 1024
SGU_GROUPS = 8
SGU_GROUP_DIM = D_SGU // SGU_GROUPS
SGU_CHUNK = 128
D_FF = 4 * D_MODEL
DEPTH = 1
DN_ALPHA = (2.0 * DEPTH) ** 0.25
EPS = 1e-5

LANES = 128
SUBLANES = 8
BF16_ROWS = 16
V7X_VMEM_MIB = 64
VMEM_MIB_SMALL = V7X_VMEM_MIB // 2
VMEM_MIB_FRONT = 48
VMEM_MIB_MAIN = 56
VMEM_MIB_FFN = 48

DT_GROUP = 2 * SSD_HEADS
DT_COPIES = 3
DT_PAD = LANES
HEAD_PAIRS = SSD_HEADS // 2
PAIRS_PER_GROUP = HEAD_PAIRS // SSD_GROUPS
GROUP_COLS = D_SSD // SSD_GROUPS
BC_COLS = SSD_GROUPS * SSD_STATE

BF16 = jnp.bfloat16
F32 = jnp.float32
HIGHEST = lax.Precision.HIGHEST


def _dot(a, b):
    return jnp.dot(a.astype(BF16), b.astype(BF16), preferred_element_type=F32)


def _dot_exact(a, b):
    return jnp.dot(a, b, precision=HIGHEST, preferred_element_type=F32)


def _layer_norm(x, g, b):
    mu = jnp.mean(x, axis=-1, keepdims=True)
    xc = x - mu
    var = jnp.mean(xc * xc, axis=-1, keepdims=True)
    return xc * lax.rsqrt(var + EPS) * g + b


def _softplus(x):
    return jnp.maximum(x, 0.0) + jnp.log1p(jnp.exp(-jnp.abs(x)))


def _gelu(x):
    return 0.5 * x * (1.0 + lax.erf(x * (1.0 / math.sqrt(2.0))))


def _sigmoid(x):
    return 0.5 * jnp.tanh(0.5 * x) + 0.5


def _silu(x):
    half = 0.5 * x
    return half * jnp.tanh(half) + half


def _const_spec(shape):
    zeros = (0,) * len(shape)
    return pl.BlockSpec(shape, lambda *_: zeros, pipeline_mode=pl.Buffered(1))


def _params(vmem_mib, n_axes):
    return pltpu.CompilerParams(
        dimension_semantics=("arbitrary",) * n_axes,
        vmem_limit_bytes=vmem_mib * 1024 * 1024)


def _ada_kernel(c_ref, w_ref, b_ref, o_ref):
    c = _silu(c_ref[...])
    c_hi = c.astype(BF16)
    c_lo = (c - c_hi.astype(F32)).astype(BF16)
    w = w_ref[...]
    w_hi = w.astype(BF16)
    w_lo = (w - w_hi.astype(F32)).astype(BF16)
    rows = c.shape[0]
    by_hi = jnp.dot(jnp.concatenate([c_hi, c_lo], axis=0), w_hi, preferred_element_type=F32)
    by_lo = jnp.dot(c_hi, w_lo, preferred_element_type=F32)
    o_ref[...] = by_hi[:rows] + by_hi[rows:] + by_lo + b_ref[...]


def _ada(cond, w_ada, b_ada):
    n = w_ada.shape[1]
    tn = 2048
    return pl.pallas_call(
        _ada_kernel,
        grid=(n // tn,),
        in_specs=[pl.BlockSpec((SUBLANES, D_MODEL), lambda j: (0, 0)),
                  pl.BlockSpec((D_MODEL, tn), lambda j: (0, j)),
                  pl.BlockSpec((1, tn), lambda j: (0, j))],
        out_specs=pl.BlockSpec((SUBLANES, tn), lambda j: (0, j)),
        out_shape=jax.ShapeDtypeStruct((SUBLANES, n), F32),
        compiler_params=_params(VMEM_MIB_SMALL, 1),
        name="ada",
    )(cond, w_ada, b_ada.reshape(1, n))


W_COL_TILE = 512


def _w_cols_kernel(src_ref, o_ref):
    o_ref[...] = src_ref[...].T.astype(BF16)


def _w_dt_kernel(src_ref, o_ref):
    t = src_ref[...].T
    lane = lax.broadcasted_iota(jnp.int32, t.shape, 1)
    dt = jnp.where(lane < DT_GROUP, t, 0.0)
    o_ref[...] = (dt + pltpu.roll(dt, DT_GROUP, 1) + pltpu.roll(dt, 2 * DT_GROUP, 1)).astype(BF16)


def _w_dt(w_t, col0):
    d = w_t.shape[1]
    return pl.pallas_call(
        _w_dt_kernel,
        grid=(1,),
        in_specs=[pl.BlockSpec((LANES, d), lambda i: (col0 // LANES, 0))],
        out_specs=pl.BlockSpec((d, LANES), lambda i: (0, 0)),
        out_shape=jax.ShapeDtypeStruct((d, LANES), BF16),
        name="w_dt",
    )(w_t)


def _w_cols(w_t, col_start, n_tiles):
    d = w_t.shape[1]
    return pl.pallas_call(
        _w_cols_kernel,
        grid=(n_tiles,),
        in_specs=[pl.BlockSpec((pl.Element(W_COL_TILE), pl.Element(d)),
                               lambda i: (col_start(i) * DT_GROUP, 0))],
        out_specs=pl.BlockSpec((d, W_COL_TILE), lambda i: (0, i)),
        out_shape=jax.ShapeDtypeStruct((d, n_tiles * W_COL_TILE), BF16),
        compiler_params=_params(VMEM_MIB_SMALL, 1),
        name="w_cols",
    )(w_t)


IN_NTILE = 512
HALO = BF16_ROWS


def _add_pos(x_ref, embr_ref, embc_ref, grid_row0):
    n_rows = x_ref.shape[1] // GRID_W
    col = embc_ref[...]
    blocks = []
    for i in range(n_rows):
        row = embr_ref[grid_row0 + i]
        row = jnp.concatenate([row] * (GRID_W // SUBLANES), axis=0)
        blocks.append(x_ref[0, i * GRID_W:(i + 1) * GRID_W, :]
                      + jnp.concatenate([row, col], axis=1))
    return jnp.concatenate(blocks, axis=0)


def _pos_halo(embr_ref, embc_ref, grid_row, col0):
    left = jnp.concatenate([embr_ref[grid_row]] * (HALO // SUBLANES), axis=0)
    return jnp.concatenate([left, embc_ref[col0:col0 + HALO, :]], axis=1)


def _block_diag_pair(xp):
    lane = lax.broadcasted_iota(jnp.int32, xp.shape, 1)
    first = lane < SSD_HEAD_DIM
    return jnp.concatenate([jnp.where(first, xp, 0.0), jnp.where(first, 0.0, xp)], axis=0)


def _split_terms(v):
    lane = lax.broadcasted_iota(jnp.int32, v.shape, 1)
    hi = v.astype(BF16).astype(F32)
    r1 = v - hi
    mid = r1.astype(BF16).astype(F32)
    lo = r1 - mid
    return jnp.where(lane < DT_GROUP, hi, jnp.where(lane < 2 * DT_GROUP, mid, lo)).astype(BF16)


def _expand(v, e_ref):
    return jnp.dot(_split_terms(v), e_ref[...], preferred_element_type=F32)


def _chunk_decay(dtp, dtbias_ref, aneg_ref, on_mxu=True):
    q = dtp.shape[0]
    dt_all = _softplus(dtp + dtbias_ref[...])
    a_all = dt_all * aneg_ref[...]
    if on_mxu:
        ii = lax.broadcasted_iota(jnp.int32, (q, q), 0)
        jj = lax.broadcasted_iota(jnp.int32, (q, q), 1)
        acum = _dot_exact((ii >= jj).astype(F32), a_all)
    else:
        row = lax.broadcasted_iota(jnp.int32, a_all.shape, 0)
        acum = a_all
        shift = 1
        while shift < q:
            acum = acum + jnp.where(row >= shift, pltpu.roll(acum, shift, 0), 0.0)
            shift *= 2
    atot = acum[q - 1:q, :]
    rcum = atot - acum + a_all
    return dt_all, a_all, acum, atot, rcum, rcum[0:1, :]


def _ssd_chunk(xh, b_all, c_all, dtp, hf_in, hb_in, dtbias_ref, aneg_ref, dskip_ref, ef_ref,
               between_pairs=lambda: None):
    q = SSD_CHUNK
    dt_all, a_all, acum, atot, rcum, _ = _chunk_decay(dtp, dtbias_ref, aneg_ref, on_mxu=False)
    ii = lax.broadcasted_iota(jnp.int32, (q, q), 0)
    jj = lax.broadcasted_iota(jnp.int32, (q, q), 1)
    lower = ii >= jj
    diag = ii == jj
    dt_t = dt_all.T
    acum_t = acum.T
    rcum_t = rcum.T

    off_f, off_b, cb = [], [], []
    for g in range(SSD_GROUPS):
        gs = slice(g * GROUP_COLS, (g + 1) * GROUP_COLS)
        c_g = c_all[:, g * SSD_STATE:(g + 1) * SSD_STATE].astype(BF16)
        b_g = b_all[:, g * SSD_STATE:(g + 1) * SSD_STATE].astype(BF16)
        cb.append(lax.dot_general(c_g, b_g, (((1,), (1,)), ((), ())),
                                  preferred_element_type=F32))
        off_f.append(jnp.dot(c_g, hf_in[:, gs].astype(BF16), preferred_element_type=F32))
        off_b.append(jnp.dot(c_g, hb_in[:, gs].astype(BF16), preferred_element_type=F32))
    off_f = jnp.concatenate(off_f, axis=1)
    off_b = jnp.concatenate(off_b, axis=1)

    first_head = lax.broadcasted_iota(jnp.int32, (q, LANES), 1) < SSD_HEAD_DIM
    w_tok = dt_all * jnp.exp(atot - acum)
    y_cols, wx = [], []
    for pr in range(HEAD_PAIRS):
        g = pr // PAIRS_PER_GROUP
        m_parts, col_f, col_b, col_w = [], [], [], []
        for h in (2 * pr, 2 * pr + 1):
            hb_lane = SSD_HEADS + h
            colf = jnp.broadcast_to(acum[:, h:h + 1], (q, q))
            colb = jnp.broadcast_to(rcum[:, hb_lane:hb_lane + 1], (q, q))
            col_w.append(jnp.broadcast_to(w_tok[:, h:h + 1], (q, q)))
            seg = jnp.where(lower, colf - acum_t[h:h + 1, :],
                            colb - rcum_t[hb_lane:hb_lane + 1, :])
            dt_b_row = dt_t[hb_lane:hb_lane + 1, :]
            w = (jnp.exp(seg) * jnp.where(lower, dt_t[h:h + 1, :], dt_b_row)
                 + jnp.where(diag, dt_b_row, 0.0))
            m_parts.append(cb[g] * w)
            col_f.append(colf)
            col_b.append(colb)
        sl = slice(pr * LANES, (pr + 1) * LANES)
        xp = xh[:, sl]
        y_cols.append(_dot(jnp.concatenate(m_parts, axis=1), _block_diag_pair(xp))
                      + jnp.exp(jnp.where(first_head, col_f[0], col_f[1])) * off_f[:, sl]
                      + jnp.exp(jnp.where(first_head, col_b[0], col_b[1])) * off_b[:, sl]
                      + dskip_ref[:, sl] * xp)
        wx.append((jnp.where(first_head, col_w[0], col_w[1]) * xp).astype(BF16))
        between_pairs()
    y = jnp.concatenate(y_cols, axis=1)

    wx = jnp.concatenate(wx, axis=1)
    s = jnp.concatenate(
        [jnp.dot(b_all[:, g * SSD_STATE:(g + 1) * SSD_STATE].T.astype(BF16),
                 wx[:, g * GROUP_COLS:(g + 1) * GROUP_COLS], preferred_element_type=F32)
         for g in range(SSD_GROUPS)], axis=1)
    decay = jnp.exp(_expand(jnp.broadcast_to(atot, (BF16_ROWS, LANES)), ef_ref))[0:1, :]
    return y, decay * hf_in + s


def _tiling(x, tm, seq_len):
    s_n, l_n, _ = x.shape
    return s_n, l_n, l_n // tm, seq_len // tm, l_n // seq_len


FRONT_SEGS = (SSD_CONV_DIM, DT_PAD)
FRONT_NTILE = 256


def _front_kernel(*refs, has_pos, has_h0, tiles_per_seq, n_tiles):
    x_ref, xp_ref, xn_ref = refs[:3]
    refs = refs[3:]
    if has_pos:
        embr_ref, embc_ref = refs[:2]
        refs = refs[2:]
    if has_h0:
        h0b_ref = refs[0]
        refs = refs[1:]
    mod_ref, w_ref, wdt_ref, convw_ref, convb_ref, dtbias_ref, aneg_ref, eb_ref = refs[:8]
    xa_ref, dt_ref, hbs_ref, hfb_ref, hb_s, xb_s, dts_s = refs[8:]
    shift1 = mod_ref[0, :, 0:D_MODEL]
    scale1 = mod_ref[0, :, D_MODEL:2 * D_MODEL]
    tm = x_ref.shape[1]
    n_ext = tm + 2 * HALO
    q = SSD_CHUNK
    j = pl.program_id(1)
    tile = n_tiles - 1 - jnp.minimum(j, n_tiles - 1)
    t = tile % tiles_per_seq
    t_scan = jnp.minimum(n_tiles - j, n_tiles - 1) % tiles_per_seq
    slot = j % 2
    scan_slot = 1 - slot

    def init_state():
        if has_h0:
            hb_s[...] = h0b_ref[0].T
        else:
            hb_s[...] = jnp.zeros_like(hb_s)

    def modulate(x):
        return (x * (1.0 + scale1) + shift1).astype(BF16)

    def project():
        if has_pos:
            rows_per_tile = tm // GRID_W
            n_grid_rows = embr_ref.shape[0]
            r0 = tile * rows_per_tile
            x = _add_pos(x_ref, embr_ref, embc_ref, r0)
            x_prev = xp_ref[0] + _pos_halo(embr_ref, embc_ref, jnp.maximum(r0 - 1, 0),
                                           GRID_W - HALO)
            x_next = xn_ref[0] + _pos_halo(embr_ref, embc_ref,
                                           jnp.minimum(r0 + rows_per_tile, n_grid_rows - 1), 0)
        else:
            x, x_prev, x_next = x_ref[0], xp_ref[0], xn_ref[0]
        h = modulate(x)
        h_prev = jnp.zeros((HALO, D_MODEL), BF16)
        h_next = h_prev
        if tiles_per_seq > 1:
            h_prev = jnp.where(t == 0, h_prev, modulate(x_prev))
            h_next = jnp.where(t == tiles_per_seq - 1, h_next, modulate(x_next))
        h_ext = jnp.concatenate([h_prev, h, h_next], axis=0)

        dt = jnp.dot(h, wdt_ref[...], preferred_element_type=F32)
        dt_ref[0] = dt
        dts_s[slot] = dt
        yield
        for c0 in range(0, SSD_CONV_DIM, FRONT_NTILE):
            c1 = c0 + FRONT_NTILE
            p = jnp.dot(h_ext, w_ref[:, c0:c1], preferred_element_type=F32)
            cur = p[HALO:HALO + tm]
            prv = pltpu.roll(p, 1, 0)[HALO:HALO + tm]
            nxt = pltpu.roll(p, n_ext - 1, 0)[HALO:HALO + tm]
            xc = (convw_ref[0:1, c0:c1] * prv + convw_ref[1:2, c0:c1] * cur
                  + convw_ref[2:3, c0:c1] * nxt + convb_ref[:, c0:c1])
            xa = _silu(xc)
            xa_ref[0, :, c0:c1] = xa
            if c0 < D_SSD + BC_COLS:
                xb_s[slot, :, c0:c1] = xa
            yield

    def scan():
        h_state = hb_s[...]
        for ch in reversed(range(tm // q)):
            rows = slice(ch * q, (ch + 1) * q)
            hbs_ref[0, ch] = h_state
            dt_all, _, _, _, rcum, rtot = _chunk_decay(dts_s[scan_slot, rows, :], dtbias_ref,
                                                       aneg_ref)
            yield
            w_exp = _expand(dt_all * jnp.exp(rtot - rcum), eb_ref)
            decay = jnp.exp(_expand(jnp.broadcast_to(rtot, (BF16_ROWS, LANES)), eb_ref))[0:1, :]
            yield
            wx = (w_exp * xb_s[scan_slot, rows, 0:D_SSD]).astype(BF16)
            b_all = xb_s[scan_slot, rows, D_SSD:D_SSD + BC_COLS]
            parts = []
            for g in range(SSD_GROUPS):
                parts.append(jnp.dot(b_all[:, g * SSD_STATE:(g + 1) * SSD_STATE].T.astype(BF16),
                                     wx[:, g * GROUP_COLS:(g + 1) * GROUP_COLS],
                                     preferred_element_type=F32))
                yield
            h_state = decay * h_state + jnp.concatenate(parts, axis=1)
        hb_s[...] = h_state

        def final_state():
            hfb_ref[0] = h_state.T

        if tiles_per_seq > 1:
            pl.when(t_scan == 0)(final_state)
        else:
            final_state()

    def run(*stage_fns):
        if scan in stage_fns:
            if tiles_per_seq > 1:
                pl.when(t_scan == tiles_per_seq - 1)(init_state)
            else:
                init_state()
        stages = [fn() for fn in stage_fns]
        while stages:
            for gen in list(stages):
                if next(gen, StopIteration) is StopIteration:
                    stages.remove(gen)

    pl.when(j == 0)(functools.partial(run, project))
    pl.when(jnp.logical_and(j > 0, j < n_tiles))(functools.partial(run, project, scan))
    pl.when(j == n_tiles)(functools.partial(run, scan))


def _front(x, pos, mods, h0b, w_xbc, w_dt, convw, convb, dtbias, aneg, e_b, mod_row0, tm,
           seq_len):
    s_n, l_n, n_tiles, tiles_per_seq, seqs_per_row = _tiling(x, tm, seq_len)
    has_pos = pos is not None
    has_h0 = h0b is not None
    hb = tm // HALO
    n_hb = l_n // HALO
    cpt = tm // SSD_CHUNK
    fwd = lambda j: n_tiles - 1 - jnp.minimum(j, n_tiles - 1)
    scanned = lambda j: jnp.minimum(n_tiles - j, n_tiles - 1)
    tok = lambda s, j: (s, fwd(j), 0)
    seq_of = lambda s, j: (s * seqs_per_row + scanned(j) // tiles_per_seq, 0, 0)
    in_specs = [pl.BlockSpec((1, tm, D_MODEL), tok),
                pl.BlockSpec((1, HALO, D_MODEL),
                             lambda s, j: (s, jnp.maximum(fwd(j) * hb - 1, 0), 0)),
                pl.BlockSpec((1, HALO, D_MODEL),
                             lambda s, j: (s, jnp.minimum((fwd(j) + 1) * hb, n_hb - 1), 0))]
    args = [x, x, x]
    if has_pos:
        in_specs += [_const_spec(pos[0].shape), _const_spec(pos[1].shape)]
        args += list(pos)
    state_spec = pl.BlockSpec((1, D_SSD, SSD_STATE), seq_of)
    if has_h0:
        in_specs.append(state_spec)
        args.append(h0b)
    in_specs.append(pl.BlockSpec((1, 1, 6 * D_MODEL), lambda s, j: (mod_row0 + s, 0, 0)))
    args.append(mods)
    for a in (w_xbc, w_dt, convw, convb, dtbias, aneg, e_b):
        in_specs.append(_const_spec(a.shape))
        args.append(a)
    return pl.pallas_call(
        functools.partial(_front_kernel, has_pos=has_pos, has_h0=has_h0,
                          tiles_per_seq=tiles_per_seq, n_tiles=n_tiles),
        grid=(s_n, n_tiles + 1),
        in_specs=in_specs,
        out_specs=[pl.BlockSpec((1, tm, w), tok) for w in FRONT_SEGS]
        + [pl.BlockSpec((1, cpt, SSD_STATE, D_SSD), lambda s, j: (s, scanned(j), 0, 0)),
           state_spec],
        out_shape=[jax.ShapeDtypeStruct((s_n, l_n, w), F32) for w in FRONT_SEGS]
        + [jax.ShapeDtypeStruct((s_n, l_n // SSD_CHUNK, SSD_STATE, D_SSD), F32),
           jax.ShapeDtypeStruct((s_n * seqs_per_row, D_SSD, SSD_STATE), F32)],
        scratch_shapes=[pltpu.VMEM((SSD_STATE, D_SSD), F32),
                        pltpu.VMEM((2, tm, D_SSD + BC_COLS), F32),
                        pltpu.VMEM((2, tm, DT_PAD), F32)],
        compiler_params=_params(VMEM_MIB_FRONT, 2),
        name="front",
    )(*args)


MAIN_Z, MAIN_U, MAIN_V, MAIN_GA, MAIN_GB = (i * D_MODEL for i in range(5))
MAIN_COLS = 5 * D_MODEL


def _main_kernel(*refs, has_pos, has_h0, tiles_per_seq):
    x_ref, xa_ref, dtp_ref, hbs_ref = refs[:4]
    refs = refs[4:]
    if has_pos:
        embr_ref, embc_ref = refs[:2]
        refs = refs[2:]
    if has_h0:
        h0f_ref = refs[0]
        refs = refs[1:]
    (mod_ref, w_ref, dtbias_ref, aneg_ref, dskip_ref, normw_ref, ef_ref, lng_ref, lnb_ref,
     wsp_ref, bsp_ref, wpa_ref, wpb_ref, wout_ref, ln1g_ref, ln1b_ref) = refs[:16]
    o_ref, hff_ref, hf_s = refs[16:]
    tm = x_ref.shape[1]
    q = SSD_CHUNK
    tile = pl.program_id(1)
    t = tile % tiles_per_seq

    if has_pos:
        x = _add_pos(x_ref, embr_ref, embc_ref, tile * (tm // GRID_W))
    else:
        x = x_ref[0]
    shift1 = mod_ref[0, :, 0:D_MODEL]
    scale1 = mod_ref[0, :, D_MODEL:2 * D_MODEL]
    gate1 = mod_ref[0, :, 2 * D_MODEL:3 * D_MODEL]
    h = (x * (1.0 + scale1) + shift1).astype(BF16)

    col_tiles = {}
    pending = list(range(0, MAIN_COLS, IN_NTILE))
    n_slots = (tm // q) * HEAD_PAIRS
    n_fill = len(pending)
    slot = [0]

    def project_next():
        c = pending.pop(0)
        act = _silu if c < MAIN_U else (_gelu if c < MAIN_GA else _sigmoid)
        col_tiles[c] = act(jnp.dot(h, w_ref[:, c:c + IN_NTILE], preferred_element_type=F32))

    def between_pairs():
        slot[0] += 1
        done = n_fill - len(pending)
        for _ in range(-(-slot[0] * n_fill // n_slots) - done):
            project_next()

    def in_proj(c0):
        while pending and pending[0] < c0 + D_MODEL:
            project_next()
        return jnp.concatenate([col_tiles.pop(c) for c in range(c0, c0 + D_MODEL, IN_NTILE)],
                               axis=1)

    def init_state():
        if has_h0:
            hf_s[...] = h0f_ref[0].T
        else:
            hf_s[...] = jnp.zeros_like(hf_s)

    if tiles_per_seq > 1:
        pl.when(t == 0)(init_state)
    else:
        init_state()
    h_state = hf_s[...]
    ya = []
    for ch in range(tm // q):
        rows = slice(ch * q, (ch + 1) * q)
        y, h_state = _ssd_chunk(
            xa_ref[0, rows, 0:D_SSD], xa_ref[0, rows, D_SSD:D_SSD + BC_COLS],
            xa_ref[0, rows, D_SSD + BC_COLS:], dtp_ref[0, rows, :], h_state, hbs_ref[0, ch],
            dtbias_ref, aneg_ref, dskip_ref, ef_ref, between_pairs)
        if ch == 0:
            zz = in_proj(MAIN_Z)
        yg = y * zz[rows]
        ms = jnp.mean(yg * yg, axis=-1, keepdims=True)
        ya.append(yg * lax.rsqrt(ms + EPS) * normw_ref[...])
    hf_s[...] = h_state

    def final_state():
        hff_ref[0] = h_state.T

    if tiles_per_seq > 1:
        pl.when(t == tiles_per_seq - 1)(final_state)
    else:
        final_state()
    ya = jnp.concatenate(ya, axis=0)
    proj_a = _dot(ya, wpa_ref[...])

    u = in_proj(MAIN_U)
    v = _layer_norm(in_proj(MAIN_V), lng_ref[...], lnb_ref[...])
    n_ch = tm // SGU_CHUNK
    vb = v.astype(BF16)
    per_group = []
    for g in range(SGU_GROUPS):
        gs = slice(g * SGU_GROUP_DIM, (g + 1) * SGU_GROUP_DIM)
        rhs = jnp.concatenate([vb[ch * SGU_CHUNK:(ch + 1) * SGU_CHUNK, gs] for ch in range(n_ch)],
                              axis=1)
        per_group.append(jnp.dot(wsp_ref[g], rhs, preferred_element_type=F32))
    rows_out = []
    for ch in range(n_ch):
        cs = slice(ch * SGU_GROUP_DIM, (ch + 1) * SGU_GROUP_DIM)
        rows_out.append(jnp.concatenate([p[:, cs] for p in per_group], axis=1) + bsp_ref[...])
    yb = u * jnp.concatenate(rows_out, axis=0)

    merged = in_proj(MAIN_GA) * proj_a + in_proj(MAIN_GB) * _dot(yb, wpb_ref[...])
    o = _dot(merged, wout_ref[...])
    o_ref[0] = _layer_norm(DN_ALPHA * x + gate1 * o, ln1g_ref[...], ln1b_ref[...])


def _main(x, pos, xa, dtp, hbs, h0f, mods, mod_row0, tm, seq_len, consts):
    s_n, l_n, n_tiles, tiles_per_seq, seqs_per_row = _tiling(x, tm, seq_len)
    has_pos = pos is not None
    has_h0 = h0f is not None
    cpt = tm // SSD_CHUNK
    tok = lambda s, j: (s, j, 0)
    in_specs = [pl.BlockSpec((1, tm, D_MODEL), tok),
                pl.BlockSpec((1, tm, SSD_CONV_DIM), tok),
                pl.BlockSpec((1, tm, DT_PAD), tok),
                pl.BlockSpec((1, cpt, SSD_STATE, D_SSD), lambda s, j: (s, j, 0, 0))]
    args = [x, xa, dtp, hbs]
    if has_pos:
        in_specs += [_const_spec(pos[0].shape), _const_spec(pos[1].shape)]
        args += list(pos)
    state_spec = pl.BlockSpec((1, D_SSD, SSD_STATE),
                              lambda s, j: (s * seqs_per_row + j // tiles_per_seq, 0, 0))
    if has_h0:
        in_specs.append(state_spec)
        args.append(h0f)
    in_specs.append(pl.BlockSpec((1, 1, 6 * D_MODEL), lambda s, j: (mod_row0 + s, 0, 0)))
    args.append(mods)
    for a in consts:
        in_specs.append(_const_spec(a.shape))
        args.append(a)
    return pl.pallas_call(
        functools.partial(_main_kernel, has_pos=has_pos, has_h0=has_h0,
                          tiles_per_seq=tiles_per_seq),
        grid=(s_n, n_tiles),
        in_specs=in_specs,
        out_specs=[pl.BlockSpec((1, tm, D_MODEL), tok), state_spec],
        out_shape=[jax.ShapeDtypeStruct((s_n, l_n, D_MODEL), F32),
                   jax.ShapeDtypeStruct((s_n * seqs_per_row, D_SSD, SSD_STATE), F32)],
        scratch_shapes=[pltpu.VMEM((SSD_STATE, D_SSD), F32)],
        compiler_params=_params(VMEM_MIB_MAIN, 2),
        name="main",
    )(*args)


FF_TILE = 1024
FFN_TM = 1024


def _ffn_kernel(x_ref, mod_ref, w1_ref, w2_ref, g_ref, b_ref, o_ref):
    x = x_ref[0]
    shift2 = mod_ref[0, :, 3 * D_MODEL:4 * D_MODEL]
    scale2 = mod_ref[0, :, 4 * D_MODEL:5 * D_MODEL]
    gate2 = mod_ref[0, :, 5 * D_MODEL:6 * D_MODEL]
    h = (x * (1.0 + scale2) + shift2).astype(BF16)
    f = jnp.zeros_like(x)
    for k0 in range(0, D_FF, FF_TILE):
        a = jnp.dot(h, w1_ref[:, k0:k0 + FF_TILE], preferred_element_type=F32)
        a = jnp.maximum(a, 0.0)
        f = f + _dot(a * a, w2_ref[k0:k0 + FF_TILE, :])
    o_ref[0] = _layer_norm(DN_ALPHA * x + gate2 * f, g_ref[...], b_ref[...])


def _ffn(x, mods, mod_row0, tm, w1, w2, g, b):
    s_n, l_n, _ = x.shape
    tok = lambda s, j: (s, j, 0)
    return pl.pallas_call(
        _ffn_kernel,
        grid=(s_n, l_n // tm),
        in_specs=[pl.BlockSpec((1, tm, D_MODEL), tok),
                  pl.BlockSpec((1, 1, 6 * D_MODEL), lambda s, j: (mod_row0 + s, 0, 0)),
                  _const_spec(w1.shape), _const_spec(w2.shape),
                  _const_spec(g.shape), _const_spec(b.shape)],
        out_specs=pl.BlockSpec((1, tm, D_MODEL), tok),
        out_shape=jax.ShapeDtypeStruct((s_n, l_n, D_MODEL), F32),
        compiler_params=_params(VMEM_MIB_FFN, 2),
        name="ffn",
    )(x, mods, w1, w2, g, b)


def _grid_pos_tables(n_tokens):
    rows = n_tokens // GRID_W
    quarter = D_MODEL // 4
    omega = 1.0 / (POS_BASE ** (jnp.arange(quarter, dtype=F32) / quarter))
    r = jnp.arange(rows, dtype=F32)[:, None] * omega
    col = jnp.arange(GRID_W, dtype=F32)[:, None] * omega
    emb_r = jnp.concatenate([jnp.sin(r), jnp.cos(r)], axis=-1)
    emb_c = jnp.concatenate([jnp.sin(col), jnp.cos(col)], axis=-1)
    return jnp.broadcast_to(emb_r[:, None, :], (rows, SUBLANES, D_MODEL // 2)), emb_c


def _dt_lanes(v, axis):
    pad = [(0, 0)] * v.ndim
    pad[axis] = (0, DT_PAD - DT_COPIES * DT_GROUP)
    return jnp.pad(jnp.concatenate([v] * DT_COPIES, axis=axis), pad)


def _head_expanders():
    lane = jnp.arange(LANES)[:, None]
    head = jnp.arange(D_SSD)[None, :] // SSD_HEAD_DIM
    valid = lane < DT_COPIES * DT_GROUP
    e_f = jnp.logical_and(valid, lane % DT_GROUP == head)
    e_b = jnp.logical_and(valid, lane % DT_GROUP == head + SSD_HEADS)
    return e_f.astype(BF16), e_b.astype(BF16)


def _stream(x, pos, mods, mod_row0, seq_len, h0f, h0b, prm, tm):
    xa, dtp, hbs, hb = _front(x, pos, mods, h0b, prm["w_xbc"], prm["w_dt"], prm["conv_w"],
                              prm["conv_b"],
                              prm["dt_bias"], prm["a_neg"], prm["e_b"], mod_row0, tm, seq_len)
    x1, hf = _main(x, pos, xa, dtp, hbs, h0f, mods, mod_row0, tm, seq_len, prm["main"])
    y = _ffn(x1, mods, mod_row0, FFN_TM, prm["w_ff1"], prm["w_ff2"], prm["ln2_g"], prm["ln2_b"])
    return y, hf, hb


def kernel(x_prompt, x_sample, state_ssd_fwd, state_ssd_bwd, c, c_ctx, w_ada, b_ada, w_in, conv_w, conv_b, dt_bias_fwd, dt_bias_bwd, a_log_fwd, a_log_bwd, d_skip, ssd_norm_w, sgu_ln_g, sgu_ln_b, w_spatial, b_spatial, w_proj_a, w_proj_b, w_out, ln1_g, ln1_b, w_ff1, w_ff2, ln2_g, ln2_b):
    batch, seq, _ = x_prompt.shape
    dec_batch, dec_seq, _ = x_sample.shape
    l = 0
    row = lambda v: v.reshape(1, -1)

    cond = jnp.zeros((SUBLANES, D_MODEL), F32).at[0].set(c_ctx).at[1:1 + dec_batch].set(c)
    mods = _ada(cond, w_ada[l], b_ada[l]).reshape(SUBLANES, 1, 6 * D_MODEL)

    o1 = D_SSD
    o2 = o1 + SSD_CONV_DIM
    o3 = o2 + DT_GROUP
    w_t = jnp.swapaxes(w_in[l], 0, 1)
    step = W_COL_TILE // DT_GROUP
    w_xbc = _w_cols(w_t, lambda i: o1 // DT_GROUP + i * step, SSD_CONV_DIM // W_COL_TILE)
    z_tiles = o1 // W_COL_TILE
    w_main = _w_cols(
        w_t, lambda i: jnp.where(i < z_tiles, i * step, o3 // DT_GROUP + (i - z_tiles) * step),
        MAIN_COLS // W_COL_TILE)
    assert o2 % LANES == 0
    w_dt = _w_dt(w_t, o2)
    e_f, e_b = _head_expanders()
    dt_bias = row(_dt_lanes(jnp.concatenate([dt_bias_fwd[l], dt_bias_bwd[l]]), 0))
    a_neg = row(_dt_lanes(-jnp.exp(jnp.concatenate([a_log_fwd[l], a_log_bwd[l]])), 0))

    prm = dict(
        w_xbc=w_xbc, w_dt=w_dt, conv_w=conv_w[l], conv_b=row(conv_b[l]),
        dt_bias=dt_bias, a_neg=a_neg, e_b=e_b,
        main=(w_main, dt_bias, a_neg, row(jnp.repeat(d_skip[l], SSD_HEAD_DIM)),
              row(ssd_norm_w[l]), e_f, row(sgu_ln_g[l]), row(sgu_ln_b[l]),
              w_spatial[l].astype(BF16), jnp.repeat(b_spatial[l].T, SGU_GROUP_DIM, axis=1),
              w_proj_a[l].astype(BF16), w_proj_b[l].astype(BF16), w_out[l].astype(BF16),
              row(ln1_g[l]), row(ln1_b[l])),
        w_ff1=w_ff1[l].astype(BF16), w_ff2=w_ff2[l].astype(BF16),
        ln2_g=row(ln2_g[l]), ln2_b=row(ln2_b[l]),
    )

    tm = 256
    xp, hf, hb = _stream(x_prompt.reshape(1, batch * seq, D_MODEL), None, mods, 0, seq,
                         None, None, prm, tm)
    pos = _grid_pos_tables(dec_seq)
    h0f = state_ssd_fwd[:, l].reshape(dec_batch, D_SSD, SSD_STATE)
    h0b = state_ssd_bwd[:, l].reshape(dec_batch, D_SSD, SSD_STATE)
    xs, _, _ = _stream(x_sample, pos, mods, 1, dec_seq, h0f, h0b, prm, tm)

    state_shape = (batch, DEPTH, SSD_HEADS, SSD_HEAD_DIM, SSD_STATE)
    return (xp.reshape(batch, seq, D_MODEL), xs,
            hf.reshape(state_shape), hb.reshape(state_shape))
```

```python
import functools
import math

import jax
import jax.numpy as jnp
from jax import lax
from jax.experimental import pallas as pl
from jax.experimental.pallas import tpu as pltpu

D_MODEL = 1024
GRID_W = 64
POS_BASE = 10000.0
D_SSD = 1024
SSD_HEAD_DIM = 64
SSD_HEADS = D_SSD // SSD_HEAD_DIM
SSD_GROUPS = 2
SSD_STATE = 128
SSD_CHUNK = 128
SSD_CONV_DIM = D_SSD + 2 * SSD_GROUPS * SSD_STATE
D_SGU = 1024
SGU_GROUPS = 8
SGU_GROUP_DIM = D_SGU // SGU_GROUPS
SGU_CHUNK = 128
D_FF = 4 * D_MODEL
DEPTH = 1
DN_ALPHA = (2.0 * DEPTH) ** 0.25
EPS = 1e-5

LANES = 128
SUBLANES = 8
BF16_ROWS = 16
V7X_VMEM_MIB = 64
VMEM_MIB_SMALL = V7X_VMEM_MIB // 2
VMEM_MIB_FRONT = 48
VMEM_MIB_MAIN = 56
VMEM_MIB_FFN = 48

DT_GROUP = 2 * SSD_HEADS
DT_COPIES = 3
DT_PAD = LANES
HEAD_PAIRS = SSD_HEADS // 2
PAIRS_PER_GROUP = HEAD_PAIRS // SSD_GROUPS
GROUP_COLS = D_SSD // SSD_GROUPS
BC_COLS = SSD_GROUPS * SSD_STATE

BF16 = jnp.bfloat16
F32 = jnp.float32
HIGHEST = lax.Precision.HIGHEST


def _dot(a, b):
    return jnp.dot(a.astype(BF16), b.astype(BF16), preferred_element_type=F32)


def _dot_exact(a, b):
    return jnp.dot(a, b, precision=HIGHEST, preferred_element_type=F32)


def _layer_norm(x, g, b):
    mu = jnp.mean(x, axis=-1, keepdims=True)
    xc = x - mu
    var = jnp.mean(xc * xc, axis=-1, keepdims=True)
    return xc * lax.rsqrt(var + EPS) * g + b


def _softplus(x):
    return jnp.maximum(x, 0.0) + jnp.log1p(jnp.exp(-jnp.abs(x)))


def _gelu(x):
    return 0.5 * x * (1.0 + lax.erf(x * (1.0 / math.sqrt(2.0))))


def _sigmoid(x):
    return 0.5 * jnp.tanh(0.5 * x) + 0.5


def _silu(x):
    half = 0.5 * x
    return half * jnp.tanh(half) + half


def _const_spec(shape):
    zeros = (0,) * len(shape)
    return pl.BlockSpec(shape, lambda *_: zeros, pipeline_mode=pl.Buffered(1))


def _params(vmem_mib, n_axes):
    return pltpu.CompilerParams(
        dimension_semantics=("arbitrary",) * n_axes,
        vmem_limit_bytes=vmem_mib * 1024 * 1024)


def _ada_kernel(c_ref, w_ref, b_ref, o_ref):
    c = _silu(c_ref[...])
    c_hi = c.astype(BF16)
    c_lo = (c - c_hi.astype(F32)).astype(BF16)
    w = w_ref[...]
    w_hi = w.astype(BF16)
    w_lo = (w - w_hi.astype(F32)).astype(BF16)
    rows = c.shape[0]
    by_hi = jnp.dot(jnp.concatenate([c_hi, c_lo], axis=0), w_hi, preferred_element_type=F32)
    by_lo = jnp.dot(c_hi, w_lo, preferred_element_type=F32)
    o_ref[...] = by_hi[:rows] + by_hi[rows:] + by_lo + b_ref[...]


def _ada(cond, w_ada, b_ada):
    n = w_ada.shape[1]
    tn = 2048
    return pl.pallas_call(
        _ada_kernel,
        grid=(n // tn,),
        in_specs=[pl.BlockSpec((SUBLANES, D_MODEL), lambda j: (0, 0)),
                  pl.BlockSpec((D_MODEL, tn), lambda j: (0, j)),
                  pl.BlockSpec((1, tn), lambda j: (0, j))],
        out_specs=pl.BlockSpec((SUBLANES, tn), lambda j: (0, j)),
        out_shape=jax.ShapeDtypeStruct((SUBLANES, n), F32),
        compiler_params=_params(VMEM_MIB_SMALL, 1),
        name="ada",
    )(cond, w_ada, b_ada.reshape(1, n))


W_COL_TILE = 512


def _w_cols_kernel(src_ref, o_ref):
    o_ref[...] = src_ref[...].T.astype(BF16)


def _w_dt_kernel(src_ref, o_ref):
    t = src_ref[...].T
    lane = lax.broadcasted_iota(jnp.int32, t.shape, 1)
    dt = jnp.where(lane < DT_GROUP, t, 0.0)
    o_ref[...] = (dt + pltpu.roll(dt, DT_GROUP, 1) + pltpu.roll(dt, 2 * DT_GROUP, 1)).astype(BF16)


def _w_dt(w_t, col0):
    d = w_t.shape[1]
    return pl.pallas_call(
        _w_dt_kernel,
        grid=(1,),
        in_specs=[pl.BlockSpec((LANES, d), lambda i: (col0 // LANES, 0))],
        out_specs=pl.BlockSpec((d, LANES), lambda i: (0, 0)),
        out_shape=jax.ShapeDtypeStruct((d, LANES), BF16),
        name="w_dt",
    )(w_t)


def _w_cols(w_t, col_start, n_tiles):
    d = w_t.shape[1]
    return pl.pallas_call(
        _w_cols_kernel,
        grid=(n_tiles,),
        in_specs=[pl.BlockSpec((pl.Element(W_COL_TILE), pl.Element(d)),
                               lambda i: (col_start(i) * DT_GROUP, 0))],
        out_specs=pl.BlockSpec((d, W_COL_TILE), lambda i: (0, i)),
        out_shape=jax.ShapeDtypeStruct((d, n_tiles * W_COL_TILE), BF16),
        compiler_params=_params(VMEM_MIB_SMALL, 1),
        name="w_cols",
    )(w_t)


IN_NTILE = 512
HALO = BF16_ROWS


def _add_pos(x_ref, embr_ref, embc_ref, grid_row0):
    n_rows = x_ref.shape[1] // GRID_W
    col = embc_ref[...]
    blocks = []
    for i in range(n_rows):
        row = embr_ref[grid_row0 + i]
        row = jnp.concatenate([row] * (GRID_W // SUBLANES), axis=0)
        blocks.append(x_ref[0, i * GRID_W:(i + 1) * GRID_W, :]
                      + jnp.concatenate([row, col], axis=1))
    return jnp.concatenate(blocks, axis=0)


def _pos_halo(embr_ref, embc_ref, grid_row, col0):
    left = jnp.concatenate([embr_ref[grid_row]] * (HALO // SUBLANES), axis=0)
    return jnp.concatenate([left, embc_ref[col0:col0 + HALO, :]], axis=1)


def _block_diag_pair(xp):
    lane = lax.broadcasted_iota(jnp.int32, xp.shape, 1)
    first = lane < SSD_HEAD_DIM
    return jnp.concatenate([jnp.where(first, xp, 0.0), jnp.where(first, 0.0, xp)], axis=0)


def _split_terms(v):
    lane = lax.broadcasted_iota(jnp.int32, v.shape, 1)
    hi = v.astype(BF16).astype(F32)
    r1 = v - hi
    mid = r1.astype(BF16).astype(F32)
    lo = r1 - mid
    return jnp.where(lane < DT_GROUP, hi, jnp.where(lane < 2 * DT_GROUP, mid, lo)).astype(BF16)


def _expand(v, e_ref):
    return jnp.dot(_split_terms(v), e_ref[...], preferred_element_type=F32)


def _chunk_decay(dtp, dtbias_ref, aneg_ref, on_mxu=True):
    q = dtp.shape[0]
    dt_all = _softplus(dtp + dtbias_ref[...])
    a_all = dt_all * aneg_ref[...]
    if on_mxu:
        ii = lax.broadcasted_iota(jnp.int32, (q, q), 0)
        jj = lax.broadcasted_iota(jnp.int32, (q, q), 1)
        acum = _dot_exact((ii >= jj).astype(F32), a_all)
    else:
        row = lax.broadcasted_iota(jnp.int32, a_all.shape, 0)
        acum = a_all
        shift = 1
        while shift < q:
            acum = acum + jnp.where(row >= shift, pltpu.roll(acum, shift, 0), 0.0)
            shift *= 2
    atot = acum[q - 1:q, :]
    rcum = atot - acum + a_all
    return dt_all, a_all, acum, atot, rcum, rcum[0:1, :]


def _ssd_chunk(xh, b_all, c_all, dtp, hf_in, hb_in, dtbias_ref, aneg_ref, dskip_ref, ef_ref,
               between_pairs=lambda: None):
    q = SSD_CHUNK
    dt_all, a_all, acum, atot, rcum, _ = _chunk_decay(dtp, dtbias_ref, aneg_ref, on_mxu=False)
    ii = lax.broadcasted_iota(jnp.int32, (q, q), 0)
    jj = lax.broadcasted_iota(jnp.int32, (q, q), 1)
    lower = ii >= jj
    diag = ii == jj
    dt_t = dt_all.T
    acum_t = acum.T
    rcum_t = rcum.T

    off_f, off_b, cb = [], [], []
    for g in range(SSD_GROUPS):
        gs = slice(g * GROUP_COLS, (g + 1) * GROUP_COLS)
        c_g = c_all[:, g * SSD_STATE:(g + 1) * SSD_STATE].astype(BF16)
        b_g = b_all[:, g * SSD_STATE:(g + 1) * SSD_STATE].astype(BF16)
        cb.append(lax.dot_general(c_g, b_g, (((1,), (1,)), ((), ())),
                                  preferred_element_type=F32))
        off_f.append(jnp.dot(c_g, hf_in[:, gs].astype(BF16), preferred_element_type=F32))
        off_b.append(jnp.dot(c_g, hb_in[:, gs].astype(BF16), preferred_element_type=F32))
    off_f = jnp.concatenate(off_f, axis=1)
    off_b = jnp.concatenate(off_b, axis=1)

    first_head = lax.broadcasted_iota(jnp.int32, (q, LANES), 1) < SSD_HEAD_DIM
    w_tok = dt_all * jnp.exp(atot - acum)
    y_cols, wx = [], []
    for pr in range(HEAD_PAIRS):
        g = pr // PAIRS_PER_GROUP
        m_parts, col_f, col_b, col_w = [], [], [], []
        for h in (2 * pr, 2 * pr + 1):
            hb_lane = SSD_HEADS + h
            colf = jnp.broadcast_to(acum[:, h:h + 1], (q, q))
            colb = jnp.broadcast_to(rcum[:, hb_lane:hb_lane + 1], (q, q))
            col_w.append(jnp.broadcast_to(w_tok[:, h:h + 1], (q, q)))
            seg = jnp.where(lower, colf - acum_t[h:h + 1, :],
                            colb - rcum_t[hb_lane:hb_lane + 1, :])
            dt_b_row = dt_t[hb_lane:hb_lane + 1, :]
            w = (jnp.exp(seg) * jnp.where(lower, dt_t[h:h + 1, :], dt_b_row)
                 + jnp.where(diag, dt_b_row, 0.0))
            m_parts.append(cb[g] * w)
            col_f.append(colf)
            col_b.append(colb)
        sl = slice(pr * LANES, (pr + 1) * LANES)
        xp = xh[:, sl]
        y_cols.append(_dot(jnp.concatenate(m_parts, axis=1), _block_diag_pair(xp))
                      + jnp.exp(jnp.where(first_head, col_f[0], col_f[1])) * off_f[:, sl]
                      + jnp.exp(jnp.where(first_head, col_b[0], col_b[1])) * off_b[:, sl]
                      + dskip_ref[:, sl] * xp)
        wx.append((jnp.where(first_head, col_w[0], col_w[1]) * xp).astype(BF16))
        between_pairs()
    y = jnp.concatenate(y_cols, axis=1)

    wx = jnp.concatenate(wx, axis=1)
    s = jnp.concatenate(
        [jnp.dot(b_all[:, g * SSD_STATE:(g + 1) * SSD_STATE].T.astype(BF16),
                 wx[:, g * GROUP_COLS:(g + 1) * GROUP_COLS], preferred_element_type=F32)
         for g in range(SSD_GROUPS)], axis=1)
    decay = jnp.exp(_expand(jnp.broadcast_to(atot, (BF16_ROWS, LANES)), ef_ref))[0:1, :]
    return y, decay * hf_in + s


def _tiling(x, tm, seq_len):
    s_n, l_n, _ = x.shape
    return s_n, l_n, l_n // tm, seq_len // tm, l_n // seq_len


FRONT_SEGS = (SSD_CONV_DIM, DT_PAD)
FRONT_ORDER = (0, 0)
FRONT_NTILE = 256


def _front_kernel(*refs, has_pos, has_h0, tiles_per_seq, n_tiles):
    x_ref, xp_ref, xn_ref = refs[:3]
    refs = refs[3:]
    if has_pos:
        embr_ref, embc_ref = refs[:2]
        refs = refs[2:]
    if has_h0:
        h0b_ref = refs[0]
        refs = refs[1:]
    mod_ref, w_ref, wdt_ref, convw_ref, convb_ref, dtbias_ref, aneg_ref, eb_ref = refs[:8]
    xa_ref, dt_ref, hbs_ref, hfb_ref, hb_s, xb_s, dts_s = refs[8:]
    shift1 = mod_ref[0, :, 0:D_MODEL]
    scale1 = mod_ref[0, :, D_MODEL:2 * D_MODEL]
    tm = x_ref.shape[1]
    n_ext = tm + 2 * HALO
    q = SSD_CHUNK
    j = pl.program_id(1)
    tile = n_tiles - 1 - jnp.minimum(j, n_tiles - 1)
    t = tile % tiles_per_seq
    t_scan = jnp.minimum(n_tiles - j, n_tiles - 1) % tiles_per_seq
    slot = j % 2
    scan_slot = 1 - slot

    def init_state():
        if has_h0:
            hb_s[...] = h0b_ref[0].T
        else:
            hb_s[...] = jnp.zeros_like(hb_s)

    def modulate(x):
        return (x * (1.0 + scale1) + shift1).astype(BF16)

    def project():
        if has_pos:
            rows_per_tile = tm // GRID_W
            n_grid_rows = embr_ref.shape[0]
            r0 = tile * rows_per_tile
            x = _add_pos(x_ref, embr_ref, embc_ref, r0)
            x_prev = xp_ref[0] + _pos_halo(embr_ref, embc_ref, jnp.maximum(r0 - 1, 0),
                                           GRID_W - HALO)
            x_next = xn_ref[0] + _pos_halo(embr_ref, embc_ref,
                                           jnp.minimum(r0 + rows_per_tile, n_grid_rows - 1), 0)
        else:
            x, x_prev, x_next = x_ref[0], xp_ref[0], xn_ref[0]
        h = modulate(x)
        h_ext = jnp.concatenate([modulate(x_prev), h, modulate(x_next)], axis=0)

        row = lax.broadcasted_iota(jnp.int32, (tm, 1), 0)
        zero_prev = row == 0
        zero_next = row == tm - 1
        if tiles_per_seq > 1:
            zero_prev = jnp.logical_and(zero_prev, t == 0)
            zero_next = jnp.logical_and(zero_next, t == tiles_per_seq - 1)

        dt = jnp.dot(h, wdt_ref[...], preferred_element_type=F32)
        dt_ref[0] = dt
        dts_s[slot] = dt
        yield
        for c0 in range(0, SSD_CONV_DIM, FRONT_NTILE):
            c1 = c0 + FRONT_NTILE
            p = jnp.dot(h_ext, w_ref[:, c0:c1], preferred_element_type=F32)
            cur = p[HALO:HALO + tm]
            prv = jnp.where(zero_prev, 0.0, pltpu.roll(p, 1, 0)[HALO:HALO + tm])
            nxt = jnp.where(zero_next, 0.0, pltpu.roll(p, n_ext - 1, 0)[HALO:HALO + tm])
            xc = (convw_ref[0:1, c0:c1] * prv + convw_ref[1:2, c0:c1] * cur
                  + convw_ref[2:3, c0:c1] * nxt + convb_ref[:, c0:c1])
            xa = _silu(xc)
            xa_ref[0, :, c0:c1] = xa
            if c0 < D_SSD + BC_COLS:
                xb_s[slot, :, c0:c1] = xa
            yield

    def scan():
        h_state = hb_s[...]
        for ch in reversed(range(tm // q)):
            rows = slice(ch * q, (ch + 1) * q)
            hbs_ref[0, ch] = h_state
            dt_all, _, _, _, rcum, rtot = _chunk_decay(dts_s[scan_slot, rows, :], dtbias_ref,
                                                       aneg_ref)
            yield
            w_exp = _expand(dt_all * jnp.exp(rtot - rcum), eb_ref)
            decay = jnp.exp(_expand(jnp.broadcast_to(rtot, (BF16_ROWS, LANES)), eb_ref))[0:1, :]
            yield
            wx = (w_exp * xb_s[scan_slot, rows, 0:D_SSD]).astype(BF16)
            b_all = xb_s[scan_slot, rows, D_SSD:D_SSD + BC_COLS]
            parts = []
            for g in range(SSD_GROUPS):
                parts.append(jnp.dot(b_all[:, g * SSD_STATE:(g + 1) * SSD_STATE].T.astype(BF16),
                                     wx[:, g * GROUP_COLS:(g + 1) * GROUP_COLS],
                                     preferred_element_type=F32))
                yield
            h_state = decay * h_state + jnp.concatenate(parts, axis=1)
        hb_s[...] = h_state

        def final_state():
            hfb_ref[0] = h_state.T

        if tiles_per_seq > 1:
            pl.when(t_scan == 0)(final_state)
        else:
            final_state()

    def run(*stage_fns):
        if scan in stage_fns:
            if tiles_per_seq > 1:
                pl.when(t_scan == tiles_per_seq - 1)(init_state)
            else:
                init_state()
        stages = [fn() for fn in stage_fns]
        if len(stages) == 2:
            for k in FRONT_ORDER:
                next(stages[k], None)
        while stages:
            for gen in list(stages):
                if next(gen, StopIteration) is StopIteration:
                    stages.remove(gen)

    pl.when(j == 0)(functools.partial(run, project))
    pl.when(jnp.logical_and(j > 0, j < n_tiles))(functools.partial(run, project, scan))
    pl.when(j == n_tiles)(functools.partial(run, scan))


def _front(x, pos, mods, h0b, w_xbc, w_dt, convw, convb, dtbias, aneg, e_b, mod_row0, tm,
           seq_len):
    s_n, l_n, n_tiles, tiles_per_seq, seqs_per_row = _tiling(x, tm, seq_len)
    has_pos = pos is not None
    has_h0 = h0b is not None
    hb = tm // HALO
    n_hb = l_n // HALO
    cpt = tm // SSD_CHUNK
    fwd = lambda j: n_tiles - 1 - jnp.minimum(j, n_tiles - 1)
    scanned = lambda j: jnp.minimum(n_tiles - j, n_tiles - 1)
    tok = lambda s, j: (s, fwd(j), 0)
    seq_of = lambda s, j: (s * seqs_per_row + scanned(j) // tiles_per_seq, 0, 0)
    in_specs = [pl.BlockSpec((1, tm, D_MODEL), tok),
                pl.BlockSpec((1, HALO, D_MODEL),
                             lambda s, j: (s, jnp.maximum(fwd(j) * hb - 1, 0), 0)),
                pl.BlockSpec((1, HALO, D_MODEL),
                             lambda s, j: (s, jnp.minimum((fwd(j) + 1) * hb, n_hb - 1), 0))]
    args = [x, x, x]
    if has_pos:
        in_specs += [_const_spec(pos[0].shape), _const_spec(pos[1].shape)]
        args += list(pos)
    state_spec = pl.BlockSpec((1, D_SSD, SSD_STATE), seq_of)
    if has_h0:
        in_specs.append(state_spec)
        args.append(h0b)
    in_specs.append(pl.BlockSpec((1, 1, 6 * D_MODEL), lambda s, j: (mod_row0 + s, 0, 0)))
    args.append(mods)
    for a in (w_xbc, w_dt, convw, convb, dtbias, aneg, e_b):
        in_specs.append(_const_spec(a.shape))
        args.append(a)
    return pl.pallas_call(
        functools.partial(_front_kernel, has_pos=has_pos, has_h0=has_h0,
                          tiles_per_seq=tiles_per_seq, n_tiles=n_tiles),
        grid=(s_n, n_tiles + 1),
        in_specs=in_specs,
        out_specs=[pl.BlockSpec((1, tm, w), tok) for w in FRONT_SEGS]
        + [pl.BlockSpec((1, cpt, SSD_STATE, D_SSD), lambda s, j: (s, scanned(j), 0, 0)),
           state_spec],
        out_shape=[jax.ShapeDtypeStruct((s_n, l_n, w), F32) for w in FRONT_SEGS]
        + [jax.ShapeDtypeStruct((s_n, l_n // SSD_CHUNK, SSD_STATE, D_SSD), F32),
           jax.ShapeDtypeStruct((s_n * seqs_per_row, D_SSD, SSD_STATE), F32)],
        scratch_shapes=[pltpu.VMEM((SSD_STATE, D_SSD), F32),
                        pltpu.VMEM((2, tm, D_SSD + BC_COLS), F32),
                        pltpu.VMEM((2, tm, DT_PAD), F32)],
        compiler_params=_params(VMEM_MIB_FRONT, 2),
        name="front",
    )(*args)


MAIN_Z, MAIN_U, MAIN_V, MAIN_GA, MAIN_GB = (i * D_MODEL for i in range(5))
MAIN_COLS = 5 * D_MODEL


def _main_kernel(*refs, has_pos, has_h0, tiles_per_seq):
    x_ref, xa_ref, dtp_ref, hbs_ref = refs[:4]
    refs = refs[4:]
    if has_pos:
        embr_ref, embc_ref = refs[:2]
        refs = refs[2:]
    if has_h0:
        h0f_ref = refs[0]
        refs = refs[1:]
    (mod_ref, w_ref, dtbias_ref, aneg_ref, dskip_ref, normw_ref, ef_ref, lng_ref, lnb_ref,
     wsp_ref, bsp_ref, wpa_ref, wpb_ref, wout_ref, ln1g_ref, ln1b_ref) = refs[:16]
    o_ref, hff_ref, hf_s = refs[16:]
    tm = x_ref.shape[1]
    q = SSD_CHUNK
    tile = pl.program_id(1)
    t = tile % tiles_per_seq

    if has_pos:
        x = _add_pos(x_ref, embr_ref, embc_ref, tile * (tm // GRID_W))
    else:
        x = x_ref[0]
    shift1 = mod_ref[0, :, 0:D_MODEL]
    scale1 = mod_ref[0, :, D_MODEL:2 * D_MODEL]
    gate1 = mod_ref[0, :, 2 * D_MODEL:3 * D_MODEL]
    h = (x * (1.0 + scale1) + shift1).astype(BF16)

    col_tiles = {}
    pending = list(range(0, MAIN_COLS, IN_NTILE))
    n_slots = (tm // q) * HEAD_PAIRS
    n_fill = len(pending)
    slot = [0]

    def project_next():
        c = pending.pop(0)
        act = _silu if c < MAIN_U else (_gelu if c < MAIN_GA else _sigmoid)
        col_tiles[c] = act(jnp.dot(h, w_ref[:, c:c + IN_NTILE], preferred_element_type=F32))

    def between_pairs():
        slot[0] += 1
        done = n_fill - len(pending)
        for _ in range(-(-slot[0] * n_fill // n_slots) - done):
            project_next()

    def in_proj(c0):
        while pending and pending[0] < c0 + D_MODEL:
            project_next()
        return jnp.concatenate([col_tiles.pop(c) for c in range(c0, c0 + D_MODEL, IN_NTILE)],
                               axis=1)

    def init_state():
        if has_h0:
            hf_s[...] = h0f_ref[0].T
        else:
            hf_s[...] = jnp.zeros_like(hf_s)

    if tiles_per_seq > 1:
        pl.when(t == 0)(init_state)
    else:
        init_state()
    h_state = hf_s[...]
    ya = []
    for ch in range(tm // q):
        rows = slice(ch * q, (ch + 1) * q)
        y, h_state = _ssd_chunk(
            xa_ref[0, rows, 0:D_SSD], xa_ref[0, rows, D_SSD:D_SSD + BC_COLS],
            xa_ref[0, rows, D_SSD + BC_COLS:], dtp_ref[0, rows, :], h_state, hbs_ref[0, ch],
            dtbias_ref, aneg_ref, dskip_ref, ef_ref, between_pairs)
        if ch == 0:
            zz = in_proj(MAIN_Z)
        yg = y * zz[rows]
        ms = jnp.mean(yg * yg, axis=-1, keepdims=True)
        ya.append(yg * lax.rsqrt(ms + EPS) * normw_ref[...])
    hf_s[...] = h_state

    def final_state():
        hff_ref[0] = h_state.T

    if tiles_per_seq > 1:
        pl.when(t == tiles_per_seq - 1)(final_state)
    else:
        final_state()
    ya = jnp.concatenate(ya, axis=0)
    proj_a = _dot(ya, wpa_ref[...])

    u = in_proj(MAIN_U)
    v = _layer_norm(in_proj(MAIN_V), lng_ref[...], lnb_ref[...])
    n_ch = tm // SGU_CHUNK
    vb = v.astype(BF16)
    per_group = []
    for g in range(SGU_GROUPS):
        gs = slice(g * SGU_GROUP_DIM, (g + 1) * SGU_GROUP_DIM)
        rhs = jnp.concatenate([vb[ch * SGU_CHUNK:(ch + 1) * SGU_CHUNK, gs] for ch in range(n_ch)],
                              axis=1)
        per_group.append(jnp.dot(wsp_ref[g], rhs, preferred_element_type=F32))
    rows_out = []
    for ch in range(n_ch):
        cs = slice(ch * SGU_GROUP_DIM, (ch + 1) * SGU_GROUP_DIM)
        rows_out.append(jnp.concatenate([p[:, cs] for p in per_group], axis=1) + bsp_ref[...])
    yb = u * jnp.concatenate(rows_out, axis=0)

    merged = in_proj(MAIN_GA) * proj_a + in_proj(MAIN_GB) * _dot(yb, wpb_ref[...])
    o = _dot(merged, wout_ref[...])
    o_ref[0] = _layer_norm(DN_ALPHA * x + gate1 * o, ln1g_ref[...], ln1b_ref[...])


def _main(x, pos, xa, dtp, hbs, h0f, mods, mod_row0, tm, seq_len, consts):
    s_n, l_n, n_tiles, tiles_per_seq, seqs_per_row = _tiling(x, tm, seq_len)
    has_pos = pos is not None
    has_h0 = h0f is not None
    cpt = tm // SSD_CHUNK
    tok = lambda s, j: (s, j, 0)
    in_specs = [pl.BlockSpec((1, tm, D_MODEL), tok),
                pl.BlockSpec((1, tm, SSD_CONV_DIM), tok),
                pl.BlockSpec((1, tm, DT_PAD), tok),
                pl.BlockSpec((1, cpt, SSD_STATE, D_SSD), lambda s, j: (s, j, 0, 0))]
    args = [x, xa, dtp, hbs]
    if has_pos:
        in_specs += [_const_spec(pos[0].shape), _const_spec(pos[1].shape)]
        args += list(pos)
    state_spec = pl.BlockSpec((1, D_SSD, SSD_STATE),
                              lambda s, j: (s * seqs_per_row + j // tiles_per_seq, 0, 0))
    if has_h0:
        in_specs.append(state_spec)
        args.append(h0f)
    in_specs.append(pl.BlockSpec((1, 1, 6 * D_MODEL), lambda s, j: (mod_row0 + s, 0, 0)))
    args.append(mods)
    for a in consts:
        in_specs.append(_const_spec(a.shape))
        args.append(a)
    return pl.pallas_call(
        functools.partial(_main_kernel, has_pos=has_pos, has_h0=has_h0,
                          tiles_per_seq=tiles_per_seq),
        grid=(s_n, n_tiles),
        in_specs=in_specs,
        out_specs=[pl.BlockSpec((1, tm, D_MODEL), tok), state_spec],
        out_shape=[jax.ShapeDtypeStruct((s_n, l_n, D_MODEL), F32),
                   jax.ShapeDtypeStruct((s_n * seqs_per_row, D_SSD, SSD_STATE), F32)],
        scratch_shapes=[pltpu.VMEM((SSD_STATE, D_SSD), F32)],
        compiler_params=_params(VMEM_MIB_MAIN, 2),
        name="main",
    )(*args)


FF_TILE = 1024
FFN_TM = 1024


def _ffn_kernel(x_ref, mod_ref, w1_ref, w2_ref, g_ref, b_ref, o_ref):
    x = x_ref[0]
    shift2 = mod_ref[0, :, 3 * D_MODEL:4 * D_MODEL]
    scale2 = mod_ref[0, :, 4 * D_MODEL:5 * D_MODEL]
    gate2 = mod_ref[0, :, 5 * D_MODEL:6 * D_MODEL]
    h = (x * (1.0 + scale2) + shift2).astype(BF16)
    f = jnp.zeros_like(x)
    for k0 in range(0, D_FF, FF_TILE):
        a = jnp.dot(h, w1_ref[:, k0:k0 + FF_TILE], preferred_element_type=F32)
        a = jnp.maximum(a, 0.0)
        f = f + _dot(a * a, w2_ref[k0:k0 + FF_TILE, :])
    o_ref[0] = _layer_norm(DN_ALPHA * x + gate2 * f, g_ref[...], b_ref[...])


def _ffn(x, mods, mod_row0, tm, w1, w2, g, b):
    s_n, l_n, _ = x.shape
    tok = lambda s, j: (s, j, 0)
    return pl.pallas_call(
        _ffn_kernel,
        grid=(s_n, l_n // tm),
        in_specs=[pl.BlockSpec((1, tm, D_MODEL), tok),
                  pl.BlockSpec((1, 1, 6 * D_MODEL), lambda s, j: (mod_row0 + s, 0, 0)),
                  _const_spec(w1.shape), _const_spec(w2.shape),
                  _const_spec(g.shape), _const_spec(b.shape)],
        out_specs=pl.BlockSpec((1, tm, D_MODEL), tok),
        out_shape=jax.ShapeDtypeStruct((s_n, l_n, D_MODEL), F32),
        compiler_params=_params(VMEM_MIB_FFN, 2),
        name="ffn",
    )(x, mods, w1, w2, g, b)


def _grid_pos_tables(n_tokens):
    rows = n_tokens // GRID_W
    quarter = D_MODEL // 4
    omega = 1.0 / (POS_BASE ** (jnp.arange(quarter, dtype=F32) / quarter))
    r = jnp.arange(rows, dtype=F32)[:, None] * omega
    col = jnp.arange(GRID_W, dtype=F32)[:, None] * omega
    emb_r = jnp.concatenate([jnp.sin(r), jnp.cos(r)], axis=-1)
    emb_c = jnp.concatenate([jnp.sin(col), jnp.cos(col)], axis=-1)
    return jnp.broadcast_to(emb_r[:, None, :], (rows, SUBLANES, D_MODEL // 2)), emb_c


def _dt_lanes(v, axis):
    pad = [(0, 0)] * v.ndim
    pad[axis] = (0, DT_PAD - DT_COPIES * DT_GROUP)
    return jnp.pad(jnp.concatenate([v] * DT_COPIES, axis=axis), pad)


def _head_expanders():
    lane = jnp.arange(LANES)[:, None]
    head = jnp.arange(D_SSD)[None, :] // SSD_HEAD_DIM
    valid = lane < DT_COPIES * DT_GROUP
    e_f = jnp.logical_and(valid, lane % DT_GROUP == head)
    e_b = jnp.logical_and(valid, lane % DT_GROUP == head + SSD_HEADS)
    return e_f.astype(BF16), e_b.astype(BF16)


def _stream(x, pos, mods, mod_row0, seq_len, h0f, h0b, prm, tm):
    xa, dtp, hbs, hb = _front(x, pos, mods, h0b, prm["w_xbc"], prm["w_dt"], prm["conv_w"],
                              prm["conv_b"],
                              prm["dt_bias"], prm["a_neg"], prm["e_b"], mod_row0, tm, seq_len)
    x1, hf = _main(x, pos, xa, dtp, hbs, h0f, mods, mod_row0, tm, seq_len, prm["main"])
    y = _ffn(x1, mods, mod_row0, FFN_TM, prm["w_ff1"], prm["w_ff2"], prm["ln2_g"], prm["ln2_b"])
    return y, hf, hb


def kernel(x_prompt, x_sample, state_ssd_fwd, state_ssd_bwd, c, c_ctx, w_ada, b_ada, w_in, conv_w, conv_b, dt_bias_fwd, dt_bias_bwd, a_log_fwd, a_log_bwd, d_skip, ssd_norm_w, sgu_ln_g, sgu_ln_b, w_spatial, b_spatial, w_proj_a, w_proj_b, w_out, ln1_g, ln1_b, w_ff1, w_ff2, ln2_g, ln2_b):
    batch, seq, _ = x_prompt.shape
    dec_batch, dec_seq, _ = x_sample.shape
    l = 0
    row = lambda v: v.reshape(1, -1)

    cond = jnp.zeros((SUBLANES, D_MODEL), F32).at[0].set(c_ctx).at[1:1 + dec_batch].set(c)
    mods = _ada(cond, w_ada[l], b_ada[l]).reshape(SUBLANES, 1, 6 * D_MODEL)

    o1 = D_SSD
    o2 = o1 + SSD_CONV_DIM
    o3 = o2 + DT_GROUP
    w_t = jnp.swapaxes(w_in[l], 0, 1)
    step = W_COL_TILE // DT_GROUP
    w_xbc = _w_cols(w_t, lambda i: o1 // DT_GROUP + i * step, SSD_CONV_DIM // W_COL_TILE)
    z_tiles = o1 // W_COL_TILE
    w_main = _w_cols(
        w_t, lambda i: jnp.where(i < z_tiles, i * step, o3 // DT_GROUP + (i - z_tiles) * step),
        MAIN_COLS // W_COL_TILE)
    assert o2 % LANES == 0
    w_dt = _w_dt(w_t, o2)
    e_f, e_b = _head_expanders()
    dt_bias = row(_dt_lanes(jnp.concatenate([dt_bias_fwd[l], dt_bias_bwd[l]]), 0))
    a_neg = row(_dt_lanes(-jnp.exp(jnp.concatenate([a_log_fwd[l], a_log_bwd[l]])), 0))

    prm = dict(
        w_xbc=w_xbc, w_dt=w_dt, conv_w=conv_w[l], conv_b=row(conv_b[l]),
        dt_bias=dt_bias, a_neg=a_neg, e_b=e_b,
        main=(w_main, dt_bias, a_neg, row(jnp.repeat(d_skip[l], SSD_HEAD_DIM)),
              row(ssd_norm_w[l]), e_f, row(sgu_ln_g[l]), row(sgu_ln_b[l]),
              w_spatial[l].astype(BF16), jnp.repeat(b_spatial[l].T, SGU_GROUP_DIM, axis=1),
              w_proj_a[l].astype(BF16), w_proj_b[l].astype(BF16), w_out[l].astype(BF16),
              row(ln1_g[l]), row(ln1_b[l])),
        w_ff1=w_ff1[l].astype(BF16), w_ff2=w_ff2[l].astype(BF16),
        ln2_g=row(ln2_g[l]), ln2_b=row(ln2_b[l]),
    )

    tm = 256
    xp, hf, hb = _stream(x_prompt.reshape(1, batch * seq, D_MODEL), None, mods, 0, seq,
                         None, None, prm, tm)
    pos = _grid_pos_tables(dec_seq)
    h0f = state_ssd_fwd[:, l].reshape(dec_batch, D_SSD, SSD_STATE)
    h0b = state_ssd_bwd[:, l].reshape(dec_batch, D_SSD, SSD_STATE)
    xs, _, _ = _stream(x_sample, pos, mods, 1, dec_seq, h0f, h0b, prm, tm)

    state_shape = (batch, DEPTH, SSD_HEADS, SSD_HEAD_DIM, SSD_STATE)
    return (xp.reshape(batch, seq, D_MODEL), xs,
            hf.reshape(state_shape), hb.reshape(state_shape))
```

```python
import functools
import math

import jax
import jax.numpy as jnp
from jax import lax
from jax.experimental import pallas as pl
from jax.experimental.pallas import tpu as pltpu

D_MODEL = 1024
GRID_W = 64
POS_BASE = 10000.0
D_SSD = 1024
SSD_HEAD_DIM = 64
SSD_HEADS = D_SSD // SSD_HEAD_DIM
SSD_GROUPS = 2
SSD_STATE = 128
SSD_CHUNK = 128
SSD_CONV_DIM = D_SSD + 2 * SSD_GROUPS * SSD_STATE
D_SGU = 1024
SGU_GROUPS = 8
SGU_GROUP_DIM = D_SGU // SGU_GROUPS
SGU_CHUNK = 128
D_FF = 4 * D_MODEL
DEPTH = 1
DN_ALPHA = (2.0 * DEPTH) ** 0.25
EPS = 1e-5

LANES = 128
SUBLANES = 8
BF16_ROWS = 16
V7X_VMEM_MIB = 64
VMEM_MIB_SMALL = V7X_VMEM_MIB // 2
VMEM_MIB_FRONT = 48
VMEM_MIB_MAIN = 56
VMEM_MIB_FFN = 48

DT_GROUP = 2 * SSD_HEADS
DT_COPIES = 3
DT_PAD = LANES
HEAD_PAIRS = SSD_HEADS // 2
PAIRS_PER_GROUP = HEAD_PAIRS // SSD_GROUPS
GROUP_COLS = D_SSD // SSD_GROUPS
BC_COLS = SSD_GROUPS * SSD_STATE

BF16 = jnp.bfloat16
F32 = jnp.float32
HIGHEST = lax.Precision.HIGHEST


def _dot(a, b):
    return jnp.dot(a.astype(BF16), b.astype(BF16), preferred_element_type=F32)


def _dot_exact(a, b):
    return jnp.dot(a, b, precision=HIGHEST, preferred_element_type=F32)


def _layer_norm(x, g, b):
    mu = jnp.mean(x, axis=-1, keepdims=True)
    xc = x - mu
    var = jnp.mean(xc * xc, axis=-1, keepdims=True)
    return xc * lax.rsqrt(var + EPS) * g + b


def _softplus(x):
    return jnp.maximum(x, 0.0) + jnp.log1p(jnp.exp(-jnp.abs(x)))


def _gelu(x):
    return 0.5 * x * (1.0 + lax.erf(x * (1.0 / math.sqrt(2.0))))


def _sigmoid(x):
    return 0.5 * jnp.tanh(0.5 * x) + 0.5


def _silu(x):
    half = 0.5 * x
    return half * jnp.tanh(half) + half


def _const_spec(shape):
    zeros = (0,) * len(shape)
    return pl.BlockSpec(shape, lambda *_: zeros, pipeline_mode=pl.Buffered(1))


def _params(vmem_mib, n_axes):
    return pltpu.CompilerParams(
        dimension_semantics=("arbitrary",) * n_axes,
        vmem_limit_bytes=vmem_mib * 1024 * 1024)


def _ada_kernel(c_ref, w_ref, b_ref, o_ref):
    c = _silu(c_ref[...])
    c_hi = c.astype(BF16)
    c_lo = (c - c_hi.astype(F32)).astype(BF16)
    w = w_ref[...]
    w_hi = w.astype(BF16)
    w_lo = (w - w_hi.astype(F32)).astype(BF16)
    rows = c.shape[0]
    by_hi = jnp.dot(jnp.concatenate([c_hi, c_lo], axis=0), w_hi, preferred_element_type=F32)
    by_lo = jnp.dot(c_hi, w_lo, preferred_element_type=F32)
    o_ref[...] = by_hi[:rows] + by_hi[rows:] + by_lo + b_ref[...]


def _ada(cond, w_ada, b_ada):
    n = w_ada.shape[1]
    tn = 2048
    return pl.pallas_call(
        _ada_kernel,
        grid=(n // tn,),
        in_specs=[pl.BlockSpec((SUBLANES, D_MODEL), lambda j: (0, 0)),
                  pl.BlockSpec((D_MODEL, tn), lambda j: (0, j)),
                  pl.BlockSpec((1, tn), lambda j: (0, j))],
        out_specs=pl.BlockSpec((SUBLANES, tn), lambda j: (0, j)),
        out_shape=jax.ShapeDtypeStruct((SUBLANES, n), F32),
        compiler_params=_params(VMEM_MIB_SMALL, 1),
        name="ada",
    )(cond, w_ada, b_ada.reshape(1, n))


W_COL_TILE = 512


def _w_cols_kernel(src_ref, o_ref):
    o_ref[...] = src_ref[...].T.astype(BF16)


def _w_dt_kernel(src_ref, o_ref):
    t = src_ref[...].T
    lane = lax.broadcasted_iota(jnp.int32, t.shape, 1)
    dt = jnp.where(lane < DT_GROUP, t, 0.0)
    o_ref[...] = (dt + pltpu.roll(dt, DT_GROUP, 1) + pltpu.roll(dt, 2 * DT_GROUP, 1)).astype(BF16)


def _w_dt(w_t, col0):
    d = w_t.shape[1]
    return pl.pallas_call(
        _w_dt_kernel,
        grid=(1,),
        in_specs=[pl.BlockSpec((LANES, d), lambda i: (col0 // LANES, 0))],
        out_specs=pl.BlockSpec((d, LANES), lambda i: (0, 0)),
        out_shape=jax.ShapeDtypeStruct((d, LANES), BF16),
        name="w_dt",
    )(w_t)


def _w_cols(w_t, col_start, n_tiles):
    d = w_t.shape[1]
    return pl.pallas_call(
        _w_cols_kernel,
        grid=(n_tiles,),
        in_specs=[pl.BlockSpec((pl.Element(W_COL_TILE), pl.Element(d)),
                               lambda i: (col_start(i) * DT_GROUP, 0))],
        out_specs=pl.BlockSpec((d, W_COL_TILE), lambda i: (0, i)),
        out_shape=jax.ShapeDtypeStruct((d, n_tiles * W_COL_TILE), BF16),
        compiler_params=_params(VMEM_MIB_SMALL, 1),
        name="w_cols",
    )(w_t)


IN_NTILE = 512
HALO = BF16_ROWS


def _add_pos(x_ref, embr_ref, embc_ref, grid_row0):
    n_rows = x_ref.shape[1] // GRID_W
    col = embc_ref[...]
    blocks = []
    for i in range(n_rows):
        row = embr_ref[grid_row0 + i]
        row = jnp.concatenate([row] * (GRID_W // SUBLANES), axis=0)
        blocks.append(x_ref[0, i * GRID_W:(i + 1) * GRID_W, :]
                      + jnp.concatenate([row, col], axis=1))
    return jnp.concatenate(blocks, axis=0)


def _pos_halo(embr_ref, embc_ref, grid_row, col0):
    left = jnp.concatenate([embr_ref[grid_row]] * (HALO // SUBLANES), axis=0)
    return jnp.concatenate([left, embc_ref[col0:col0 + HALO, :]], axis=1)


def _block_diag_pair(xp):
    lane = lax.broadcasted_iota(jnp.int32, xp.shape, 1)
    first = lane < SSD_HEAD_DIM
    return jnp.concatenate([jnp.where(first, xp, 0.0), jnp.where(first, 0.0, xp)], axis=0)


def _split_terms(v):
    lane = lax.broadcasted_iota(jnp.int32, v.shape, 1)
    hi = v.astype(BF16).astype(F32)
    r1 = v - hi
    mid = r1.astype(BF16).astype(F32)
    lo = r1 - mid
    return jnp.where(lane < DT_GROUP, hi, jnp.where(lane < 2 * DT_GROUP, mid, lo)).astype(BF16)


def _expand(v, e_ref):
    return jnp.dot(_split_terms(v), e_ref[...], preferred_element_type=F32)


def _chunk_decay(dtp, dtbias_ref, aneg_ref, on_mxu=True):
    q = dtp.shape[0]
    dt_all = _softplus(dtp + dtbias_ref[...])
    a_all = dt_all * aneg_ref[...]
    if on_mxu:
        ii = lax.broadcasted_iota(jnp.int32, (q, q), 0)
        jj = lax.broadcasted_iota(jnp.int32, (q, q), 1)
        acum = _dot_exact((ii >= jj).astype(F32), a_all)
    else:
        row = lax.broadcasted_iota(jnp.int32, a_all.shape, 0)
        acum = a_all
        shift = 1
        while shift < q:
            acum = acum + jnp.where(row >= shift, pltpu.roll(acum, shift, 0), 0.0)
            shift *= 2
    atot = acum[q - 1:q, :]
    rcum = atot - acum + a_all
    return dt_all, a_all, acum, atot, rcum, rcum[0:1, :]


def _ssd_chunk(xh, b_all, c_all, dtp, hf_in, hb_in, dtbias_ref, aneg_ref, dskip_ref, ef_ref,
               between_pairs=lambda: None):
    q = SSD_CHUNK
    dt_all, a_all, acum, atot, rcum, _ = _chunk_decay(dtp, dtbias_ref, aneg_ref, on_mxu=False)
    ii = lax.broadcasted_iota(jnp.int32, (q, q), 0)
    jj = lax.broadcasted_iota(jnp.int32, (q, q), 1)
    lower = ii >= jj
    diag = ii == jj
    dt_t = dt_all.T
    acum_t = acum.T
    rcum_t = rcum.T

    off_f, off_b, cb = [], [], []
    for g in range(SSD_GROUPS):
        gs = slice(g * GROUP_COLS, (g + 1) * GROUP_COLS)
        c_g = c_all[:, g * SSD_STATE:(g + 1) * SSD_STATE].astype(BF16)
        b_g = b_all[:, g * SSD_STATE:(g + 1) * SSD_STATE].astype(BF16)
        cb.append(lax.dot_general(c_g, b_g, (((1,), (1,)), ((), ())),
                                  preferred_element_type=F32))
        off_f.append(jnp.dot(c_g, hf_in[:, gs].astype(BF16), preferred_element_type=F32))
        off_b.append(jnp.dot(c_g, hb_in[:, gs].astype(BF16), preferred_element_type=F32))
    off_f = jnp.concatenate(off_f, axis=1)
    off_b = jnp.concatenate(off_b, axis=1)

    first_head = lax.broadcasted_iota(jnp.int32, (q, LANES), 1) < SSD_HEAD_DIM
    w_tok = dt_all * jnp.exp(atot - acum)
    y_cols, wx = [], []
    for pr in range(HEAD_PAIRS):
        g = pr // PAIRS_PER_GROUP
        m_parts, col_f, col_b, col_w = [], [], [], []
        for h in (2 * pr, 2 * pr + 1):
            hb_lane = SSD_HEADS + h
            colf = jnp.broadcast_to(acum[:, h:h + 1], (q, q))
            colb = jnp.broadcast_to(rcum[:, hb_lane:hb_lane + 1], (q, q))
            col_w.append(jnp.broadcast_to(w_tok[:, h:h + 1], (q, q)))
            seg = jnp.where(lower, colf - acum_t[h:h + 1, :],
                            colb - rcum_t[hb_lane:hb_lane + 1, :])
            dt_b_row = dt_t[hb_lane:hb_lane + 1, :]
            w = (jnp.exp(seg) * jnp.where(lower, dt_t[h:h + 1, :], dt_b_row)
                 + jnp.where(diag, dt_b_row, 0.0))
            m_parts.append(cb[g] * w)
            col_f.append(colf)
            col_b.append(colb)
        sl = slice(pr * LANES, (pr + 1) * LANES)
        xp = xh[:, sl]
        y_cols.append(_dot(jnp.concatenate(m_parts, axis=1), _block_diag_pair(xp))
                      + jnp.exp(jnp.where(first_head, col_f[0], col_f[1])) * off_f[:, sl]
                      + jnp.exp(jnp.where(first_head, col_b[0], col_b[1])) * off_b[:, sl]
                      + dskip_ref[:, sl] * xp)
        wx.append((jnp.where(first_head, col_w[0], col_w[1]) * xp).astype(BF16))
        between_pairs()
    y = jnp.concatenate(y_cols, axis=1)

    wx = jnp.concatenate(wx, axis=1)
    s = jnp.concatenate(
        [jnp.dot(b_all[:, g * SSD_STATE:(g + 1) * SSD_STATE].T.astype(BF16),
                 wx[:, g * GROUP_COLS:(g + 1) * GROUP_COLS], preferred_element_type=F32)
         for g in range(SSD_GROUPS)], axis=1)
    decay = jnp.exp(_expand(jnp.broadcast_to(atot, (BF16_ROWS, LANES)), ef_ref))[0:1, :]
    return y, decay * hf_in + s


def _tiling(x, tm, seq_len):
    s_n, l_n, _ = x.shape
    return s_n, l_n, l_n // tm, seq_len // tm, l_n // seq_len


FRONT_SEGS = (SSD_CONV_DIM, DT_PAD)
FRONT_NTILE = 256


def _front_kernel(*refs, has_pos, has_h0, tiles_per_seq, n_tiles):
    x_ref, xp_ref, xn_ref = refs[:3]
    refs = refs[3:]
    if has_pos:
        embr_ref, embc_ref = refs[:2]
        refs = refs[2:]
    if has_h0:
        h0b_ref = refs[0]
        refs = refs[1:]
    mod_ref, w_ref, wdt_ref, convw_ref, convb_ref, dtbias_ref, aneg_ref, eb_ref = refs[:8]
    xa_ref, dt_ref, hbs_ref, hfb_ref, hb_s, xb_s, dts_s = refs[8:]
    shift1 = mod_ref[0, :, 0:D_MODEL]
    scale1 = mod_ref[0, :, D_MODEL:2 * D_MODEL]
    tm = x_ref.shape[1]
    n_ext = tm + 2 * HALO
    q = SSD_CHUNK
    j = pl.program_id(1)
    tile = n_tiles - 1 - jnp.minimum(j, n_tiles - 1)
    t = tile % tiles_per_seq
    t_scan = jnp.minimum(n_tiles - j, n_tiles - 1) % tiles_per_seq
    slot = j % 2
    scan_slot = 1 - slot

    def init_state():
        if has_h0:
            hb_s[...] = h0b_ref[0].T
        else:
            hb_s[...] = jnp.zeros_like(hb_s)

    def modulate(x):
        return (x * (1.0 + scale1) + shift1).astype(BF16)

    def project():
        if has_pos:
            rows_per_tile = tm // GRID_W
            n_grid_rows = embr_ref.shape[0]
            r0 = tile * rows_per_tile
            x = _add_pos(x_ref, embr_ref, embc_ref, r0)
            x_prev = xp_ref[0] + _pos_halo(embr_ref, embc_ref, jnp.maximum(r0 - 1, 0),
                                           GRID_W - HALO)
            x_next = xn_ref[0] + _pos_halo(embr_ref, embc_ref,
                                           jnp.minimum(r0 + rows_per_tile, n_grid_rows - 1), 0)
        else:
            x, x_prev, x_next = x_ref[0], xp_ref[0], xn_ref[0]
        h = modulate(x)
        h_ext = jnp.concatenate([modulate(x_prev), h, modulate(x_next)], axis=0)

        row = lax.broadcasted_iota(jnp.int32, (tm, 1), 0)
        zero_prev = row == 0
        zero_next = row == tm - 1
        if tiles_per_seq > 1:
            zero_prev = jnp.logical_and(zero_prev, t == 0)
            zero_next = jnp.logical_and(zero_next, t == tiles_per_seq - 1)

        dt = jnp.dot(h, wdt_ref[...], preferred_element_type=F32)
        dt_ref[0] = dt
        dts_s[slot] = dt
        yield
        for c0 in range(0, SSD_CONV_DIM, FRONT_NTILE):
            c1 = c0 + FRONT_NTILE
            p = jnp.dot(h_ext, w_ref[:, c0:c1], preferred_element_type=F32)
            cur = p[HALO:HALO + tm]
            prv = jnp.where(zero_prev, 0.0, pltpu.roll(p, 1, 0)[HALO:HALO + tm])
            nxt = jnp.where(zero_next, 0.0, pltpu.roll(p, n_ext - 1, 0)[HALO:HALO + tm])
            xc = (convw_ref[0:1, c0:c1] * prv + convw_ref[1:2, c0:c1] * cur
                  + convw_ref[2:3, c0:c1] * nxt + convb_ref[:, c0:c1])
            xa = _silu(xc)
            xa_ref[0, :, c0:c1] = xa
            if c0 < D_SSD + BC_COLS:
                xb_s[slot, :, c0:c1] = xa
            yield

    def scan():
        h_state = hb_s[...]
        for ch in reversed(range(tm // q)):
            rows = slice(ch * q, (ch + 1) * q)
            hbs_ref[0, ch] = h_state.astype(BF16)
            dt_all, _, _, _, rcum, rtot = _chunk_decay(dts_s[scan_slot, rows, :], dtbias_ref,
                                                       aneg_ref)
            yield
            w_exp = _expand(dt_all * jnp.exp(rtot - rcum), eb_ref)
            decay = jnp.exp(_expand(jnp.broadcast_to(rtot, (BF16_ROWS, LANES)), eb_ref))[0:1, :]
            yield
            wx = (w_exp * xb_s[scan_slot, rows, 0:D_SSD]).astype(BF16)
            b_all = xb_s[scan_slot, rows, D_SSD:D_SSD + BC_COLS]
            parts = []
            for g in range(SSD_GROUPS):
                parts.append(jnp.dot(b_all[:, g * SSD_STATE:(g + 1) * SSD_STATE].T.astype(BF16),
                                     wx[:, g * GROUP_COLS:(g + 1) * GROUP_COLS],
                                     preferred_element_type=F32))
                yield
            h_state = decay * h_state + jnp.concatenate(parts, axis=1)
        hb_s[...] = h_state

        def final_state():
            hfb_ref[0] = h_state.T

        if tiles_per_seq > 1:
            pl.when(t_scan == 0)(final_state)
        else:
            final_state()

    def run(*stage_fns):
        if scan in stage_fns:
            if tiles_per_seq > 1:
                pl.when(t_scan == tiles_per_seq - 1)(init_state)
            else:
                init_state()
        stages = [fn() for fn in stage_fns]
        while stages:
            for gen in list(stages):
                if next(gen, StopIteration) is StopIteration:
                    stages.remove(gen)

    pl.when(j == 0)(functools.partial(run, project))
    pl.when(jnp.logical_and(j > 0, j < n_tiles))(functools.partial(run, project, scan))
    pl.when(j == n_tiles)(functools.partial(run, scan))


def _front(x, pos, mods, h0b, w_xbc, w_dt, convw, convb, dtbias, aneg, e_b, mod_row0, tm,
           seq_len):
    s_n, l_n, n_tiles, tiles_per_seq, seqs_per_row = _tiling(x, tm, seq_len)
    has_pos = pos is not None
    has_h0 = h0b is not None
    hb = tm // HALO
    n_hb = l_n // HALO
    cpt = tm // SSD_CHUNK
    fwd = lambda j: n_tiles - 1 - jnp.minimum(j, n_tiles - 1)
    scanned = lambda j: jnp.minimum(n_tiles - j, n_tiles - 1)
    tok = lambda s, j: (s, fwd(j), 0)
    seq_of = lambda s, j: (s * seqs_per_row + scanned(j) // tiles_per_seq, 0, 0)
    in_specs = [pl.BlockSpec((1, tm, D_MODEL), tok),
                pl.BlockSpec((1, HALO, D_MODEL),
                             lambda s, j: (s, jnp.maximum(fwd(j) * hb - 1, 0), 0)),
                pl.BlockSpec((1, HALO, D_MODEL),
                             lambda s, j: (s, jnp.minimum((fwd(j) + 1) * hb, n_hb - 1), 0))]
    args = [x, x, x]
    if has_pos:
        in_specs += [_const_spec(pos[0].shape), _const_spec(pos[1].shape)]
        args += list(pos)
    state_spec = pl.BlockSpec((1, D_SSD, SSD_STATE), seq_of)
    if has_h0:
        in_specs.append(state_spec)
        args.append(h0b)
    in_specs.append(pl.BlockSpec((1, 1, 6 * D_MODEL), lambda s, j: (mod_row0 + s, 0, 0)))
    args.append(mods)
    for a in (w_xbc, w_dt, convw, convb, dtbias, aneg, e_b):
        in_specs.append(_const_spec(a.shape))
        args.append(a)
    return pl.pallas_call(
        functools.partial(_front_kernel, has_pos=has_pos, has_h0=has_h0,
                          tiles_per_seq=tiles_per_seq, n_tiles=n_tiles),
        grid=(s_n, n_tiles + 1),
        in_specs=in_specs,
        out_specs=[pl.BlockSpec((1, tm, w), tok) for w in FRONT_SEGS]
        + [pl.BlockSpec((1, cpt, SSD_STATE, D_SSD), lambda s, j: (s, scanned(j), 0, 0)),
           state_spec],
        out_shape=[jax.ShapeDtypeStruct((s_n, l_n, w), F32) for w in FRONT_SEGS]
        + [jax.ShapeDtypeStruct((s_n, l_n // SSD_CHUNK, SSD_STATE, D_SSD), BF16),
           jax.ShapeDtypeStruct((s_n * seqs_per_row, D_SSD, SSD_STATE), F32)],
        scratch_shapes=[pltpu.VMEM((SSD_STATE, D_SSD), F32),
                        pltpu.VMEM((2, tm, D_SSD + BC_COLS), F32),
                        pltpu.VMEM((2, tm, DT_PAD), F32)],
        compiler_params=_params(VMEM_MIB_FRONT, 2),
        name="front",
    )(*args)


MAIN_Z, MAIN_U, MAIN_V, MAIN_GA, MAIN_GB = (i * D_MODEL for i in range(5))
MAIN_COLS = 5 * D_MODEL


def _main_kernel(*refs, has_pos, has_h0, tiles_per_seq):
    x_ref, xa_ref, dtp_ref, hbs_ref = refs[:4]
    refs = refs[4:]
    if has_pos:
        embr_ref, embc_ref = refs[:2]
        refs = refs[2:]
    if has_h0:
        h0f_ref = refs[0]
        refs = refs[1:]
    (mod_ref, w_ref, dtbias_ref, aneg_ref, dskip_ref, normw_ref, ef_ref, lng_ref, lnb_ref,
     wsp_ref, bsp_ref, wpa_ref, wpb_ref, wout_ref, ln1g_ref, ln1b_ref) = refs[:16]
    o_ref, hff_ref, hf_s = refs[16:]
    tm = x_ref.shape[1]
    q = SSD_CHUNK
    tile = pl.program_id(1)
    t = tile % tiles_per_seq

    if has_pos:
        x = _add_pos(x_ref, embr_ref, embc_ref, tile * (tm // GRID_W))
    else:
        x = x_ref[0]
    shift1 = mod_ref[0, :, 0:D_MODEL]
    scale1 = mod_ref[0, :, D_MODEL:2 * D_MODEL]
    gate1 = mod_ref[0, :, 2 * D_MODEL:3 * D_MODEL]
    h = (x * (1.0 + scale1) + shift1).astype(BF16)

    col_tiles = {}
    pending = list(range(0, MAIN_COLS, IN_NTILE))
    n_slots = (tm // q) * HEAD_PAIRS
    n_fill = len(pending)
    slot = [0]

    def project_next():
        c = pending.pop(0)
        act = _silu if c < MAIN_U else (_gelu if c < MAIN_GA else _sigmoid)
        col_tiles[c] = act(jnp.dot(h, w_ref[:, c:c + IN_NTILE], preferred_element_type=F32))

    def between_pairs():
        slot[0] += 1
        done = n_fill - len(pending)
        for _ in range(-(-slot[0] * n_fill // n_slots) - done):
            project_next()

    def in_proj(c0):
        while pending and pending[0] < c0 + D_MODEL:
            project_next()
        return jnp.concatenate([col_tiles.pop(c) for c in range(c0, c0 + D_MODEL, IN_NTILE)],
                               axis=1)

    def init_state():
        if has_h0:
            hf_s[...] = h0f_ref[0].T
        else:
            hf_s[...] = jnp.zeros_like(hf_s)

    if tiles_per_seq > 1:
        pl.when(t == 0)(init_state)
    else:
        init_state()
    h_state = hf_s[...]
    ya = []
    for ch in range(tm // q):
        rows = slice(ch * q, (ch + 1) * q)
        y, h_state = _ssd_chunk(
            xa_ref[0, rows, 0:D_SSD], xa_ref[0, rows, D_SSD:D_SSD + BC_COLS],
            xa_ref[0, rows, D_SSD + BC_COLS:], dtp_ref[0, rows, :], h_state, hbs_ref[0, ch],
            dtbias_ref, aneg_ref, dskip_ref, ef_ref, between_pairs)
        if ch == 0:
            zz = in_proj(MAIN_Z)
        yg = y * zz[rows]
        ms = jnp.mean(yg * yg, axis=-1, keepdims=True)
        ya.append(yg * lax.rsqrt(ms + EPS) * normw_ref[...])
    hf_s[...] = h_state

    def final_state():
        hff_ref[0] = h_state.T

    if tiles_per_seq > 1:
        pl.when(t == tiles_per_seq - 1)(final_state)
    else:
        final_state()
    ya = jnp.concatenate(ya, axis=0)
    proj_a = _dot(ya, wpa_ref[...])

    u = in_proj(MAIN_U)
    v = _layer_norm(in_proj(MAIN_V), lng_ref[...], lnb_ref[...])
    n_ch = tm // SGU_CHUNK
    vb = v.astype(BF16)
    per_group = []
    for g in range(SGU_GROUPS):
        gs = slice(g * SGU_GROUP_DIM, (g + 1) * SGU_GROUP_DIM)
        rhs = jnp.concatenate([vb[ch * SGU_CHUNK:(ch + 1) * SGU_CHUNK, gs] for ch in range(n_ch)],
                              axis=1)
        per_group.append(jnp.dot(wsp_ref[g], rhs, preferred_element_type=F32))
    rows_out = []
    for ch in range(n_ch):
        cs = slice(ch * SGU_GROUP_DIM, (ch + 1) * SGU_GROUP_DIM)
        rows_out.append(jnp.concatenate([p[:, cs] for p in per_group], axis=1) + bsp_ref[...])
    yb = u * jnp.concatenate(rows_out, axis=0)

    merged = in_proj(MAIN_GA) * proj_a + in_proj(MAIN_GB) * _dot(yb, wpb_ref[...])
    o = _dot(merged, wout_ref[...])
    o_ref[0] = _layer_norm(DN_ALPHA * x + gate1 * o, ln1g_ref[...], ln1b_ref[...])


def _main(x, pos, xa, dtp, hbs, h0f, mods, mod_row0, tm, seq_len, consts):
    s_n, l_n, n_tiles, tiles_per_seq, seqs_per_row = _tiling(x, tm, seq_len)
    has_pos = pos is not None
    has_h0 = h0f is not None
    cpt = tm // SSD_CHUNK
    tok = lambda s, j: (s, j, 0)
    in_specs = [pl.BlockSpec((1, tm, D_MODEL), tok),
                pl.BlockSpec((1, tm, SSD_CONV_DIM), tok),
                pl.BlockSpec((1, tm, DT_PAD), tok),
                pl.BlockSpec((1, cpt, SSD_STATE, D_SSD), lambda s, j: (s, j, 0, 0))]
    args = [x, xa, dtp, hbs]
    if has_pos:
        in_specs += [_const_spec(pos[0].shape), _const_spec(pos[1].shape)]
        args += list(pos)
    state_spec = pl.BlockSpec((1, D_SSD, SSD_STATE),
                              lambda s, j: (s * seqs_per_row + j // tiles_per_seq, 0, 0))
    if has_h0:
        in_specs.append(state_spec)
        args.append(h0f)
    in_specs.append(pl.BlockSpec((1, 1, 6 * D_MODEL), lambda s, j: (mod_row0 + s, 0, 0)))
    args.append(mods)
    for a in consts:
        in_specs.append(_const_spec(a.shape))
        args.append(a)
    return pl.pallas_call(
        functools.partial(_main_kernel, has_pos=has_pos, has_h0=has_h0,
                          tiles_per_seq=tiles_per_seq),
        grid=(s_n, n_tiles),
        in_specs=in_specs,
        out_specs=[pl.BlockSpec((1, tm, D_MODEL), tok), state_spec],
        out_shape=[jax.ShapeDtypeStruct((s_n, l_n, D_MODEL), F32),
                   jax.ShapeDtypeStruct((s_n * seqs_per_row, D_SSD, SSD_STATE), F32)],
        scratch_shapes=[pltpu.VMEM((SSD_STATE, D_SSD), F32)],
        compiler_params=_params(VMEM_MIB_MAIN, 2),
        name="main",
    )(*args)


FF_TILE = 1024
FFN_TM = 1024


def _ffn_kernel(x_ref, mod_ref, w1_ref, w2_ref, g_ref, b_ref, o_ref):
    x = x_ref[0]
    shift2 = mod_ref[0, :, 3 * D_MODEL:4 * D_MODEL]
    scale2 = mod_ref[0, :, 4 * D_MODEL:5 * D_MODEL]
    gate2 = mod_ref[0, :, 5 * D_MODEL:6 * D_MODEL]
    h = (x * (1.0 + scale2) + shift2).astype(BF16)
    f = jnp.zeros_like(x)
    for k0 in range(0, D_FF, FF_TILE):
        a = jnp.dot(h, w1_ref[:, k0:k0 + FF_TILE], preferred_element_type=F32)
        a = jnp.maximum(a, 0.0)
        f = f + _dot(a * a, w2_ref[k0:k0 + FF_TILE, :])
    o_ref[0] = _layer_norm(DN_ALPHA * x + gate2 * f, g_ref[...], b_ref[...])


def _ffn(x, mods, mod_row0, tm, w1, w2, g, b):
    s_n, l_n, _ = x.shape
    tok = lambda s, j: (s, j, 0)
    return pl.pallas_call(
        _ffn_kernel,
        grid=(s_n, l_n // tm),
        in_specs=[pl.BlockSpec((1, tm, D_MODEL), tok),
                  pl.BlockSpec((1, 1, 6 * D_MODEL), lambda s, j: (mod_row0 + s, 0, 0)),
                  _const_spec(w1.shape), _const_spec(w2.shape),
                  _const_spec(g.shape), _const_spec(b.shape)],
        out_specs=pl.BlockSpec((1, tm, D_MODEL), tok),
        out_shape=jax.ShapeDtypeStruct((s_n, l_n, D_MODEL), F32),
        compiler_params=_params(VMEM_MIB_FFN, 2),
        name="ffn",
    )(x, mods, w1, w2, g, b)


def _grid_pos_tables(n_tokens):
    rows = n_tokens // GRID_W
    quarter = D_MODEL // 4
    omega = 1.0 / (POS_BASE ** (jnp.arange(quarter, dtype=F32) / quarter))
    r = jnp.arange(rows, dtype=F32)[:, None] * omega
    col = jnp.arange(GRID_W, dtype=F32)[:, None] * omega
    emb_r = jnp.concatenate([jnp.sin(r), jnp.cos(r)], axis=-1)
    emb_c = jnp.concatenate([jnp.sin(col), jnp.cos(col)], axis=-1)
    return jnp.broadcast_to(emb_r[:, None, :], (rows, SUBLANES, D_MODEL // 2)), emb_c


def _dt_lanes(v, axis):
    pad = [(0, 0)] * v.ndim
    pad[axis] = (0, DT_PAD - DT_COPIES * DT_GROUP)
    return jnp.pad(jnp.concatenate([v] * DT_COPIES, axis=axis), pad)


def _head_expanders():
    lane = jnp.arange(LANES)[:, None]
    head = jnp.arange(D_SSD)[None, :] // SSD_HEAD_DIM
    valid = lane < DT_COPIES * DT_GROUP
    e_f = jnp.logical_and(valid, lane % DT_GROUP == head)
    e_b = jnp.logical_and(valid, lane % DT_GROUP == head + SSD_HEADS)
    return e_f.astype(BF16), e_b.astype(BF16)


def _stream(x, pos, mods, mod_row0, seq_len, h0f, h0b, prm, tm):
    xa, dtp, hbs, hb = _front(x, pos, mods, h0b, prm["w_xbc"], prm["w_dt"], prm["conv_w"],
                              prm["conv_b"],
                              prm["dt_bias"], prm["a_neg"], prm["e_b"], mod_row0, tm, seq_len)
    x1, hf = _main(x, pos, xa, dtp, hbs, h0f, mods, mod_row0, tm, seq_len, prm["main"])
    y = _ffn(x1, mods, mod_row0, FFN_TM, prm["w_ff1"], prm["w_ff2"], prm["ln2_g"], prm["ln2_b"])
    return y, hf, hb


def kernel(x_prompt, x_sample, state_ssd_fwd, state_ssd_bwd, c, c_ctx, w_ada, b_ada, w_in, conv_w, conv_b, dt_bias_fwd, dt_bias_bwd, a_log_fwd, a_log_bwd, d_skip, ssd_norm_w, sgu_ln_g, sgu_ln_b, w_spatial, b_spatial, w_proj_a, w_proj_b, w_out, ln1_g, ln1_b, w_ff1, w_ff2, ln2_g, ln2_b):
    batch, seq, _ = x_prompt.shape
    dec_batch, dec_seq, _ = x_sample.shape
    l = 0
    row = lambda v: v.reshape(1, -1)

    cond = jnp.zeros((SUBLANES, D_MODEL), F32).at[0].set(c_ctx).at[1:1 + dec_batch].set(c)
    mods = _ada(cond, w_ada[l], b_ada[l]).reshape(SUBLANES, 1, 6 * D_MODEL)

    o1 = D_SSD
    o2 = o1 + SSD_CONV_DIM
    o3 = o2 + DT_GROUP
    w_t = jnp.swapaxes(w_in[l], 0, 1)
    step = W_COL_TILE // DT_GROUP
    w_xbc = _w_cols(w_t, lambda i: o1 // DT_GROUP + i * step, SSD_CONV_DIM // W_COL_TILE)
    z_tiles = o1 // W_COL_TILE
    w_main = _w_cols(
        w_t, lambda i: jnp.where(i < z_tiles, i * step, o3 // DT_GROUP + (i - z_tiles) * step),
        MAIN_COLS // W_COL_TILE)
    assert o2 % LANES == 0
    w_dt = _w_dt(w_t, o2)
    e_f, e_b = _head_expanders()
    dt_bias = row(_dt_lanes(jnp.concatenate([dt_bias_fwd[l], dt_bias_bwd[l]]), 0))
    a_neg = row(_dt_lanes(-jnp.exp(jnp.concatenate([a_log_fwd[l], a_log_bwd[l]])), 0))

    prm = dict(
        w_xbc=w_xbc, w_dt=w_dt, conv_w=conv_w[l], conv_b=row(conv_b[l]),
        dt_bias=dt_bias, a_neg=a_neg, e_b=e_b,
        main=(w_main, dt_bias, a_neg, row(jnp.repeat(d_skip[l], SSD_HEAD_DIM)),
              row(ssd_norm_w[l]), e_f, row(sgu_ln_g[l]), row(sgu_ln_b[l]),
              w_spatial[l].astype(BF16), jnp.repeat(b_spatial[l].T, SGU_GROUP_DIM, axis=1),
              w_proj_a[l].astype(BF16), w_proj_b[l].astype(BF16), w_out[l].astype(BF16),
              row(ln1_g[l]), row(ln1_b[l])),
        w_ff1=w_ff1[l].astype(BF16), w_ff2=w_ff2[l].astype(BF16),
        ln2_g=row(ln2_g[l]), ln2_b=row(ln2_b[l]),
    )

    tm = 256
    xp, hf, hb = _stream(x_prompt.reshape(1, batch * seq, D_MODEL), None, mods, 0, seq,
                         None, None, prm, tm)
    pos = _grid_pos_tables(dec_seq)
    h0f = state_ssd_fwd[:, l].reshape(dec_batch, D_SSD, SSD_STATE)
    h0b = state_ssd_bwd[:, l].reshape(dec_batch, D_SSD, SSD_STATE)
    xs, _, _ = _stream(x_sample, pos, mods, 1, dec_seq, h0f, h0b, prm, tm)

    state_shape = (batch, DEPTH, SSD_HEADS, SSD_HEAD_DIM, SSD_STATE)
    return (xp.reshape(batch, seq, D_MODEL), xs,
            hf.reshape(state_shape), hb.reshape(state_shape))
```

```python
import functools
import math

import jax
import jax.numpy as jnp
from jax import lax
from jax.experimental import pallas as pl
from jax.experimental.pallas import tpu as pltpu

D_MODEL = 1024
GRID_W = 64
POS_BASE = 10000.0
D_SSD = 1024
SSD_HEAD_DIM = 64
SSD_HEADS = D_SSD // SSD_HEAD_DIM
SSD_GROUPS = 2
SSD_STATE = 128
SSD_CHUNK = 128
SSD_CONV_DIM = D_SSD + 2 * SSD_GROUPS * SSD_STATE
D_SGU = 1024
SGU_GROUPS = 8
SGU_GROUP_DIM = D_SGU // SGU_GROUPS
SGU_CHUNK = 128
D_FF = 4 * D_MODEL
DEPTH = 1
DN_ALPHA = (2.0 * DEPTH) ** 0.25
EPS = 1e-5

LANES = 128
SUBLANES = 8
BF16_ROWS = 16
V7X_VMEM_MIB = 64
VMEM_MIB_SMALL = V7X_VMEM_MIB // 2
VMEM_MIB_FRONT = 48
VMEM_MIB_MAIN = 56
VMEM_MIB_FFN = 48

DT_GROUP = 2 * SSD_HEADS
DT_COPIES = 3
DT_PAD = LANES
HEAD_PAIRS = SSD_HEADS // 2
PAIRS_PER_GROUP = HEAD_PAIRS // SSD_GROUPS
GROUP_COLS = D_SSD // SSD_GROUPS
BC_COLS = SSD_GROUPS * SSD_STATE

BF16 = jnp.bfloat16
F32 = jnp.float32
HIGHEST = lax.Precision.HIGHEST


def _dot(a, b):
    return jnp.dot(a.astype(BF16), b.astype(BF16), preferred_element_type=F32)


def _dot_exact(a, b):
    return jnp.dot(a, b, precision=HIGHEST, preferred_element_type=F32)


def _layer_norm(x, g, b):
    mu = jnp.mean(x, axis=-1, keepdims=True)
    xc = x - mu
    var = jnp.mean(xc * xc, axis=-1, keepdims=True)
    return xc * lax.rsqrt(var + EPS) * g + b


def _softplus(x):
    return jnp.maximum(x, 0.0) + jnp.log1p(jnp.exp(-jnp.abs(x)))


def _gelu(x):
    return 0.5 * x * (1.0 + lax.erf(x * (1.0 / math.sqrt(2.0))))


def _sigmoid(x):
    return 0.5 * jnp.tanh(0.5 * x) + 0.5


def _silu(x):
    half = 0.5 * x
    return half * jnp.tanh(half) + half


def _const_spec(shape):
    zeros = (0,) * len(shape)
    return pl.BlockSpec(shape, lambda *_: zeros, pipeline_mode=pl.Buffered(1))


def _params(vmem_mib, n_axes):
    return pltpu.CompilerParams(
        dimension_semantics=("arbitrary",) * n_axes,
        vmem_limit_bytes=vmem_mib * 1024 * 1024)


def _ada_kernel(c_ref, w_ref, b_ref, o_ref):
    c = _silu(c_ref[...])
    c_hi = c.astype(BF16)
    c_lo = (c - c_hi.astype(F32)).astype(BF16)
    w = w_ref[...]
    w_hi = w.astype(BF16)
    w_lo = (w - w_hi.astype(F32)).astype(BF16)
    rows = c.shape[0]
    by_hi = jnp.dot(jnp.concatenate([c_hi, c_lo], axis=0), w_hi, preferred_element_type=F32)
    by_lo = jnp.dot(c_hi, w_lo, preferred_element_type=F32)
    o_ref[...] = by_hi[:rows] + by_hi[rows:] + by_lo + b_ref[...]


def _ada(cond, w_ada, b_ada):
    n = w_ada.shape[1]
    tn = 1024
    return pl.pallas_call(
        _ada_kernel,
        grid=(n // tn,),
        in_specs=[pl.BlockSpec((SUBLANES, D_MODEL), lambda j: (0, 0)),
                  pl.BlockSpec((D_MODEL, tn), lambda j: (0, j)),
                  pl.BlockSpec((1, tn), lambda j: (0, j))],
        out_specs=pl.BlockSpec((SUBLANES, tn), lambda j: (0, j)),
        out_shape=jax.ShapeDtypeStruct((SUBLANES, n), F32),
        compiler_params=_params(VMEM_MIB_SMALL, 1),
        name="ada",
    )(cond, w_ada, b_ada.reshape(1, n))


W_COL_TILE = 512


def _w_cols_kernel(src_ref, o_ref):
    o_ref[...] = src_ref[...].T.astype(BF16)


def _w_dt_kernel(src_ref, o_ref):
    t = src_ref[...].T
    lane = lax.broadcasted_iota(jnp.int32, t.shape, 1)
    dt = jnp.where(lane < DT_GROUP, t, 0.0)
    o_ref[...] = (dt + pltpu.roll(dt, DT_GROUP, 1) + pltpu.roll(dt, 2 * DT_GROUP, 1)).astype(BF16)


def _w_dt(w_t, col0):
    d = w_t.shape[1]
    return pl.pallas_call(
        _w_dt_kernel,
        grid=(1,),
        in_specs=[pl.BlockSpec((LANES, d), lambda i: (col0 // LANES, 0))],
        out_specs=pl.BlockSpec((d, LANES), lambda i: (0, 0)),
        out_shape=jax.ShapeDtypeStruct((d, LANES), BF16),
        name="w_dt",
    )(w_t)


def _w_cols(w_t, col_start, n_tiles):
    d = w_t.shape[1]
    return pl.pallas_call(
        _w_cols_kernel,
        grid=(n_tiles,),
        in_specs=[pl.BlockSpec((pl.Element(W_COL_TILE), pl.Element(d)),
                               lambda i: (col_start(i) * DT_GROUP, 0))],
        out_specs=pl.BlockSpec((d, W_COL_TILE), lambda i: (0, i)),
        out_shape=jax.ShapeDtypeStruct((d, n_tiles * W_COL_TILE), BF16),
        compiler_params=_params(VMEM_MIB_SMALL, 1),
        name="w_cols",
    )(w_t)


IN_NTILE = 512
HALO = BF16_ROWS


def _add_pos(x_ref, embr_ref, embc_ref, grid_row0):
    n_rows = x_ref.shape[1] // GRID_W
    col = embc_ref[...]
    blocks = []
    for i in range(n_rows):
        row = embr_ref[grid_row0 + i]
        row = jnp.concatenate([row] * (GRID_W // SUBLANES), axis=0)
        blocks.append(x_ref[0, i * GRID_W:(i + 1) * GRID_W, :]
                      + jnp.concatenate([row, col], axis=1))
    return jnp.concatenate(blocks, axis=0)


def _pos_halo(embr_ref, embc_ref, grid_row, col0):
    left = jnp.concatenate([embr_ref[grid_row]] * (HALO // SUBLANES), axis=0)
    return jnp.concatenate([left, embc_ref[col0:col0 + HALO, :]], axis=1)


def _block_diag_pair(xp):
    lane = lax.broadcasted_iota(jnp.int32, xp.shape, 1)
    first = lane < SSD_HEAD_DIM
    return jnp.concatenate([jnp.where(first, xp, 0.0), jnp.where(first, 0.0, xp)], axis=0)


def _split_terms(v):
    lane = lax.broadcasted_iota(jnp.int32, v.shape, 1)
    hi = v.astype(BF16).astype(F32)
    r1 = v - hi
    mid = r1.astype(BF16).astype(F32)
    lo = r1 - mid
    return jnp.where(lane < DT_GROUP, hi, jnp.where(lane < 2 * DT_GROUP, mid, lo)).astype(BF16)


def _expand(v, e_ref):
    return jnp.dot(_split_terms(v), e_ref[...], preferred_element_type=F32)


def _chunk_decay(dtp, dtbias_ref, aneg_ref, on_mxu=True):
    q = dtp.shape[0]
    dt_all = _softplus(dtp + dtbias_ref[...])
    a_all = dt_all * aneg_ref[...]
    if on_mxu:
        ii = lax.broadcasted_iota(jnp.int32, (q, q), 0)
        jj = lax.broadcasted_iota(jnp.int32, (q, q), 1)
        acum = _dot_exact((ii >= jj).astype(F32), a_all)
    else:
        row = lax.broadcasted_iota(jnp.int32, a_all.shape, 0)
        acum = a_all
        shift = 1
        while shift < q:
            acum = acum + jnp.where(row >= shift, pltpu.roll(acum, shift, 0), 0.0)
            shift *= 2
    atot = acum[q - 1:q, :]
    rcum = atot - acum + a_all
    return dt_all, a_all, acum, atot, rcum, rcum[0:1, :]


def _ssd_chunk(xh, b_all, c_all, dtp, hf_in, hb_in, dtbias_ref, aneg_ref, dskip_ref, ef_ref,
               between_pairs=lambda: None):
    q = SSD_CHUNK
    dt_all, a_all, acum, atot, rcum, _ = _chunk_decay(dtp, dtbias_ref, aneg_ref, on_mxu=False)
    ii = lax.broadcasted_iota(jnp.int32, (q, q), 0)
    jj = lax.broadcasted_iota(jnp.int32, (q, q), 1)
    lower = ii >= jj
    diag = ii == jj
    dt_t = dt_all.T
    acum_t = acum.T
    rcum_t = rcum.T

    off_f, off_b, cb = [], [], []
    for g in range(SSD_GROUPS):
        gs = slice(g * GROUP_COLS, (g + 1) * GROUP_COLS)
        c_g = c_all[:, g * SSD_STATE:(g + 1) * SSD_STATE].astype(BF16)
        b_g = b_all[:, g * SSD_STATE:(g + 1) * SSD_STATE].astype(BF16)
        cb.append(lax.dot_general(c_g, b_g, (((1,), (1,)), ((), ())),
                                  preferred_element_type=F32))
        off_f.append(jnp.dot(c_g, hf_in[:, gs].astype(BF16), preferred_element_type=F32))
        off_b.append(jnp.dot(c_g, hb_in[:, gs].astype(BF16), preferred_element_type=F32))
    off_f = jnp.concatenate(off_f, axis=1)
    off_b = jnp.concatenate(off_b, axis=1)

    first_head = lax.broadcasted_iota(jnp.int32, (q, LANES), 1) < SSD_HEAD_DIM
    w_tok = dt_all * jnp.exp(atot - acum)
    y_cols, wx = [], []
    for pr in range(HEAD_PAIRS):
        g = pr // PAIRS_PER_GROUP
        m_parts, col_f, col_b, col_w = [], [], [], []
        for h in (2 * pr, 2 * pr + 1):
            hb_lane = SSD_HEADS + h
            colf = jnp.broadcast_to(acum[:, h:h + 1], (q, q))
            colb = jnp.broadcast_to(rcum[:, hb_lane:hb_lane + 1], (q, q))
            col_w.append(jnp.broadcast_to(w_tok[:, h:h + 1], (q, q)))
            seg = jnp.where(lower, colf - acum_t[h:h + 1, :],
                            colb - rcum_t[hb_lane:hb_lane + 1, :])
            dt_b_row = dt_t[hb_lane:hb_lane + 1, :]
            w = (jnp.exp(seg) * jnp.where(lower, dt_t[h:h + 1, :], dt_b_row)
                 + jnp.where(diag, dt_b_row, 0.0))
            m_parts.append(cb[g] * w)
            col_f.append(colf)
            col_b.append(colb)
        sl = slice(pr * LANES, (pr + 1) * LANES)
        xp = xh[:, sl]
        y_cols.append(_dot(jnp.concatenate(m_parts, axis=1), _block_diag_pair(xp))
                      + jnp.exp(jnp.where(first_head, col_f[0], col_f[1])) * off_f[:, sl]
                      + jnp.exp(jnp.where(first_head, col_b[0], col_b[1])) * off_b[:, sl]
                      + dskip_ref[:, sl] * xp)
        wx.append((jnp.where(first_head, col_w[0], col_w[1]) * xp).astype(BF16))
        between_pairs()
    y = jnp.concatenate(y_cols, axis=1)

    wx = jnp.concatenate(wx, axis=1)
    s = jnp.concatenate(
        [jnp.dot(b_all[:, g * SSD_STATE:(g + 1) * SSD_STATE].T.astype(BF16),
                 wx[:, g * GROUP_COLS:(g + 1) * GROUP_COLS], preferred_element_type=F32)
         for g in range(SSD_GROUPS)], axis=1)
    decay = jnp.exp(_expand(jnp.broadcast_to(atot, (BF16_ROWS, LANES)), ef_ref))[0:1, :]
    return y, decay * hf_in + s


def _tiling(x, tm, seq_len):
    s_n, l_n, _ = x.shape
    return s_n, l_n, l_n // tm, seq_len // tm, l_n // seq_len


FRONT_SEGS = (SSD_CONV_DIM, DT_PAD)
FRONT_NTILE = 256


def _front_kernel(*refs, has_pos, has_h0, tiles_per_seq, n_tiles):
    x_ref, xp_ref, xn_ref = refs[:3]
    refs = refs[3:]
    if has_pos:
        embr_ref, embc_ref = refs[:2]
        refs = refs[2:]
    if has_h0:
        h0b_ref = refs[0]
        refs = refs[1:]
    mod_ref, w_ref, wdt_ref, convw_ref, convb_ref, dtbias_ref, aneg_ref, eb_ref = refs[:8]
    xa_ref, dt_ref, hbs_ref, hfb_ref, hb_s, xb_s, dts_s = refs[8:]
    shift1 = mod_ref[0, :, 0:D_MODEL]
    scale1 = mod_ref[0, :, D_MODEL:2 * D_MODEL]
    tm = x_ref.shape[1]
    n_ext = tm + 2 * HALO
    q = SSD_CHUNK
    j = pl.program_id(1)
    tile = n_tiles - 1 - jnp.minimum(j, n_tiles - 1)
    t = tile % tiles_per_seq
    t_scan = jnp.minimum(n_tiles - j, n_tiles - 1) % tiles_per_seq
    slot = j % 2
    scan_slot = 1 - slot

    def init_state():
        if has_h0:
            hb_s[...] = h0b_ref[0].T
        else:
            hb_s[...] = jnp.zeros_like(hb_s)

    def modulate(x):
        return (x * (1.0 + scale1) + shift1).astype(BF16)

    def project():
        if has_pos:
            rows_per_tile = tm // GRID_W
            n_grid_rows = embr_ref.shape[0]
            r0 = tile * rows_per_tile
            x = _add_pos(x_ref, embr_ref, embc_ref, r0)
            x_prev = xp_ref[0] + _pos_halo(embr_ref, embc_ref, jnp.maximum(r0 - 1, 0),
                                           GRID_W - HALO)
            x_next = xn_ref[0] + _pos_halo(embr_ref, embc_ref,
                                           jnp.minimum(r0 + rows_per_tile, n_grid_rows - 1), 0)
        else:
            x, x_prev, x_next = x_ref[0], xp_ref[0], xn_ref[0]
        h = modulate(x)
        h_ext = jnp.concatenate([modulate(x_prev), h, modulate(x_next)], axis=0)

        row = lax.broadcasted_iota(jnp.int32, (tm, 1), 0)
        zero_prev = row == 0
        zero_next = row == tm - 1
        if tiles_per_seq > 1:
            zero_prev = jnp.logical_and(zero_prev, t == 0)
            zero_next = jnp.logical_and(zero_next, t == tiles_per_seq - 1)

        dt = jnp.dot(h, wdt_ref[...], preferred_element_type=F32)
        dt_ref[0] = dt
        dts_s[slot] = dt
        yield
        for c0 in range(0, SSD_CONV_DIM, FRONT_NTILE):
            c1 = c0 + FRONT_NTILE
            p = jnp.dot(h_ext, w_ref[:, c0:c1], preferred_element_type=F32)
            cur = p[HALO:HALO + tm]
            prv = jnp.where(zero_prev, 0.0, pltpu.roll(p, 1, 0)[HALO:HALO + tm])
            nxt = jnp.where(zero_next, 0.0, pltpu.roll(p, n_ext - 1, 0)[HALO:HALO + tm])
            xc = (convw_ref[0:1, c0:c1] * prv + convw_ref[1:2, c0:c1] * cur
                  + convw_ref[2:3, c0:c1] * nxt + convb_ref[:, c0:c1])
            xa = _silu(xc)
            xa_ref[0, :, c0:c1] = xa
            if c0 < D_SSD + BC_COLS:
                xb_s[slot, :, c0:c1] = xa
            yield

    def scan():
        h_state = hb_s[...]
        for ch in reversed(range(tm // q)):
            rows = slice(ch * q, (ch + 1) * q)
            hbs_ref[0, ch] = h_state.astype(BF16)
            dt_all, _, _, _, rcum, rtot = _chunk_decay(dts_s[scan_slot, rows, :], dtbias_ref,
                                                       aneg_ref)
            yield
            w_exp = _expand(dt_all * jnp.exp(rtot - rcum), eb_ref)
            decay = jnp.exp(_expand(jnp.broadcast_to(rtot, (BF16_ROWS, LANES)), eb_ref))[0:1, :]
            yield
            wx = (w_exp * xb_s[scan_slot, rows, 0:D_SSD]).astype(BF16)
            b_all = xb_s[scan_slot, rows, D_SSD:D_SSD + BC_COLS]
            parts = []
            for g in range(SSD_GROUPS):
                parts.append(jnp.dot(b_all[:, g * SSD_STATE:(g + 1) * SSD_STATE].T.astype(BF16),
                                     wx[:, g * GROUP_COLS:(g + 1) * GROUP_COLS],
                                     preferred_element_type=F32))
                yield
            h_state = decay * h_state + jnp.concatenate(parts, axis=1)
        hb_s[...] = h_state

        def final_state():
            hfb_ref[0] = h_state.T

        if tiles_per_seq > 1:
            pl.when(t_scan == 0)(final_state)
        else:
            final_state()

    def run(*stage_fns):
        if scan in stage_fns:
            if tiles_per_seq > 1:
                pl.when(t_scan == tiles_per_seq - 1)(init_state)
            else:
                init_state()
        stages = [fn() for fn in stage_fns]
        while stages:
            for gen in list(stages):
                if next(gen, StopIteration) is StopIteration:
                    stages.remove(gen)

    pl.when(j == 0)(functools.partial(run, project))
    pl.when(jnp.logical_and(j > 0, j < n_tiles))(functools.partial(run, project, scan))
    pl.when(j == n_tiles)(functools.partial(run, scan))


def _front(x, pos, mods, h0b, w_xbc, w_dt, convw, convb, dtbias, aneg, e_b, mod_row0, tm,
           seq_len):
    s_n, l_n, n_tiles, tiles_per_seq, seqs_per_row = _tiling(x, tm, seq_len)
    has_pos = pos is not None
    has_h0 = h0b is not None
    hb = tm // HALO
    n_hb = l_n // HALO
    cpt = tm // SSD_CHUNK
    fwd = lambda j: n_tiles - 1 - jnp.minimum(j, n_tiles - 1)
    scanned = lambda j: jnp.minimum(n_tiles - j, n_tiles - 1)
    tok = lambda s, j: (s, fwd(j), 0)
    seq_of = lambda s, j: (s * seqs_per_row + scanned(j) // tiles_per_seq, 0, 0)
    in_specs = [pl.BlockSpec((1, tm, D_MODEL), tok),
                pl.BlockSpec((1, HALO, D_MODEL),
                             lambda s, j: (s, jnp.maximum(fwd(j) * hb - 1, 0), 0)),
                pl.BlockSpec((1, HALO, D_MODEL),
                             lambda s, j: (s, jnp.minimum((fwd(j) + 1) * hb, n_hb - 1), 0))]
    args = [x, x, x]
    if has_pos:
        in_specs += [_const_spec(pos[0].shape), _const_spec(pos[1].shape)]
        args += list(pos)
    state_spec = pl.BlockSpec((1, D_SSD, SSD_STATE), seq_of)
    if has_h0:
        in_specs.append(state_spec)
        args.append(h0b)
    in_specs.append(pl.BlockSpec((1, 1, 6 * D_MODEL), lambda s, j: (mod_row0 + s, 0, 0)))
    args.append(mods)
    for a in (w_xbc, w_dt, convw, convb, dtbias, aneg, e_b):
        in_specs.append(_const_spec(a.shape))
        args.append(a)
    return pl.pallas_call(
        functools.partial(_front_kernel, has_pos=has_pos, has_h0=has_h0,
                          tiles_per_seq=tiles_per_seq, n_tiles=n_tiles),
        grid=(s_n, n_tiles + 1),
        in_specs=in_specs,
        out_specs=[pl.BlockSpec((1, tm, w), tok) for w in FRONT_SEGS]
        + [pl.BlockSpec((1, cpt, SSD_STATE, D_SSD), lambda s, j: (s, scanned(j), 0, 0)),
           state_spec],
        out_shape=[jax.ShapeDtypeStruct((s_n, l_n, w), F32) for w in FRONT_SEGS]
        + [jax.ShapeDtypeStruct((s_n, l_n // SSD_CHUNK, SSD_STATE, D_SSD), BF16),
           jax.ShapeDtypeStruct((s_n * seqs_per_row, D_SSD, SSD_STATE), F32)],
        scratch_shapes=[pltpu.VMEM((SSD_STATE, D_SSD), F32),
                        pltpu.VMEM((2, tm, D_SSD + BC_COLS), F32),
                        pltpu.VMEM((2, tm, DT_PAD), F32)],
        compiler_params=_params(VMEM_MIB_FRONT, 2),
        name="front",
    )(*args)


MAIN_Z, MAIN_U, MAIN_V, MAIN_GA, MAIN_GB = (i * D_MODEL for i in range(5))
MAIN_COLS = 5 * D_MODEL


def _main_kernel(*refs, has_pos, has_h0, tiles_per_seq):
    x_ref, xa_ref, dtp_ref, hbs_ref = refs[:4]
    refs = refs[4:]
    if has_pos:
        embr_ref, embc_ref = refs[:2]
        refs = refs[2:]
    if has_h0:
        h0f_ref = refs[0]
        refs = refs[1:]
    (mod_ref, w_ref, dtbias_ref, aneg_ref, dskip_ref, normw_ref, ef_ref, lng_ref, lnb_ref,
     wsp_ref, bsp_ref, wpa_ref, wpb_ref, wout_ref, ln1g_ref, ln1b_ref) = refs[:16]
    o_ref, hff_ref, hf_s = refs[16:]
    tm = x_ref.shape[1]
    q = SSD_CHUNK
    tile = pl.program_id(1)
    t = tile % tiles_per_seq

    if has_pos:
        x = _add_pos(x_ref, embr_ref, embc_ref, tile * (tm // GRID_W))
    else:
        x = x_ref[0]
    shift1 = mod_ref[0, :, 0:D_MODEL]
    scale1 = mod_ref[0, :, D_MODEL:2 * D_MODEL]
    gate1 = mod_ref[0, :, 2 * D_MODEL:3 * D_MODEL]
    h = (x * (1.0 + scale1) + shift1).astype(BF16)

    col_tiles = {}
    pending = list(range(0, MAIN_COLS, IN_NTILE))
    n_slots = (tm // q) * HEAD_PAIRS
    n_fill = len(pending)
    slot = [0]

    def project_next():
        c = pending.pop(0)
        act = _silu if c < MAIN_U else (_gelu if c < MAIN_GA else _sigmoid)
        col_tiles[c] = act(jnp.dot(h, w_ref[:, c:c + IN_NTILE], preferred_element_type=F32))

    def between_pairs():
        slot[0] += 1
        done = n_fill - len(pending)
        for _ in range(-(-slot[0] * n_fill // n_slots) - done):
            project_next()

    def in_proj(c0):
        while pending and pending[0] < c0 + D_MODEL:
            project_next()
        return jnp.concatenate([col_tiles.pop(c) for c in range(c0, c0 + D_MODEL, IN_NTILE)],
                               axis=1)

    def init_state():
        if has_h0:
            hf_s[...] = h0f_ref[0].T
        else:
            hf_s[...] = jnp.zeros_like(hf_s)

    if tiles_per_seq > 1:
        pl.when(t == 0)(init_state)
    else:
        init_state()
    h_state = hf_s[...]
    ya = []
    for ch in range(tm // q):
        rows = slice(ch * q, (ch + 1) * q)
        y, h_state = _ssd_chunk(
            xa_ref[0, rows, 0:D_SSD], xa_ref[0, rows, D_SSD:D_SSD + BC_COLS],
            xa_ref[0, rows, D_SSD + BC_COLS:], dtp_ref[0, rows, :], h_state, hbs_ref[0, ch],
            dtbias_ref, aneg_ref, dskip_ref, ef_ref, between_pairs)
        if ch == 0:
            zz = in_proj(MAIN_Z)
        yg = y * zz[rows]
        ms = jnp.mean(yg * yg, axis=-1, keepdims=True)
        ya.append(yg * lax.rsqrt(ms + EPS) * normw_ref[...])
    hf_s[...] = h_state

    def final_state():
        hff_ref[0] = h_state.T

    if tiles_per_seq > 1:
        pl.when(t == tiles_per_seq - 1)(final_state)
    else:
        final_state()
    ya = jnp.concatenate(ya, axis=0)
    proj_a = _dot(ya, wpa_ref[...])

    u = in_proj(MAIN_U)
    v = _layer_norm(in_proj(MAIN_V), lng_ref[...], lnb_ref[...])
    n_ch = tm // SGU_CHUNK
    vb = v.astype(BF16)
    per_group = []
    for g in range(SGU_GROUPS):
        gs = slice(g * SGU_GROUP_DIM, (g + 1) * SGU_GROUP_DIM)
        rhs = jnp.concatenate([vb[ch * SGU_CHUNK:(ch + 1) * SGU_CHUNK, gs] for ch in range(n_ch)],
                              axis=1)
        per_group.append(jnp.dot(wsp_ref[g], rhs, preferred_element_type=F32))
    rows_out = []
    for ch in range(n_ch):
        cs = slice(ch * SGU_GROUP_DIM, (ch + 1) * SGU_GROUP_DIM)
        rows_out.append(jnp.concatenate([p[:, cs] for p in per_group], axis=1) + bsp_ref[...])
    yb = u * jnp.concatenate(rows_out, axis=0)

    merged = in_proj(MAIN_GA) * proj_a + in_proj(MAIN_GB) * _dot(yb, wpb_ref[...])
    o = _dot(merged, wout_ref[...])
    o_ref[0] = _layer_norm(DN_ALPHA * x + gate1 * o, ln1g_ref[...], ln1b_ref[...])


def _main(x, pos, xa, dtp, hbs, h0f, mods, mod_row0, tm, seq_len, consts):
    s_n, l_n, n_tiles, tiles_per_seq, seqs_per_row = _tiling(x, tm, seq_len)
    has_pos = pos is not None
    has_h0 = h0f is not None
    cpt = tm // SSD_CHUNK
    tok = lambda s, j: (s, j, 0)
    in_specs = [pl.BlockSpec((1, tm, D_MODEL), tok),
                pl.BlockSpec((1, tm, SSD_CONV_DIM), tok),
                pl.BlockSpec((1, tm, DT_PAD), tok),
                pl.BlockSpec((1, cpt, SSD_STATE, D_SSD), lambda s, j: (s, j, 0, 0))]
    args = [x, xa, dtp, hbs]
    if has_pos:
        in_specs += [_const_spec(pos[0].shape), _const_spec(pos[1].shape)]
        args += list(pos)
    state_spec = pl.BlockSpec((1, D_SSD, SSD_STATE),
                              lambda s, j: (s * seqs_per_row + j // tiles_per_seq, 0, 0))
    if has_h0:
        in_specs.append(state_spec)
        args.append(h0f)
    in_specs.append(pl.BlockSpec((1, 1, 6 * D_MODEL), lambda s, j: (mod_row0 + s, 0, 0)))
    args.append(mods)
    for a in consts:
        in_specs.append(_const_spec(a.shape))
        args.append(a)
    return pl.pallas_call(
        functools.partial(_main_kernel, has_pos=has_pos, has_h0=has_h0,
                          tiles_per_seq=tiles_per_seq),
        grid=(s_n, n_tiles),
        in_specs=in_specs,
        out_specs=[pl.BlockSpec((1, tm, D_MODEL), tok), state_spec],
        out_shape=[jax.ShapeDtypeStruct((s_n, l_n, D_MODEL), F32),
                   jax.ShapeDtypeStruct((s_n * seqs_per_row, D_SSD, SSD_STATE), F32)],
        scratch_shapes=[pltpu.VMEM((SSD_STATE, D_SSD), F32)],
        compiler_params=_params(VMEM_MIB_MAIN, 2),
        name="main",
    )(*args)


FF_TILE = 1024
FFN_TM = 1024


def _ffn_kernel(x_ref, mod_ref, w1_ref, w2_ref, g_ref, b_ref, o_ref):
    x = x_ref[0]
    shift2 = mod_ref[0, :, 3 * D_MODEL:4 * D_MODEL]
    scale2 = mod_ref[0, :, 4 * D_MODEL:5 * D_MODEL]
    gate2 = mod_ref[0, :, 5 * D_MODEL:6 * D_MODEL]
    h = (x * (1.0 + scale2) + shift2).astype(BF16)
    f = jnp.zeros_like(x)
    for k0 in range(0, D_FF, FF_TILE):
        a = jnp.dot(h, w1_ref[:, k0:k0 + FF_TILE], preferred_element_type=F32)
        a = jnp.maximum(a, 0.0)
        f = f + _dot(a * a, w2_ref[k0:k0 + FF_TILE, :])
    o_ref[0] = _layer_norm(DN_ALPHA * x + gate2 * f, g_ref[...], b_ref[...])


def _ffn(x, mods, mod_row0, tm, w1, w2, g, b):
    s_n, l_n, _ = x.shape
    tok = lambda s, j: (s, j, 0)
    return pl.pallas_call(
        _ffn_kernel,
        grid=(s_n, l_n // tm),
        in_specs=[pl.BlockSpec((1, tm, D_MODEL), tok),
                  pl.BlockSpec((1, 1, 6 * D_MODEL), lambda s, j: (mod_row0 + s, 0, 0)),
                  _const_spec(w1.shape), _const_spec(w2.shape),
                  _const_spec(g.shape), _const_spec(b.shape)],
        out_specs=pl.BlockSpec((1, tm, D_MODEL), tok),
        out_shape=jax.ShapeDtypeStruct((s_n, l_n, D_MODEL), F32),
        compiler_params=_params(VMEM_MIB_FFN, 2),
        name="ffn",
    )(x, mods, w1, w2, g, b)


def _grid_pos_tables(n_tokens):
    rows = n_tokens // GRID_W
    quarter = D_MODEL // 4
    omega = 1.0 / (POS_BASE ** (jnp.arange(quarter, dtype=F32) / quarter))
    r = jnp.arange(rows, dtype=F32)[:, None] * omega
    col = jnp.arange(GRID_W, dtype=F32)[:, None] * omega
    emb_r = jnp.concatenate([jnp.sin(r), jnp.cos(r)], axis=-1)
    emb_c = jnp.concatenate([jnp.sin(col), jnp.cos(col)], axis=-1)
    return jnp.broadcast_to(emb_r[:, None, :], (rows, SUBLANES, D_MODEL // 2)), emb_c


def _dt_lanes(v, axis):
    pad = [(0, 0)] * v.ndim
    pad[axis] = (0, DT_PAD - DT_COPIES * DT_GROUP)
    return jnp.pad(jnp.concatenate([v] * DT_COPIES, axis=axis), pad)


def _head_expanders():
    lane = jnp.arange(LANES)[:, None]
    head = jnp.arange(D_SSD)[None, :] // SSD_HEAD_DIM
    valid = lane < DT_COPIES * DT_GROUP
    e_f = jnp.logical_and(valid, lane % DT_GROUP == head)
    e_b = jnp.logical_and(valid, lane % DT_GROUP == head + SSD_HEADS)
    return e_f.astype(BF16), e_b.astype(BF16)


def _stream(x, pos, mods, mod_row0, seq_len, h0f, h0b, prm, tm):
    xa, dtp, hbs, hb = _front(x, pos, mods, h0b, prm["w_xbc"], prm["w_dt"], prm["conv_w"],
                              prm["conv_b"],
                              prm["dt_bias"], prm["a_neg"], prm["e_b"], mod_row0, tm, seq_len)
    x1, hf = _main(x, pos, xa, dtp, hbs, h0f, mods, mod_row0, tm, seq_len, prm["main"])
    y = _ffn(x1, mods, mod_row0, FFN_TM, prm["w_ff1"], prm["w_ff2"], prm["ln2_g"], prm["ln2_b"])
    return y, hf, hb


def kernel(x_prompt, x_sample, state_ssd_fwd, state_ssd_bwd, c, c_ctx, w_ada, b_ada, w_in, conv_w, conv_b, dt_bias_fwd, dt_bias_bwd, a_log_fwd, a_log_bwd, d_skip, ssd_norm_w, sgu_ln_g, sgu_ln_b, w_spatial, b_spatial, w_proj_a, w_proj_b, w_out, ln1_g, ln1_b, w_ff1, w_ff2, ln2_g, ln2_b):
    batch, seq, _ = x_prompt.shape
    dec_batch, dec_seq, _ = x_sample.shape
    l = 0
    row = lambda v: v.reshape(1, -1)

    cond = jnp.zeros((SUBLANES, D_MODEL), F32).at[0].set(c_ctx).at[1:1 + dec_batch].set(c)
    mods = _ada(cond, w_ada[l], b_ada[l]).reshape(SUBLANES, 1, 6 * D_MODEL)

    o1 = D_SSD
    o2 = o1 + SSD_CONV_DIM
    o3 = o2 + DT_GROUP
    w_t = jnp.swapaxes(w_in[l], 0, 1)
    step = W_COL_TILE // DT_GROUP
    w_xbc = _w_cols(w_t, lambda i: o1 // DT_GROUP + i * step, SSD_CONV_DIM // W_COL_TILE)
    z_tiles = o1 // W_COL_TILE
    w_main = _w_cols(
        w_t, lambda i: jnp.where(i < z_tiles, i * step, o3 // DT_GROUP + (i - z_tiles) * step),
        MAIN_COLS // W_COL_TILE)
    assert o2 % LANES == 0
    w_dt = _w_dt(w_t, o2)
    e_f, e_b = _head_expanders()
    dt_bias = row(_dt_lanes(jnp.concatenate([dt_bias_fwd[l], dt_bias_bwd[l]]), 0))
    a_neg = row(_dt_lanes(-jnp.exp(jnp.concatenate([a_log_fwd[l], a_log_bwd[l]])), 0))

    prm = dict(
        w_xbc=w_xbc, w_dt=w_dt, conv_w=conv_w[l], conv_b=row(conv_b[l]),
        dt_bias=dt_bias, a_neg=a_neg, e_b=e_b,
        main=(w_main, dt_bias, a_neg, row(jnp.repeat(d_skip[l], SSD_HEAD_DIM)),
              row(ssd_norm_w[l]), e_f, row(sgu_ln_g[l]), row(sgu_ln_b[l]),
              w_spatial[l].astype(BF16), jnp.repeat(b_spatial[l].T, SGU_GROUP_DIM, axis=1),
              w_proj_a[l].astype(BF16), w_proj_b[l].astype(BF16), w_out[l].astype(BF16),
              row(ln1_g[l]), row(ln1_b[l])),
        w_ff1=w_ff1[l].astype(BF16), w_ff2=w_ff2[l].astype(BF16),
        ln2_g=row(ln2_g[l]), ln2_b=row(ln2_b[l]),
    )

    tm = 256
    xp, hf, hb = _stream(x_prompt.reshape(1, batch * seq, D_MODEL), None, mods, 0, seq,
                         None, None, prm, tm)
    pos = _grid_pos_tables(dec_seq)
    h0f = state_ssd_fwd[:, l].reshape(dec_batch, D_SSD, SSD_STATE)
    h0b = state_ssd_bwd[:, l].reshape(dec_batch, D_SSD, SSD_STATE)
    xs, _, _ = _stream(x_sample, pos, mods, 1, dec_seq, h0f, h0b, prm, tm)

    state_shape = (batch, DEPTH, SSD_HEADS, SSD_HEAD_DIM, SSD_STATE)
    return (xp.reshape(batch, seq, D_MODEL), xs,
            hf.reshape(state_shape), hb.reshape(state_shape))
```

```python
import functools
import math

import jax
import jax.numpy as jnp
from jax import lax
from jax.experimental import pallas as pl
from jax.experimental.pallas import tpu as pltpu

D_MODEL = 1024
GRID_W = 64
POS_BASE = 10000.0
D_SSD = 1024
SSD_HEAD_DIM = 64
SSD_HEADS = D_SSD // SSD_HEAD_DIM
SSD_GROUPS = 2
SSD_STATE = 128
SSD_CHUNK = 128
SSD_CONV_DIM = D_SSD + 2 * SSD_GROUPS * SSD_STATE
D_SGU = 1024
SGU_GROUPS = 8
SGU_GROUP_DIM = D_SGU // SGU_GROUPS
SGU_CHUNK = 128
D_FF = 4 * D_MODEL
DEPTH = 1
DN_ALPHA = (2.0 * DEPTH) ** 0.25
EPS = 1e-5

LANES = 128
SUBLANES = 8
BF16_ROWS = 16
V7X_VMEM_MIB = 64
VMEM_MIB_SMALL = V7X_VMEM_MIB // 2
VMEM_MIB_FRONT = 48
VMEM_MIB_MAIN = 56
VMEM_MIB_FFN = 48

DT_GROUP = 2 * SSD_HEADS
DT_COPIES = 3
DT_PAD = LANES
HEAD_PAIRS = SSD_HEADS // 2
PAIRS_PER_GROUP = HEAD_PAIRS // SSD_GROUPS
GROUP_COLS = D_SSD // SSD_GROUPS
BC_COLS = SSD_GROUPS * SSD_STATE

BF16 = jnp.bfloat16
F32 = jnp.float32
HIGHEST = lax.Precision.HIGHEST


def _dot(a, b):
    return jnp.dot(a.astype(BF16), b.astype(BF16), preferred_element_type=F32)


def _dot_exact(a, b):
    return jnp.dot(a, b, precision=HIGHEST, preferred_element_type=F32)


def _layer_norm(x, g, b):
    mu = jnp.mean(x, axis=-1, keepdims=True)
    xc = x - mu
    var = jnp.mean(xc * xc, axis=-1, keepdims=True)
    return xc * lax.rsqrt(var + EPS) * g + b


def _softplus(x):
    return jnp.maximum(x, 0.0) + jnp.log1p(jnp.exp(-jnp.abs(x)))


def _gelu(x):
    return 0.5 * x * (1.0 + lax.erf(x * (1.0 / math.sqrt(2.0))))


def _sigmoid(x):
    return 0.5 * jnp.tanh(0.5 * x) + 0.5


def _silu(x):
    half = 0.5 * x
    return half * jnp.tanh(half) + half


def _const_spec(shape):
    zeros = (0,) * len(shape)
    return pl.BlockSpec(shape, lambda *_: zeros, pipeline_mode=pl.Buffered(1))


def _params(vmem_mib, n_axes):
    return pltpu.CompilerParams(
        dimension_semantics=("arbitrary",) * n_axes,
        vmem_limit_bytes=vmem_mib * 1024 * 1024)


def _ada_kernel(c_ref, w_ref, b_ref, o_ref):
    c = _silu(c_ref[...])
    c_hi = c.astype(BF16)
    c_lo = (c - c_hi.astype(F32)).astype(BF16)
    w = w_ref[...]
    w_hi = w.astype(BF16)
    w_lo = (w - w_hi.astype(F32)).astype(BF16)
    rows = c.shape[0]
    by_hi = jnp.dot(jnp.concatenate([c_hi, c_lo], axis=0), w_hi, preferred_element_type=F32)
    by_lo = jnp.dot(c_hi, w_lo, preferred_element_type=F32)
    o_ref[...] = by_hi[:rows] + by_hi[rows:] + by_lo + b_ref[...]


def _ada(cond, w_ada, b_ada):
    n = w_ada.shape[1]
    tn = 2048
    return pl.pallas_call(
        _ada_kernel,
        grid=(n // tn,),
        in_specs=[pl.BlockSpec((SUBLANES, D_MODEL), lambda j: (0, 0)),
                  pl.BlockSpec((D_MODEL, tn), lambda j: (0, j)),
                  pl.BlockSpec((1, tn), lambda j: (0, j))],
        out_specs=pl.BlockSpec((SUBLANES, tn), lambda j: (0, j)),
        out_shape=jax.ShapeDtypeStruct((SUBLANES, n), F32),
        compiler_params=_params(VMEM_MIB_SMALL, 1),
        name="ada",
    )(cond, w_ada, b_ada.reshape(1, n))


W_COL_TILE = 512


def _w_cols_kernel(src_ref, o_ref):
    o_ref[...] = src_ref[...].T.astype(BF16)


def _w_dt_kernel(src_ref, o_ref):
    t = src_ref[...].T
    lane = lax.broadcasted_iota(jnp.int32, t.shape, 1)
    dt = jnp.where(lane < DT_GROUP, t, 0.0)
    o_ref[...] = (dt + pltpu.roll(dt, DT_GROUP, 1) + pltpu.roll(dt, 2 * DT_GROUP, 1)).astype(BF16)


def _w_dt(w_t, col0):
    d = w_t.shape[1]
    return pl.pallas_call(
        _w_dt_kernel,
        grid=(1,),
        in_specs=[pl.BlockSpec((LANES, d), lambda i: (col0 // LANES, 0))],
        out_specs=pl.BlockSpec((d, LANES), lambda i: (0, 0)),
        out_shape=jax.ShapeDtypeStruct((d, LANES), BF16),
        name="w_dt",
    )(w_t)


def _w_cols(w_t, col_start, n_tiles):
    d = w_t.shape[1]
    return pl.pallas_call(
        _w_cols_kernel,
        grid=(n_tiles,),
        in_specs=[pl.BlockSpec((pl.Element(W_COL_TILE), pl.Element(d)),
                               lambda i: (col_start(i) * DT_GROUP, 0))],
        out_specs=pl.BlockSpec((d, W_COL_TILE), lambda i: (0, i)),
        out_shape=jax.ShapeDtypeStruct((d, n_tiles * W_COL_TILE), BF16),
        compiler_params=_params(VMEM_MIB_SMALL, 1),
        name="w_cols",
    )(w_t)


IN_NTILE = 512
HALO = BF16_ROWS


def _add_pos(x_ref, embr_ref, embc_ref, grid_row0):
    n_rows = x_ref.shape[1] // GRID_W
    col = embc_ref[...]
    blocks = []
    for i in range(n_rows):
        row = embr_ref[grid_row0 + i]
        row = jnp.concatenate([row] * (GRID_W // SUBLANES), axis=0)
        blocks.append(x_ref[0, i * GRID_W:(i + 1) * GRID_W, :]
                      + jnp.concatenate([row, col], axis=1))
    return jnp.concatenate(blocks, axis=0)


def _pos_halo(embr_ref, embc_ref, grid_row, col0):
    left = jnp.concatenate([embr_ref[grid_row]] * (HALO // SUBLANES), axis=0)
    return jnp.concatenate([left, embc_ref[col0:col0 + HALO, :]], axis=1)


def _block_diag_pair(xp):
    lane = lax.broadcasted_iota(jnp.int32, xp.shape, 1)
    first = lane < SSD_HEAD_DIM
    return jnp.concatenate([jnp.where(first, xp, 0.0), jnp.where(first, 0.0, xp)], axis=0)


def _split_terms(v):
    lane = lax.broadcasted_iota(jnp.int32, v.shape, 1)
    hi = v.astype(BF16).astype(F32)
    r1 = v - hi
    mid = r1.astype(BF16).astype(F32)
    lo = r1 - mid
    return jnp.where(lane < DT_GROUP, hi, jnp.where(lane < 2 * DT_GROUP, mid, lo)).astype(BF16)


def _expand(v, e_ref):
    return jnp.dot(_split_terms(v), e_ref[...], preferred_element_type=F32)


def _chunk_decay(dtp, dtbias_ref, aneg_ref, on_mxu=True):
    q = dtp.shape[0]
    dt_all = _softplus(dtp + dtbias_ref[...])
    a_all = dt_all * aneg_ref[...]
    if on_mxu:
        ii = lax.broadcasted_iota(jnp.int32, (q, q), 0)
        jj = lax.broadcasted_iota(jnp.int32, (q, q), 1)
        acum = _dot_exact((ii >= jj).astype(F32), a_all)
    else:
        row = lax.broadcasted_iota(jnp.int32, a_all.shape, 0)
        acum = a_all
        shift = 1
        while shift < q:
            acum = acum + jnp.where(row >= shift, pltpu.roll(acum, shift, 0), 0.0)
            shift *= 2
    atot = acum[q - 1:q, :]
    rcum = atot - acum + a_all
    return dt_all, a_all, acum, atot, rcum, rcum[0:1, :]


def _ssd_chunk(xh, b_all, c_all, dtp, hf_in, hb_in, dtbias_ref, aneg_ref, dskip_ref, ef_ref,
               between_pairs=lambda: None):
    q = SSD_CHUNK
    dt_all, a_all, acum, atot, rcum, _ = _chunk_decay(dtp, dtbias_ref, aneg_ref, on_mxu=False)
    ii = lax.broadcasted_iota(jnp.int32, (q, q), 0)
    jj = lax.broadcasted_iota(jnp.int32, (q, q), 1)
    lower = ii >= jj
    diag = ii == jj
    dt_t = dt_all.T
    acum_t = acum.T
    rcum_t = rcum.T

    off_f, off_b, cb = [], [], []
    for g in range(SSD_GROUPS):
        gs = slice(g * GROUP_COLS, (g + 1) * GROUP_COLS)
        c_g = c_all[:, g * SSD_STATE:(g + 1) * SSD_STATE].astype(BF16)
        b_g = b_all[:, g * SSD_STATE:(g + 1) * SSD_STATE].astype(BF16)
        cb.append(lax.dot_general(c_g, b_g, (((1,), (1,)), ((), ())),
                                  preferred_element_type=F32))
        off_f.append(jnp.dot(c_g, hf_in[:, gs].astype(BF16), preferred_element_type=F32))
        off_b.append(jnp.dot(c_g, hb_in[:, gs].astype(BF16), preferred_element_type=F32))
    off_f = jnp.concatenate(off_f, axis=1)
    off_b = jnp.concatenate(off_b, axis=1)

    first_head = lax.broadcasted_iota(jnp.int32, (q, LANES), 1) < SSD_HEAD_DIM
    w_tok = dt_all * jnp.exp(atot - acum)
    y_cols, wx = [], []
    for pr in range(HEAD_PAIRS):
        g = pr // PAIRS_PER_GROUP
        m_parts, col_f, col_b, col_w = [], [], [], []
        for h in (2 * pr, 2 * pr + 1):
            hb_lane = SSD_HEADS + h
            colf = jnp.broadcast_to(acum[:, h:h + 1], (q, q))
            colb = jnp.broadcast_to(rcum[:, hb_lane:hb_lane + 1], (q, q))
            col_w.append(jnp.broadcast_to(w_tok[:, h:h + 1], (q, q)))
            seg = jnp.where(lower, colf - acum_t[h:h + 1, :],
                            colb - rcum_t[hb_lane:hb_lane + 1, :])
            dt_b_row = dt_t[hb_lane:hb_lane + 1, :]
            w = (jnp.exp(seg) * jnp.where(lower, dt_t[h:h + 1, :], dt_b_row)
                 + jnp.where(diag, dt_b_row, 0.0))
            m_parts.append(cb[g] * w)
            col_f.append(colf)
            col_b.append(colb)
        sl = slice(pr * LANES, (pr + 1) * LANES)
        xp = xh[:, sl]
        y_cols.append(_dot(jnp.concatenate(m_parts, axis=1), _block_diag_pair(xp))
                      + jnp.exp(jnp.where(first_head, col_f[0], col_f[1])) * off_f[:, sl]
                      + jnp.exp(jnp.where(first_head, col_b[0], col_b[1])) * off_b[:, sl]
                      + dskip_ref[:, sl] * xp)
        wx.append((jnp.where(first_head, col_w[0], col_w[1]) * xp).astype(BF16))
        between_pairs()
    y = jnp.concatenate(y_cols, axis=1)

    wx = jnp.concatenate(wx, axis=1)
    s = jnp.concatenate(
        [jnp.dot(b_all[:, g * SSD_STATE:(g + 1) * SSD_STATE].astype(F32).T.astype(BF16),
                 wx[:, g * GROUP_COLS:(g + 1) * GROUP_COLS], preferred_element_type=F32)
         for g in range(SSD_GROUPS)], axis=1)
    decay = jnp.exp(_expand(jnp.broadcast_to(atot, (BF16_ROWS, LANES)), ef_ref))[0:1, :]
    return y, decay * hf_in + s


def _tiling(x, tm, seq_len):
    s_n, l_n, _ = x.shape
    return s_n, l_n, l_n // tm, seq_len // tm, l_n // seq_len


FRONT_SEGS = (SSD_CONV_DIM, DT_PAD)
FRONT_NTILE = 256


def _front_kernel(*refs, has_pos, has_h0, tiles_per_seq, n_tiles):
    x_ref, xp_ref, xn_ref = refs[:3]
    refs = refs[3:]
    if has_pos:
        embr_ref, embc_ref = refs[:2]
        refs = refs[2:]
    if has_h0:
        h0b_ref = refs[0]
        refs = refs[1:]
    mod_ref, w_ref, wdt_ref, convw_ref, convb_ref, dtbias_ref, aneg_ref, eb_ref = refs[:8]
    xa_ref, bc_ref, dt_ref, hbs_ref, hfb_ref, hb_s, xb_s, dts_s = refs[8:]
    shift1 = mod_ref[0, :, 0:D_MODEL]
    scale1 = mod_ref[0, :, D_MODEL:2 * D_MODEL]
    tm = x_ref.shape[1]
    n_ext = tm + 2 * HALO
    q = SSD_CHUNK
    j = pl.program_id(1)
    tile = n_tiles - 1 - jnp.minimum(j, n_tiles - 1)
    t = tile % tiles_per_seq
    t_scan = jnp.minimum(n_tiles - j, n_tiles - 1) % tiles_per_seq
    slot = j % 2
    scan_slot = 1 - slot

    def init_state():
        if has_h0:
            hb_s[...] = h0b_ref[0].T
        else:
            hb_s[...] = jnp.zeros_like(hb_s)

    def modulate(x):
        return (x * (1.0 + scale1) + shift1).astype(BF16)

    def project():
        if has_pos:
            rows_per_tile = tm // GRID_W
            n_grid_rows = embr_ref.shape[0]
            r0 = tile * rows_per_tile
            x = _add_pos(x_ref, embr_ref, embc_ref, r0)
            x_prev = xp_ref[0] + _pos_halo(embr_ref, embc_ref, jnp.maximum(r0 - 1, 0),
                                           GRID_W - HALO)
            x_next = xn_ref[0] + _pos_halo(embr_ref, embc_ref,
                                           jnp.minimum(r0 + rows_per_tile, n_grid_rows - 1), 0)
        else:
            x, x_prev, x_next = x_ref[0], xp_ref[0], xn_ref[0]
        h = modulate(x)
        h_ext = jnp.concatenate([modulate(x_prev), h, modulate(x_next)], axis=0)

        row = lax.broadcasted_iota(jnp.int32, (tm, 1), 0)
        zero_prev = row == 0
        zero_next = row == tm - 1
        if tiles_per_seq > 1:
            zero_prev = jnp.logical_and(zero_prev, t == 0)
            zero_next = jnp.logical_and(zero_next, t == tiles_per_seq - 1)

        dt = jnp.dot(h, wdt_ref[...], preferred_element_type=F32)
        dt_ref[0] = dt
        dts_s[slot] = dt
        yield
        for c0 in range(0, SSD_CONV_DIM, FRONT_NTILE):
            c1 = c0 + FRONT_NTILE
            p = jnp.dot(h_ext, w_ref[:, c0:c1], preferred_element_type=F32)
            cur = p[HALO:HALO + tm]
            prv = jnp.where(zero_prev, 0.0, pltpu.roll(p, 1, 0)[HALO:HALO + tm])
            nxt = jnp.where(zero_next, 0.0, pltpu.roll(p, n_ext - 1, 0)[HALO:HALO + tm])
            xc = (convw_ref[0:1, c0:c1] * prv + convw_ref[1:2, c0:c1] * cur
                  + convw_ref[2:3, c0:c1] * nxt + convb_ref[:, c0:c1])
            xa = _silu(xc)
            if c0 < D_SSD:
                xa_ref[0, :, c0:c1] = xa
            else:
                bc_ref[0, :, c0 - D_SSD:c1 - D_SSD] = xa.astype(BF16)
            if c0 < D_SSD + BC_COLS:
                xb_s[slot, :, c0:c1] = xa
            yield

    def scan():
        h_state = hb_s[...]
        for ch in reversed(range(tm // q)):
            rows = slice(ch * q, (ch + 1) * q)
            hbs_ref[0, ch] = h_state.astype(BF16)
            dt_all, _, _, _, rcum, rtot = _chunk_decay(dts_s[scan_slot, rows, :], dtbias_ref,
                                                       aneg_ref)
            yield
            w_exp = _expand(dt_all * jnp.exp(rtot - rcum), eb_ref)
            decay = jnp.exp(_expand(jnp.broadcast_to(rtot, (BF16_ROWS, LANES)), eb_ref))[0:1, :]
            yield
            wx = (w_exp * xb_s[scan_slot, rows, 0:D_SSD]).astype(BF16)
            b_all = xb_s[scan_slot, rows, D_SSD:D_SSD + BC_COLS]
            parts = []
            for g in range(SSD_GROUPS):
                parts.append(jnp.dot(b_all[:, g * SSD_STATE:(g + 1) * SSD_STATE].T.astype(BF16),
                                     wx[:, g * GROUP_COLS:(g + 1) * GROUP_COLS],
                                     preferred_element_type=F32))
                yield
            h_state = decay * h_state + jnp.concatenate(parts, axis=1)
        hb_s[...] = h_state

        def final_state():
            hfb_ref[0] = h_state.T

        if tiles_per_seq > 1:
            pl.when(t_scan == 0)(final_state)
        else:
            final_state()

    def run(*stage_fns):
        if scan in stage_fns:
            if tiles_per_seq > 1:
                pl.when(t_scan == tiles_per_seq - 1)(init_state)
            else:
                init_state()
        stages = [fn() for fn in stage_fns]
        while stages:
            for gen in list(stages):
                if next(gen, StopIteration) is StopIteration:
                    stages.remove(gen)

    pl.when(j == 0)(functools.partial(run, project))
    pl.when(jnp.logical_and(j > 0, j < n_tiles))(functools.partial(run, project, scan))
    pl.when(j == n_tiles)(functools.partial(run, scan))


def _front(x, pos, mods, h0b, w_xbc, w_dt, convw, convb, dtbias, aneg, e_b, mod_row0, tm,
           seq_len):
    s_n, l_n, n_tiles, tiles_per_seq, seqs_per_row = _tiling(x, tm, seq_len)
    has_pos = pos is not None
    has_h0 = h0b is not None
    hb = tm // HALO
    n_hb = l_n // HALO
    cpt = tm // SSD_CHUNK
    fwd = lambda j: n_tiles - 1 - jnp.minimum(j, n_tiles - 1)
    scanned = lambda j: jnp.minimum(n_tiles - j, n_tiles - 1)
    tok = lambda s, j: (s, fwd(j), 0)
    seq_of = lambda s, j: (s * seqs_per_row + scanned(j) // tiles_per_seq, 0, 0)
    in_specs = [pl.BlockSpec((1, tm, D_MODEL), tok),
                pl.BlockSpec((1, HALO, D_MODEL),
                             lambda s, j: (s, jnp.maximum(fwd(j) * hb - 1, 0), 0)),
                pl.BlockSpec((1, HALO, D_MODEL),
                             lambda s, j: (s, jnp.minimum((fwd(j) + 1) * hb, n_hb - 1), 0))]
    args = [x, x, x]
    if has_pos:
        in_specs += [_const_spec(pos[0].shape), _const_spec(pos[1].shape)]
        args += list(pos)
    state_spec = pl.BlockSpec((1, D_SSD, SSD_STATE), seq_of)
    if has_h0:
        in_specs.append(state_spec)
        args.append(h0b)
    in_specs.append(pl.BlockSpec((1, 1, 6 * D_MODEL), lambda s, j: (mod_row0 + s, 0, 0)))
    args.append(mods)
    for a in (w_xbc, w_dt, convw, convb, dtbias, aneg, e_b):
        in_specs.append(_const_spec(a.shape))
        args.append(a)
    return pl.pallas_call(
        functools.partial(_front_kernel, has_pos=has_pos, has_h0=has_h0,
                          tiles_per_seq=tiles_per_seq, n_tiles=n_tiles),
        grid=(s_n, n_tiles + 1),
        in_specs=in_specs,
        out_specs=[pl.BlockSpec((1, tm, D_SSD), tok), pl.BlockSpec((1, tm, 2 * BC_COLS), tok),
                   pl.BlockSpec((1, tm, DT_PAD), tok),
                   pl.BlockSpec((1, cpt, SSD_STATE, D_SSD), lambda s, j: (s, scanned(j), 0, 0)),
                   state_spec],
        out_shape=[jax.ShapeDtypeStruct((s_n, l_n, D_SSD), F32),
                   jax.ShapeDtypeStruct((s_n, l_n, 2 * BC_COLS), BF16),
                   jax.ShapeDtypeStruct((s_n, l_n, DT_PAD), F32),
                   jax.ShapeDtypeStruct((s_n, l_n // SSD_CHUNK, SSD_STATE, D_SSD), BF16),
           jax.ShapeDtypeStruct((s_n * seqs_per_row, D_SSD, SSD_STATE), F32)],
        scratch_shapes=[pltpu.VMEM((SSD_STATE, D_SSD), F32),
                        pltpu.VMEM((2, tm, D_SSD + BC_COLS), F32),
                        pltpu.VMEM((2, tm, DT_PAD), F32)],
        compiler_params=_params(VMEM_MIB_FRONT, 2),
        name="front",
    )(*args)


MAIN_Z, MAIN_U, MAIN_V, MAIN_GA, MAIN_GB = (i * D_MODEL for i in range(5))
MAIN_COLS = 5 * D_MODEL


def _main_kernel(*refs, has_pos, has_h0, tiles_per_seq):
    x_ref, xa_ref, bc_ref, dtp_ref, hbs_ref = refs[:5]
    refs = refs[5:]
    if has_pos:
        embr_ref, embc_ref = refs[:2]
        refs = refs[2:]
    if has_h0:
        h0f_ref = refs[0]
        refs = refs[1:]
    (mod_ref, w_ref, dtbias_ref, aneg_ref, dskip_ref, normw_ref, ef_ref, lng_ref, lnb_ref,
     wsp_ref, bsp_ref, wpa_ref, wpb_ref, wout_ref, ln1g_ref, ln1b_ref) = refs[:16]
    o_ref, hff_ref, hf_s = refs[16:]
    tm = x_ref.shape[1]
    q = SSD_CHUNK
    tile = pl.program_id(1)
    t = tile % tiles_per_seq

    if has_pos:
        x = _add_pos(x_ref, embr_ref, embc_ref, tile * (tm // GRID_W))
    else:
        x = x_ref[0]
    shift1 = mod_ref[0, :, 0:D_MODEL]
    scale1 = mod_ref[0, :, D_MODEL:2 * D_MODEL]
    gate1 = mod_ref[0, :, 2 * D_MODEL:3 * D_MODEL]
    h = (x * (1.0 + scale1) + shift1).astype(BF16)

    col_tiles = {}
    pending = list(range(0, MAIN_COLS, IN_NTILE))
    n_slots = (tm // q) * HEAD_PAIRS
    n_fill = len(pending)
    slot = [0]

    def project_next():
        c = pending.pop(0)
        act = _silu if c < MAIN_U else (_gelu if c < MAIN_GA else _sigmoid)
        col_tiles[c] = act(jnp.dot(h, w_ref[:, c:c + IN_NTILE], preferred_element_type=F32))

    def between_pairs():
        slot[0] += 1
        done = n_fill - len(pending)
        for _ in range(-(-slot[0] * n_fill // n_slots) - done):
            project_next()

    def in_proj(c0):
        while pending and pending[0] < c0 + D_MODEL:
            project_next()
        return jnp.concatenate([col_tiles.pop(c) for c in range(c0, c0 + D_MODEL, IN_NTILE)],
                               axis=1)

    def init_state():
        if has_h0:
            hf_s[...] = h0f_ref[0].T
        else:
            hf_s[...] = jnp.zeros_like(hf_s)

    if tiles_per_seq > 1:
        pl.when(t == 0)(init_state)
    else:
        init_state()
    h_state = hf_s[...]
    ya = []
    for ch in range(tm // q):
        rows = slice(ch * q, (ch + 1) * q)
        y, h_state = _ssd_chunk(
            xa_ref[0, rows, :], bc_ref[0, rows, 0:BC_COLS],
            bc_ref[0, rows, BC_COLS:], dtp_ref[0, rows, :], h_state, hbs_ref[0, ch],
            dtbias_ref, aneg_ref, dskip_ref, ef_ref, between_pairs)
        if ch == 0:
            zz = in_proj(MAIN_Z)
        yg = y * zz[rows]
        ms = jnp.mean(yg * yg, axis=-1, keepdims=True)
        ya.append(yg * lax.rsqrt(ms + EPS) * normw_ref[...])
    hf_s[...] = h_state

    def final_state():
        hff_ref[0] = h_state.T

    if tiles_per_seq > 1:
        pl.when(t == tiles_per_seq - 1)(final_state)
    else:
        final_state()
    ya = jnp.concatenate(ya, axis=0)
    proj_a = _dot(ya, wpa_ref[...])

    u = in_proj(MAIN_U)
    v = _layer_norm(in_proj(MAIN_V), lng_ref[...], lnb_ref[...])
    n_ch = tm // SGU_CHUNK
    vb = v.astype(BF16)
    per_group = []
    for g in range(SGU_GROUPS):
        gs = slice(g * SGU_GROUP_DIM, (g + 1) * SGU_GROUP_DIM)
        rhs = jnp.concatenate([vb[ch * SGU_CHUNK:(ch + 1) * SGU_CHUNK, gs] for ch in range(n_ch)],
                              axis=1)
        per_group.append(jnp.dot(wsp_ref[g], rhs, preferred_element_type=F32))
    rows_out = []
    for ch in range(n_ch):
        cs = slice(ch * SGU_GROUP_DIM, (ch + 1) * SGU_GROUP_DIM)
        rows_out.append(jnp.concatenate([p[:, cs] for p in per_group], axis=1) + bsp_ref[...])
    yb = u * jnp.concatenate(rows_out, axis=0)

    merged = in_proj(MAIN_GA) * proj_a + in_proj(MAIN_GB) * _dot(yb, wpb_ref[...])
    o = _dot(merged, wout_ref[...])
    o_ref[0] = _layer_norm(DN_ALPHA * x + gate1 * o, ln1g_ref[...], ln1b_ref[...])


def _main(x, pos, xa, bc, dtp, hbs, h0f, mods, mod_row0, tm, seq_len, consts):
    s_n, l_n, n_tiles, tiles_per_seq, seqs_per_row = _tiling(x, tm, seq_len)
    has_pos = pos is not None
    has_h0 = h0f is not None
    cpt = tm // SSD_CHUNK
    tok = lambda s, j: (s, j, 0)
    in_specs = [pl.BlockSpec((1, tm, D_MODEL), tok),
                pl.BlockSpec((1, tm, D_SSD), tok),
                pl.BlockSpec((1, tm, 2 * BC_COLS), tok),
                pl.BlockSpec((1, tm, DT_PAD), tok),
                pl.BlockSpec((1, cpt, SSD_STATE, D_SSD), lambda s, j: (s, j, 0, 0))]
    args = [x, xa, bc, dtp, hbs]
    if has_pos:
        in_specs += [_const_spec(pos[0].shape), _const_spec(pos[1].shape)]
        args += list(pos)
    state_spec = pl.BlockSpec((1, D_SSD, SSD_STATE),
                              lambda s, j: (s * seqs_per_row + j // tiles_per_seq, 0, 0))
    if has_h0:
        in_specs.append(state_spec)
        args.append(h0f)
    in_specs.append(pl.BlockSpec((1, 1, 6 * D_MODEL), lambda s, j: (mod_row0 + s, 0, 0)))
    args.append(mods)
    for a in consts:
        in_specs.append(_const_spec(a.shape))
        args.append(a)
    return pl.pallas_call(
        functools.partial(_main_kernel, has_pos=has_pos, has_h0=has_h0,
                          tiles_per_seq=tiles_per_seq),
        grid=(s_n, n_tiles),
        in_specs=in_specs,
        out_specs=[pl.BlockSpec((1, tm, D_MODEL), tok), state_spec],
        out_shape=[jax.ShapeDtypeStruct((s_n, l_n, D_MODEL), F32),
                   jax.ShapeDtypeStruct((s_n * seqs_per_row, D_SSD, SSD_STATE), F32)],
        scratch_shapes=[pltpu.VMEM((SSD_STATE, D_SSD), F32)],
        compiler_params=_params(VMEM_MIB_MAIN, 2),
        name="main",
    )(*args)


FF_TILE = 1024
FFN_TM = 1024


def _ffn_kernel(x_ref, mod_ref, w1_ref, w2_ref, g_ref, b_ref, o_ref):
    x = x_ref[0]
    shift2 = mod_ref[0, :, 3 * D_MODEL:4 * D_MODEL]
    scale2 = mod_ref[0, :, 4 * D_MODEL:5 * D_MODEL]
    gate2 = mod_ref[0, :, 5 * D_MODEL:6 * D_MODEL]
    h = (x * (1.0 + scale2) + shift2).astype(BF16)
    f = jnp.zeros_like(x)
    for k0 in range(0, D_FF, FF_TILE):
        a = jnp.dot(h, w1_ref[:, k0:k0 + FF_TILE], preferred_element_type=F32)
        a = jnp.maximum(a, 0.0)
        f = f + _dot(a * a, w2_ref[k0:k0 + FF_TILE, :])
    o_ref[0] = _layer_norm(DN_ALPHA * x + gate2 * f, g_ref[...], b_ref[...])


def _ffn(x, mods, mod_row0, tm, w1, w2, g, b):
    s_n, l_n, _ = x.shape
    tok = lambda s, j: (s, j, 0)
    return pl.pallas_call(
        _ffn_kernel,
        grid=(s_n, l_n // tm),
        in_specs=[pl.BlockSpec((1, tm, D_MODEL), tok),
                  pl.BlockSpec((1, 1, 6 * D_MODEL), lambda s, j: (mod_row0 + s, 0, 0)),
                  _const_spec(w1.shape), _const_spec(w2.shape),
                  _const_spec(g.shape), _const_spec(b.shape)],
        out_specs=pl.BlockSpec((1, tm, D_MODEL), tok),
        out_shape=jax.ShapeDtypeStruct((s_n, l_n, D_MODEL), F32),
        compiler_params=_params(VMEM_MIB_FFN, 2),
        name="ffn",
    )(x, mods, w1, w2, g, b)


def _grid_pos_tables(n_tokens):
    rows = n_tokens // GRID_W
    quarter = D_MODEL // 4
    omega = 1.0 / (POS_BASE ** (jnp.arange(quarter, dtype=F32) / quarter))
    r = jnp.arange(rows, dtype=F32)[:, None] * omega
    col = jnp.arange(GRID_W, dtype=F32)[:, None] * omega
    emb_r = jnp.concatenate([jnp.sin(r), jnp.cos(r)], axis=-1)
    emb_c = jnp.concatenate([jnp.sin(col), jnp.cos(col)], axis=-1)
    return jnp.broadcast_to(emb_r[:, None, :], (rows, SUBLANES, D_MODEL // 2)), emb_c


def _dt_lanes(v, axis):
    pad = [(0, 0)] * v.ndim
    pad[axis] = (0, DT_PAD - DT_COPIES * DT_GROUP)
    return jnp.pad(jnp.concatenate([v] * DT_COPIES, axis=axis), pad)


def _head_expanders():
    lane = jnp.arange(LANES)[:, None]
    head = jnp.arange(D_SSD)[None, :] // SSD_HEAD_DIM
    valid = lane < DT_COPIES * DT_GROUP
    e_f = jnp.logical_and(valid, lane % DT_GROUP == head)
    e_b = jnp.logical_and(valid, lane % DT_GROUP == head + SSD_HEADS)
    return e_f.astype(BF16), e_b.astype(BF16)


def _stream(x, pos, mods, mod_row0, seq_len, h0f, h0b, prm, tm):
    xa, bc, dtp, hbs, hb = _front(x, pos, mods, h0b, prm["w_xbc"], prm["w_dt"], prm["conv_w"],
                              prm["conv_b"],
                              prm["dt_bias"], prm["a_neg"], prm["e_b"], mod_row0, tm, seq_len)
    x1, hf = _main(x, pos, xa, bc, dtp, hbs, h0f, mods, mod_row0, tm, seq_len, prm["main"])
    y = _ffn(x1, mods, mod_row0, FFN_TM, prm["w_ff1"], prm["w_ff2"], prm["ln2_g"], prm["ln2_b"])
    return y, hf, hb


def kernel(x_prompt, x_sample, state_ssd_fwd, state_ssd_bwd, c, c_ctx, w_ada, b_ada, w_in, conv_w, conv_b, dt_bias_fwd, dt_bias_bwd, a_log_fwd, a_log_bwd, d_skip, ssd_norm_w, sgu_ln_g, sgu_ln_b, w_spatial, b_spatial, w_proj_a, w_proj_b, w_out, ln1_g, ln1_b, w_ff1, w_ff2, ln2_g, ln2_b):
    batch, seq, _ = x_prompt.shape
    dec_batch, dec_seq, _ = x_sample.shape
    l = 0
    row = lambda v: v.reshape(1, -1)

    cond = jnp.zeros((SUBLANES, D_MODEL), F32).at[0].set(c_ctx).at[1:1 + dec_batch].set(c)
    mods = _ada(cond, w_ada[l], b_ada[l]).reshape(SUBLANES, 1, 6 * D_MODEL)

    o1 = D_SSD
    o2 = o1 + SSD_CONV_DIM
    o3 = o2 + DT_GROUP
    w_t = jnp.swapaxes(w_in[l], 0, 1)
    step = W_COL_TILE // DT_GROUP
    w_xbc = _w_cols(w_t, lambda i: o1 // DT_GROUP + i * step, SSD_CONV_DIM // W_COL_TILE)
    z_tiles = o1 // W_COL_TILE
    w_main = _w_cols(
        w_t, lambda i: jnp.where(i < z_tiles, i * step, o3 // DT_GROUP + (i - z_tiles) * step),
        MAIN_COLS // W_COL_TILE)
    assert o2 % LANES == 0
    w_dt = _w_dt(w_t, o2)
    e_f, e_b = _head_expanders()
    dt_bias = row(_dt_lanes(jnp.concatenate([dt_bias_fwd[l], dt_bias_bwd[l]]), 0))
    a_neg = row(_dt_lanes(-jnp.exp(jnp.concatenate([a_log_fwd[l], a_log_bwd[l]])), 0))

    prm = dict(
        w_xbc=w_xbc, w_dt=w_dt, conv_w=conv_w[l], conv_b=row(conv_b[l]),
        dt_bias=dt_bias, a_neg=a_neg, e_b=e_b,
        main=(w_main, dt_bias, a_neg, row(jnp.repeat(d_skip[l], SSD_HEAD_DIM)),
              row(ssd_norm_w[l]), e_f, row(sgu_ln_g[l]), row(sgu_ln_b[l]),
              w_spatial[l].astype(BF16), jnp.repeat(b_spatial[l].T, SGU_GROUP_DIM, axis=1),
              w_proj_a[l].astype(BF16), w_proj_b[l].astype(BF16), w_out[l].astype(BF16),
              row(ln1_g[l]), row(ln1_b[l])),
        w_ff1=w_ff1[l].astype(BF16), w_ff2=w_ff2[l].astype(BF16),
        ln2_g=row(ln2_g[l]), ln2_b=row(ln2_b[l]),
    )

    tm = 256
    xp, hf, hb = _stream(x_prompt.reshape(1, batch * seq, D_MODEL), None, mods, 0, seq,
                         None, None, prm, tm)
    pos = _grid_pos_tables(dec_seq)
    h0f = state_ssd_fwd[:, l].reshape(dec_batch, D_SSD, SSD_STATE)
    h0b = state_ssd_bwd[:, l].reshape(dec_batch, D_SSD, SSD_STATE)
    xs, _, _ = _stream(x_sample, pos, mods, 1, dec_seq, h0f, h0b, prm, tm)

    state_shape = (batch, DEPTH, SSD_HEADS, SSD_HEAD_DIM, SSD_STATE)
    return (xp.reshape(batch, seq, D_MODEL), xs,
            hf.reshape(state_shape), hb.reshape(state_shape))
```
